```python
import jax, jax.numpy as jnp
from jax import lax
import numpy as np

D_MODEL = 1024
BATCH = 8
SEQ = 8192
DEPTH = 4
DEC_BATCH = 16
DEC_SEQ = 64
PAST_LEN = 1024

CHUNK = 64
N_EVEN = (DEPTH + 1) // 2
N_ODD = DEPTH // 2
A_WIDTH = D_MODEL // 2
A_HEAD_DIM = 64
A_HEADS = A_WIDTH // A_HEAD_DIM
A_DECAY_LORA = 64
A_ICL_LORA = 64
A_GATE_LORA = 128
A_PROJ = 3 * A_WIDTH + A_DECAY_LORA + A_ICL_LORA + A_GATE_LORA
A_NORM_EPS = 64e-5
B_VWIDTH = D_MODEL // 2
B_KWIDTH = B_VWIDTH // 2
B_HEADS = 4
B_KEY_DIM = B_KWIDTH // B_HEADS
B_VAL_DIM = B_VWIDTH // B_HEADS
B_GATE_LORA = 16
B_GATE_NORM = 16.0
B_PROJ = 2 * B_KWIDTH + 2 * B_VWIDTH + B_GATE_LORA
MIX_PROJ = A_PROJ + B_PROJ
C_HEAD_DIM = 64
C_HEADS = D_MODEL // C_HEAD_DIM
C_PAST_CHUNKS = 8
C_PAST_ROWS = C_PAST_CHUNKS * CHUNK
C_BAND = C_PAST_ROWS + CHUNK
C_REL_CLIP = 128
FFN_HIDDEN = ((8 * D_MODEL + 3 * 256 - 1) // (3 * 256)) * 256
LN_EPS = 1e-5
DEEPNORM_ALPHA = (2.0 * DEPTH) ** 0.25
DEEPNORM_BETA = (8.0 * DEPTH) ** -0.25
NEG_INF = -1e30

kernel_name = "hybrid_rwkv7_gla_chunkband_streaming_step"

F32 = jnp.float32


def layer_norm(x, w, b):
    xf = x.astype(F32)
    mu = jnp.mean(xf, -1, keepdims=True)
    var = jnp.mean(jnp.square(xf - mu), -1, keepdims=True)
    return ((xf - mu) * lax.rsqrt(var + LN_EPS) * w.astype(F32) + b.astype(F32)).astype(x.dtype)


def swiglu(x, w_in, w_out):
    gate, up = jnp.split(x @ w_in, 2, axis=-1)
    return (jax.nn.silu(gate) * up) @ w_out


def wkv7_scan(r, decay, k, v, kk, a, S0):
    def step(S, inp):
        r_t, w_t, k_t, v_t, kk_t, a_t = inp
        sa = jnp.einsum('bhvk,bhk->bhv', S, kk_t)
        S = (S * w_t[:, :, None, :] - sa[..., None] * (kk_t * a_t)[:, :, None, :]
             + v_t[..., None] * k_t[:, :, None, :])
        return S, jnp.einsum('bhvk,bhk->bhv', S, r_t)
    xs = tuple(jnp.swapaxes(t, 0, 1) for t in (r, decay, k, v, kk, a))
    S, o = lax.scan(step, S0.astype(F32), xs)
    return S, jnp.swapaxes(o, 0, 1)


def rwkv7_time_mix(pa, shift0, wkv0, mu, w0, w2, a0, a2, g2, k_k, k_a, r_k, ln_w, ln_b):
    Bsz, T, _ = pa.shape
    prev = jnp.concatenate([shift0[:, None, :].astype(pa.dtype), pa[:, :-1]], axis=1)
    xs = pa + (prev - pa) * mu
    i3 = 3 * A_WIDTH
    r, k, v, xw, xa, xg = jnp.split(
        xs, [A_WIDTH, 2 * A_WIDTH, i3, i3 + A_DECAY_LORA, i3 + A_DECAY_LORA + A_ICL_LORA], axis=-1)
    w = -jax.nn.softplus(-(w0 + jnp.tanh(xw) @ w2).astype(F32)) - 0.5
    decay = jnp.exp(-jnp.exp(w))
    a = jax.nn.sigmoid((a0 + xa @ a2).astype(F32))
    g = jax.nn.sigmoid(xg) @ g2
    hs = (Bsz, T, A_HEADS, A_HEAD_DIM)
    hd = (A_HEADS, A_HEAD_DIM)
    r, k, v, decay, a = (t.astype(F32).reshape(hs) for t in (r, k, v, decay, a))
    kk = k * k_k.astype(F32).reshape(hd)
    kk = kk * lax.rsqrt(jnp.sum(kk * kk, -1, keepdims=True) + 1e-12)
    k = k * (1.0 + (a - 1.0) * k_a.astype(F32).reshape(hd))
    wkv, o = wkv7_scan(r, decay, k, v, kk, a, wkv0)
    mean = jnp.mean(o, -1, keepdims=True)
    var = jnp.mean(jnp.square(o - mean), -1, keepdims=True)
    o = (o - mean) * lax.rsqrt(var + A_NORM_EPS) * ln_w.astype(F32).reshape(hd) + ln_b.astype(F32).reshape(hd)
    o = o + jnp.sum(r * k * r_k.astype(F32), -1, keepdims=True) * v
    y = o.reshape(Bsz, T, A_WIDTH).astype(pa.dtype) * g
    return y, wkv.astype(wkv0.dtype), pa[:, -1]


def _to_chunks(t, L):
    Bsz, T, H, d = t.shape
    return t.reshape(Bsz, T // L, L, H, d).transpose(1, 0, 3, 2, 4)


def gla_chunked(q, k, v, log_a, S0):
    Bsz, T, H, _ = q.shape
    L = min(CHUNK, T)
    qc, kc, vc, lc = (_to_chunks(t, L) for t in (q, k, v, log_a))
    bc = jnp.cumsum(lc, axis=3)
    mask = jnp.tril(jnp.ones((L, L), bool))

    def step(S, inp):
        q_i, k_i, v_i, b_i = inp
        b_last = b_i[:, :, -1, :]
        q_dec = q_i * jnp.exp(b_i)
        scores = jnp.einsum('bhtd,bhsd->bhts', q_dec, k_i * jnp.exp(-b_i))
        scores = jnp.where(mask, scores, 0.0)
        o = jnp.einsum('bhtd,bhdv->bhtv', q_dec, S) + jnp.einsum('bhts,bhsv->bhtv', scores, v_i)
        k_state = k_i * jnp.exp(b_last[:, :, None, :] - b_i)
        S = S * jnp.exp(b_last)[..., None] + jnp.einsum('bhsd,bhsv->bhdv', k_state, v_i)
        return S, o

    S, o = lax.scan(step, S0.astype(F32), (qc, kc, vc, bc))
    o = o.transpose(1, 0, 3, 2, 4).reshape(Bsz, T, H, v.shape[-1])
    return o, S


def gla_mix(pb, S0, alpha_up, alpha_bias, norm_w):
    Bsz, T, _ = pb.shape
    kw, vw = B_KWIDTH, B_VWIDTH
    q, k, v, xa, rg = jnp.split(pb, [kw, 2 * kw, 2 * kw + vw, 2 * kw + vw + B_GATE_LORA], axis=-1)
    log_a = jax.nn.log_sigmoid((xa @ alpha_up + alpha_bias).astype(F32)) / B_GATE_NORM
    ks = (Bsz, T, B_HEADS, B_KEY_DIM)
    q = q.astype(F32).reshape(ks) * (B_KEY_DIM ** -0.5)
    k = k.astype(F32).reshape(ks)
    log_a = log_a.reshape(ks)
    v = v.astype(F32).reshape(Bsz, T, B_HEADS, B_VAL_DIM)
    o, S = gla_chunked(q, k, v, log_a, S0)
    o = o * lax.rsqrt(jnp.mean(o * o, -1, keepdims=True) + LN_EPS) * norm_w.astype(F32)
    y = o.reshape(Bsz, T, B_VWIDTH).astype(pb.dtype) * jax.nn.silu(rg)
    return y, S.astype(S0.dtype)


def rel_bias_lookup(table, rel):
    idx = jnp.clip(rel, -C_REL_CLIP, C_REL_CLIP) + C_REL_CLIP
    return table.astype(F32)[:, idx]


def band_attention_prompt(q, k, v, rel_table):
    Bsz, H, S, hd = q.shape
    kp = jnp.pad(k, ((0, 0), (0, 0), (C_PAST_ROWS, 0), (0, 0)))
    vp = jnp.pad(v, ((0, 0), (0, 0), (C_PAST_ROWS, 0), (0, 0)))
    i = jnp.arange(CHUNK)
    j = jnp.arange(C_BAND)
    bias = rel_bias_lookup(rel_table, C_PAST_ROWS + i[:, None] - j[None, :])
    scale = hd ** -0.5

    def one_chunk(c):
        qc = lax.dynamic_slice_in_dim(q, c * CHUNK, CHUNK, axis=2)
        kc = lax.dynamic_slice_in_dim(kp, c * CHUNK, C_BAND, axis=2)
        vc = lax.dynamic_slice_in_dim(vp, c * CHUNK, C_BAND, axis=2)
        valid = j >= C_PAST_ROWS - c * CHUNK
        s = jnp.einsum('bhqd,bhkd->bhqk', qc, kc).astype(F32) * scale + bias
        s = jnp.where(valid[None, None, None, :], s, NEG_INF)
        p = jax.nn.softmax(s, axis=-1)
        return jnp.einsum('bhqk,bhkd->bhqd', p.astype(vc.dtype), vc)

    out = lax.map(one_chunk, jnp.arange(S // CHUNK))
    return out.transpose(1, 2, 0, 3, 4).reshape(Bsz, H, S, hd)


def band_attention_sample(q, k_new, v_new, k_cache, v_cache, rel_table):
    R = k_cache.shape[2]
    T = q.shape[2]
    k = jnp.concatenate([k_cache.astype(k_new.dtype), k_new], axis=2)
    v = jnp.concatenate([v_cache.astype(v_new.dtype), v_new], axis=2)
    rel = R + jnp.arange(T)[:, None] - jnp.arange(R + T)[None, :]
    s = jnp.einsum('bhqd,bhkd->bhqk', q, k).astype(F32) * (q.shape[-1] ** -0.5) + rel_bias_lookup(rel_table, rel)
    p = jax.nn.softmax(s, axis=-1)
    return jnp.einsum('bhqk,bhkd->bhqd', p.astype(v.dtype), v)


def chunk_band_attention(x, w_qkv, rel_table, w_o, cache_k, cache_v):
    Bsz, T, _ = x.shape
    qkv = (x @ w_qkv).reshape(Bsz, T, 3, C_HEADS, C_HEAD_DIM)
    q, k, v = (jnp.swapaxes(qkv[:, :, i], 1, 2) for i in range(3))
    if cache_k is None:
        o = band_attention_prompt(q, k, v, rel_table)
        keep = min(C_PAST_ROWS, T)
        k_rows, v_rows = k[:, :, T - keep:], v[:, :, T - keep:]
    else:
        o = band_attention_sample(q, k, v, cache_k, cache_v, rel_table)
        k_rows, v_rows = k, v
    y = jnp.swapaxes(o, 1, 2).reshape(Bsz, T, D_MODEL) @ w_o
    return y, k_rows, v_rows


def run_trunk(x, wkv0, shift0, gla0, cache_k, cache_v, P):
    wkv_o, shift_o, gla_o, k_o, v_o = [], [], [], [], []
    for layer in range(DEPTH):
        if layer % 2 == 0:
            e = layer // 2
            p = x @ P['w_in_mix'][e]
            ya, wkv, sh = rwkv7_time_mix(
                p[..., :A_PROJ], shift0[e], wkv0[e], P['a_mu'][e], P['a_w0'][e], P['a_w2'][e],
                P['a_a0'][e], P['a_a2'][e], P['a_g2'][e], P['a_k_k'][e], P['a_k_a'][e],
                P['a_r_k'][e], P['a_ln_w'][e], P['a_ln_b'][e])
            yb, gs = gla_mix(p[..., A_PROJ:], gla0[e], P['b_alpha_up'][e], P['b_alpha_bias'][e], P['b_norm_w'][e])
            mix = jnp.concatenate([ya, yb], axis=-1) @ P['w_out_mix'][e]
            wkv_o.append(wkv)
            shift_o.append(sh)
            gla_o.append(gs)
        else:
            o = layer // 2
            ck = None if cache_k is None else cache_k[o]
            cv = None if cache_v is None else cache_v[o]
            mix, kr, vr = chunk_band_attention(x, P['c_w_qkv'][o], P['c_rel_bias'][o], P['c_w_o'][o], ck, cv)
            k_o.append(kr)
            v_o.append(vr)
        x = layer_norm(DEEPNORM_ALPHA * x + mix, P['ln1_w'][layer], P['ln1_b'][layer])
        x = layer_norm(DEEPNORM_ALPHA * x + swiglu(x, P['ffn_w_in'][layer], P['ffn_w_out'][layer]),
                       P['ln2_w'][layer], P['ln2_b'][layer])
    return x, jnp.stack(wkv_o), jnp.stack(shift_o), jnp.stack(gla_o), jnp.stack(k_o), jnp.stack(v_o)


def setup_inputs(seed: int = 0) -> dict:
    key = jax.random.key(seed)
    keys = iter(jax.random.split(key, 40))

    def nrm(shape, scale):
        return jax.random.normal(next(keys), shape, F32) * scale

    def uni(shape, lo, hi):
        return jax.random.uniform(next(keys), shape, F32, minval=lo, maxval=hi)

    kv_rows = min(C_PAST_ROWS, PAST_LEN)
    return {
        'x_prompt': nrm((BATCH, SEQ, D_MODEL), 1.0),
        'x_sample': nrm((DEC_BATCH, DEC_SEQ, D_MODEL), 1.0),
        'state_a_wkv': nrm((N_EVEN, DEC_BATCH, A_HEADS, A_HEAD_DIM, A_HEAD_DIM), 0.3),
        'state_a_shift': nrm((N_EVEN, DEC_BATCH, A_PROJ), 1.0),
        'state_b_gla': nrm((N_EVEN, DEC_BATCH, B_HEADS, B_KEY_DIM, B_VAL_DIM), 0.3),
        'cache_c_k': nrm((N_ODD, DEC_BATCH, C_HEADS, kv_rows, C_HEAD_DIM), 1.0),
        'cache_c_v': nrm((N_ODD, DEC_BATCH, C_HEADS, kv_rows, C_HEAD_DIM), 1.0),
        'w_in_mix': nrm((N_EVEN, D_MODEL, MIX_PROJ), D_MODEL ** -0.5),
        'a_mu': uni((N_EVEN, A_PROJ), 0.0, 1.0),
        'a_w0': uni((N_EVEN, A_WIDTH), -6.0, 1.0),
        'a_w2': nrm((N_EVEN, A_DECAY_LORA, A_WIDTH), 0.1 * A_DECAY_LORA ** -0.5),
        'a_a0': nrm((N_EVEN, A_WIDTH), 0.1),
        'a_a2': nrm((N_EVEN, A_ICL_LORA, A_WIDTH), A_ICL_LORA ** -0.5),
        'a_g2': nrm((N_EVEN, A_GATE_LORA, A_WIDTH), A_GATE_LORA ** -0.5),
        'a_k_k': 0.85 + nrm((N_EVEN, A_WIDTH), 0.05),
        'a_k_a': 1.0 + nrm((N_EVEN, A_WIDTH), 0.05),
        'a_r_k': nrm((N_EVEN, A_HEADS, A_HEAD_DIM), 0.1),
        'a_ln_w': 1.0 + nrm((N_EVEN, A_WIDTH), 0.05),
        'a_ln_b': nrm((N_EVEN, A_WIDTH), 0.02),
        'b_alpha_up': nrm((N_EVEN, B_GATE_LORA, B_KWIDTH), B_GATE_LORA ** -0.5),
        'b_alpha_bias': nrm((N_EVEN, B_KWIDTH), 0.1) + 1.0,
        'b_norm_w': 1.0 + nrm((N_EVEN, B_VAL_DIM), 0.05),
        'w_out_mix': nrm((N_EVEN, A_WIDTH + B_VWIDTH, D_MODEL), DEEPNORM_BETA * (A_WIDTH + B_VWIDTH) ** -0.5),
        'c_w_qkv': nrm((N_ODD, D_MODEL, 3 * D_MODEL), D_MODEL ** -0.5),
        'c_rel_bias': nrm((N_ODD, C_HEADS, 2 * C_REL_CLIP + 1), 0.2),
        'c_w_o': nrm((N_ODD, D_MODEL, D_MODEL), DEEPNORM_BETA * D_MODEL ** -0.5),
        'ln1_w': 1.0 + nrm((DEPTH, D_MODEL), 0.05),
        'ln1_b': nrm((DEPTH, D_MODEL), 0.02),
        'ln2_w': 1.0 + nrm((DEPTH, D_MODEL), 0.05),
        'ln2_b': nrm((DEPTH, D_MODEL), 0.02),
        'ffn_w_in': nrm((DEPTH, D_MODEL, 2 * FFN_HIDDEN), D_MODEL ** -0.5),
        'ffn_w_out': nrm((DEPTH, FFN_HIDDEN, D_MODEL), DEEPNORM_BETA * FFN_HIDDEN ** -0.5),
    }


def reference(x_prompt, x_sample, state_a_wkv, state_a_shift, state_b_gla, cache_c_k, cache_c_v,
              w_in_mix, a_mu, a_w0, a_w2, a_a0, a_a2, a_g2, a_k_k, a_k_a, a_r_k, a_ln_w, a_ln_b,
              b_alpha_up, b_alpha_bias, b_norm_w, w_out_mix, c_w_qkv, c_rel_bias, c_w_o,
              ln1_w, ln1_b, ln2_w, ln2_b, ffn_w_in, ffn_w_out):
    P = dict(w_in_mix=w_in_mix, a_mu=a_mu, a_w0=a_w0, a_w2=a_w2, a_a0=a_a0, a_a2=a_a2, a_g2=a_g2,
             a_k_k=a_k_k, a_k_a=a_k_a, a_r_k=a_r_k, a_ln_w=a_ln_w, a_ln_b=a_ln_b,
             b_alpha_up=b_alpha_up, b_alpha_bias=b_alpha_bias, b_norm_w=b_norm_w, w_out_mix=w_out_mix,
             c_w_qkv=c_w_qkv, c_rel_bias=c_rel_bias, c_w_o=c_w_o,
             ln1_w=ln1_w, ln1_b=ln1_b, ln2_w=ln2_w, ln2_b=ln2_b, ffn_w_in=ffn_w_in, ffn_w_out=ffn_w_out)
    bp = x_prompt.shape[0]
    dt = x_prompt.dtype
    wkv_zero = jnp.zeros((N_EVEN, bp, A_HEADS, A_HEAD_DIM, A_HEAD_DIM), dt)
    shift_zero = jnp.zeros((N_EVEN, bp, A_PROJ), dt)
    gla_zero = jnp.zeros((N_EVEN, bp, B_HEADS, B_KEY_DIM, B_VAL_DIM), dt)
    y_prompt, p_wkv, p_shift, p_gla, p_k, p_v = run_trunk(
        x_prompt, wkv_zero, shift_zero, gla_zero, None, None, P)
    y_sample, s_wkv, s_shift, s_gla, s_k, s_v = run_trunk(
        x_sample, state_a_wkv, state_a_shift, state_b_gla, cache_c_k, cache_c_v, P)
    return (y_prompt, y_sample, p_wkv, p_shift, p_gla, p_k, p_v, s_wkv, s_shift, s_gla, s_k, s_v)
```

```python
import functools

import jax
import jax.numpy as jnp
from jax import lax
from jax.experimental import pallas as pl
from jax.experimental.pallas import tpu as pltpu

F32 = jnp.float32
BF16 = jnp.bfloat16

D_MODEL = 1024
DEPTH = 4
CHUNK = 64
A_WIDTH = 512
A_HEAD_DIM = 64
A_HEADS = 8
A_DECAY_LORA = 64
A_ICL_LORA = 64
A_GATE_LORA = 128
A_PROJ = 3 * A_WIDTH + A_DECAY_LORA + A_ICL_LORA + A_GATE_LORA
A_NORM_EPS = 64e-5
B_VWIDTH = 512
B_KWIDTH = 256
B_HEADS = 4
B_KEY_DIM = 64
B_VAL_DIM = 128
B_GATE_LORA = 16
B_GATE_NORM = 16.0
B_LORA_PAD = 128
B_PROJ_PAD = 2 * B_KWIDTH + 2 * B_VWIDTH + B_LORA_PAD
C_HEAD_DIM = 64
C_HEADS = 16
C_PAST_CHUNKS = 8
C_PAST_ROWS = C_PAST_CHUNKS * CHUNK
C_BAND = C_PAST_ROWS + CHUNK
C_REL_CLIP = 128
FFN_HIDDEN = 2816
LN_EPS = 1e-5
DEEPNORM_ALPHA = (2.0 * DEPTH) ** 0.25
NEG_INF = -1e30

LANE = 128
VMEM_LIMIT = 56 * 1024 * 1024

NN = (((1,), (0,)), ((), ()))
NT = (((1,), (1,)), ((), ()))
TN = (((0,), (0,)), ((), ()))


def _params(*sem):
    return pltpu.CompilerParams(dimension_semantics=sem, vmem_limit_bytes=VMEM_LIMIT)


def _dg(a, b, dn):
    return lax.dot_general(a, b, dn, preferred_element_type=F32)


def _dot1(a, b, dn=NN):
    return _dg(a.astype(BF16), b.astype(BF16), dn)


def _split(a):
    hi = a.astype(BF16)
    lo = (a - hi.astype(F32)).astype(BF16)
    return hi, lo


def _dot3(a, b, dn=NN):
    ah, al = _split(a)
    bh, bl = _split(b)
    return _dg(ah, bh, dn) + _dg(ah, bl, dn) + _dg(al, bh, dn)


def _dot_xl(a_exact, b, dn=NN):
    bh, bl = _split(b)
    return _dg(a_exact, bh, dn) + _dg(a_exact, bl, dn)


def _dot_xr(a, b_exact, dn=NN):
    ah, al = _split(a)
    return _dg(ah, b_exact, dn) + _dg(al, b_exact, dn)


def _ln(z, w, b):
    mu = jnp.mean(z, -1, keepdims=True)
    d = z - mu
    var = jnp.mean(d * d, -1, keepdims=True)
    return d * lax.rsqrt(var + LN_EPS) * w + b


def _sigmoid(x):
    return 1.0 / (1.0 + jnp.exp(-x))


def _softplus(x):
    return jnp.maximum(x, 0.0) + jnp.log(1.0 + jnp.exp(-jnp.abs(x)))


def _mm_kernel(x_ref, w_ref, o_ref):
    o_ref[...] = jnp.dot(x_ref[...].astype(BF16), w_ref[...], preferred_element_type=F32)


def _matmul(x, w, tm, tn):
    M, K = x.shape
    N = w.shape[1]
    tm = min(tm, M)
    return pl.pallas_call(
        _mm_kernel,
        grid=(M // tm, N // tn),
        in_specs=[pl.BlockSpec((tm, K), lambda i, j: (i, 0)),
                  pl.BlockSpec((K, tn), lambda i, j: (0, j))],
        out_specs=pl.BlockSpec((tm, tn), lambda i, j: (i, j)),
        out_shape=jax.ShapeDtypeStruct((M, N), F32),
        compiler_params=_params("parallel", "arbitrary"),
    )(x, w)


def _mm_ln_kernel(n_pairs, *refs):
    res_ref, lw_ref, lb_ref, o_ref = refs[2 * n_pairs:]
    acc = None
    for p in range(n_pairs):
        part = jnp.dot(refs[2 * p][...].astype(BF16), refs[2 * p + 1][...], preferred_element_type=F32)
        acc = part if acc is None else acc + part
    o_ref[...] = _ln(DEEPNORM_ALPHA * res_ref[...] + acc, lw_ref[...], lb_ref[...])


def _matmul_ln(pairs, res, lw, lb, tm):
    M = res.shape[0]
    tm = min(tm, M)
    in_specs, args = [], []
    for a, w in pairs:
        K = a.shape[1]
        in_specs += [pl.BlockSpec((tm, K), lambda i: (i, 0)), pl.BlockSpec((K, D_MODEL), lambda i: (0, 0))]
        args += [a, w]
    in_specs += [pl.BlockSpec((tm, D_MODEL), lambda i: (i, 0)),
                 pl.BlockSpec((1, D_MODEL), lambda i: (0, 0)),
                 pl.BlockSpec((1, D_MODEL), lambda i: (0, 0))]
    return pl.pallas_call(
        functools.partial(_mm_ln_kernel, len(pairs)),
        grid=(M // tm,),
        in_specs=in_specs,
        out_specs=pl.BlockSpec((tm, D_MODEL), lambda i: (i, 0)),
        out_shape=jax.ShapeDtypeStruct((M, D_MODEL), F32),
        compiler_params=_params("parallel"),
    )(*args, res, lw, lb)


FFN_SPLIT = 2
FFN_TH = FFN_HIDDEN // FFN_SPLIT


def _ffn_kernel(x_ref, wg_ref, wu_ref, wo_ref, lw_ref, lb_ref, o_ref, acc_ref):
    j = pl.program_id(1)
    x = x_ref[...]
    xb = x.astype(BF16)
    g = jnp.dot(xb, wg_ref[...], preferred_element_type=F32)
    u = jnp.dot(xb, wu_ref[...], preferred_element_type=F32)
    act = g * _sigmoid(g) * u
    part = jnp.dot(act.astype(BF16), wo_ref[...], preferred_element_type=F32)

    @pl.when(j == 0)
    def _():
        acc_ref[...] = part

    @pl.when(j == FFN_SPLIT - 1)
    def _():
        o_ref[...] = _ln(DEEPNORM_ALPHA * x + acc_ref[...] + part, lw_ref[...], lb_ref[...])


def _ffn_ln(x, w_in, w_out, lw, lb, tm):
    M = x.shape[0]
    tm = min(tm, M)
    return pl.pallas_call(
        _ffn_kernel,
        grid=(M // tm, FFN_SPLIT),
        in_specs=[pl.BlockSpec((tm, D_MODEL), lambda i, j: (i, 0)),
                  pl.BlockSpec((D_MODEL, FFN_TH), lambda i, j: (0, j)),
                  pl.BlockSpec((D_MODEL, FFN_TH), lambda i, j: (0, j + FFN_SPLIT)),
                  pl.BlockSpec((FFN_TH, D_MODEL), lambda i, j: (j, 0)),
                  pl.BlockSpec((1, D_MODEL), lambda i, j: (0, 0)),
                  pl.BlockSpec((1, D_MODEL), lambda i, j: (0, 0))],
        out_specs=pl.BlockSpec((tm, D_MODEL), lambda i, j: (i, 0)),
        out_shape=jax.ShapeDtypeStruct((M, D_MODEL), F32),
        scratch_shapes=[pltpu.VMEM((tm, D_MODEL), F32)],
        compiler_params=_params("parallel", "arbitrary"),
    )(x, w_in, w_in, w_out, lw, lb)


def _inv_unit_lower(n, eye):
    size = n.shape[0]
    x = eye - n
    p = _dot1(n, n)
    k = 2
    while k < size:
        x = x + _dot1(x, p)
        k *= 2
        if k < size:
            p = _dot1(p, p)
    return x


def _rwkv_prep_kernel(n_chunks, pa_ref, prev_ref, shift_ref, mu_ref, w0_ref, w2_ref, a0_ref, a2_ref, g2_ref,
                      kk_ref, ka_ref, rk_ref, seg_ref,
                      mh_ref, ch_ref, qc_ref, oc_ref, g_ref, bonus_ref):
    i = pl.program_id(1)
    L = CHUNK
    R = n_chunks * L
    pa = pa_ref[...]
    first = jnp.where(i == 0, shift_ref[...], prev_ref[7:8, :])
    rows = lax.broadcasted_iota(jnp.int32, pa.shape, 0)
    prev = jnp.where(rows == 0, first, pltpu.roll(pa, shift=1, axis=0))
    xs = pa + (prev - pa) * mu_ref[...]
    W = A_WIDTH
    r = xs[:, 0:W]
    k0 = xs[:, W:2 * W]
    v = xs[:, 2 * W:3 * W]
    xw = xs[:, 3 * W:3 * W + A_DECAY_LORA]
    xa = xs[:, 3 * W + A_DECAY_LORA:3 * W + A_DECAY_LORA + A_ICL_LORA]
    xg = xs[:, 3 * W + A_DECAY_LORA + A_ICL_LORA:]
    w = -_softplus(-(w0_ref[...] + _dot3(jnp.tanh(xw), w2_ref[...]))) - 0.5
    logw = -jnp.exp(w)
    a = _sigmoid(a0_ref[...] + _dot3(xa, a2_ref[...]))
    g_ref[...] = _dot3(_sigmoid(xg), g2_ref[...])
    seg = seg_ref[...]
    kk = k0 * kk_ref[...]
    kk = kk * lax.rsqrt(_dot_xr(kk * kk, seg) + 1e-12)
    k = k0 * (1.0 + (a - 1.0) * ka_ref[...])
    bonus_ref[...] = _dot_xr(r * k * rk_ref[...], seg) * v
    b = kk * a

    row = lax.broadcasted_iota(jnp.int32, (L, L), 0)
    col = lax.broadcasted_iota(jnp.int32, (L, L), 1)
    tri = (row >= col).astype(BF16)
    eye = (row == col).astype(F32)
    strict = row > col
    incl = row >= col
    for c in range(n_chunks):
        sl = slice(c * L, (c + 1) * L)
        lw = logw[sl]
        cum = _dot_xl(tri, lw)
        c_last = cum[L - 1:L, :]
        e_neg = jnp.exp(-cum)
        e_rem = jnp.exp(c_last - cum)
        kkg = kk[sl] * jnp.exp(cum - lw)
        rg = r[sl] * jnp.exp(cum)
        kinv = k[sl] * e_neg
        binv = b[sl] * e_neg
        kd = k[sl] * e_rem
        bd = b[sl] * e_rem
        e_last = jnp.exp(c_last)
        vc = v[sl]
        for h in range(A_HEADS):
            hs = slice(h * A_HEAD_DIM, (h + 1) * A_HEAD_DIM)
            kkg_h, rg_h, kinv_h, binv_h, v_h = kkg[:, hs], rg[:, hs], kinv[:, hs], binv[:, hs], vc[:, hs]
            a_kb = jnp.where(strict, _dot1(kkg_h, binv_h, NT), 0.0)
            a_kk = jnp.where(strict, _dot1(kkg_h, kinv_h, NT), 0.0)
            a_rk = jnp.where(incl, _dot1(rg_h, kinv_h, NT), 0.0)
            a_rb = jnp.where(incl, _dot1(rg_h, binv_h, NT), 0.0)
            t = _inv_unit_lower(a_kb, eye)
            av = _dot3(a_kk, v_h)
            wk = _dot3(t, kkg_h)
            uv = _dot3(t, av)
            bd_h = bd[:, hs]
            mh_ref[c, :, hs] = eye * e_last[:, hs] - _dot3(bd_h, wk, TN)
            ch_ref[c, :, hs] = _dot3(kd[:, hs], v_h, TN) - _dot3(bd_h, uv, TN)
            qc_ref[sl, hs] = rg_h - _dot3(a_rb, wk)
            oc_ref[sl, hs] = _dot3(a_rk, v_h) - _dot3(a_rb, uv)


def _rwkv_scan_kernel(n_chunks, mh_ref, ch_ref, qc_ref, oc_ref, g_ref, bonus_ref, h0_ref, lnw_ref, lnb_ref, seg_ref,
                      y_ref, hout_ref, h_ref):
    i = pl.program_id(1)
    L = CHUNK

    @pl.when(i == 0)
    def _():
        h_ref[...] = h0_ref[...]

    seg = seg_ref[...]
    inv_n = 1.0 / A_HEAD_DIM
    for c in range(n_chunks):
        sl = slice(c * L, (c + 1) * L)
        for h in range(A_HEADS):
            hs = slice(h * A_HEAD_DIM, (h + 1) * A_HEAD_DIM)
            hh = h_ref[:, hs]
            y_ref[sl, hs] = _dot3(qc_ref[sl, hs], hh) + oc_ref[sl, hs]
            h_ref[:, hs] = _dot3(mh_ref[c, :, hs], hh) + ch_ref[c, :, hs]
        o = y_ref[sl, :]
        mean = _dot_xr(o, seg) * inv_n
        d = o - mean
        var = _dot_xr(d * d, seg) * inv_n
        on = d * lax.rsqrt(var + A_NORM_EPS) * lnw_ref[...] + lnb_ref[...]
        y_ref[sl, :] = (on + bonus_ref[sl, :]) * g_ref[sl, :]
    hout_ref[...] = h_ref[...]


def _rwkv_mix(pa, shift0, wkv0, prm):
    Bsz, T, _ = pa.shape
    n1 = 1
    R1 = n1 * CHUNK
    W = A_WIDTH
    nblk = T // R1
    row = lambda n: pl.BlockSpec((1, n), lambda b, i: (0, 0))
    full = lambda s: pl.BlockSpec(s, lambda b, i: (0,) * len(s))
    act = lambda r, n: pl.BlockSpec((None, r, n), lambda b, i: (b, i, 0))
    mat = lambda n: pl.BlockSpec((None, n, CHUNK, W), lambda b, i: (b, i, 0, 0))
    sub = R1 // 8
    mh, ch, qc, oc, g, bonus = pl.pallas_call(
        functools.partial(_rwkv_prep_kernel, n1),
        grid=(Bsz, nblk),
        in_specs=[act(R1, A_PROJ),
                  pl.BlockSpec((None, 8, A_PROJ), lambda b, i: (b, jnp.maximum(i * sub - 1, 0), 0)),
                  pl.BlockSpec((None, 1, A_PROJ), lambda b, i: (b, 0, 0)),
                  row(A_PROJ), row(W), full((A_DECAY_LORA, W)), row(W), full((A_ICL_LORA, W)),
                  full((A_GATE_LORA, W)), row(W), row(W), row(W), full((W, W))],
        out_specs=[mat(n1), mat(n1), act(R1, W), act(R1, W), act(R1, W), act(R1, W)],
        out_shape=[jax.ShapeDtypeStruct((Bsz, T // CHUNK, CHUNK, W), F32)] * 2
        + [jax.ShapeDtypeStruct((Bsz, T, W), F32)] * 4,
        compiler_params=_params("parallel", "parallel"),
    )(pa, pa, shift0[:, None, :], prm['mu'], prm['w0'], prm['w2'], prm['a0'], prm['a2'], prm['g2'],
      prm['k_k'], prm['k_a'], prm['r_k'], prm['seg'])

    n2 = min(4, T // CHUNK)
    R2 = n2 * CHUNK
    h0 = wkv0.transpose(0, 3, 1, 2).reshape(Bsz, A_HEAD_DIM, W)
    state = pl.BlockSpec((None, A_HEAD_DIM, W), lambda b, i: (b, 0, 0))
    y, hout = pl.pallas_call(
        functools.partial(_rwkv_scan_kernel, n2),
        grid=(Bsz, T // R2),
        in_specs=[mat(n2), mat(n2), act(R2, W), act(R2, W), act(R2, W), act(R2, W), state,
                  row(W), row(W), full((W, W))],
        out_specs=[act(R2, W), state],
        out_shape=[jax.ShapeDtypeStruct((Bsz, T, W), F32), jax.ShapeDtypeStruct((Bsz, A_HEAD_DIM, W), F32)],
        scratch_shapes=[pltpu.VMEM((A_HEAD_DIM, W), F32)],
        compiler_params=_params("parallel", "arbitrary"),
    )(mh, ch, qc, oc, g, bonus, h0, prm['ln_w'], prm['ln_b'], prm['seg'])
    wkv = hout.reshape(Bsz, A_HEAD_DIM, A_HEADS, A_HEAD_DIM).transpose(0, 2, 3, 1)
    return y, wkv


def _gla_kernel(n_chunks, pb_ref, s0_ref, up_ref, bias_ref, nw_ref, y_ref, sout_ref, s_ref):
    i = pl.program_id(1)
    L = CHUNK

    @pl.when(i == 0)
    def _():
        s_ref[...] = s0_ref[...]

    kw, vw = B_KWIDTH, B_VWIDTH
    row = lax.broadcasted_iota(jnp.int32, (L, L), 0)
    col = lax.broadcasted_iota(jnp.int32, (L, L), 1)
    incl = row >= col
    tri = incl.astype(BF16)
    scale = B_KEY_DIM ** -0.5
    for c in range(n_chunks):
        sl = slice(c * L, (c + 1) * L)
        q = pb_ref[sl, 0:kw] * scale
        k = pb_ref[sl, kw:2 * kw]
        v = pb_ref[sl, 2 * kw:2 * kw + vw]
        rg = pb_ref[sl, 2 * kw + vw:2 * kw + 2 * vw]
        xa = pb_ref[sl, 2 * kw + 2 * vw:]
        z = _dot3(xa, up_ref[...]) + bias_ref[...]
        log_a = -_softplus(-z) * (1.0 / B_GATE_NORM)
        bc = _dot_xl(tri, log_a)
        b_last = bc[L - 1:L, :]
        q_dec = q * jnp.exp(bc)
        k_inv = k * jnp.exp(-bc)
        k_state = k * jnp.exp(b_last - bc)
        e_last_t = jnp.broadcast_to(jnp.exp(b_last), (LANE, kw)).T
        for h in range(B_HEADS):
            ks = slice(h * B_KEY_DIM, (h + 1) * B_KEY_DIM)
            vs = slice(h * B_VAL_DIM, (h + 1) * B_VAL_DIM)
            s_prev = s_ref[h]
            v_h = v[:, vs]
            scores = jnp.where(incl, _dot1(q_dec[:, ks], k_inv[:, ks], NT), 0.0)
            o = _dot1(q_dec[:, ks], s_prev) + _dot1(scores, v_h)
            s_ref[h] = s_prev * e_last_t[ks, :] + _dot1(k_state[:, ks], v_h, TN)
            o = o * lax.rsqrt(jnp.mean(o * o, -1, keepdims=True) + LN_EPS) * nw_ref[...]
            rg_h = rg[:, vs]
            y_ref[sl, vs] = o * (rg_h * _sigmoid(rg_h))
    sout_ref[...] = s_ref[...]


def _gla_mix(pb, s0, prm):
    Bsz, T, _ = pb.shape
    n = min(4, T // CHUNK)
    Rr = n * CHUNK
    st = pl.BlockSpec((None, B_HEADS, B_KEY_DIM, B_VAL_DIM), lambda b, i: (b, 0, 0, 0))
    return pl.pallas_call(
        functools.partial(_gla_kernel, n),
        grid=(Bsz, T // Rr),
        in_specs=[pl.BlockSpec((None, Rr, B_PROJ_PAD), lambda b, i: (b, i, 0)), st,
                  pl.BlockSpec((B_LORA_PAD, B_KWIDTH), lambda b, i: (0, 0)),
                  pl.BlockSpec((1, B_KWIDTH), lambda b, i: (0, 0)),
                  pl.BlockSpec((1, B_VAL_DIM), lambda b, i: (0, 0))],
        out_specs=[pl.BlockSpec((None, Rr, B_VWIDTH), lambda b, i: (b, i, 0)), st],
        out_shape=[jax.ShapeDtypeStruct((Bsz, T, B_VWIDTH), F32),
                   jax.ShapeDtypeStruct((Bsz, B_HEADS, B_KEY_DIM, B_VAL_DIM), F32)],
        scratch_shapes=[pltpu.VMEM((B_HEADS, B_KEY_DIM, B_VAL_DIM), F32)],
        compiler_params=_params("parallel", "arbitrary"),
    )(pb, s0, prm['up'], prm['bias'], prm['nw'])


ATT_QB = 2 * CHUNK
ATT_KB = LANE
ATT_NKB = (C_PAST_ROWS + ATT_QB) // ATT_KB
ATT_SCALE = C_HEAD_DIM ** -0.5


def _att_prompt_kernel(q_ref, k_ref, v_ref, bias_ref, o_ref):
    qi = pl.program_id(2)
    ic = lax.broadcasted_iota(jnp.int32, (ATT_QB, ATT_KB), 0) // CHUNK
    jc = lax.broadcasted_iota(jnp.int32, (ATT_QB, ATT_KB), 1) // CHUNK
    back = C_PAST_ROWS // ATT_KB
    for hh in range(2):
        hs = slice(hh * C_HEAD_DIM, (hh + 1) * C_HEAD_DIM)
        q = (q_ref[:, hs] * ATT_SCALE).astype(BF16)
        s_blocks, v_blocks = [], []
        for kb in range(ATT_NKB):
            kbi = qi - back + kb
            start = pl.multiple_of(jnp.maximum(kbi, 0) * ATT_KB, ATT_KB)
            kblk = k_ref[pl.ds(start, ATT_KB), hs].astype(BF16)
            v_blocks.append(v_ref[pl.ds(start, ATT_KB), hs].astype(BF16))
            s = _dg(q, kblk, NT) + bias_ref[hh, :, kb * ATT_KB:(kb + 1) * ATT_KB]
            rel = 2 * kb + jc - ic
            valid = (rel >= 0) & (rel <= C_PAST_CHUNKS) & (kbi >= 0)
            s_blocks.append(jnp.where(valid, s, NEG_INF))
        m = s_blocks[0].max(-1, keepdims=True)
        for s in s_blocks[1:]:
            m = jnp.maximum(m, s.max(-1, keepdims=True))
        den = None
        acc = None
        for s, vblk in zip(s_blocks, v_blocks):
            p = jnp.exp(s - m)
            ps = p.sum(-1, keepdims=True)
            pv = _dg(p.astype(BF16), vblk, NN)
            den = ps if den is None else den + ps
            acc = pv if acc is None else acc + pv
        o_ref[:, hs] = acc / den


def _attention_prompt(qkv, bias):
    Bsz, T, _ = qkv.shape
    nhp = C_HEADS // 2
    return pl.pallas_call(
        _att_prompt_kernel,
        grid=(Bsz, nhp, T // ATT_QB),
        in_specs=[pl.BlockSpec((None, ATT_QB, LANE), lambda b, hp, qi: (b, qi, hp)),
                  pl.BlockSpec((None, T, LANE), lambda b, hp, qi: (b, 0, nhp + hp)),
                  pl.BlockSpec((None, T, LANE), lambda b, hp, qi: (b, 0, 2 * nhp + hp)),
                  pl.BlockSpec((2, ATT_QB, ATT_NKB * ATT_KB), lambda b, hp, qi: (hp, 0, 0))],
        out_specs=pl.BlockSpec((None, ATT_QB, LANE), lambda b, hp, qi: (b, qi, hp)),
        out_shape=jax.ShapeDtypeStruct((Bsz, T, D_MODEL), F32),
        compiler_params=_params("parallel", "parallel", "arbitrary"),
    )(qkv, qkv, qkv, bias)


def _att_sample_kernel(q_ref, kn_ref, vn_ref, kc_ref, vc_ref, bias_ref, o_ref):
    R = kc_ref.shape[1]
    for hh in range(2):
        hs = slice(hh * C_HEAD_DIM, (hh + 1) * C_HEAD_DIM)
        q = (q_ref[:, hs] * ATT_SCALE).astype(BF16)
        s_c = _dg(q, kc_ref[hh].astype(BF16), NT) + bias_ref[hh, :, 0:R]
        s_n = _dg(q, kn_ref[:, hs].astype(BF16), NT) + bias_ref[hh, :, R:]
        m = jnp.maximum(s_c.max(-1, keepdims=True), s_n.max(-1, keepdims=True))
        p_c = jnp.exp(s_c - m)
        p_n = jnp.exp(s_n - m)
        den = p_c.sum(-1, keepdims=True) + p_n.sum(-1, keepdims=True)
        acc = _dg(p_c.astype(BF16), vc_ref[hh].astype(BF16), NN) + _dg(p_n.astype(BF16), vn_ref[:, hs].astype(BF16), NN)
        o_ref[:, hs] = acc / den


def _attention_sample(qkv, cache_k, cache_v, bias):
    Bsz, T, _ = qkv.shape
    R = cache_k.shape[2]
    nhp = C_HEADS // 2
    cache = pl.BlockSpec((None, 2, R, C_HEAD_DIM), lambda b, hp: (b, hp, 0, 0))
    return pl.pallas_call(
        _att_sample_kernel,
        grid=(Bsz, nhp),
        in_specs=[pl.BlockSpec((None, T, LANE), lambda b, hp: (b, 0, hp)),
                  pl.BlockSpec((None, T, LANE), lambda b, hp: (b, 0, nhp + hp)),
                  pl.BlockSpec((None, T, LANE), lambda b, hp: (b, 0, 2 * nhp + hp)),
                  cache, cache,
                  pl.BlockSpec((2, T, R + T), lambda b, hp: (hp, 0, 0))],
        out_specs=pl.BlockSpec((None, T, LANE), lambda b, hp: (b, 0, hp)),
        out_shape=jax.ShapeDtypeStruct((Bsz, T, D_MODEL), F32),
        compiler_params=_params("parallel", "parallel"),
    )(qkv, qkv, qkv, cache_k, cache_v, bias)


def _rel_bias_tile(table, n_q, n_k, q_offset):
    rel = q_offset + jnp.arange(n_q)[:, None] - jnp.arange(n_k)[None, :]
    return table[:, jnp.clip(rel, -C_REL_CLIP, C_REL_CLIP) + C_REL_CLIP]


TM = 512


def _trunk(x, wkv0, shift0, gla0, cache_k, cache_v, P):
    Bsz, T, _ = x.shape
    M = Bsz * T
    xf = x.reshape(M, D_MODEL)
    wkv_o, shift_o, gla_o, k_o, v_o = [], [], [], [], []
    for layer in range(DEPTH):
        if layer % 2 == 0:
            e = layer // 2
            pa = _matmul(xf, P['w_in_a'][e], TM, A_PROJ).reshape(Bsz, T, A_PROJ)
            pb = _matmul(xf, P['w_in_b'][e], TM, B_PROJ_PAD).reshape(Bsz, T, B_PROJ_PAD)
            ya, wkv = _rwkv_mix(pa, shift0[e], wkv0[e], {k: v[e] for k, v in P['a'].items()})
            yb, gs = _gla_mix(pb, gla0[e], {k: v[e] for k, v in P['b'].items()})
            wkv_o.append(wkv)
            shift_o.append(pa[:, -1])
            gla_o.append(gs)
            pairs = [(ya.reshape(M, A_WIDTH), P['w_out_a'][e]), (yb.reshape(M, B_VWIDTH), P['w_out_b'][e])]
        else:
            o = layer // 2
            qkv = _matmul(xf, P['c_w_qkv'][o], TM, D_MODEL).reshape(Bsz, T, 3 * D_MODEL)
            if cache_k is None:
                att = _attention_prompt(qkv, P['c_bias_prompt'][o])
                keep = min(C_PAST_ROWS, T)
                rows = qkv[:, T - keep:]
            else:
                att = _attention_sample(qkv, cache_k[o], cache_v[o], P['c_bias_sample'][o])
                keep = T
                rows = qkv
            rows = rows.reshape(Bsz, keep, 3, C_HEADS, C_HEAD_DIM)
            k_o.append(rows[:, :, 1].transpose(0, 2, 1, 3))
            v_o.append(rows[:, :, 2].transpose(0, 2, 1, 3))
            pairs = [(att.reshape(M, D_MODEL), P['c_w_o'][o])]
        xf = _matmul_ln(pairs, xf, P['ln1_w'][layer], P['ln1_b'][layer], TM)
        xf = _ffn_ln(xf, P['ffn_w_in'][layer], P['ffn_w_out'][layer], P['ln2_w'][layer], P['ln2_b'][layer], TM)
    return (xf.reshape(Bsz, T, D_MODEL), jnp.stack(wkv_o), jnp.stack(shift_o), jnp.stack(gla_o),
            jnp.stack(k_o), jnp.stack(v_o))


def _prepare(w_in_mix, a_mu, a_w0, a_w2, a_a0, a_a2, a_g2, a_k_k, a_k_a, a_r_k, a_ln_w, a_ln_b,
             b_alpha_up, b_alpha_bias, b_norm_w, w_out_mix, c_w_qkv, c_rel_bias, c_w_o,
             ln1_w, ln1_b, ln2_w, ln2_b, ffn_w_in, ffn_w_out, sample_len, cache_rows):
    n_even = w_in_mix.shape[0]
    kw, vw = B_KWIDTH, B_VWIDTH
    wb = w_in_mix[:, :, A_PROJ:]
    main = jnp.concatenate([wb[:, :, :2 * kw + vw], wb[:, :, 2 * kw + vw + B_GATE_LORA:]], axis=-1)
    lora = jnp.pad(wb[:, :, 2 * kw + vw:2 * kw + vw + B_GATE_LORA], ((0, 0), (0, 0), (0, B_LORA_PAD - B_GATE_LORA)))
    seg_id = jnp.arange(A_WIDTH) // A_HEAD_DIM
    seg = (seg_id[:, None] == seg_id[None, :]).astype(BF16)
    row3 = lambda t: t.reshape(t.shape[0], 1, -1)
    P = {
        'w_in_a': w_in_mix[:, :, :A_PROJ].astype(BF16),
        'w_in_b': jnp.concatenate([main, lora], axis=-1).astype(BF16),
        'a': dict(mu=row3(a_mu), w0=row3(a_w0), w2=a_w2, a0=row3(a_a0), a2=a_a2, g2=a_g2, k_k=row3(a_k_k),
                  k_a=row3(a_k_a), r_k=row3(a_r_k), ln_w=row3(a_ln_w), ln_b=row3(a_ln_b),
                  seg=jnp.broadcast_to(seg, (n_even,) + seg.shape)),
        'b': dict(up=jnp.pad(b_alpha_up, ((0, 0), (0, B_LORA_PAD - B_GATE_LORA), (0, 0))),
                  bias=row3(b_alpha_bias), nw=row3(b_norm_w)),
        'w_out_a': w_out_mix[:, :A_WIDTH].astype(BF16),
        'w_out_b': w_out_mix[:, A_WIDTH:].astype(BF16),
        'c_w_qkv': c_w_qkv.astype(BF16),
        'c_w_o': c_w_o.astype(BF16),
        'c_bias_prompt': jnp.stack([_rel_bias_tile(t, ATT_QB, ATT_NKB * ATT_KB, C_PAST_ROWS) for t in c_rel_bias]),
        'c_bias_sample': jnp.stack([_rel_bias_tile(t, sample_len, cache_rows + sample_len, cache_rows)
                                    for t in c_rel_bias]),
        'ln1_w': row3(ln1_w), 'ln1_b': row3(ln1_b), 'ln2_w': row3(ln2_w), 'ln2_b': row3(ln2_b),
        'ffn_w_in': ffn_w_in.astype(BF16),
        'ffn_w_out': ffn_w_out.astype(BF16),
    }
    return P


def kernel(x_prompt, x_sample, state_a_wkv, state_a_shift, state_b_gla, cache_c_k, cache_c_v, w_in_mix, a_mu, a_w0, a_w2, a_a0, a_a2, a_g2, a_k_k, a_k_a, a_r_k, a_ln_w, a_ln_b, b_alpha_up, b_alpha_bias, b_norm_w, w_out_mix, c_w_qkv, c_rel_bias, c_w_o, ln1_w, ln1_b, ln2_w, ln2_b, ffn_w_in, ffn_w_out):
    P = _prepare(w_in_mix, a_mu, a_w0, a_w2, a_a0, a_a2, a_g2, a_k_k, a_k_a, a_r_k, a_ln_w, a_ln_b,
                 b_alpha_up, b_alpha_bias, b_norm_w, w_out_mix, c_w_qkv, c_rel_bias, c_w_o,
                 ln1_w, ln1_b, ln2_w, ln2_b, ffn_w_in, ffn_w_out, x_sample.shape[1], cache_c_k.shape[3])
    bp = x_prompt.shape[0]
    dt = x_prompt.dtype
    n_even = state_a_wkv.shape[0]
    wkv_zero = jnp.zeros((n_even, bp, A_HEADS, A_HEAD_DIM, A_HEAD_DIM), dt)
    shift_zero = jnp.zeros((n_even, bp, A_PROJ), dt)
    gla_zero = jnp.zeros((n_even, bp, B_HEADS, B_KEY_DIM, B_VAL_DIM), dt)
    y_prompt, p_wkv, p_shift, p_gla, p_k, p_v = _trunk(x_prompt, wkv_zero, shift_zero, gla_zero, None, None, P)
    y_sample, s_wkv, s_shift, s_gla, s_k, s_v = _trunk(
        x_sample, state_a_wkv, state_a_shift, state_b_gla, cache_c_k, cache_c_v, P)
    return (y_prompt, y_sample, p_wkv, p_shift, p_gla, p_k, p_v, s_wkv, s_shift, s_gla, s_k, s_v)
```

```python
import functools

import jax
import jax.numpy as jnp
import numpy as np
from jax import lax
from jax.experimental import pallas as pl
from jax.experimental.pallas import tpu as pltpu

F32 = jnp.float32
BF16 = jnp.bfloat16

D_MODEL = 1024
DEPTH = 4
CHUNK = 64
A_WIDTH = 512
A_HEAD_DIM = 64
A_HEADS = 8
A_DECAY_LORA = 64
A_ICL_LORA = 64
A_GATE_LORA = 128
A_PROJ = 3 * A_WIDTH + A_DECAY_LORA + A_ICL_LORA + A_GATE_LORA
A_NORM_EPS = 64e-5
B_VWIDTH = 512
B_KWIDTH = 256
B_HEADS = 4
B_KEY_DIM = 64
B_VAL_DIM = 128
B_GATE_LORA = 16
B_GATE_NORM = 16.0
B_LORA_PAD = 128
B_PROJ_PAD = 2 * B_KWIDTH + 2 * B_VWIDTH + B_LORA_PAD
C_HEAD_DIM = 64
C_HEADS = 16
C_PAST_CHUNKS = 8
C_PAST_ROWS = C_PAST_CHUNKS * CHUNK
C_REL_CLIP = 128
FFN_HIDDEN = 2816
LN_EPS = 1e-5
DEEPNORM_ALPHA = (2.0 * DEPTH) ** 0.25
NEG_INF = -1e30

LANE = 128
PAIR = LANE
HALF = PAIR // 2
A_PAIRS = A_HEADS // 2
B_PAIRS = B_HEADS // 2
VMEM_LIMIT = 56 * 1024 * 1024

NN = (((1,), (0,)), ((), ()))
NT = (((1,), (1,)), ((), ()))
TN = (((0,), (0,)), ((), ()))


def _params(*sem):
    return pltpu.CompilerParams(dimension_semantics=sem, vmem_limit_bytes=VMEM_LIMIT)


def _dg(a, b, dn):
    return lax.dot_general(a, b, dn, preferred_element_type=F32)


def _dot1(a, b, dn=NN):
    return _dg(a.astype(BF16), b.astype(BF16), dn)


def _split(a):
    hi = a.astype(BF16)
    lo = (a - hi.astype(F32)).astype(BF16)
    return hi, lo


def _dot3s(a, b, dn=NN):
    return _dg(a[0], b[0], dn) + _dg(a[0], b[1], dn) + _dg(a[1], b[0], dn)


def _dot3(a, b, dn=NN):
    return _dot3s(_split(a), _split(b), dn)


def _dot_xl(a_exact, b, dn=NN):
    bh, bl = _split(b)
    return _dg(a_exact, bh, dn) + _dg(a_exact, bl, dn)


def _dot_xr(a, b_exact, dn=NN):
    ah, al = _split(a)
    return _dg(ah, b_exact, dn) + _dg(al, b_exact, dn)


def _ln(z, w, b):
    mu = jnp.mean(z, -1, keepdims=True)
    d = z - mu
    var = jnp.mean(d * d, -1, keepdims=True)
    return d * lax.rsqrt(var + LN_EPS) * w + b


def _sigmoid(x):
    return 1.0 / (1.0 + jnp.exp(-x))


def _softplus(x):
    return jnp.maximum(x, 0.0) + jnp.log(1.0 + jnp.exp(-jnp.abs(x)))


def _stack_masked(x, m0):
    return jnp.concatenate([jnp.where(m0, x, 0.0), jnp.where(m0, 0.0, x)], axis=0)


def _fold(x):
    n = x.shape[0] // 2
    return x[:n] + x[n:]


def _mm_kernel(x_ref, w_ref, o_ref):
    o_ref[...] = jnp.dot(x_ref[...].astype(BF16), w_ref[...], preferred_element_type=F32)


def _matmul(x, w, tm, tn):
    M, K = x.shape
    N = w.shape[1]
    tm = min(tm, M)
    return pl.pallas_call(
        _mm_kernel,
        grid=(M // tm, N // tn),
        in_specs=[pl.BlockSpec((tm, K), lambda i, j: (i, 0)),
                  pl.BlockSpec((K, tn), lambda i, j: (0, j))],
        out_specs=pl.BlockSpec((tm, tn), lambda i, j: (i, j)),
        out_shape=jax.ShapeDtypeStruct((M, N), F32),
        compiler_params=_params("parallel", "arbitrary"),
    )(x, w)


def _mm_ln_kernel(n_pairs, *refs):
    res_ref, lw_ref, lb_ref, o_ref = refs[2 * n_pairs:]
    acc = None
    for p in range(n_pairs):
        part = jnp.dot(refs[2 * p][...].astype(BF16), refs[2 * p + 1][...], preferred_element_type=F32)
        acc = part if acc is None else acc + part
    o_ref[...] = _ln(DEEPNORM_ALPHA * res_ref[...] + acc, lw_ref[...], lb_ref[...])


def _matmul_ln(pairs, res, lw, lb, tm):
    M = res.shape[0]
    tm = min(tm, M)
    in_specs, args = [], []
    for a, w in pairs:
        K = a.shape[1]
        in_specs += [pl.BlockSpec((tm, K), lambda i: (i, 0)), pl.BlockSpec((K, D_MODEL), lambda i: (0, 0))]
        args += [a, w]
    in_specs += [pl.BlockSpec((tm, D_MODEL), lambda i: (i, 0)),
                 pl.BlockSpec((1, D_MODEL), lambda i: (0, 0)),
                 pl.BlockSpec((1, D_MODEL), lambda i: (0, 0))]
    return pl.pallas_call(
        functools.partial(_mm_ln_kernel, len(pairs)),
        grid=(M // tm,),
        in_specs=in_specs,
        out_specs=pl.BlockSpec((tm, D_MODEL), lambda i: (i, 0)),
        out_shape=jax.ShapeDtypeStruct((M, D_MODEL), F32),
        compiler_params=_params("parallel"),
    )(*args, res, lw, lb)


FFN_SPLIT = 2
FFN_TH = FFN_HIDDEN // FFN_SPLIT


def _ffn_kernel(x_ref, wg_ref, wu_ref, wo_ref, lw_ref, lb_ref, o_ref, acc_ref):
    j = pl.program_id(1)
    x = x_ref[...]
    xb = x.astype(BF16)
    g = jnp.dot(xb, wg_ref[...], preferred_element_type=F32)
    u = jnp.dot(xb, wu_ref[...], preferred_element_type=F32)
    act = g * _sigmoid(g) * u
    part = jnp.dot(act.astype(BF16), wo_ref[...], preferred_element_type=F32)

    @pl.when(j == 0)
    def _():
        acc_ref[...] = part

    @pl.when(j == FFN_SPLIT - 1)
    def _():
        o_ref[...] = _ln(DEEPNORM_ALPHA * x + acc_ref[...] + part, lw_ref[...], lb_ref[...])


def _ffn_ln(x, w_in, w_out, lw, lb, tm):
    M = x.shape[0]
    tm = min(tm, M)
    return pl.pallas_call(
        _ffn_kernel,
        grid=(M // tm, FFN_SPLIT),
        in_specs=[pl.BlockSpec((tm, D_MODEL), lambda i, j: (i, 0)),
                  pl.BlockSpec((D_MODEL, FFN_TH), lambda i, j: (0, j)),
                  pl.BlockSpec((D_MODEL, FFN_TH), lambda i, j: (0, j + FFN_SPLIT)),
                  pl.BlockSpec((FFN_TH, D_MODEL), lambda i, j: (j, 0)),
                  pl.BlockSpec((1, D_MODEL), lambda i, j: (0, 0)),
                  pl.BlockSpec((1, D_MODEL), lambda i, j: (0, 0))],
        out_specs=pl.BlockSpec((tm, D_MODEL), lambda i, j: (i, 0)),
        out_shape=jax.ShapeDtypeStruct((M, D_MODEL), F32),
        scratch_shapes=[pltpu.VMEM((tm, D_MODEL), F32)],
        compiler_params=_params("parallel", "arbitrary"),
    )(x, w_in, w_in, w_out, lw, lb)


def _inv_unit_lower(ns, eye, nilpotency):
    xs = [eye - n for n in ns]
    ps = [_dot1(n, n) for n in ns]
    k = 2
    while k < nilpotency:
        xs = [x + _dot1(x, p) for x, p in zip(xs, ps)]
        k *= 2
        if k < nilpotency:
            ps = [_dot1(p, p) for p in ps]
    return xs


def _rwkv_prep_kernel(n_chunks, pa_ref, prev_ref, shift_ref, mu_ref, w0_ref, w2_ref, a0_ref, a2_ref, g2_ref,
                      kk_ref, ka_ref, rk_ref, seg_ref,
                      mh_ref, ch_ref, qc_ref, oc_ref, g_ref, bonus_ref):
    i = pl.program_id(1)
    L = CHUNK
    pa = pa_ref[...]
    first = jnp.where(i == 0, shift_ref[...], prev_ref[7:8, :])
    rows = lax.broadcasted_iota(jnp.int32, pa.shape, 0)
    prev = jnp.where(rows == 0, first, pltpu.roll(pa, shift=1, axis=0))
    xs = pa + (prev - pa) * mu_ref[...]
    W = A_WIDTH
    r = xs[:, 0:W]
    k0 = xs[:, W:2 * W]
    v = xs[:, 2 * W:3 * W]
    xw = xs[:, 3 * W:3 * W + A_DECAY_LORA]
    xa = xs[:, 3 * W + A_DECAY_LORA:3 * W + A_DECAY_LORA + A_ICL_LORA]
    xg = xs[:, 3 * W + A_DECAY_LORA + A_ICL_LORA:]
    w = -_softplus(-(w0_ref[...] + _dot3(jnp.tanh(xw), w2_ref[...]))) - 0.5
    logw = -jnp.exp(w)
    a = _sigmoid(a0_ref[...] + _dot3(xa, a2_ref[...]))
    g_ref[...] = _dot3(_sigmoid(xg), g2_ref[...])
    seg = seg_ref[...]
    kk = k0 * kk_ref[...]
    kk = kk * lax.rsqrt(_dot_xr(kk * kk, seg) + 1e-12)
    k = k0 * (1.0 + (a - 1.0) * ka_ref[...])
    bonus_ref[...] = _dot_xr(r * k * rk_ref[...], seg) * v
    b = kk * a

    tri = (lax.broadcasted_iota(jnp.int32, (L, L), 0) >= lax.broadcasted_iota(jnp.int32, (L, L), 1)).astype(BF16)
    row = lax.broadcasted_iota(jnp.int32, (PAIR, PAIR), 0)
    col = lax.broadcasted_iota(jnp.int32, (PAIR, PAIR), 1)
    eye = (row == col).astype(F32)
    strict = row > col
    incl = row >= col
    m0 = lax.broadcasted_iota(jnp.int32, (L, PAIR), 1) < HALF
    pairs = range(A_PAIRS)
    lanes = [slice(j * PAIR, (j + 1) * PAIR) for j in pairs]
    for c in range(n_chunks):
        sl = slice(c * L, (c + 1) * L)
        lw = logw[sl]
        cum = _dot_xl(tri, lw)
        c_last = cum[L - 1:L, :]
        e_neg = jnp.exp(-cum)
        e_rem = jnp.exp(c_last - cum)
        kkg = kk[sl] * jnp.exp(cum - lw)
        rg = r[sl] * jnp.exp(cum)
        kinv = k[sl] * e_neg
        binv = b[sl] * e_neg
        kd = k[sl] * e_rem
        bd = b[sl] * e_rem
        e_last = jnp.exp(c_last)
        vc = v[sl]
        sm = lambda x, p: _stack_masked(x[:, p], m0)
        kkg_s = [_split(sm(kkg, p)) for p in lanes]
        v_s = [_split(sm(vc, p)) for p in lanes]
        bd_s = [_split(sm(bd, p)) for p in lanes]
        kd_s = [_split(sm(kd, p)) for p in lanes]
        rg_h = [sm(rg, p).astype(BF16) for p in lanes]
        kinv_h = [sm(kinv, p).astype(BF16) for p in lanes]
        binv_h = [sm(binv, p).astype(BF16) for p in lanes]
        a_kb = [jnp.where(strict, _dg(kkg_s[j][0], binv_h[j], NT), 0.0) for j in pairs]
        a_kk = [jnp.where(strict, _dg(kkg_s[j][0], kinv_h[j], NT), 0.0) for j in pairs]
        a_rk = [jnp.where(incl, _dg(rg_h[j], kinv_h[j], NT), 0.0) for j in pairs]
        a_rb = [jnp.where(incl, _dg(rg_h[j], binv_h[j], NT), 0.0) for j in pairs]
        t = _inv_unit_lower(a_kb, eye, L)
        av = [_dot3s(_split(a_kk[j]), v_s[j]) for j in pairs]
        av_s = [_split(x) for x in av]
        cat = lambda x, y: (jnp.concatenate([x[0], y[0]], axis=1), jnp.concatenate([x[1], y[1]], axis=1))
        wu_s = [_split(_dot3s(_split(t[j]), cat(kkg_s[j], av_s[j]))) for j in pairs]
        mc = [_dot3s(bd_s[j], wu_s[j], TN) for j in pairs]
        kv = [_dot3s(kd_s[j], v_s[j], TN) for j in pairs]
        qo = [_dot3s(_split(a_rb[j]), wu_s[j]) for j in pairs]
        rv = [_dot3s(_split(a_rk[j]), v_s[j]) for j in pairs]
        for j in pairs:
            mh_ref[c, j] = eye * e_last[:, lanes[j]] - mc[j][:, :PAIR]
            ch_ref[c, j] = kv[j] - mc[j][:, PAIR:]
            qc_ref[sl, lanes[j]] = rg[:, lanes[j]] - _fold(qo[j][:, :PAIR])
            oc_ref[sl, lanes[j]] = _fold(rv[j] - qo[j][:, PAIR:])


def _rwkv_scan_kernel(n_chunks, mh_ref, ch_ref, qc_ref, oc_ref, g_ref, bonus_ref, h0_ref, lnw_ref, lnb_ref, seg_ref,
                      y_ref, hout_ref, h_ref):
    i = pl.program_id(1)
    L = CHUNK

    @pl.when(i == 0)
    def _():
        h_ref[...] = h0_ref[...]

    seg = seg_ref[...]
    inv_n = 1.0 / A_HEAD_DIM
    pairs = range(A_PAIRS)
    lanes = [slice(j * PAIR, (j + 1) * PAIR) for j in pairs]
    hb = [h_ref[j] for j in pairs]
    for c in range(n_chunks):
        sl = slice(c * L, (c + 1) * L)
        hb_s = [_split(x) for x in hb]
        o = [_dot3s(_split(qc_ref[sl, lanes[j]]), hb_s[j]) + oc_ref[sl, lanes[j]] for j in pairs]
        hb = [_dot3s(_split(mh_ref[c, j]), hb_s[j]) + ch_ref[c, j] for j in pairs]
        o = jnp.concatenate(o, axis=1)
        mean = _dot_xr(o, seg) * inv_n
        d = o - mean
        var = _dot_xr(d * d, seg) * inv_n
        on = d * lax.rsqrt(var + A_NORM_EPS) * lnw_ref[...] + lnb_ref[...]
        y_ref[sl, :] = (on + bonus_ref[sl, :]) * g_ref[sl, :]
    for j in pairs:
        h_ref[j] = hb[j]
    hout_ref[...] = h_ref[...]


def _rwkv_mix(pa, shift0, wkv0, prm):
    Bsz, T, _ = pa.shape
    n1 = 1
    R1 = n1 * CHUNK
    W = A_WIDTH
    nblk = T // R1
    row = lambda n: pl.BlockSpec((1, n), lambda b, i: (0, 0))
    full = lambda s: pl.BlockSpec(s, lambda b, i: (0,) * len(s))
    act = lambda r, n: pl.BlockSpec((None, r, n), lambda b, i: (b, i, 0))
    mat = lambda n: pl.BlockSpec((None, n, A_PAIRS, PAIR, PAIR), lambda b, i: (b, i, 0, 0, 0))
    sub = R1 // 8
    mh, ch, qc, oc, g, bonus = pl.pallas_call(
        functools.partial(_rwkv_prep_kernel, n1),
        grid=(Bsz, nblk),
        in_specs=[act(R1, A_PROJ),
                  pl.BlockSpec((None, 8, A_PROJ), lambda b, i: (b, jnp.maximum(i * sub - 1, 0), 0)),
                  pl.BlockSpec((None, 1, A_PROJ), lambda b, i: (b, 0, 0)),
                  row(A_PROJ), row(W), full((A_DECAY_LORA, W)), row(W), full((A_ICL_LORA, W)),
                  full((A_GATE_LORA, W)), row(W), row(W), row(W), full((W, W))],
        out_specs=[mat(n1), mat(n1), act(R1, W), act(R1, W), act(R1, W), act(R1, W)],
        out_shape=[jax.ShapeDtypeStruct((Bsz, T // CHUNK, A_PAIRS, PAIR, PAIR), F32)] * 2
        + [jax.ShapeDtypeStruct((Bsz, T, W), F32)] * 4,
        compiler_params=_params("parallel", "parallel"),
    )(pa, pa, shift0[:, None, :], prm['mu'], prm['w0'], prm['w2'], prm['a0'], prm['a2'], prm['g2'],
      prm['k_k'], prm['k_a'], prm['r_k'], prm['seg'])

    n2 = min(4, T // CHUNK)
    R2 = n2 * CHUNK
    hh = wkv0.transpose(0, 1, 3, 2).reshape(Bsz, A_PAIRS, 2, A_HEAD_DIM, A_HEAD_DIM)
    zero = jnp.zeros_like(hh[:, :, 0])
    h0 = jnp.concatenate([jnp.concatenate([hh[:, :, 0], zero], axis=-1),
                          jnp.concatenate([zero, hh[:, :, 1]], axis=-1)], axis=-2)
    state = pl.BlockSpec((None, A_PAIRS, PAIR, PAIR), lambda b, i: (b, 0, 0, 0))
    y, hout = pl.pallas_call(
        functools.partial(_rwkv_scan_kernel, n2),
        grid=(Bsz, T // R2),
        in_specs=[mat(n2), mat(n2), act(R2, W), act(R2, W), act(R2, W), act(R2, W), state,
                  row(W), row(W), full((W, W))],
        out_specs=[act(R2, W), state],
        out_shape=[jax.ShapeDtypeStruct((Bsz, T, W), F32), jax.ShapeDtypeStruct((Bsz, A_PAIRS, PAIR, PAIR), F32)],
        scratch_shapes=[pltpu.VMEM((A_PAIRS, PAIR, PAIR), F32)],
        compiler_params=_params("parallel", "arbitrary"),
    )(mh, ch, qc, oc, g, bonus, h0, prm['ln_w'], prm['ln_b'], prm['seg'])
    hd = jnp.stack([hout[:, :, :HALF, :HALF], hout[:, :, HALF:, HALF:]], axis=2)
    wkv = hd.reshape(Bsz, A_HEADS, A_HEAD_DIM, A_HEAD_DIM).transpose(0, 1, 3, 2)
    return y, wkv


def _gla_kernel(n_chunks, pb_ref, s0_ref, up_ref, bias_ref, nw_ref, y_ref, sout_ref, s_ref):
    i = pl.program_id(1)
    L = CHUNK

    @pl.when(i == 0)
    def _():
        s_ref[...] = s0_ref[...]

    kw, vw = B_KWIDTH, B_VWIDTH
    incl = lax.broadcasted_iota(jnp.int32, (L, L), 0) >= lax.broadcasted_iota(jnp.int32, (L, L), 1)
    tri = incl.astype(BF16)
    m0 = lax.broadcasted_iota(jnp.int32, (L, PAIR), 1) < HALF
    top = lax.broadcasted_iota(jnp.int32, (PAIR, B_VAL_DIM), 0) < HALF
    scale = B_KEY_DIM ** -0.5
    pairs = range(B_PAIRS)
    heads = range(B_HEADS)
    st = [s_ref[j] for j in pairs]
    for c in range(n_chunks):
        sl = slice(c * L, (c + 1) * L)
        q = pb_ref[sl, 0:kw] * scale
        k = pb_ref[sl, kw:2 * kw]
        v = pb_ref[sl, 2 * kw:2 * kw + vw]
        rg = pb_ref[sl, 2 * kw + vw:2 * kw + 2 * vw]
        xa = pb_ref[sl, 2 * kw + 2 * vw:]
        z = _dot3(xa, up_ref[...]) + bias_ref[...]
        log_a = -_softplus(-z) * (1.0 / B_GATE_NORM)
        bc = _dot_xl(tri, log_a)
        b_last = bc[L - 1:L, :]
        q_dec = q * jnp.exp(bc)
        k_inv = (k * jnp.exp(-bc)).astype(BF16)
        k_state = (k * jnp.exp(b_last - bc)).astype(BF16)
        e_last_t = jnp.broadcast_to(jnp.exp(b_last), (LANE, kw)).T
        vb = [v[:, h * B_VAL_DIM:(h + 1) * B_VAL_DIM].astype(BF16) for h in heads]
        pl_ = lambda h: slice((h // 2) * PAIR, (h // 2 + 1) * PAIR)
        qm = [jnp.where(m0 if h % 2 == 0 else ~m0, q_dec[:, pl_(h)], 0.0).astype(BF16) for h in heads]
        scores = [jnp.where(incl, _dg(qm[h], k_inv[:, pl_(h)], NT), 0.0).astype(BF16) for h in heads]
        st_b = [x.astype(BF16) for x in st]
        o = [_dg(qm[h], st_b[h // 2], NN) + _dg(scores[h], vb[h], NN) for h in heads]
        upd = [_dg(k_state[:, pl_(h)], vb[h], TN) for h in heads]
        st = [st[j] * e_last_t[j * PAIR:(j + 1) * PAIR, :] + jnp.where(top, upd[2 * j], upd[2 * j + 1]) for j in pairs]
        for h in heads:
            vs = slice(h * B_VAL_DIM, (h + 1) * B_VAL_DIM)
            oh = o[h] * lax.rsqrt(jnp.mean(o[h] * o[h], -1, keepdims=True) + LN_EPS) * nw_ref[...]
            rg_h = rg[:, vs]
            y_ref[sl, vs] = oh * (rg_h * _sigmoid(rg_h))
    for j in pairs:
        s_ref[j] = st[j]
    sout_ref[...] = s_ref[...]


def _gla_mix(pb, s0, prm):
    Bsz, T, _ = pb.shape
    n = min(4, T // CHUNK)
    Rr = n * CHUNK
    st = pl.BlockSpec((None, B_PAIRS, PAIR, B_VAL_DIM), lambda b, i: (b, 0, 0, 0))
    y, s = pl.pallas_call(
        functools.partial(_gla_kernel, n),
        grid=(Bsz, T // Rr),
        in_specs=[pl.BlockSpec((None, Rr, B_PROJ_PAD), lambda b, i: (b, i, 0)), st,
                  pl.BlockSpec((B_LORA_PAD, B_KWIDTH), lambda b, i: (0, 0)),
                  pl.BlockSpec((1, B_KWIDTH), lambda b, i: (0, 0)),
                  pl.BlockSpec((1, B_VAL_DIM), lambda b, i: (0, 0))],
        out_specs=[pl.BlockSpec((None, Rr, B_VWIDTH), lambda b, i: (b, i, 0)), st],
        out_shape=[jax.ShapeDtypeStruct((Bsz, T, B_VWIDTH), F32),
                   jax.ShapeDtypeStruct((Bsz, B_PAIRS, PAIR, B_VAL_DIM), F32)],
        scratch_shapes=[pltpu.VMEM((B_PAIRS, PAIR, B_VAL_DIM), F32)],
        compiler_params=_params("parallel", "arbitrary"),
    )(pb, s0.reshape(Bsz, B_PAIRS, PAIR, B_VAL_DIM), prm['up'], prm['bias'], prm['nw'])
    return y, s.reshape(Bsz, B_HEADS, B_KEY_DIM, B_VAL_DIM)


ATT_QB = 2 * CHUNK
ATT_KB = LANE
ATT_NKB = (C_PAST_ROWS + ATT_QB) // ATT_KB
ATT_SCALE = C_HEAD_DIM ** -0.5


def _att_prompt_kernel(n_qb, q_ref, k_ref, v_ref, bias_ref, o_ref):
    g = pl.program_id(2)
    ic = (lax.broadcasted_iota(jnp.int32, (2 * ATT_QB, ATT_KB), 0) % ATT_QB) // CHUNK
    jc = lax.broadcasted_iota(jnp.int32, (2 * ATT_QB, ATT_KB), 1) // CHUNK
    m0 = lax.broadcasted_iota(jnp.int32, (ATT_QB, PAIR), 1) < HALF
    back = C_PAST_ROWS // ATT_KB
    band = []
    for kb in range(ATT_NKB):
        rel = 2 * kb + jc - ic
        band.append((rel >= 0) & (rel <= C_PAST_CHUNKS))

    def body(t, carry):
        qi = g * n_qb + t
        r0 = pl.multiple_of(t * ATT_QB, ATT_QB)
        q = _stack_masked(q_ref[pl.ds(r0, ATT_QB), :] * ATT_SCALE, m0).astype(BF16)
        s_blocks, v_blocks = [], []
        for kb in range(ATT_NKB):
            kbi = qi - back + kb
            start = pl.multiple_of(jnp.maximum(kbi, 0) * ATT_KB, ATT_KB)
            kblk = k_ref[pl.ds(start, ATT_KB), :].astype(BF16)
            v_blocks.append(v_ref[pl.ds(start, ATT_KB), :].astype(BF16))
            s = _dg(q, kblk, NT) + bias_ref[:, kb * ATT_KB:(kb + 1) * ATT_KB]
            s_blocks.append(jnp.where(band[kb] & (kbi >= 0), s, NEG_INF))
        m = s_blocks[0]
        for s in s_blocks[1:]:
            m = jnp.maximum(m, s)
        m = m.max(-1, keepdims=True)
        psum = None
        acc = None
        for s, vblk in zip(s_blocks, v_blocks):
            p = jnp.exp(s - m)
            pv = _dg(p.astype(BF16), vblk, NN)
            psum = p if psum is None else psum + p
            acc = pv if acc is None else acc + pv
        o = acc / psum.sum(-1, keepdims=True)
        o_ref[pl.ds(r0, ATT_QB), :] = jnp.where(m0, o[:ATT_QB], o[ATT_QB:])
        return carry

    lax.fori_loop(0, n_qb, body, 0)


def _attention_prompt(qkv, bias):
    Bsz, T, _ = qkv.shape
    nhp = C_HEADS // 2
    nblk = T // ATT_QB
    n_qb = next(n for n in (4, 2, 1) if nblk % n == 0)
    rows = n_qb * ATT_QB
    return pl.pallas_call(
        functools.partial(_att_prompt_kernel, n_qb),
        grid=(Bsz, nhp, T // rows),
        in_specs=[pl.BlockSpec((None, rows, LANE), lambda b, hp, g: (b, g, hp)),
                  pl.BlockSpec((None, T, LANE), lambda b, hp, g: (b, 0, nhp + hp)),
                  pl.BlockSpec((None, T, LANE), lambda b, hp, g: (b, 0, 2 * nhp + hp)),
                  pl.BlockSpec((None, 2 * ATT_QB, ATT_NKB * ATT_KB), lambda b, hp, g: (hp, 0, 0))],
        out_specs=pl.BlockSpec((None, rows, LANE), lambda b, hp, g: (b, g, hp)),
        out_shape=jax.ShapeDtypeStruct((Bsz, T, D_MODEL), F32),
        compiler_params=_params("parallel", "parallel", "arbitrary"),
    )(qkv, qkv, qkv, bias)


def _att_sample_kernel(q_ref, kn_ref, vn_ref, kc_ref, vc_ref, bias_ref, o_ref):
    R = kc_ref.shape[1]
    for hh in range(2):
        hs = slice(hh * C_HEAD_DIM, (hh + 1) * C_HEAD_DIM)
        q = (q_ref[:, hs] * ATT_SCALE).astype(BF16)
        s_c = _dg(q, kc_ref[hh].astype(BF16), NT) + bias_ref[hh, :, 0:R]
        s_n = _dg(q, kn_ref[:, hs].astype(BF16), NT) + bias_ref[hh, :, R:]
        m = jnp.maximum(s_c.max(-1, keepdims=True), s_n.max(-1, keepdims=True))
        p_c = jnp.exp(s_c - m)
        p_n = jnp.exp(s_n - m)
        den = p_c.sum(-1, keepdims=True) + p_n.sum(-1, keepdims=True)
        acc = _dg(p_c.astype(BF16), vc_ref[hh].astype(BF16), NN) + _dg(p_n.astype(BF16), vn_ref[:, hs].astype(BF16), NN)
        o_ref[:, hs] = acc / den


def _attention_sample(qkv, cache_k, cache_v, bias):
    Bsz, T, _ = qkv.shape
    R = cache_k.shape[2]
    nhp = C_HEADS // 2
    cache = pl.BlockSpec((None, 2, R, C_HEAD_DIM), lambda b, hp: (b, hp, 0, 0))
    return pl.pallas_call(
        _att_sample_kernel,
        grid=(Bsz, nhp),
        in_specs=[pl.BlockSpec((None, T, LANE), lambda b, hp: (b, 0, hp)),
                  pl.BlockSpec((None, T, LANE), lambda b, hp: (b, 0, nhp + hp)),
                  pl.BlockSpec((None, T, LANE), lambda b, hp: (b, 0, 2 * nhp + hp)),
                  cache, cache,
                  pl.BlockSpec((2, T, R + T), lambda b, hp: (hp, 0, 0))],
        out_specs=pl.BlockSpec((None, T, LANE), lambda b, hp: (b, 0, hp)),
        out_shape=jax.ShapeDtypeStruct((Bsz, T, D_MODEL), F32),
        compiler_params=_params("parallel", "parallel"),
    )(qkv, qkv, qkv, cache_k, cache_v, bias)


def _rel_bias_tile(table, n_q, n_k, q_offset):
    u = np.arange(n_q + n_k - 1)
    idx = np.clip(q_offset + n_q - 1 - u, -C_REL_CLIP, C_REL_CLIP) + C_REL_CLIP
    diag = table[:, idx]
    return jnp.stack([diag[:, n_q - 1 - i:n_q - 1 - i + n_k] for i in range(n_q)], axis=1)


TM = 512


def _trunk(x, wkv0, shift0, gla0, cache_k, cache_v, P):
    Bsz, T, _ = x.shape
    M = Bsz * T
    xf = x.reshape(M, D_MODEL)
    wkv_o, shift_o, gla_o, k_o, v_o = [], [], [], [], []
    for layer in range(DEPTH):
        if layer % 2 == 0:
            e = layer // 2
            pa = _matmul(xf, P['w_in_a'][e], TM, A_PROJ).reshape(Bsz, T, A_PROJ)
            pb = _matmul(xf, P['w_in_b'][e], TM, B_PROJ_PAD).reshape(Bsz, T, B_PROJ_PAD)
            ya, wkv = _rwkv_mix(pa, shift0[e], wkv0[e], {k: v[e] for k, v in P['a'].items()})
            yb, gs = _gla_mix(pb, gla0[e], {k: v[e] for k, v in P['b'].items()})
            wkv_o.append(wkv)
            shift_o.append(pa[:, -1])
            gla_o.append(gs)
            pairs = [(ya.reshape(M, A_WIDTH), P['w_out_a'][e]), (yb.reshape(M, B_VWIDTH), P['w_out_b'][e])]
        else:
            o = layer // 2
            qkv = _matmul(xf, P['c_w_qkv'][o], TM, D_MODEL).reshape(Bsz, T, 3 * D_MODEL)
            if cache_k is None:
                att = _attention_prompt(qkv, P['c_bias_prompt'][o])
                keep = min(C_PAST_ROWS, T)
                rows = qkv[:, T - keep:]
            else:
                att = _attention_sample(qkv, cache_k[o], cache_v[o], P['c_bias_sample'][o])
                keep = T
                rows = qkv
            rows = rows.reshape(Bsz, keep, 3, C_HEADS, C_HEAD_DIM)
            k_o.append(rows[:, :, 1].transpose(0, 2, 1, 3))
            v_o.append(rows[:, :, 2].transpose(0, 2, 1, 3))
            pairs = [(att.reshape(M, D_MODEL), P['c_w_o'][o])]
        xf = _matmul_ln(pairs, xf, P['ln1_w'][layer], P['ln1_b'][layer], TM)
        xf = _ffn_ln(xf, P['ffn_w_in'][layer], P['ffn_w_out'][layer], P['ln2_w'][layer], P['ln2_b'][layer], TM)
    return (xf.reshape(Bsz, T, D_MODEL), jnp.stack(wkv_o), jnp.stack(shift_o), jnp.stack(gla_o),
            jnp.stack(k_o), jnp.stack(v_o))


def _prepare(w_in_mix, a_mu, a_w0, a_w2, a_a0, a_a2, a_g2, a_k_k, a_k_a, a_r_k, a_ln_w, a_ln_b,
             b_alpha_up, b_alpha_bias, b_norm_w, w_out_mix, c_w_qkv, c_rel_bias, c_w_o,
             ln1_w, ln1_b, ln2_w, ln2_b, ffn_w_in, ffn_w_out, sample_len, cache_rows):
    n_even = w_in_mix.shape[0]
    kw, vw = B_KWIDTH, B_VWIDTH
    wb = w_in_mix[:, :, A_PROJ:]
    main = jnp.concatenate([wb[:, :, :2 * kw + vw], wb[:, :, 2 * kw + vw + B_GATE_LORA:]], axis=-1)
    lora = jnp.pad(wb[:, :, 2 * kw + vw:2 * kw + vw + B_GATE_LORA], ((0, 0), (0, 0), (0, B_LORA_PAD - B_GATE_LORA)))
    seg_id = jnp.arange(A_WIDTH) // A_HEAD_DIM
    seg = (seg_id[:, None] == seg_id[None, :]).astype(BF16)
    row3 = lambda t: t.reshape(t.shape[0], 1, -1)
    bias_prompt = jnp.stack([_rel_bias_tile(t, ATT_QB, ATT_NKB * ATT_KB, C_PAST_ROWS) for t in c_rel_bias])
    P = {
        'w_in_a': w_in_mix[:, :, :A_PROJ].astype(BF16),
        'w_in_b': jnp.concatenate([main, lora], axis=-1).astype(BF16),
        'a': dict(mu=row3(a_mu), w0=row3(a_w0), w2=a_w2, a0=row3(a_a0), a2=a_a2, g2=a_g2, k_k=row3(a_k_k),
                  k_a=row3(a_k_a), r_k=row3(a_r_k), ln_w=row3(a_ln_w), ln_b=row3(a_ln_b),
                  seg=jnp.broadcast_to(seg, (n_even,) + seg.shape)),
        'b': dict(up=jnp.pad(b_alpha_up, ((0, 0), (0, B_LORA_PAD - B_GATE_LORA), (0, 0))),
                  bias=row3(b_alpha_bias), nw=row3(b_norm_w)),
        'w_out_a': w_out_mix[:, :A_WIDTH].astype(BF16),
        'w_out_b': w_out_mix[:, A_WIDTH:].astype(BF16),
        'c_w_qkv': c_w_qkv.astype(BF16),
        'c_w_o': c_w_o.astype(BF16),
        'c_bias_prompt': bias_prompt.reshape(bias_prompt.shape[0], C_HEADS // 2, 2 * ATT_QB, ATT_NKB * ATT_KB),
        'c_bias_sample': jnp.stack([_rel_bias_tile(t, sample_len, cache_rows + sample_len, cache_rows)
                                    for t in c_rel_bias]),
        'ln1_w': row3(ln1_w), 'ln1_b': row3(ln1_b), 'ln2_w': row3(ln2_w), 'ln2_b': row3(ln2_b),
        'ffn_w_in': ffn_w_in.astype(BF16),
        'ffn_w_out': ffn_w_out.astype(BF16),
    }
    return P


def kernel(x_prompt, x_sample, state_a_wkv, state_a_shift, state_b_gla, cache_c_k, cache_c_v, w_in_mix, a_mu, a_w0, a_w2, a_a0, a_a2, a_g2, a_k_k, a_k_a, a_r_k, a_ln_w, a_ln_b, b_alpha_up, b_alpha_bias, b_norm_w, w_out_mix, c_w_qkv, c_rel_bias, c_w_o, ln1_w, ln1_b, ln2_w, ln2_b, ffn_w_in, ffn_w_out):
    P = _prepare(w_in_mix, a_mu, a_w0, a_w2, a_a0, a_a2, a_g2, a_k_k, a_k_a, a_r_k, a_ln_w, a_ln_b,
                 b_alpha_up, b_alpha_bias, b_norm_w, w_out_mix, c_w_qkv, c_rel_bias, c_w_o,
                 ln1_w, ln1_b, ln2_w, ln2_b, ffn_w_in, ffn_w_out, x_sample.shape[1], cache_c_k.shape[3])
    bp = x_prompt.shape[0]
    dt = x_prompt.dtype
    n_even = state_a_wkv.shape[0]
    wkv_zero = jnp.zeros((n_even, bp, A_HEADS, A_HEAD_DIM, A_HEAD_DIM), dt)
    shift_zero = jnp.zeros((n_even, bp, A_PROJ), dt)
    gla_zero = jnp.zeros((n_even, bp, B_HEADS, B_KEY_DIM, B_VAL_DIM), dt)
    y_prompt, p_wkv, p_shift, p_gla, p_k, p_v = _trunk(x_prompt, wkv_zero, shift_zero, gla_zero, None, None, P)
    y_sample, s_wkv, s_shift, s_gla, s_k, s_v = _trunk(
        x_sample, state_a_wkv, state_a_shift, state_b_gla, cache_c_k, cache_c_v, P)
    return (y_prompt, y_sample, p_wkv, p_shift, p_gla, p_k, p_v, s_wkv, s_shift, s_gla, s_k, s_v)
```

```python
import functools

import jax
import jax.numpy as jnp
import numpy as np
from jax import lax
from jax.experimental import pallas as pl
from jax.experimental.pallas import tpu as pltpu

F32 = jnp.float32
BF16 = jnp.bfloat16

D_MODEL = 1024
DEPTH = 4
CHUNK = 64
A_WIDTH = 512
A_HEAD_DIM = 64
A_HEADS = 8
A_DECAY_LORA = 64
A_ICL_LORA = 64
A_GATE_LORA = 128
A_PROJ = 3 * A_WIDTH + A_DECAY_LORA + A_ICL_LORA + A_GATE_LORA
A_NORM_EPS = 64e-5
B_VWIDTH = 512
B_KWIDTH = 256
B_HEADS = 4
B_KEY_DIM = 64
B_VAL_DIM = 128
B_GATE_LORA = 16
B_GATE_NORM = 16.0
B_LORA_PAD = 128
B_PROJ_PAD = 2 * B_KWIDTH + 2 * B_VWIDTH + B_LORA_PAD
C_HEAD_DIM = 64
C_HEADS = 16
C_PAST_CHUNKS = 8
C_PAST_ROWS = C_PAST_CHUNKS * CHUNK
C_REL_CLIP = 128
FFN_HIDDEN = 2816
LN_EPS = 1e-5
DEEPNORM_ALPHA = (2.0 * DEPTH) ** 0.25
NEG_INF = -1e30

LANE = 128
PAIR = LANE
HALF = PAIR // 2
A_PAIRS = A_HEADS // 2
B_PAIRS = B_HEADS // 2
VMEM_LIMIT = 56 * 1024 * 1024

NN = (((1,), (0,)), ((), ()))
NT = (((1,), (1,)), ((), ()))
TN = (((0,), (0,)), ((), ()))


def _params(*sem):
    return pltpu.CompilerParams(dimension_semantics=sem, vmem_limit_bytes=VMEM_LIMIT)


def _dg(a, b, dn):
    return lax.dot_general(a, b, dn, preferred_element_type=F32)


def _dot1(a, b, dn=NN):
    return _dg(a.astype(BF16), b.astype(BF16), dn)


def _split(a):
    hi = a.astype(BF16)
    lo = (a - hi.astype(F32)).astype(BF16)
    return hi, lo


def _dot3s(a, b, dn=NN):
    return _dg(a[0], b[0], dn) + _dg(a[0], b[1], dn) + _dg(a[1], b[0], dn)


def _dot3(a, b, dn=NN):
    return _dot3s(_split(a), _split(b), dn)


def _dot_xl(a_exact, b, dn=NN):
    bh, bl = _split(b)
    return _dg(a_exact, bh, dn) + _dg(a_exact, bl, dn)


def _dot_xr(a, b_exact, dn=NN):
    ah, al = _split(a)
    return _dg(ah, b_exact, dn) + _dg(al, b_exact, dn)


def _ln(z, w, b):
    mu = jnp.mean(z, -1, keepdims=True)
    d = z - mu
    var = jnp.mean(d * d, -1, keepdims=True)
    return d * lax.rsqrt(var + LN_EPS) * w + b


def _sigmoid(x):
    return 1.0 / (1.0 + jnp.exp(-x))


def _softplus(x):
    return jnp.maximum(x, 0.0) + jnp.log(1.0 + jnp.exp(-jnp.abs(x)))


def _stack_masked(x, m0):
    return jnp.concatenate([jnp.where(m0, x, 0.0), jnp.where(m0, 0.0, x)], axis=0)


def _fold(x):
    n = x.shape[0] // 2
    return x[:n] + x[n:]


def _mm_kernel(x_ref, w_ref, o_ref):
    o_ref[...] = jnp.dot(x_ref[...].astype(BF16), w_ref[...], preferred_element_type=F32)


def _matmul(x, w, tm, tn):
    M, K = x.shape
    N = w.shape[1]
    tm = min(tm, M)
    return pl.pallas_call(
        _mm_kernel,
        grid=(M // tm, N // tn),
        in_specs=[pl.BlockSpec((tm, K), lambda i, j: (i, 0)),
                  pl.BlockSpec((K, tn), lambda i, j: (0, j))],
        out_specs=pl.BlockSpec((tm, tn), lambda i, j: (i, j)),
        out_shape=jax.ShapeDtypeStruct((M, N), F32),
        compiler_params=_params("parallel", "arbitrary"),
    )(x, w)


def _mm_ln_kernel(n_pairs, *refs):
    res_ref, lw_ref, lb_ref, o_ref = refs[2 * n_pairs:]
    acc = None
    for p in range(n_pairs):
        part = jnp.dot(refs[2 * p][...].astype(BF16), refs[2 * p + 1][...], preferred_element_type=F32)
        acc = part if acc is None else acc + part
    o_ref[...] = _ln(DEEPNORM_ALPHA * res_ref[...] + acc, lw_ref[...], lb_ref[...])


def _matmul_ln(pairs, res, lw, lb, tm):
    M = res.shape[0]
    tm = min(tm, M)
    in_specs, args = [], []
    for a, w in pairs:
        K = a.shape[1]
        in_specs += [pl.BlockSpec((tm, K), lambda i: (i, 0)), pl.BlockSpec((K, D_MODEL), lambda i: (0, 0))]
        args += [a, w]
    in_specs += [pl.BlockSpec((tm, D_MODEL), lambda i: (i, 0)),
                 pl.BlockSpec((1, D_MODEL), lambda i: (0, 0)),
                 pl.BlockSpec((1, D_MODEL), lambda i: (0, 0))]
    return pl.pallas_call(
        functools.partial(_mm_ln_kernel, len(pairs)),
        grid=(M // tm,),
        in_specs=in_specs,
        out_specs=pl.BlockSpec((tm, D_MODEL), lambda i: (i, 0)),
        out_shape=jax.ShapeDtypeStruct((M, D_MODEL), F32),
        compiler_params=_params("parallel"),
    )(*args, res, lw, lb)


FFN_SPLIT = 2
FFN_TH = FFN_HIDDEN // FFN_SPLIT


def _ffn_kernel(x_ref, wg_ref, wu_ref, wo_ref, lw_ref, lb_ref, o_ref, acc_ref):
    j = pl.program_id(1)
    x = x_ref[...]
    xb = x.astype(BF16)
    g = jnp.dot(xb, wg_ref[...], preferred_element_type=F32)
    u = jnp.dot(xb, wu_ref[...], preferred_element_type=F32)
    act = g * _sigmoid(g) * u
    part = jnp.dot(act.astype(BF16), wo_ref[...], preferred_element_type=F32)

    @pl.when(j == 0)
    def _():
        acc_ref[...] = part

    @pl.when(j == FFN_SPLIT - 1)
    def _():
        o_ref[...] = _ln(DEEPNORM_ALPHA * x + acc_ref[...] + part, lw_ref[...], lb_ref[...])


def _ffn_ln(x, w_in, w_out, lw, lb, tm):
    M = x.shape[0]
    tm = min(tm, M)
    return pl.pallas_call(
        _ffn_kernel,
        grid=(M // tm, FFN_SPLIT),
        in_specs=[pl.BlockSpec((tm, D_MODEL), lambda i, j: (i, 0)),
                  pl.BlockSpec((D_MODEL, FFN_TH), lambda i, j: (0, j)),
                  pl.BlockSpec((D_MODEL, FFN_TH), lambda i, j: (0, j + FFN_SPLIT)),
                  pl.BlockSpec((FFN_TH, D_MODEL), lambda i, j: (j, 0)),
                  pl.BlockSpec((1, D_MODEL), lambda i, j: (0, 0)),
                  pl.BlockSpec((1, D_MODEL), lambda i, j: (0, 0))],
        out_specs=pl.BlockSpec((tm, D_MODEL), lambda i, j: (i, 0)),
        out_shape=jax.ShapeDtypeStruct((M, D_MODEL), F32),
        scratch_shapes=[pltpu.VMEM((tm, D_MODEL), F32)],
        compiler_params=_params("parallel", "arbitrary"),
    )(x, w_in, w_in, w_out, lw, lb)


def _inv_unit_lower(ns, eye, nilpotency):
    size = eye.shape[0]
    xs = [eye - n for n in ns]
    ps = [_dot1(n, n) for n in ns]
    k = 2
    while k < nilpotency:
        k *= 2
        if k < nilpotency:
            rs = [_dot1(jnp.concatenate([x, p], axis=0), p) for x, p in zip(xs, ps)]
            xs = [x + r[:size] for x, r in zip(xs, rs)]
            ps = [r[size:] for r in rs]
        else:
            xs = [x + _dot1(x, p) for x, p in zip(xs, ps)]
    return xs


def _segsum(x, seg):
    n = x.shape[1] // PAIR
    rows = x.shape[0]
    xs = jnp.concatenate([x[:, j * PAIR:(j + 1) * PAIR] for j in range(n)], axis=0)
    s = _dot_xr(xs, seg)
    return jnp.concatenate([s[j * rows:(j + 1) * rows] for j in range(n)], axis=1)


def _head_mask():
    hr = lax.broadcasted_iota(jnp.int32, (PAIR, PAIR), 0) // A_HEAD_DIM
    hc = lax.broadcasted_iota(jnp.int32, (PAIR, PAIR), 1) // A_HEAD_DIM
    return (hr == hc).astype(BF16)


def _rwkv_prep_kernel(n_chunks, pa_ref, prev_ref, shift_ref, mu_ref, w0_ref, w2_ref, a0_ref, a2_ref, g2_ref,
                      kk_ref, ka_ref, rk_ref,
                      mh_ref, ch_ref, qc_ref, oc_ref, g_ref, bonus_ref):
    i = pl.program_id(1)
    L = CHUNK
    pa = pa_ref[...]
    first = jnp.where(i == 0, shift_ref[...], prev_ref[7:8, :])
    rows = lax.broadcasted_iota(jnp.int32, pa.shape, 0)
    prev = jnp.where(rows == 0, first, pltpu.roll(pa, shift=1, axis=0))
    xs = pa + (prev - pa) * mu_ref[...]
    W = A_WIDTH
    r = xs[:, 0:W]
    k0 = xs[:, W:2 * W]
    v = xs[:, 2 * W:3 * W]
    xw = xs[:, 3 * W:3 * W + A_DECAY_LORA]
    xa = xs[:, 3 * W + A_DECAY_LORA:3 * W + A_DECAY_LORA + A_ICL_LORA]
    xg = xs[:, 3 * W + A_DECAY_LORA + A_ICL_LORA:]
    w = -_softplus(-(w0_ref[...] + _dot3(jnp.tanh(xw), w2_ref[...]))) - 0.5
    logw = -jnp.exp(w)
    a = _sigmoid(a0_ref[...] + _dot3(xa, a2_ref[...]))
    g_ref[...] = _dot3(_sigmoid(xg), g2_ref[...])
    seg = _head_mask()
    kk = k0 * kk_ref[...]
    kk = kk * lax.rsqrt(_segsum(kk * kk, seg) + 1e-12)
    k = k0 * (1.0 + (a - 1.0) * ka_ref[...])
    bonus_ref[...] = _segsum(r * k * rk_ref[...], seg) * v
    b = kk * a

    tri = (lax.broadcasted_iota(jnp.int32, (L, L), 0) >= lax.broadcasted_iota(jnp.int32, (L, L), 1)).astype(BF16)
    row = lax.broadcasted_iota(jnp.int32, (PAIR, PAIR), 0)
    col = lax.broadcasted_iota(jnp.int32, (PAIR, PAIR), 1)
    eye = (row == col).astype(F32)
    strict = row > col
    incl = row >= col
    m0 = lax.broadcasted_iota(jnp.int32, (L, PAIR), 1) < HALF
    pairs = range(A_PAIRS)
    lanes = [slice(j * PAIR, (j + 1) * PAIR) for j in pairs]
    for c in range(n_chunks):
        sl = slice(c * L, (c + 1) * L)
        lw = logw[sl]
        cum = _dot_xl(tri, lw)
        c_last = cum[L - 1:L, :]
        e_neg = jnp.exp(-cum)
        e_rem = jnp.exp(c_last - cum)
        kkg = kk[sl] * jnp.exp(cum - lw)
        rg = r[sl] * jnp.exp(cum)
        kinv = k[sl] * e_neg
        binv = b[sl] * e_neg
        kd = k[sl] * e_rem
        bd = b[sl] * e_rem
        e_last = jnp.exp(c_last)
        vc = v[sl]
        sm = lambda x, p: _stack_masked(x[:, p], m0).astype(BF16)
        kkg_b = [sm(kkg, p) for p in lanes]
        v_b = [sm(vc, p) for p in lanes]
        bd_b = [sm(bd, p) for p in lanes]
        kd_b = [sm(kd, p) for p in lanes]
        rg_b = [sm(rg, p) for p in lanes]
        kinv_b = [sm(kinv, p) for p in lanes]
        binv_b = [sm(binv, p) for p in lanes]
        a_kb = [jnp.where(strict, _dg(kkg_b[j], binv_b[j], NT), 0.0) for j in pairs]
        a_kk = [jnp.where(strict, _dg(kkg_b[j], kinv_b[j], NT), 0.0).astype(BF16) for j in pairs]
        a_rk = [jnp.where(incl, _dg(rg_b[j], kinv_b[j], NT), 0.0).astype(BF16) for j in pairs]
        a_rb = [jnp.where(incl, _dg(rg_b[j], binv_b[j], NT), 0.0).astype(BF16) for j in pairs]
        t = _inv_unit_lower(a_kb, eye, L)
        av = [_dg(a_kk[j], v_b[j], NN).astype(BF16) for j in pairs]
        wu = [_dg(t[j].astype(BF16), jnp.concatenate([kkg_b[j], av[j]], axis=1), NN).astype(BF16)
              for j in pairs]
        mc = [_dg(bd_b[j], wu[j], TN) for j in pairs]
        kv = [_dg(kd_b[j], v_b[j], TN) for j in pairs]
        qo = [_dg(a_rb[j], wu[j], NN) for j in pairs]
        rv = [_dg(a_rk[j], v_b[j], NN) for j in pairs]
        for j in pairs:
            mh_ref[c, j] = (eye * e_last[:, lanes[j]] - mc[j][:, :PAIR]).astype(BF16)
            ch_ref[c, j] = kv[j] - mc[j][:, PAIR:]
            qc_ref[sl, lanes[j]] = (rg[:, lanes[j]] - _fold(qo[j][:, :PAIR])).astype(BF16)
            oc_ref[sl, lanes[j]] = _fold(rv[j] - qo[j][:, PAIR:])


def _rwkv_scan_kernel(n_chunks, mh_ref, ch_ref, qc_ref, oc_ref, g_ref, bonus_ref, h0_ref, lnw_ref, lnb_ref,
                      y_ref, hout_ref, h_ref):
    i = pl.program_id(1)
    L = CHUNK

    @pl.when(i == 0)
    def _():
        h_ref[...] = h0_ref[...]

    seg = _head_mask()
    inv_n = 1.0 / A_HEAD_DIM
    pairs = range(A_PAIRS)
    lanes = [slice(j * PAIR, (j + 1) * PAIR) for j in pairs]
    hb = [h_ref[j] for j in pairs]
    for c in range(n_chunks):
        sl = slice(c * L, (c + 1) * L)
        hb_b = [x.astype(BF16) for x in hb]
        o = [_dg(qc_ref[sl, lanes[j]], hb_b[j], NN) + oc_ref[sl, lanes[j]] for j in pairs]
        hb = [_dg(mh_ref[c, j], hb_b[j], NN) + ch_ref[c, j] for j in pairs]
        o = jnp.concatenate(o, axis=1)
        mean = _segsum(o, seg) * inv_n
        d = o - mean
        var = _segsum(d * d, seg) * inv_n
        on = d * lax.rsqrt(var + A_NORM_EPS) * lnw_ref[...] + lnb_ref[...]
        y_ref[sl, :] = ((on + bonus_ref[sl, :]) * g_ref[sl, :]).astype(y_ref.dtype)
    for j in pairs:
        h_ref[j] = hb[j]
    hout_ref[...] = h_ref[...]


def _rwkv_mix(pa, shift0, wkv0, prm):
    Bsz, T, _ = pa.shape
    n1 = min(2, T // CHUNK)
    R1 = n1 * CHUNK
    W = A_WIDTH
    nblk = T // R1
    row = lambda n: pl.BlockSpec((1, n), lambda b, i: (0, 0))
    full = lambda s: pl.BlockSpec(s, lambda b, i: (0,) * len(s))
    act = lambda r, n: pl.BlockSpec((None, r, n), lambda b, i: (b, i, 0))
    mat = lambda n: pl.BlockSpec((None, n, A_PAIRS, PAIR, PAIR), lambda b, i: (b, i, 0, 0, 0))
    sub = R1 // 8
    mats = lambda dt: jax.ShapeDtypeStruct((Bsz, T // CHUNK, A_PAIRS, PAIR, PAIR), dt)
    acts = lambda dt: jax.ShapeDtypeStruct((Bsz, T, W), dt)
    mh, ch, qc, oc, g, bonus = pl.pallas_call(
        functools.partial(_rwkv_prep_kernel, n1),
        grid=(Bsz, nblk),
        in_specs=[act(R1, A_PROJ),
                  pl.BlockSpec((None, 8, A_PROJ), lambda b, i: (b, jnp.maximum(i * sub - 1, 0), 0)),
                  pl.BlockSpec((None, 1, A_PROJ), lambda b, i: (b, 0, 0)),
                  row(A_PROJ), row(W), full((A_DECAY_LORA, W)), row(W), full((A_ICL_LORA, W)),
                  full((A_GATE_LORA, W)), row(W), row(W), row(W)],
        out_specs=[mat(n1), mat(n1), act(R1, W), act(R1, W), act(R1, W), act(R1, W)],
        out_shape=[mats(BF16), mats(F32), acts(BF16), acts(F32), acts(F32), acts(F32)],
        compiler_params=_params("parallel", "parallel"),
    )(pa, pa, shift0[:, None, :], prm['mu'], prm['w0'], prm['w2'], prm['a0'], prm['a2'], prm['g2'],
      prm['k_k'], prm['k_a'], prm['r_k'])

    n2 = min(4, T // CHUNK)
    R2 = n2 * CHUNK
    hh = wkv0.transpose(0, 1, 3, 2).reshape(Bsz, A_PAIRS, 2, A_HEAD_DIM, A_HEAD_DIM)
    zero = jnp.zeros_like(hh[:, :, 0])
    h0 = jnp.concatenate([jnp.concatenate([hh[:, :, 0], zero], axis=-1),
                          jnp.concatenate([zero, hh[:, :, 1]], axis=-1)], axis=-2)
    state = pl.BlockSpec((None, A_PAIRS, PAIR, PAIR), lambda b, i: (b, 0, 0, 0))
    y, hout = pl.pallas_call(
        functools.partial(_rwkv_scan_kernel, n2),
        grid=(Bsz, T // R2),
        in_specs=[mat(n2), mat(n2), act(R2, W), act(R2, W), act(R2, W), act(R2, W), state,
                  row(W), row(W)],
        out_specs=[act(R2, W), state],
        out_shape=[acts(BF16), jax.ShapeDtypeStruct((Bsz, A_PAIRS, PAIR, PAIR), F32)],
        scratch_shapes=[pltpu.VMEM((A_PAIRS, PAIR, PAIR), F32)],
        compiler_params=_params("parallel", "arbitrary"),
    )(mh, ch, qc, oc, g, bonus, h0, prm['ln_w'], prm['ln_b'])
    hd = jnp.stack([hout[:, :, :HALF, :HALF], hout[:, :, HALF:, HALF:]], axis=2)
    wkv = hd.reshape(Bsz, A_HEADS, A_HEAD_DIM, A_HEAD_DIM).transpose(0, 1, 3, 2)
    return y, wkv


def _gla_kernel(n_chunks, pb_ref, s0_ref, up_ref, bias_ref, nw_ref, y_ref, sout_ref, s_ref):
    i = pl.program_id(1)
    L = CHUNK

    @pl.when(i == 0)
    def _():
        s_ref[...] = s0_ref[...]

    kw, vw = B_KWIDTH, B_VWIDTH
    incl = lax.broadcasted_iota(jnp.int32, (L, L), 0) >= lax.broadcasted_iota(jnp.int32, (L, L), 1)
    tri = incl.astype(BF16)
    m0 = lax.broadcasted_iota(jnp.int32, (L, PAIR), 1) < HALF
    top = lax.broadcasted_iota(jnp.int32, (PAIR, B_VAL_DIM), 0) < HALF
    scale = B_KEY_DIM ** -0.5
    pairs = range(B_PAIRS)
    heads = range(B_HEADS)
    st = [s_ref[j] for j in pairs]
    for c in range(n_chunks):
        sl = slice(c * L, (c + 1) * L)
        q = pb_ref[sl, 0:kw] * scale
        k = pb_ref[sl, kw:2 * kw]
        v = pb_ref[sl, 2 * kw:2 * kw + vw]
        rg = pb_ref[sl, 2 * kw + vw:2 * kw + 2 * vw]
        xa = pb_ref[sl, 2 * kw + 2 * vw:]
        z = _dot3(xa, up_ref[...]) + bias_ref[...]
        log_a = -_softplus(-z) * (1.0 / B_GATE_NORM)
        bc = _dot_xl(tri, log_a)
        b_last = bc[L - 1:L, :]
        q_dec = q * jnp.exp(bc)
        k_inv = (k * jnp.exp(-bc)).astype(BF16)
        k_state = (k * jnp.exp(b_last - bc)).astype(BF16)
        e_last_t = jnp.broadcast_to(jnp.exp(b_last), (LANE, kw)).T
        vb = [v[:, h * B_VAL_DIM:(h + 1) * B_VAL_DIM].astype(BF16) for h in heads]
        pl_ = lambda h: slice((h // 2) * PAIR, (h // 2 + 1) * PAIR)
        qm = [jnp.where(m0 if h % 2 == 0 else ~m0, q_dec[:, pl_(h)], 0.0).astype(BF16) for h in heads]
        scores = [jnp.where(incl, _dg(qm[h], k_inv[:, pl_(h)], NT), 0.0).astype(BF16) for h in heads]
        st_b = [x.astype(BF16) for x in st]
        o = [_dg(qm[h], st_b[h // 2], NN) + _dg(scores[h], vb[h], NN) for h in heads]
        upd = [_dg(k_state[:, pl_(h)], vb[h], TN) for h in heads]
        st = [st[j] * e_last_t[j * PAIR:(j + 1) * PAIR, :] + jnp.where(top, upd[2 * j], upd[2 * j + 1]) for j in pairs]
        for h in heads:
            vs = slice(h * B_VAL_DIM, (h + 1) * B_VAL_DIM)
            oh = o[h] * lax.rsqrt(jnp.mean(o[h] * o[h], -1, keepdims=True) + LN_EPS) * nw_ref[...]
            rg_h = rg[:, vs]
            y_ref[sl, vs] = (oh * (rg_h * _sigmoid(rg_h))).astype(y_ref.dtype)
    for j in pairs:
        s_ref[j] = st[j]
    sout_ref[...] = s_ref[...]


def _gla_mix(pb, s0, prm):
    Bsz, T, _ = pb.shape
    n = min(4, T // CHUNK)
    Rr = n * CHUNK
    st = pl.BlockSpec((None, B_PAIRS, PAIR, B_VAL_DIM), lambda b, i: (b, 0, 0, 0))
    y, s = pl.pallas_call(
        functools.partial(_gla_kernel, n),
        grid=(Bsz, T // Rr),
        in_specs=[pl.BlockSpec((None, Rr, B_PROJ_PAD), lambda b, i: (b, i, 0)), st,
                  pl.BlockSpec((B_LORA_PAD, B_KWIDTH), lambda b, i: (0, 0)),
                  pl.BlockSpec((1, B_KWIDTH), lambda b, i: (0, 0)),
                  pl.BlockSpec((1, B_VAL_DIM), lambda b, i: (0, 0))],
        out_specs=[pl.BlockSpec((None, Rr, B_VWIDTH), lambda b, i: (b, i, 0)), st],
        out_shape=[jax.ShapeDtypeStruct((Bsz, T, B_VWIDTH), BF16),
                   jax.ShapeDtypeStruct((Bsz, B_PAIRS, PAIR, B_VAL_DIM), F32)],
        scratch_shapes=[pltpu.VMEM((B_PAIRS, PAIR, B_VAL_DIM), F32)],
        compiler_params=_params("parallel", "arbitrary"),
    )(pb, s0.reshape(Bsz, B_PAIRS, PAIR, B_VAL_DIM), prm['up'], prm['bias'], prm['nw'])
    return y, s.reshape(Bsz, B_HEADS, B_KEY_DIM, B_VAL_DIM)


ATT_QB = 2 * CHUNK
ATT_KB = LANE
ATT_NKB = (C_PAST_ROWS + ATT_QB) // ATT_KB
ATT_SCALE = C_HEAD_DIM ** -0.5


ATT_WIN = C_PAST_ROWS + ATT_QB


def _qkv_pad_kernel(n_pad, x_ref, w_ref, o_ref):
    r = pl.program_id(1)

    @pl.when(r < n_pad)
    def _():
        o_ref[...] = jnp.zeros_like(o_ref)

    @pl.when(r >= n_pad)
    def _():
        o_ref[...] = jnp.dot(x_ref[...].astype(BF16), w_ref[...], preferred_element_type=F32).astype(BF16)


def _qkv_padded(x, w):
    Bsz, T, D = x.shape
    N = w.shape[1]
    tm = next(t for t in (512, 256, 128) if T % t == 0)
    n_pad = C_PAST_ROWS // tm
    return pl.pallas_call(
        functools.partial(_qkv_pad_kernel, n_pad),
        grid=(Bsz, T // tm + n_pad, N // D),
        in_specs=[pl.BlockSpec((None, tm, D), lambda b, r, j: (b, jnp.maximum(r - n_pad, 0), 0)),
                  pl.BlockSpec((D, D), lambda b, r, j: (0, j))],
        out_specs=pl.BlockSpec((None, tm, D), lambda b, r, j: (b, r, j)),
        out_shape=jax.ShapeDtypeStruct((Bsz, C_PAST_ROWS + T, N), BF16),
        compiler_params=_params("parallel", "parallel", "arbitrary"),
    )(x, w)


def _att_prompt_kernel(n_qb, q_ref, k_ref, v_ref, bias_ref, o_ref):
    g = pl.program_id(2)
    m0 = lax.broadcasted_iota(jnp.int32, (ATT_QB, PAIR), 1) < HALF
    col = lax.broadcasted_iota(jnp.int32, (2 * ATT_QB, ATT_WIN), 1)

    def body(t, carry):
        qi = g * n_qb + t
        r0 = pl.multiple_of(t * ATT_QB, ATT_QB)
        start = pl.multiple_of(qi * ATT_QB, ATT_QB)
        q = _stack_masked(q_ref[pl.ds(r0, ATT_QB), :], m0)
        s = _dg(q, k_ref[pl.ds(start, ATT_WIN), :], NT) + bias_ref[...]
        s = jnp.where(col >= C_PAST_ROWS - qi * ATT_QB, s, NEG_INF)
        m = s.max(-1, keepdims=True)
        p = jnp.exp(s - m)
        o = _dg(p.astype(BF16), v_ref[pl.ds(start, ATT_WIN), :], NN) / p.sum(-1, keepdims=True)
        o_ref[pl.ds(r0, ATT_QB), :] = jnp.where(m0, o[:ATT_QB], o[ATT_QB:]).astype(o_ref.dtype)
        return carry

    lax.fori_loop(0, n_qb, body, 0, unroll=True)


def _attention_prompt(qkv, bias):
    Bsz, Tp, _ = qkv.shape
    T = Tp - C_PAST_ROWS
    nhp = C_HEADS // 2
    nblk = T // ATT_QB
    n_qb = next(n for n in (4, 2, 1) if nblk % n == 0)
    rows = n_qb * ATT_QB
    skip = C_PAST_ROWS // rows
    return pl.pallas_call(
        functools.partial(_att_prompt_kernel, n_qb),
        grid=(Bsz, nhp, T // rows),
        in_specs=[pl.BlockSpec((None, rows, LANE), lambda b, hp, g: (b, g + skip, hp)),
                  pl.BlockSpec((None, Tp, LANE), lambda b, hp, g: (b, 0, nhp + hp)),
                  pl.BlockSpec((None, Tp, LANE), lambda b, hp, g: (b, 0, 2 * nhp + hp)),
                  pl.BlockSpec((None, 2 * ATT_QB, ATT_WIN), lambda b, hp, g: (hp, 0, 0))],
        out_specs=pl.BlockSpec((None, rows, LANE), lambda b, hp, g: (b, g, hp)),
        out_shape=jax.ShapeDtypeStruct((Bsz, T, D_MODEL), BF16),
        compiler_params=_params("parallel", "parallel", "arbitrary"),
    )(qkv, qkv, qkv, bias)


def _att_sample_kernel(q_ref, kn_ref, vn_ref, kc_ref, vc_ref, bias_ref, o_ref):
    R = kc_ref.shape[1]
    for hh in range(2):
        hs = slice(hh * C_HEAD_DIM, (hh + 1) * C_HEAD_DIM)
        q = (q_ref[:, hs] * ATT_SCALE).astype(BF16)
        s_c = _dg(q, kc_ref[hh].astype(BF16), NT) + bias_ref[hh, :, 0:R]
        s_n = _dg(q, kn_ref[:, hs].astype(BF16), NT) + bias_ref[hh, :, R:]
        m = jnp.maximum(s_c.max(-1, keepdims=True), s_n.max(-1, keepdims=True))
        p_c = jnp.exp(s_c - m)
        p_n = jnp.exp(s_n - m)
        den = p_c.sum(-1, keepdims=True) + p_n.sum(-1, keepdims=True)
        acc = _dg(p_c.astype(BF16), vc_ref[hh].astype(BF16), NN) + _dg(p_n.astype(BF16), vn_ref[:, hs].astype(BF16), NN)
        o_ref[:, hs] = acc / den


def _attention_sample(qkv, cache_k, cache_v, bias):
    Bsz, T, _ = qkv.shape
    R = cache_k.shape[2]
    nhp = C_HEADS // 2
    cache = pl.BlockSpec((None, 2, R, C_HEAD_DIM), lambda b, hp: (b, hp, 0, 0))
    return pl.pallas_call(
        _att_sample_kernel,
        grid=(Bsz, nhp),
        in_specs=[pl.BlockSpec((None, T, LANE), lambda b, hp: (b, 0, hp)),
                  pl.BlockSpec((None, T, LANE), lambda b, hp: (b, 0, nhp + hp)),
                  pl.BlockSpec((None, T, LANE), lambda b, hp: (b, 0, 2 * nhp + hp)),
                  cache, cache,
                  pl.BlockSpec((2, T, R + T), lambda b, hp: (hp, 0, 0))],
        out_specs=pl.BlockSpec((None, T, LANE), lambda b, hp: (b, 0, hp)),
        out_shape=jax.ShapeDtypeStruct((Bsz, T, D_MODEL), F32),
        compiler_params=_params("parallel", "parallel"),
    )(qkv, qkv, qkv, cache_k, cache_v, bias)


def _rel_bias_tile(table, n_q, n_k, q_offset):
    u = np.arange(n_q + n_k - 1)
    idx = np.clip(q_offset + n_q - 1 - u, -C_REL_CLIP, C_REL_CLIP) + C_REL_CLIP
    diag = table[:, idx]
    return jnp.stack([diag[:, n_q - 1 - i:n_q - 1 - i + n_k] for i in range(n_q)], axis=1)


TM = 512


def _trunk(x, wkv0, shift0, gla0, cache_k, cache_v, P):
    Bsz, T, _ = x.shape
    M = Bsz * T
    xf = x.reshape(M, D_MODEL)
    wkv_o, shift_o, gla_o, k_o, v_o = [], [], [], [], []
    for layer in range(DEPTH):
        if layer % 2 == 0:
            e = layer // 2
            pa = _matmul(xf, P['w_in_a'][e], TM, A_PROJ).reshape(Bsz, T, A_PROJ)
            pb = _matmul(xf, P['w_in_b'][e], TM, B_PROJ_PAD).reshape(Bsz, T, B_PROJ_PAD)
            ya, wkv = _rwkv_mix(pa, shift0[e], wkv0[e], {k: v[e] for k, v in P['a'].items()})
            yb, gs = _gla_mix(pb, gla0[e], {k: v[e] for k, v in P['b'].items()})
            wkv_o.append(wkv)
            shift_o.append(pa[:, -1])
            gla_o.append(gs)
            pairs = [(ya.reshape(M, A_WIDTH), P['w_out_a'][e]), (yb.reshape(M, B_VWIDTH), P['w_out_b'][e])]
        else:
            o = layer // 2
            if cache_k is None:
                x3 = xf.reshape(Bsz, T, D_MODEL)
                att = _attention_prompt(_qkv_padded(x3, P['c_w_qkv_scaled'][o]), P['c_bias_prompt'][o])
                keep = min(C_PAST_ROWS, T)
                rows = _matmul(x3[:, T - keep:].reshape(Bsz * keep, D_MODEL), P['c_w_qkv'][o][:, D_MODEL:],
                               TM, D_MODEL)
                rows = rows.reshape(Bsz, keep, 2, C_HEADS, C_HEAD_DIM)
            else:
                qkv = _matmul(xf, P['c_w_qkv'][o], TM, D_MODEL).reshape(Bsz, T, 3 * D_MODEL)
                att = _attention_sample(qkv, cache_k[o], cache_v[o], P['c_bias_sample'][o])
                rows = qkv.reshape(Bsz, T, 3, C_HEADS, C_HEAD_DIM)[:, :, 1:]
            k_o.append(rows[:, :, 0].transpose(0, 2, 1, 3))
            v_o.append(rows[:, :, 1].transpose(0, 2, 1, 3))
            pairs = [(att.reshape(M, D_MODEL), P['c_w_o'][o])]
        xf = _matmul_ln(pairs, xf, P['ln1_w'][layer], P['ln1_b'][layer], TM)
        xf = _ffn_ln(xf, P['ffn_w_in'][layer], P['ffn_w_out'][layer], P['ln2_w'][layer], P['ln2_b'][layer], TM)
    return (xf.reshape(Bsz, T, D_MODEL), jnp.stack(wkv_o), jnp.stack(shift_o), jnp.stack(gla_o),
            jnp.stack(k_o), jnp.stack(v_o))


def _prepare(w_in_mix, a_mu, a_w0, a_w2, a_a0, a_a2, a_g2, a_k_k, a_k_a, a_r_k, a_ln_w, a_ln_b,
             b_alpha_up, b_alpha_bias, b_norm_w, w_out_mix, c_w_qkv, c_rel_bias, c_w_o,
             ln1_w, ln1_b, ln2_w, ln2_b, ffn_w_in, ffn_w_out, sample_len, cache_rows):
    kw, vw = B_KWIDTH, B_VWIDTH
    wb = w_in_mix[:, :, A_PROJ:]
    main = jnp.concatenate([wb[:, :, :2 * kw + vw], wb[:, :, 2 * kw + vw + B_GATE_LORA:]], axis=-1)
    lora = jnp.pad(wb[:, :, 2 * kw + vw:2 * kw + vw + B_GATE_LORA], ((0, 0), (0, 0), (0, B_LORA_PAD - B_GATE_LORA)))
    row3 = lambda t: t.reshape(t.shape[0], 1, -1)
    bias_prompt = jnp.stack([_rel_bias_tile(t, ATT_QB, ATT_WIN, C_PAST_ROWS) for t in c_rel_bias])
    rel = np.arange(ATT_WIN)[None, :] // CHUNK - np.arange(ATT_QB)[:, None] // CHUNK
    band = (rel >= 0) & (rel <= C_PAST_CHUNKS)
    bias_prompt = jnp.where(band, bias_prompt, NEG_INF)
    q_scale = jnp.concatenate([jnp.full((D_MODEL,), ATT_SCALE, F32), jnp.ones((2 * D_MODEL,), F32)])
    P = {
        'w_in_a': w_in_mix[:, :, :A_PROJ].astype(BF16),
        'w_in_b': jnp.concatenate([main, lora], axis=-1).astype(BF16),
        'a': dict(mu=row3(a_mu), w0=row3(a_w0), w2=a_w2, a0=row3(a_a0), a2=a_a2, g2=a_g2, k_k=row3(a_k_k),
                  k_a=row3(a_k_a), r_k=row3(a_r_k), ln_w=row3(a_ln_w), ln_b=row3(a_ln_b)),
        'b': dict(up=jnp.pad(b_alpha_up, ((0, 0), (0, B_LORA_PAD - B_GATE_LORA), (0, 0))),
                  bias=row3(b_alpha_bias), nw=row3(b_norm_w)),
        'w_out_a': w_out_mix[:, :A_WIDTH].astype(BF16),
        'w_out_b': w_out_mix[:, A_WIDTH:].astype(BF16),
        'c_w_qkv': c_w_qkv.astype(BF16),
        'c_w_qkv_scaled': (c_w_qkv * q_scale).astype(BF16),
        'c_w_o': c_w_o.astype(BF16),
        'c_bias_prompt': bias_prompt.reshape(bias_prompt.shape[0], C_HEADS // 2, 2 * ATT_QB, ATT_WIN),
        'c_bias_sample': jnp.stack([_rel_bias_tile(t, sample_len, cache_rows + sample_len, cache_rows)
                                    for t in c_rel_bias]),
        'ln1_w': row3(ln1_w), 'ln1_b': row3(ln1_b), 'ln2_w': row3(ln2_w), 'ln2_b': row3(ln2_b),
        'ffn_w_in': ffn_w_in.astype(BF16),
        'ffn_w_out': ffn_w_out.astype(BF16),
    }
    return P


def kernel(x_prompt, x_sample, state_a_wkv, state_a_shift, state_b_gla, cache_c_k, cache_c_v, w_in_mix, a_mu, a_w0, a_w2, a_a0, a_a2, a_g2, a_k_k, a_k_a, a_r_k, a_ln_w, a_ln_b, b_alpha_up, b_alpha_bias, b_norm_w, w_out_mix, c_w_qkv, c_rel_bias, c_w_o, ln1_w, ln1_b, ln2_w, ln2_b, ffn_w_in, ffn_w_out):
    P = _prepare(w_in_mix, a_mu, a_w0, a_w2, a_a0, a_a2, a_g2, a_k_k, a_k_a, a_r_k, a_ln_w, a_ln_b,
                 b_alpha_up, b_alpha_bias, b_norm_w, w_out_mix, c_w_qkv, c_rel_bias, c_w_o,
                 ln1_w, ln1_b, ln2_w, ln2_b, ffn_w_in, ffn_w_out, x_sample.shape[1], cache_c_k.shape[3])
    bp = x_prompt.shape[0]
    dt = x_prompt.dtype
    n_even = state_a_wkv.shape[0]
    wkv_zero = jnp.zeros((n_even, bp, A_HEADS, A_HEAD_DIM, A_HEAD_DIM), dt)
    shift_zero = jnp.zeros((n_even, bp, A_PROJ), dt)
    gla_zero = jnp.zeros((n_even, bp, B_HEADS, B_KEY_DIM, B_VAL_DIM), dt)
    y_prompt, p_wkv, p_shift, p_gla, p_k, p_v = _trunk(x_prompt, wkv_zero, shift_zero, gla_zero, None, None, P)
    y_sample, s_wkv, s_shift, s_gla, s_k, s_v = _trunk(
        x_sample, state_a_wkv, state_a_shift, state_b_gla, cache_c_k, cache_c_v, P)
    return (y_prompt, y_sample, p_wkv, p_shift, p_gla, p_k, p_v, s_wkv, s_shift, s_gla, s_k, s_v)
```

```python
import functools

import jax
import jax.numpy as jnp
import numpy as np
from jax import lax
from jax.experimental import pallas as pl
from jax.experimental.pallas import tpu as pltpu

F32 = jnp.float32
BF16 = jnp.bfloat16

D_MODEL = 1024
DEPTH = 4
CHUNK = 64
A_WIDTH = 512
A_HEAD_DIM = 64
A_HEADS = 8
A_DECAY_LORA = 64
A_ICL_LORA = 64
A_GATE_LORA = 128
A_PROJ = 3 * A_WIDTH + A_DECAY_LORA + A_ICL_LORA + A_GATE_LORA
A_NORM_EPS = 64e-5
B_VWIDTH = 512
B_KWIDTH = 256
B_HEADS = 4
B_KEY_DIM = 64
B_VAL_DIM = 128
B_GATE_LORA = 16
B_GATE_NORM = 16.0
B_LORA_PAD = 128
B_PROJ_PAD = 2 * B_KWIDTH + 2 * B_VWIDTH + B_LORA_PAD
C_HEAD_DIM = 64
C_HEADS = 16
C_PAST_CHUNKS = 8
C_PAST_ROWS = C_PAST_CHUNKS * CHUNK
C_REL_CLIP = 128
FFN_HIDDEN = 2816
LN_EPS = 1e-5
DEEPNORM_ALPHA = (2.0 * DEPTH) ** 0.25
NEG_INF = -1e30

LANE = 128
PAIR = LANE
HALF = PAIR // 2
A_PAIRS = A_HEADS // 2
B_PAIRS = B_HEADS // 2
VMEM_LIMIT = 56 * 1024 * 1024

NN = (((1,), (0,)), ((), ()))
NT = (((1,), (1,)), ((), ()))
TN = (((0,), (0,)), ((), ()))


def _params(*sem):
    return pltpu.CompilerParams(dimension_semantics=sem, vmem_limit_bytes=VMEM_LIMIT)


def _dg(a, b, dn):
    return lax.dot_general(a, b, dn, preferred_element_type=F32)


def _dot1(a, b, dn=NN):
    return _dg(a.astype(BF16), b.astype(BF16), dn)


def _split(a):
    hi = a.astype(BF16)
    lo = (a - hi.astype(F32)).astype(BF16)
    return hi, lo


def _dot3s(a, b, dn=NN):
    return _dg(a[0], b[0], dn) + _dg(a[0], b[1], dn) + _dg(a[1], b[0], dn)


def _dot3(a, b, dn=NN):
    return _dot3s(_split(a), _split(b), dn)


def _dot_xl(a_exact, b, dn=NN):
    bh, bl = _split(b)
    return _dg(a_exact, bh, dn) + _dg(a_exact, bl, dn)


def _dot_xr(a, b_exact, dn=NN):
    ah, al = _split(a)
    return _dg(ah, b_exact, dn) + _dg(al, b_exact, dn)


def _ln(z, w, b):
    mu = jnp.mean(z, -1, keepdims=True)
    d = z - mu
    var = jnp.mean(d * d, -1, keepdims=True)
    return d * lax.rsqrt(var + LN_EPS) * w + b


def _sigmoid(x):
    return 1.0 / (1.0 + jnp.exp(-x))


def _softplus(x):
    return jnp.maximum(x, 0.0) + jnp.log(1.0 + jnp.exp(-jnp.abs(x)))


def _stack_masked(x, m0):
    return jnp.concatenate([jnp.where(m0, x, 0.0), jnp.where(m0, 0.0, x)], axis=0)


def _fold(x):
    n = x.shape[0] // 2
    return x[:n] + x[n:]


def _mm_kernel(x_ref, w_ref, o_ref):
    o_ref[...] = jnp.dot(x_ref[...].astype(BF16), w_ref[...], preferred_element_type=F32)


def _matmul(x, w, tm, tn):
    M, K = x.shape
    N = w.shape[1]
    tm = min(tm, M)
    return pl.pallas_call(
        _mm_kernel,
        grid=(M // tm, N // tn),
        in_specs=[pl.BlockSpec((tm, K), lambda i, j: (i, 0)),
                  pl.BlockSpec((K, tn), lambda i, j: (0, j))],
        out_specs=pl.BlockSpec((tm, tn), lambda i, j: (i, j)),
        out_shape=jax.ShapeDtypeStruct((M, N), F32),
        compiler_params=_params("parallel", "arbitrary"),
    )(x, w)


def _mm_ln_kernel(n_pairs, *refs):
    res_ref, lw_ref, lb_ref, o_ref = refs[2 * n_pairs:]
    acc = None
    for p in range(n_pairs):
        part = jnp.dot(refs[2 * p][...].astype(BF16), refs[2 * p + 1][...], preferred_element_type=F32)
        acc = part if acc is None else acc + part
    o_ref[...] = _ln(DEEPNORM_ALPHA * res_ref[...] + acc, lw_ref[...], lb_ref[...])


def _matmul_ln(pairs, res, lw, lb, tm):
    M = res.shape[0]
    tm = min(tm, M)
    in_specs, args = [], []
    for a, w in pairs:
        K = a.shape[1]
        in_specs += [pl.BlockSpec((tm, K), lambda i: (i, 0)), pl.BlockSpec((K, D_MODEL), lambda i: (0, 0))]
        args += [a, w]
    in_specs += [pl.BlockSpec((tm, D_MODEL), lambda i: (i, 0)),
                 pl.BlockSpec((1, D_MODEL), lambda i: (0, 0)),
                 pl.BlockSpec((1, D_MODEL), lambda i: (0, 0))]
    return pl.pallas_call(
        functools.partial(_mm_ln_kernel, len(pairs)),
        grid=(M // tm,),
        in_specs=in_specs,
        out_specs=pl.BlockSpec((tm, D_MODEL), lambda i: (i, 0)),
        out_shape=jax.ShapeDtypeStruct((M, D_MODEL), F32),
        compiler_params=_params("parallel"),
    )(*args, res, lw, lb)


def _ffn_kernel(x_ref, wg_ref, wu_ref, wo_ref, lw_ref, lb_ref, o_ref):
    x = x_ref[...]
    xb = x.astype(BF16)
    g = jnp.dot(xb, wg_ref[...], preferred_element_type=F32)
    u = jnp.dot(xb, wu_ref[...], preferred_element_type=F32)
    act = (g * _sigmoid(g) * u).astype(BF16)
    y = jnp.dot(act, wo_ref[...], preferred_element_type=F32)
    o_ref[...] = _ln(DEEPNORM_ALPHA * x + y, lw_ref[...], lb_ref[...])


def _ffn_ln(x, w_in, w_out, lw, lb, tm):
    M = x.shape[0]
    tm = min(tm, M)
    resident = lambda shape, idx: pl.BlockSpec(shape, idx, pipeline_mode=pl.Buffered(1))
    return pl.pallas_call(
        _ffn_kernel,
        grid=(M // tm,),
        in_specs=[pl.BlockSpec((tm, D_MODEL), lambda i: (i, 0)),
                  resident((D_MODEL, FFN_HIDDEN), lambda i: (0, 0)),
                  resident((D_MODEL, FFN_HIDDEN), lambda i: (0, 1)),
                  resident((FFN_HIDDEN, D_MODEL), lambda i: (0, 0)),
                  pl.BlockSpec((1, D_MODEL), lambda i: (0, 0)),
                  pl.BlockSpec((1, D_MODEL), lambda i: (0, 0))],
        out_specs=pl.BlockSpec((tm, D_MODEL), lambda i: (i, 0)),
        out_shape=jax.ShapeDtypeStruct((M, D_MODEL), F32),
        compiler_params=_params("parallel"),
    )(x, w_in, w_in, w_out, lw, lb)


def _inv_unit_lower(ns, eye, nilpotency):
    size = eye.shape[0]
    xs = [eye - n for n in ns]
    ps = [_dot1(n, n) for n in ns]
    k = 2
    while k < nilpotency:
        k *= 2
        if k < nilpotency:
            rs = [_dot1(jnp.concatenate([x, p], axis=0), p) for x, p in zip(xs, ps)]
            xs = [x + r[:size] for x, r in zip(xs, rs)]
            ps = [r[size:] for r in rs]
        else:
            xs = [x + _dot1(x, p) for x, p in zip(xs, ps)]
    return xs


def _segsum(x, seg):
    n = x.shape[1] // PAIR
    rows = x.shape[0]
    xs = jnp.concatenate([x[:, j * PAIR:(j + 1) * PAIR] for j in range(n)], axis=0)
    s = _dot_xr(xs, seg)
    return jnp.concatenate([s[j * rows:(j + 1) * rows] for j in range(n)], axis=1)


def _head_mask():
    hr = lax.broadcasted_iota(jnp.int32, (PAIR, PAIR), 0) // A_HEAD_DIM
    hc = lax.broadcasted_iota(jnp.int32, (PAIR, PAIR), 1) // A_HEAD_DIM
    return (hr == hc).astype(BF16)


def _rwkv_prep_kernel(n_chunks, pa_ref, prev_ref, shift_ref, mu_ref, w0_ref, w2_ref, a0_ref, a2_ref, g2_ref,
                      kk_ref, ka_ref, rk_ref,
                      mh_ref, ch_ref, qc_ref, oc_ref, g_ref, bonus_ref):
    i = pl.program_id(1)
    L = CHUNK
    pa = pa_ref[...]
    first = jnp.where(i == 0, shift_ref[...], prev_ref[7:8, :])
    rows = lax.broadcasted_iota(jnp.int32, pa.shape, 0)
    prev = jnp.where(rows == 0, first, pltpu.roll(pa, shift=1, axis=0))
    xs = pa + (prev - pa) * mu_ref[...]
    W = A_WIDTH
    r = xs[:, 0:W]
    k0 = xs[:, W:2 * W]
    v = xs[:, 2 * W:3 * W]
    xw = xs[:, 3 * W:3 * W + A_DECAY_LORA]
    xa = xs[:, 3 * W + A_DECAY_LORA:3 * W + A_DECAY_LORA + A_ICL_LORA]
    xg = xs[:, 3 * W + A_DECAY_LORA + A_ICL_LORA:]
    w = -_softplus(-(w0_ref[...] + _dot3(jnp.tanh(xw), w2_ref[...]))) - 0.5
    logw = -jnp.exp(w)
    a = _sigmoid(a0_ref[...] + _dot3(xa, a2_ref[...]))
    g_ref[...] = _dot3(_sigmoid(xg), g2_ref[...])
    seg = _head_mask()
    kk = k0 * kk_ref[...]
    kk = kk * lax.rsqrt(_segsum(kk * kk, seg) + 1e-12)
    k = k0 * (1.0 + (a - 1.0) * ka_ref[...])
    bonus_ref[...] = _segsum(r * k * rk_ref[...], seg) * v
    b = kk * a

    tri = (lax.broadcasted_iota(jnp.int32, (L, L), 0) >= lax.broadcasted_iota(jnp.int32, (L, L), 1)).astype(BF16)
    row = lax.broadcasted_iota(jnp.int32, (PAIR, PAIR), 0)
    col = lax.broadcasted_iota(jnp.int32, (PAIR, PAIR), 1)
    eye = (row == col).astype(F32)
    strict = row > col
    incl2 = (lax.broadcasted_iota(jnp.int32, (PAIR, 2 * PAIR), 0)
             >= lax.broadcasted_iota(jnp.int32, (PAIR, 2 * PAIR), 1) % PAIR)
    m0 = lax.broadcasted_iota(jnp.int32, (L, PAIR), 1) < HALF
    pairs = range(A_PAIRS)
    lanes = [slice(j * PAIR, (j + 1) * PAIR) for j in pairs]
    for c in range(n_chunks):
        sl = slice(c * L, (c + 1) * L)
        lw = logw[sl]
        cum = _dot_xl(tri, lw)
        c_last = cum[L - 1:L, :]
        e_neg = jnp.exp(-cum)
        e_rem = jnp.exp(c_last - cum)
        kkg = kk[sl] * jnp.exp(cum - lw)
        rg = r[sl] * jnp.exp(cum)
        kinv = k[sl] * e_neg
        binv = b[sl] * e_neg
        kd = k[sl] * e_rem
        bd = b[sl] * e_rem
        e_last = jnp.exp(c_last)
        vc = v[sl]
        sm = lambda x, p: _stack_masked(x[:, p], m0).astype(BF16)
        kkg_b = [sm(kkg, p) for p in lanes]
        v_b = [sm(vc, p) for p in lanes]
        bd_b = [sm(bd, p) for p in lanes]
        kd_b = [sm(kd, p) for p in lanes]
        rg_b = [sm(rg, p) for p in lanes]
        kinv_b = [sm(kinv, p) for p in lanes]
        binv_b = [sm(binv, p) for p in lanes]
        aa = [_dg(jnp.concatenate([kkg_b[j], rg_b[j]], axis=0), jnp.concatenate([binv_b[j], kinv_b[j]], axis=0), NT)
              for j in pairs]
        a_kb = [jnp.where(strict, x[:PAIR, :PAIR], 0.0) for x in aa]
        a_kk = [jnp.where(strict, x[:PAIR, PAIR:], 0.0).astype(BF16) for x in aa]
        a_rbk = [jnp.where(incl2, x[PAIR:, :], 0.0).astype(BF16) for x in aa]
        t = _inv_unit_lower(a_kb, eye, L)
        av = [_dg(a_kk[j], v_b[j], NN).astype(BF16) for j in pairs]
        wu = [_dg(t[j].astype(BF16), jnp.concatenate([kkg_b[j], av[j]], axis=1), NN) for j in pairs]
        wu_b = [x.astype(BF16) for x in wu]
        mc = [_dg(bd_b[j], wu_b[j], TN) for j in pairs]
        kv = [_dg(kd_b[j], v_b[j], TN) for j in pairs]
        rhs = [jnp.concatenate([jnp.concatenate([wu_b[j][:, :PAIR], (-wu[j][:, PAIR:]).astype(BF16)], axis=1),
                                jnp.concatenate([jnp.zeros_like(v_b[j]), v_b[j]], axis=1)], axis=0) for j in pairs]
        qo = [_dg(a_rbk[j], rhs[j], NN) for j in pairs]
        for j in pairs:
            mh_ref[c, j] = (eye * e_last[:, lanes[j]] - mc[j][:, :PAIR]).astype(BF16)
            ch_ref[c, j] = kv[j] - mc[j][:, PAIR:]
            qc_ref[sl, lanes[j]] = (rg[:, lanes[j]] - _fold(qo[j][:, :PAIR])).astype(BF16)
            oc_ref[sl, lanes[j]] = _fold(qo[j][:, PAIR:])


def _rwkv_scan_kernel(n_chunks, mh_ref, ch_ref, qc_ref, oc_ref, g_ref, bonus_ref, h0_ref, lnw_ref, lnb_ref,
                      y_ref, hout_ref, h_ref):
    i = pl.program_id(1)
    L = CHUNK

    @pl.when(i == 0)
    def _():
        h_ref[...] = h0_ref[...]

    seg = _head_mask()
    inv_n = 1.0 / A_HEAD_DIM
    pairs = range(A_PAIRS)
    lanes = [slice(j * PAIR, (j + 1) * PAIR) for j in pairs]
    hb = [h_ref[j] for j in pairs]
    for c in range(n_chunks):
        sl = slice(c * L, (c + 1) * L)
        hb_b = [x.astype(BF16) for x in hb]
        o = [_dg(qc_ref[sl, lanes[j]], hb_b[j], NN) + oc_ref[sl, lanes[j]] for j in pairs]
        hb = [_dg(mh_ref[c, j], hb_b[j], NN) + ch_ref[c, j] for j in pairs]
        o = jnp.concatenate(o, axis=1)
        mean = _segsum(o, seg) * inv_n
        d = o - mean
        var = _segsum(d * d, seg) * inv_n
        on = d * lax.rsqrt(var + A_NORM_EPS) * lnw_ref[...] + lnb_ref[...]
        y_ref[sl, :] = ((on + bonus_ref[sl, :]) * g_ref[sl, :]).astype(y_ref.dtype)
    for j in pairs:
        h_ref[j] = hb[j]
    hout_ref[...] = h_ref[...]


def _rwkv_mix(pa, shift0, wkv0, prm):
    Bsz, T, _ = pa.shape
    n1 = min(2, T // CHUNK)
    R1 = n1 * CHUNK
    W = A_WIDTH
    nblk = T // R1
    row = lambda n: pl.BlockSpec((1, n), lambda b, i: (0, 0))
    full = lambda s: pl.BlockSpec(s, lambda b, i: (0,) * len(s))
    act = lambda r, n: pl.BlockSpec((None, r, n), lambda b, i: (b, i, 0))
    mat = lambda n: pl.BlockSpec((None, n, A_PAIRS, PAIR, PAIR), lambda b, i: (b, i, 0, 0, 0))
    sub = R1 // 8
    mats = lambda dt: jax.ShapeDtypeStruct((Bsz, T // CHUNK, A_PAIRS, PAIR, PAIR), dt)
    acts = lambda dt: jax.ShapeDtypeStruct((Bsz, T, W), dt)
    mh, ch, qc, oc, g, bonus = pl.pallas_call(
        functools.partial(_rwkv_prep_kernel, n1),
        grid=(Bsz, nblk),
        in_specs=[act(R1, A_PROJ),
                  pl.BlockSpec((None, 8, A_PROJ), lambda b, i: (b, jnp.maximum(i * sub - 1, 0), 0)),
                  pl.BlockSpec((None, 1, A_PROJ), lambda b, i: (b, 0, 0)),
                  row(A_PROJ), row(W), full((A_DECAY_LORA, W)), row(W), full((A_ICL_LORA, W)),
                  full((A_GATE_LORA, W)), row(W), row(W), row(W)],
        out_specs=[mat(n1), mat(n1), act(R1, W), act(R1, W), act(R1, W), act(R1, W)],
        out_shape=[mats(BF16), mats(F32), acts(BF16), acts(F32), acts(F32), acts(F32)],
        compiler_params=_params("parallel", "parallel"),
    )(pa, pa, shift0[:, None, :], prm['mu'], prm['w0'], prm['w2'], prm['a0'], prm['a2'], prm['g2'],
      prm['k_k'], prm['k_a'], prm['r_k'])

    n2 = min(4, T // CHUNK)
    R2 = n2 * CHUNK
    hh = wkv0.transpose(0, 1, 3, 2).reshape(Bsz, A_PAIRS, 2, A_HEAD_DIM, A_HEAD_DIM)
    zero = jnp.zeros_like(hh[:, :, 0])
    h0 = jnp.concatenate([jnp.concatenate([hh[:, :, 0], zero], axis=-1),
                          jnp.concatenate([zero, hh[:, :, 1]], axis=-1)], axis=-2)
    state = pl.BlockSpec((None, A_PAIRS, PAIR, PAIR), lambda b, i: (b, 0, 0, 0))
    y, hout = pl.pallas_call(
        functools.partial(_rwkv_scan_kernel, n2),
        grid=(Bsz, T // R2),
        in_specs=[mat(n2), mat(n2), act(R2, W), act(R2, W), act(R2, W), act(R2, W), state,
                  row(W), row(W)],
        out_specs=[act(R2, W), state],
        out_shape=[acts(BF16), jax.ShapeDtypeStruct((Bsz, A_PAIRS, PAIR, PAIR), F32)],
        scratch_shapes=[pltpu.VMEM((A_PAIRS, PAIR, PAIR), F32)],
        compiler_params=_params("parallel", "arbitrary"),
    )(mh, ch, qc, oc, g, bonus, h0, prm['ln_w'], prm['ln_b'])
    hd = jnp.stack([hout[:, :, :HALF, :HALF], hout[:, :, HALF:, HALF:]], axis=2)
    wkv = hd.reshape(Bsz, A_HEADS, A_HEAD_DIM, A_HEAD_DIM).transpose(0, 1, 3, 2)
    return y, wkv


def _gla_kernel(n_chunks, pb_ref, s0_ref, up_ref, bias_ref, nw_ref, y_ref, sout_ref, s_ref):
    i = pl.program_id(1)
    L = CHUNK

    @pl.when(i == 0)
    def _():
        s_ref[...] = s0_ref[...]

    kw, vw = B_KWIDTH, B_VWIDTH
    incl = lax.broadcasted_iota(jnp.int32, (L, L), 0) >= lax.broadcasted_iota(jnp.int32, (L, L), 1)
    tri = incl.astype(BF16)
    m0 = lax.broadcasted_iota(jnp.int32, (L, PAIR), 1) < HALF
    top = lax.broadcasted_iota(jnp.int32, (PAIR, B_VAL_DIM), 0) < HALF
    scale = B_KEY_DIM ** -0.5
    pairs = range(B_PAIRS)
    heads = range(B_HEADS)
    st = [s_ref[j] for j in pairs]
    for c in range(n_chunks):
        sl = slice(c * L, (c + 1) * L)
        q = pb_ref[sl, 0:kw] * scale
        k = pb_ref[sl, kw:2 * kw]
        v = pb_ref[sl, 2 * kw:2 * kw + vw]
        rg = pb_ref[sl, 2 * kw + vw:2 * kw + 2 * vw]
        xa = pb_ref[sl, 2 * kw + 2 * vw:]
        z = _dot3(xa, up_ref[...]) + bias_ref[...]
        log_a = -_softplus(-z) * (1.0 / B_GATE_NORM)
        bc = _dot_xl(tri, log_a)
        b_last = bc[L - 1:L, :]
        q_dec = q * jnp.exp(bc)
        k_inv = (k * jnp.exp(-bc)).astype(BF16)
        k_state = (k * jnp.exp(b_last - bc)).astype(BF16)
        e_last_t = jnp.broadcast_to(jnp.exp(b_last), (LANE, kw)).T
        vb = [v[:, h * B_VAL_DIM:(h + 1) * B_VAL_DIM].astype(BF16) for h in heads]
        pl_ = lambda h: slice((h // 2) * PAIR, (h // 2 + 1) * PAIR)
        qm = [jnp.where(m0 if h % 2 == 0 else ~m0, q_dec[:, pl_(h)], 0.0).astype(BF16) for h in heads]
        scores = [jnp.where(incl, _dg(qm[h], k_inv[:, pl_(h)], NT), 0.0).astype(BF16) for h in heads]
        st_b = [x.astype(BF16) for x in st]
        o = [_dg(qm[h], st_b[h // 2], NN) + _dg(scores[h], vb[h], NN) for h in heads]
        upd = [_dg(k_state[:, pl_(h)], vb[h], TN) for h in heads]
        st = [st[j] * e_last_t[j * PAIR:(j + 1) * PAIR, :] + jnp.where(top, upd[2 * j], upd[2 * j + 1]) for j in pairs]
        for h in heads:
            vs = slice(h * B_VAL_DIM, (h + 1) * B_VAL_DIM)
            oh = o[h] * lax.rsqrt(jnp.mean(o[h] * o[h], -1, keepdims=True) + LN_EPS) * nw_ref[...]
            rg_h = rg[:, vs]
            y_ref[sl, vs] = (oh * (rg_h * _sigmoid(rg_h))).astype(y_ref.dtype)
    for j in pairs:
        s_ref[j] = st[j]
    sout_ref[...] = s_ref[...]


def _gla_mix(pb, s0, prm):
    Bsz, T, _ = pb.shape
    n = min(4, T // CHUNK)
    Rr = n * CHUNK
    st = pl.BlockSpec((None, B_PAIRS, PAIR, B_VAL_DIM), lambda b, i: (b, 0, 0, 0))
    y, s = pl.pallas_call(
        functools.partial(_gla_kernel, n),
        grid=(Bsz, T // Rr),
        in_specs=[pl.BlockSpec((None, Rr, B_PROJ_PAD), lambda b, i: (b, i, 0)), st,
                  pl.BlockSpec((B_LORA_PAD, B_KWIDTH), lambda b, i: (0, 0)),
                  pl.BlockSpec((1, B_KWIDTH), lambda b, i: (0, 0)),
                  pl.BlockSpec((1, B_VAL_DIM), lambda b, i: (0, 0))],
        out_specs=[pl.BlockSpec((None, Rr, B_VWIDTH), lambda b, i: (b, i, 0)), st],
        out_shape=[jax.ShapeDtypeStruct((Bsz, T, B_VWIDTH), BF16),
                   jax.ShapeDtypeStruct((Bsz, B_PAIRS, PAIR, B_VAL_DIM), F32)],
        scratch_shapes=[pltpu.VMEM((B_PAIRS, PAIR, B_VAL_DIM), F32)],
        compiler_params=_params("parallel", "arbitrary"),
    )(pb, s0.reshape(Bsz, B_PAIRS, PAIR, B_VAL_DIM), prm['up'], prm['bias'], prm['nw'])
    return y, s.reshape(Bsz, B_HEADS, B_KEY_DIM, B_VAL_DIM)


ATT_QB = 2 * CHUNK
ATT_KB = LANE
ATT_NKB = (C_PAST_ROWS + ATT_QB) // ATT_KB
ATT_SCALE = C_HEAD_DIM ** -0.5


ATT_WIN = C_PAST_ROWS + ATT_QB


def _qkv_pad_kernel(n_pad, x_ref, w_ref, o_ref):
    r = pl.program_id(1)

    @pl.when(r < n_pad)
    def _():
        o_ref[...] = jnp.zeros_like(o_ref)

    @pl.when(r >= n_pad)
    def _():
        o_ref[...] = jnp.dot(x_ref[...].astype(BF16), w_ref[...], preferred_element_type=F32).astype(BF16)


def _qkv_padded(x, w):
    Bsz, T, D = x.shape
    N = w.shape[1]
    tm = next(t for t in (512, 256, 128) if T % t == 0)
    n_pad = C_PAST_ROWS // tm
    return pl.pallas_call(
        functools.partial(_qkv_pad_kernel, n_pad),
        grid=(Bsz, T // tm + n_pad, N // D),
        in_specs=[pl.BlockSpec((None, tm, D), lambda b, r, j: (b, jnp.maximum(r - n_pad, 0), 0)),
                  pl.BlockSpec((D, D), lambda b, r, j: (0, j))],
        out_specs=pl.BlockSpec((None, tm, D), lambda b, r, j: (b, r, j)),
        out_shape=jax.ShapeDtypeStruct((Bsz, C_PAST_ROWS + T, N), BF16),
        compiler_params=_params("parallel", "parallel", "arbitrary"),
    )(x, w)


def _att_prompt_kernel(n_qb, q_ref, k_ref, v_ref, bias_ref, o_ref):
    g = pl.program_id(2)
    m0 = lax.broadcasted_iota(jnp.int32, (ATT_QB, PAIR), 1) < HALF
    col = lax.broadcasted_iota(jnp.int32, (2 * ATT_QB, ATT_WIN), 1)

    def body(t, carry):
        qi = g * n_qb + t
        r0 = pl.multiple_of(t * ATT_QB, ATT_QB)
        start = pl.multiple_of(qi * ATT_QB, ATT_QB)
        q = _stack_masked(q_ref[pl.ds(r0, ATT_QB), :], m0)
        s = _dg(q, k_ref[pl.ds(start, ATT_WIN), :], NT) + bias_ref[...]
        s = jnp.where(col >= C_PAST_ROWS - qi * ATT_QB, s, NEG_INF)
        m = s.max(-1, keepdims=True)
        p = jnp.exp(s - m)
        o = _dg(p.astype(BF16), v_ref[pl.ds(start, ATT_WIN), :], NN) / p.sum(-1, keepdims=True)
        o_ref[pl.ds(r0, ATT_QB), :] = jnp.where(m0, o[:ATT_QB], o[ATT_QB:]).astype(o_ref.dtype)
        return carry

    lax.fori_loop(0, n_qb, body, 0, unroll=True)


def _attention_prompt(qkv, bias):
    Bsz, Tp, _ = qkv.shape
    T = Tp - C_PAST_ROWS
    nhp = C_HEADS // 2
    nblk = T // ATT_QB
    n_qb = next(n for n in (4, 2, 1) if nblk % n == 0)
    rows = n_qb * ATT_QB
    skip = C_PAST_ROWS // rows
    return pl.pallas_call(
        functools.partial(_att_prompt_kernel, n_qb),
        grid=(Bsz, nhp, T // rows),
        in_specs=[pl.BlockSpec((None, rows, LANE), lambda b, hp, g: (b, g + skip, hp)),
                  pl.BlockSpec((None, Tp, LANE), lambda b, hp, g: (b, 0, nhp + hp)),
                  pl.BlockSpec((None, Tp, LANE), lambda b, hp, g: (b, 0, 2 * nhp + hp)),
                  pl.BlockSpec((None, 2 * ATT_QB, ATT_WIN), lambda b, hp, g: (hp, 0, 0))],
        out_specs=pl.BlockSpec((None, rows, LANE), lambda b, hp, g: (b, g, hp)),
        out_shape=jax.ShapeDtypeStruct((Bsz, T, D_MODEL), BF16),
        compiler_params=_params("parallel", "parallel", "arbitrary"),
    )(qkv, qkv, qkv, bias)


def _att_sample_kernel(q_ref, kn_ref, vn_ref, kc_ref, vc_ref, bias_ref, o_ref):
    R = kc_ref.shape[1]
    for hh in range(2):
        hs = slice(hh * C_HEAD_DIM, (hh + 1) * C_HEAD_DIM)
        q = (q_ref[:, hs] * ATT_SCALE).astype(BF16)
        s_c = _dg(q, kc_ref[hh].astype(BF16), NT) + bias_ref[hh, :, 0:R]
        s_n = _dg(q, kn_ref[:, hs].astype(BF16), NT) + bias_ref[hh, :, R:]
        m = jnp.maximum(s_c.max(-1, keepdims=True), s_n.max(-1, keepdims=True))
        p_c = jnp.exp(s_c - m)
        p_n = jnp.exp(s_n - m)
        den = p_c.sum(-1, keepdims=True) + p_n.sum(-1, keepdims=True)
        acc = _dg(p_c.astype(BF16), vc_ref[hh].astype(BF16), NN) + _dg(p_n.astype(BF16), vn_ref[:, hs].astype(BF16), NN)
        o_ref[:, hs] = acc / den


def _attention_sample(qkv, cache_k, cache_v, bias):
    Bsz, T, _ = qkv.shape
    R = cache_k.shape[2]
    nhp = C_HEADS // 2
    cache = pl.BlockSpec((None, 2, R, C_HEAD_DIM), lambda b, hp: (b, hp, 0, 0))
    return pl.pallas_call(
        _att_sample_kernel,
        grid=(Bsz, nhp),
        in_specs=[pl.BlockSpec((None, T, LANE), lambda b, hp: (b, 0, hp)),
                  pl.BlockSpec((None, T, LANE), lambda b, hp: (b, 0, nhp + hp)),
                  pl.BlockSpec((None, T, LANE), lambda b, hp: (b, 0, 2 * nhp + hp)),
                  cache, cache,
                  pl.BlockSpec((2, T, R + T), lambda b, hp: (hp, 0, 0))],
        out_specs=pl.BlockSpec((None, T, LANE), lambda b, hp: (b, 0, hp)),
        out_shape=jax.ShapeDtypeStruct((Bsz, T, D_MODEL), F32),
        compiler_params=_params("parallel", "parallel"),
    )(qkv, qkv, qkv, cache_k, cache_v, bias)


def _rel_bias_tile(table, n_q, n_k, q_offset):
    period = n_q + n_k - 1
    m = np.arange(period)
    d = np.where(m < n_k, m, m - period)
    idx = np.clip(q_offset - d, -C_REL_CLIP, C_REL_CLIP) + C_REL_CLIP
    diag = table[:, idx]
    flat = jnp.tile(diag, (1, n_q))[:, :n_q * (period - 1)]
    return flat.reshape(table.shape[0], n_q, period - 1)[:, :, :n_k]


TM = 512


def _trunk(x, wkv0, shift0, gla0, cache_k, cache_v, P):
    Bsz, T, _ = x.shape
    M = Bsz * T
    xf = x.reshape(M, D_MODEL)
    wkv_o, shift_o, gla_o, k_o, v_o = [], [], [], [], []
    for layer in range(DEPTH):
        if layer % 2 == 0:
            e = layer // 2
            pa = _matmul(xf, P['w_in_a'][e], TM, A_PROJ).reshape(Bsz, T, A_PROJ)
            pb = _matmul(xf, P['w_in_b'][e], TM, B_PROJ_PAD).reshape(Bsz, T, B_PROJ_PAD)
            ya, wkv = _rwkv_mix(pa, shift0[e], wkv0[e], {k: v[e] for k, v in P['a'].items()})
            yb, gs = _gla_mix(pb, gla0[e], {k: v[e] for k, v in P['b'].items()})
            wkv_o.append(wkv)
            shift_o.append(pa[:, -1])
            gla_o.append(gs)
            pairs = [(ya.reshape(M, A_WIDTH), P['w_out_a'][e]), (yb.reshape(M, B_VWIDTH), P['w_out_b'][e])]
        else:
            o = layer // 2
            if cache_k is None:
                x3 = xf.reshape(Bsz, T, D_MODEL)
                att = _attention_prompt(_qkv_padded(x3, P['c_w_qkv_scaled'][o]), P['c_bias_prompt'][o])
                keep = min(C_PAST_ROWS, T)
                rows = _matmul(x3[:, T - keep:].reshape(Bsz * keep, D_MODEL), P['c_w_qkv'][o][:, D_MODEL:],
                               TM, D_MODEL)
                rows = rows.reshape(Bsz, keep, 2, C_HEADS, C_HEAD_DIM)
            else:
                qkv = _matmul(xf, P['c_w_qkv'][o], TM, D_MODEL).reshape(Bsz, T, 3 * D_MODEL)
                att = _attention_sample(qkv, cache_k[o], cache_v[o], P['c_bias_sample'][o])
                rows = qkv.reshape(Bsz, T, 3, C_HEADS, C_HEAD_DIM)[:, :, 1:]
            k_o.append(rows[:, :, 0].transpose(0, 2, 1, 3))
            v_o.append(rows[:, :, 1].transpose(0, 2, 1, 3))
            pairs = [(att.reshape(M, D_MODEL), P['c_w_o'][o])]
        xf = _matmul_ln(pairs, xf, P['ln1_w'][layer], P['ln1_b'][layer], TM)
        xf = _ffn_ln(xf, P['ffn_w_in'][layer], P['ffn_w_out'][layer], P['ln2_w'][layer], P['ln2_b'][layer], TM)
    return (xf.reshape(Bsz, T, D_MODEL), jnp.stack(wkv_o), jnp.stack(shift_o), jnp.stack(gla_o),
            jnp.stack(k_o), jnp.stack(v_o))


def _prepare(w_in_mix, a_mu, a_w0, a_w2, a_a0, a_a2, a_g2, a_k_k, a_k_a, a_r_k, a_ln_w, a_ln_b,
             b_alpha_up, b_alpha_bias, b_norm_w, w_out_mix, c_w_qkv, c_rel_bias, c_w_o,
             ln1_w, ln1_b, ln2_w, ln2_b, ffn_w_in, ffn_w_out, sample_len, cache_rows):
    kw, vw = B_KWIDTH, B_VWIDTH
    wb = w_in_mix[:, :, A_PROJ:]
    main = jnp.concatenate([wb[:, :, :2 * kw + vw], wb[:, :, 2 * kw + vw + B_GATE_LORA:]], axis=-1)
    lora = jnp.pad(wb[:, :, 2 * kw + vw:2 * kw + vw + B_GATE_LORA], ((0, 0), (0, 0), (0, B_LORA_PAD - B_GATE_LORA)))
    row3 = lambda t: t.reshape(t.shape[0], 1, -1)
    bias_prompt = jnp.stack([_rel_bias_tile(t, ATT_QB, ATT_WIN, C_PAST_ROWS) for t in c_rel_bias])
    rel = np.arange(ATT_WIN)[None, :] // CHUNK - np.arange(ATT_QB)[:, None] // CHUNK
    band = (rel >= 0) & (rel <= C_PAST_CHUNKS)
    bias_prompt = jnp.where(band, bias_prompt, NEG_INF)
    q_scale = jnp.concatenate([jnp.full((D_MODEL,), ATT_SCALE, F32), jnp.ones((2 * D_MODEL,), F32)])
    P = {
        'w_in_a': w_in_mix[:, :, :A_PROJ].astype(BF16),
        'w_in_b': jnp.concatenate([main, lora], axis=-1).astype(BF16),
        'a': dict(mu=row3(a_mu), w0=row3(a_w0), w2=a_w2, a0=row3(a_a0), a2=a_a2, g2=a_g2, k_k=row3(a_k_k),
                  k_a=row3(a_k_a), r_k=row3(a_r_k), ln_w=row3(a_ln_w), ln_b=row3(a_ln_b)),
        'b': dict(up=jnp.pad(b_alpha_up, ((0, 0), (0, B_LORA_PAD - B_GATE_LORA), (0, 0))),
                  bias=row3(b_alpha_bias), nw=row3(b_norm_w)),
        'w_out_a': w_out_mix[:, :A_WIDTH].astype(BF16),
        'w_out_b': w_out_mix[:, A_WIDTH:].astype(BF16),
        'c_w_qkv': c_w_qkv.astype(BF16),
        'c_w_qkv_scaled': (c_w_qkv * q_scale).astype(BF16),
        'c_w_o': c_w_o.astype(BF16),
        'c_bias_prompt': bias_prompt.reshape(bias_prompt.shape[0], C_HEADS // 2, 2 * ATT_QB, ATT_WIN),
        'c_bias_sample': jnp.stack([_rel_bias_tile(t, sample_len, cache_rows + sample_len, cache_rows)
                                    for t in c_rel_bias]),
        'ln1_w': row3(ln1_w), 'ln1_b': row3(ln1_b), 'ln2_w': row3(ln2_w), 'ln2_b': row3(ln2_b),
        'ffn_w_in': ffn_w_in.astype(BF16),
        'ffn_w_out': ffn_w_out.astype(BF16),
    }
    return P


def kernel(x_prompt, x_sample, state_a_wkv, state_a_shift, state_b_gla, cache_c_k, cache_c_v, w_in_mix, a_mu, a_w0, a_w2, a_a0, a_a2, a_g2, a_k_k, a_k_a, a_r_k, a_ln_w, a_ln_b, b_alpha_up, b_alpha_bias, b_norm_w, w_out_mix, c_w_qkv, c_rel_bias, c_w_o, ln1_w, ln1_b, ln2_w, ln2_b, ffn_w_in, ffn_w_out):
    P = _prepare(w_in_mix, a_mu, a_w0, a_w2, a_a0, a_a2, a_g2, a_k_k, a_k_a, a_r_k, a_ln_w, a_ln_b,
                 b_alpha_up, b_alpha_bias, b_norm_w, w_out_mix, c_w_qkv, c_rel_bias, c_w_o,
                 ln1_w, ln1_b, ln2_w, ln2_b, ffn_w_in, ffn_w_out, x_sample.shape[1], cache_c_k.shape[3])
    bp = x_prompt.shape[0]
    dt = x_prompt.dtype
    n_even = state_a_wkv.shape[0]
    wkv_zero = jnp.zeros((n_even, bp, A_HEADS, A_HEAD_DIM, A_HEAD_DIM), dt)
    shift_zero = jnp.zeros((n_even, bp, A_PROJ), dt)
    gla_zero = jnp.zeros((n_even, bp, B_HEADS, B_KEY_DIM, B_VAL_DIM), dt)
    y_prompt, p_wkv, p_shift, p_gla, p_k, p_v = _trunk(x_prompt, wkv_zero, shift_zero, gla_zero, None, None, P)
    y_sample, s_wkv, s_shift, s_gla, s_k, s_v = _trunk(
        x_sample, state_a_wkv, state_a_shift, state_b_gla, cache_c_k, cache_c_v, P)
    return (y_prompt, y_sample, p_wkv, p_shift, p_gla, p_k, p_v, s_wkv, s_shift, s_gla, s_k, s_v)
```

```python
import functools

import jax
import jax.numpy as jnp
import numpy as np
from jax import lax
from jax.experimental import pallas as pl
from jax.experimental.pallas import tpu as pltpu

F32 = jnp.float32
BF16 = jnp.bfloat16

D_MODEL = 1024
DEPTH = 4
CHUNK = 64
A_WIDTH = 512
A_HEAD_DIM = 64
A_HEADS = 8
A_DECAY_LORA = 64
A_ICL_LORA = 64
A_GATE_LORA = 128
A_PROJ = 3 * A_WIDTH + A_DECAY_LORA + A_ICL_LORA + A_GATE_LORA
A_NORM_EPS = 64e-5
B_VWIDTH = 512
B_KWIDTH = 256
B_HEADS = 4
B_KEY_DIM = 64
B_VAL_DIM = 128
B_GATE_LORA = 16
B_GATE_NORM = 16.0
B_LORA_PAD = 128
B_PROJ_PAD = 2 * B_KWIDTH + 2 * B_VWIDTH + B_LORA_PAD
C_HEAD_DIM = 64
C_HEADS = 16
C_PAST_CHUNKS = 8
C_PAST_ROWS = C_PAST_CHUNKS * CHUNK
C_REL_CLIP = 128
FFN_HIDDEN = 2816
LN_EPS = 1e-5
DEEPNORM_ALPHA = (2.0 * DEPTH) ** 0.25
NEG_INF = -1e30

LANE = 128
PAIR = LANE
HALF = PAIR // 2
A_PAIRS = A_HEADS // 2
B_PAIRS = B_HEADS // 2
VMEM_LIMIT = 56 * 1024 * 1024

NN = (((1,), (0,)), ((), ()))
NT = (((1,), (1,)), ((), ()))
TN = (((0,), (0,)), ((), ()))


def _params(*sem):
    return pltpu.CompilerParams(dimension_semantics=sem, vmem_limit_bytes=VMEM_LIMIT)


def _dg(a, b, dn):
    return lax.dot_general(a, b, dn, preferred_element_type=F32)


def _dot1(a, b, dn=NN):
    return _dg(a.astype(BF16), b.astype(BF16), dn)


def _split(a):
    hi = a.astype(BF16)
    lo = (a - hi.astype(F32)).astype(BF16)
    return hi, lo


def _dot3s(a, b, dn=NN):
    return _dg(a[0], b[0], dn) + _dg(a[0], b[1], dn) + _dg(a[1], b[0], dn)


def _dot3(a, b, dn=NN):
    return _dot3s(_split(a), _split(b), dn)


def _dot_xl(a_exact, b, dn=NN):
    bh, bl = _split(b)
    return _dg(a_exact, bh, dn) + _dg(a_exact, bl, dn)


def _dot_xr(a, b_exact, dn=NN):
    ah, al = _split(a)
    return _dg(ah, b_exact, dn) + _dg(al, b_exact, dn)


def _ln(z, w, b):
    mu = jnp.mean(z, -1, keepdims=True)
    d = z - mu
    var = jnp.mean(d * d, -1, keepdims=True)
    return d * lax.rsqrt(var + LN_EPS) * w + b


def _sigmoid(x):
    return 1.0 / (1.0 + jnp.exp(-x))


def _softplus(x):
    return jnp.maximum(x, 0.0) + jnp.log(1.0 + jnp.exp(-jnp.abs(x)))


def _stack_masked(x, m0):
    return jnp.concatenate([jnp.where(m0, x, 0.0), jnp.where(m0, 0.0, x)], axis=0)


def _fold(x):
    n = x.shape[0] // 2
    return x[:n] + x[n:]


def _mm_kernel(x_ref, w_ref, o_ref):
    o_ref[...] = jnp.dot(x_ref[...].astype(BF16), w_ref[...], preferred_element_type=F32)


def _matmul(x, w, tm, tn):
    M, K = x.shape
    N = w.shape[1]
    tm = min(tm, M)
    return pl.pallas_call(
        _mm_kernel,
        grid=(M // tm, N // tn),
        in_specs=[pl.BlockSpec((tm, K), lambda i, j: (i, 0)),
                  pl.BlockSpec((K, tn), lambda i, j: (0, j))],
        out_specs=pl.BlockSpec((tm, tn), lambda i, j: (i, j)),
        out_shape=jax.ShapeDtypeStruct((M, N), F32),
        compiler_params=_params("parallel", "arbitrary"),
    )(x, w)


def _mix_ffn_kernel(n_pairs, *refs):
    res_ref, lw1_ref, lb1_ref, wg_ref, wu_ref, wo_ref, lw2_ref, lb2_ref, o_ref = refs[2 * n_pairs:]
    acc = None
    for p in range(n_pairs):
        part = jnp.dot(refs[2 * p][...].astype(BF16), refs[2 * p + 1][...], preferred_element_type=F32)
        acc = part if acc is None else acc + part
    x = _ln(DEEPNORM_ALPHA * res_ref[...] + acc, lw1_ref[...], lb1_ref[...])
    xb = x.astype(BF16)
    g = jnp.dot(xb, wg_ref[...], preferred_element_type=F32)
    u = jnp.dot(xb, wu_ref[...], preferred_element_type=F32)
    act = (g * _sigmoid(g) * u).astype(BF16)
    y = jnp.dot(act, wo_ref[...], preferred_element_type=F32)
    o_ref[...] = _ln(DEEPNORM_ALPHA * x + y, lw2_ref[...], lb2_ref[...])


def _mix_ffn(pairs, res, lw1, lb1, w_in, w_out, lw2, lb2, tm):
    M = res.shape[0]
    tm = min(tm, M)
    resident = lambda shape, idx: pl.BlockSpec(shape, idx, pipeline_mode=pl.Buffered(1))
    vec = pl.BlockSpec((1, D_MODEL), lambda i: (0, 0))
    in_specs, args = [], []
    for a, w in pairs:
        K = a.shape[1]
        in_specs += [pl.BlockSpec((tm, K), lambda i: (i, 0)), resident((K, D_MODEL), lambda i: (0, 0))]
        args += [a, w]
    in_specs += [pl.BlockSpec((tm, D_MODEL), lambda i: (i, 0)), vec, vec,
                 resident((D_MODEL, FFN_HIDDEN), lambda i: (0, 0)),
                 resident((D_MODEL, FFN_HIDDEN), lambda i: (0, 1)),
                 resident((FFN_HIDDEN, D_MODEL), lambda i: (0, 0)), vec, vec]
    return pl.pallas_call(
        functools.partial(_mix_ffn_kernel, len(pairs)),
        grid=(M // tm,),
        in_specs=in_specs,
        out_specs=pl.BlockSpec((tm, D_MODEL), lambda i: (i, 0)),
        out_shape=jax.ShapeDtypeStruct((M, D_MODEL), F32),
        compiler_params=_params("parallel"),
    )(*args, res, lw1, lb1, w_in, w_in, w_out, lw2, lb2)


def _inv_unit_lower(ns, eye, nilpotency):
    size = eye.shape[0]
    xs = [eye - n for n in ns]
    ps = [_dot1(n, n) for n in ns]
    k = 2
    while k < nilpotency:
        k *= 2
        if k < nilpotency:
            rs = [_dot1(p, jnp.concatenate([x, p], axis=1)) for x, p in zip(xs, ps)]
            xs = [x + r[:, :size] for x, r in zip(xs, rs)]
            ps = [r[:, size:] for r in rs]
        else:
            xs = [x + _dot1(p, x) for x, p in zip(xs, ps)]
    return xs


def _segsum(x, seg):
    n = x.shape[1] // PAIR
    rows = x.shape[0]
    xs = jnp.concatenate([x[:, j * PAIR:(j + 1) * PAIR] for j in range(n)], axis=0)
    s = _dot_xr(xs, seg)
    return jnp.concatenate([s[j * rows:(j + 1) * rows] for j in range(n)], axis=1)


def _head_mask():
    hr = lax.broadcasted_iota(jnp.int32, (PAIR, PAIR), 0) // A_HEAD_DIM
    hc = lax.broadcasted_iota(jnp.int32, (PAIR, PAIR), 1) // A_HEAD_DIM
    return (hr == hc).astype(BF16)


def _rwkv_prep_kernel(n_chunks, pa_ref, prev_ref, shift_ref, mu_ref, w0_ref, w2_ref, a0_ref, a2_ref, g2_ref,
                      kk_ref, ka_ref, rk_ref,
                      mh_ref, ch_ref, qc_ref, oc_ref, g_ref, bonus_ref):
    i = pl.program_id(1)
    L = CHUNK
    pa = pa_ref[...]
    first = jnp.where(i == 0, shift_ref[...], prev_ref[7:8, :])
    rows = lax.broadcasted_iota(jnp.int32, pa.shape, 0)
    prev = jnp.where(rows == 0, first, pltpu.roll(pa, shift=1, axis=0))
    xs = pa + (prev - pa) * mu_ref[...]
    W = A_WIDTH
    r = xs[:, 0:W]
    k0 = xs[:, W:2 * W]
    v = xs[:, 2 * W:3 * W]
    xw = xs[:, 3 * W:3 * W + A_DECAY_LORA]
    xa = xs[:, 3 * W + A_DECAY_LORA:3 * W + A_DECAY_LORA + A_ICL_LORA]
    xg = xs[:, 3 * W + A_DECAY_LORA + A_ICL_LORA:]
    w = -_softplus(-(w0_ref[...] + _dot3(jnp.tanh(xw), w2_ref[...]))) - 0.5
    logw = -jnp.exp(w)
    a = _sigmoid(a0_ref[...] + _dot3(xa, a2_ref[...]))
    g_ref[...] = _dot3(_sigmoid(xg), g2_ref[...])
    seg = _head_mask()
    kk = k0 * kk_ref[...]
    kk = kk * lax.rsqrt(_segsum(kk * kk, seg) + 1e-12)
    k = k0 * (1.0 + (a - 1.0) * ka_ref[...])
    bonus_ref[...] = _segsum(r * k * rk_ref[...], seg) * v
    b = kk * a

    tri = (lax.broadcasted_iota(jnp.int32, (L, L), 0) >= lax.broadcasted_iota(jnp.int32, (L, L), 1)).astype(BF16)
    row = lax.broadcasted_iota(jnp.int32, (PAIR, PAIR), 0)
    col = lax.broadcasted_iota(jnp.int32, (PAIR, PAIR), 1)
    eye = (row == col).astype(F32)
    strict = row > col
    incl2 = (lax.broadcasted_iota(jnp.int32, (PAIR, 2 * PAIR), 0)
             >= lax.broadcasted_iota(jnp.int32, (PAIR, 2 * PAIR), 1) % PAIR)
    m0 = lax.broadcasted_iota(jnp.int32, (L, PAIR), 1) < HALF
    pairs = range(A_PAIRS)
    lanes = [slice(j * PAIR, (j + 1) * PAIR) for j in pairs]
    for c in range(n_chunks):
        sl = slice(c * L, (c + 1) * L)
        lw = logw[sl]
        cum = _dot_xl(tri, lw)
        c_last = cum[L - 1:L, :]
        e_neg = jnp.exp(-cum)
        e_rem = jnp.exp(c_last - cum)
        kkg = kk[sl] * jnp.exp(cum - lw)
        rg = r[sl] * jnp.exp(cum)
        kinv = k[sl] * e_neg
        binv = b[sl] * e_neg
        kd = k[sl] * e_rem
        bd = b[sl] * e_rem
        e_last = jnp.exp(c_last)
        vc = v[sl]
        sm = lambda x, p: _stack_masked(x[:, p], m0).astype(BF16)
        kkg_b = [sm(kkg, p) for p in lanes]
        v_b = [sm(vc, p) for p in lanes]
        bd_b = [sm(bd, p) for p in lanes]
        kd_b = [sm(kd, p) for p in lanes]
        rg_b = [sm(rg, p) for p in lanes]
        kinv_b = [sm(kinv, p) for p in lanes]
        binv_b = [sm(binv, p) for p in lanes]
        aa = [_dg(jnp.concatenate([kkg_b[j], rg_b[j]], axis=0), jnp.concatenate([binv_b[j], kinv_b[j]], axis=0), NT)
              for j in pairs]
        a_kb = [jnp.where(strict, x[:PAIR, :PAIR], 0.0) for x in aa]
        a_kk = [jnp.where(strict, x[:PAIR, PAIR:], 0.0).astype(BF16) for x in aa]
        a_rbk = [jnp.where(incl2, x[PAIR:, :], 0.0).astype(BF16) for x in aa]
        t = _inv_unit_lower(a_kb, eye, L)
        av = [_dg(a_kk[j], v_b[j], NN).astype(BF16) for j in pairs]
        wu = [_dg(t[j].astype(BF16), jnp.concatenate([kkg_b[j], av[j]], axis=1), NN) for j in pairs]
        wu_b = [x.astype(BF16) for x in wu]
        mc = [_dg(bd_b[j], wu_b[j], TN) for j in pairs]
        kv = [_dg(kd_b[j], v_b[j], TN) for j in pairs]
        rhs = [jnp.concatenate([jnp.concatenate([wu_b[j][:, :PAIR], (-wu[j][:, PAIR:]).astype(BF16)], axis=1),
                                jnp.concatenate([jnp.zeros_like(v_b[j]), v_b[j]], axis=1)], axis=0) for j in pairs]
        qo = [_dg(a_rbk[j], rhs[j], NN) for j in pairs]
        for j in pairs:
            mh_ref[c, j] = (eye * e_last[:, lanes[j]] - mc[j][:, :PAIR]).astype(BF16)
            ch_ref[c, j] = kv[j] - mc[j][:, PAIR:]
            qc_ref[sl, lanes[j]] = (rg[:, lanes[j]] - _fold(qo[j][:, :PAIR])).astype(BF16)
            oc_ref[sl, lanes[j]] = _fold(qo[j][:, PAIR:])


def _rwkv_scan_kernel(n_seq, n_chunks, mh_ref, ch_ref, qc_ref, oc_ref, g_ref, bonus_ref, h0_ref, lnw_ref, lnb_ref,
                      y_ref, hout_ref, h_ref):
    i = pl.program_id(1)
    L = CHUNK

    @pl.when(i == 0)
    def _():
        h_ref[...] = h0_ref[...]

    seg = _head_mask()
    inv_n = 1.0 / A_HEAD_DIM
    lanes = [slice(j * PAIR, (j + 1) * PAIR) for j in range(A_PAIRS)]
    chains = [(s, j) for s in range(n_seq) for j in range(A_PAIRS)]
    hb = [h_ref[s, j] for s, j in chains]
    for c in range(n_chunks):
        sl = slice(c * L, (c + 1) * L)
        hb_b = [x.astype(BF16) for x in hb]
        o = [_dg(qc_ref[s, sl, lanes[j]], hb_b[n], NN) + oc_ref[s, sl, lanes[j]] for n, (s, j) in enumerate(chains)]
        hb = [_dg(mh_ref[s, c, j], hb_b[n], NN) + ch_ref[s, c, j] for n, (s, j) in enumerate(chains)]
        for s in range(n_seq):
            os_ = jnp.concatenate(o[s * A_PAIRS:(s + 1) * A_PAIRS], axis=1)
            mean = _segsum(os_, seg) * inv_n
            d = os_ - mean
            var = _segsum(d * d, seg) * inv_n
            on = d * lax.rsqrt(var + A_NORM_EPS) * lnw_ref[...] + lnb_ref[...]
            y_ref[s, sl, :] = ((on + bonus_ref[s, sl, :]) * g_ref[s, sl, :]).astype(y_ref.dtype)
    for n, (s, j) in enumerate(chains):
        h_ref[s, j] = hb[n]
    hout_ref[...] = h_ref[...]


def _rwkv_mix(pa, shift0, wkv0, prm):
    Bsz, T, _ = pa.shape
    n1 = next(n for n in (4, 2, 1) if (T // CHUNK) % n == 0)
    R1 = n1 * CHUNK
    W = A_WIDTH
    nblk = T // R1
    row = lambda n: pl.BlockSpec((1, n), lambda b, i: (0, 0))
    full = lambda s: pl.BlockSpec(s, lambda b, i: (0,) * len(s))
    act = lambda r, n: pl.BlockSpec((None, r, n), lambda b, i: (b, i, 0))
    mat = lambda n: pl.BlockSpec((None, n, A_PAIRS, PAIR, PAIR), lambda b, i: (b, i, 0, 0, 0))
    sub = R1 // 8
    mats = lambda dt: jax.ShapeDtypeStruct((Bsz, T // CHUNK, A_PAIRS, PAIR, PAIR), dt)
    acts = lambda dt: jax.ShapeDtypeStruct((Bsz, T, W), dt)
    mh, ch, qc, oc, g, bonus = pl.pallas_call(
        functools.partial(_rwkv_prep_kernel, n1),
        grid=(Bsz, nblk),
        in_specs=[act(R1, A_PROJ),
                  pl.BlockSpec((None, 8, A_PROJ), lambda b, i: (b, jnp.maximum(i * sub - 1, 0), 0)),
                  pl.BlockSpec((None, 1, A_PROJ), lambda b, i: (b, 0, 0)),
                  row(A_PROJ), row(W), full((A_DECAY_LORA, W)), row(W), full((A_ICL_LORA, W)),
                  full((A_GATE_LORA, W)), row(W), row(W), row(W)],
        out_specs=[mat(n1), mat(n1), act(R1, W), act(R1, W), act(R1, W), act(R1, W)],
        out_shape=[mats(BF16), mats(F32), acts(BF16), acts(F32), acts(F32), acts(F32)],
        compiler_params=_params("parallel", "parallel"),
    )(pa, pa, shift0[:, None, :], prm['mu'], prm['w0'], prm['w2'], prm['a0'], prm['a2'], prm['g2'],
      prm['k_k'], prm['k_a'], prm['r_k'])

    n2 = min(4, T // CHUNK)
    R2 = n2 * CHUNK
    hh = wkv0.transpose(0, 1, 3, 2).reshape(Bsz, A_PAIRS, 2, A_HEAD_DIM, A_HEAD_DIM)
    zero = jnp.zeros_like(hh[:, :, 0])
    h0 = jnp.concatenate([jnp.concatenate([hh[:, :, 0], zero], axis=-1),
                          jnp.concatenate([zero, hh[:, :, 1]], axis=-1)], axis=-2)
    ns = 2 if Bsz % 2 == 0 else 1
    state = pl.BlockSpec((ns, A_PAIRS, PAIR, PAIR), lambda b, i: (b, 0, 0, 0))
    act2 = lambda r, n: pl.BlockSpec((ns, r, n), lambda b, i: (b, i, 0))
    mat2 = lambda n: pl.BlockSpec((ns, n, A_PAIRS, PAIR, PAIR), lambda b, i: (b, i, 0, 0, 0))
    y, hout = pl.pallas_call(
        functools.partial(_rwkv_scan_kernel, ns, n2),
        grid=(Bsz // ns, T // R2),
        in_specs=[mat2(n2), mat2(n2), act2(R2, W), act2(R2, W), act2(R2, W), act2(R2, W), state,
                  row(W), row(W)],
        out_specs=[act2(R2, W), state],
        out_shape=[acts(BF16), jax.ShapeDtypeStruct((Bsz, A_PAIRS, PAIR, PAIR), F32)],
        scratch_shapes=[pltpu.VMEM((ns, A_PAIRS, PAIR, PAIR), F32)],
        compiler_params=_params("parallel", "arbitrary"),
    )(mh, ch, qc, oc, g, bonus, h0, prm['ln_w'], prm['ln_b'])
    hd = jnp.stack([hout[:, :, :HALF, :HALF], hout[:, :, HALF:, HALF:]], axis=2)
    wkv = hd.reshape(Bsz, A_HEADS, A_HEAD_DIM, A_HEAD_DIM).transpose(0, 1, 3, 2)
    return y, wkv


def _gla_kernel(n_seq, n_chunks, pb_ref, s0_ref, up_ref, bias_ref, nw_ref, y_ref, sout_ref, s_ref):
    i = pl.program_id(1)
    L = CHUNK

    @pl.when(i == 0)
    def _():
        s_ref[...] = s0_ref[...]

    kw, vw = B_KWIDTH, B_VWIDTH
    incl = lax.broadcasted_iota(jnp.int32, (L, L), 0) >= lax.broadcasted_iota(jnp.int32, (L, L), 1)
    tri = incl.astype(BF16)
    m0 = lax.broadcasted_iota(jnp.int32, (L, PAIR), 1) < HALF
    top = lax.broadcasted_iota(jnp.int32, (PAIR, B_VAL_DIM), 0) < HALF
    scale = B_KEY_DIM ** -0.5
    pairs = range(B_PAIRS)
    heads = range(B_HEADS)
    pl_ = lambda h: slice((h // 2) * PAIR, (h // 2 + 1) * PAIR)
    states = [[s_ref[s, j] for j in pairs] for s in range(n_seq)]
    for c in range(n_chunks):
        for s in range(n_seq):
            st = states[s]
            sl = slice(c * L, (c + 1) * L)
            q = pb_ref[s, sl, 0:kw] * scale
            k = pb_ref[s, sl, kw:2 * kw]
            v = pb_ref[s, sl, 2 * kw:2 * kw + vw]
            rg = pb_ref[s, sl, 2 * kw + vw:2 * kw + 2 * vw]
            xa = pb_ref[s, sl, 2 * kw + 2 * vw:]
            z = _dot3(xa, up_ref[...]) + bias_ref[...]
            log_a = -_softplus(-z) * (1.0 / B_GATE_NORM)
            bc = _dot_xl(tri, log_a)
            b_last = bc[L - 1:L, :]
            q_dec = q * jnp.exp(bc)
            k_inv = (k * jnp.exp(-bc)).astype(BF16)
            k_state = (k * jnp.exp(b_last - bc)).astype(BF16)
            e_last_t = jnp.broadcast_to(jnp.exp(b_last), (LANE, kw)).T
            vb = [v[:, h * B_VAL_DIM:(h + 1) * B_VAL_DIM].astype(BF16) for h in heads]
            qm = [jnp.where(m0 if h % 2 == 0 else ~m0, q_dec[:, pl_(h)], 0.0).astype(BF16) for h in heads]
            scores = [jnp.where(incl, _dg(qm[h], k_inv[:, pl_(h)], NT), 0.0).astype(BF16) for h in heads]
            st_b = [x.astype(BF16) for x in st]
            o = [_dg(qm[h], st_b[h // 2], NN) + _dg(scores[h], vb[h], NN) for h in heads]
            upd = [_dg(k_state[:, pl_(h)], vb[h], TN) for h in heads]
            states[s] = [st[j] * e_last_t[j * PAIR:(j + 1) * PAIR, :] + jnp.where(top, upd[2 * j], upd[2 * j + 1])
                         for j in pairs]
            for h in heads:
                vs = slice(h * B_VAL_DIM, (h + 1) * B_VAL_DIM)
                oh = o[h] * lax.rsqrt(jnp.mean(o[h] * o[h], -1, keepdims=True) + LN_EPS) * nw_ref[...]
                rg_h = rg[:, vs]
                y_ref[s, sl, vs] = (oh * (rg_h * _sigmoid(rg_h))).astype(y_ref.dtype)
    for s in range(n_seq):
        for j in pairs:
            s_ref[s, j] = states[s][j]
    sout_ref[...] = s_ref[...]


def _gla_mix(pb, s0, prm):
    Bsz, T, _ = pb.shape
    n = min(4, T // CHUNK)
    Rr = n * CHUNK
    ns = 2 if Bsz % 2 == 0 else 1
    st = pl.BlockSpec((ns, B_PAIRS, PAIR, B_VAL_DIM), lambda b, i: (b, 0, 0, 0))
    y, s = pl.pallas_call(
        functools.partial(_gla_kernel, ns, n),
        grid=(Bsz // ns, T // Rr),
        in_specs=[pl.BlockSpec((ns, Rr, B_PROJ_PAD), lambda b, i: (b, i, 0)), st,
                  pl.BlockSpec((B_LORA_PAD, B_KWIDTH), lambda b, i: (0, 0)),
                  pl.BlockSpec((1, B_KWIDTH), lambda b, i: (0, 0)),
                  pl.BlockSpec((1, B_VAL_DIM), lambda b, i: (0, 0))],
        out_specs=[pl.BlockSpec((ns, Rr, B_VWIDTH), lambda b, i: (b, i, 0)), st],
        out_shape=[jax.ShapeDtypeStruct((Bsz, T, B_VWIDTH), BF16),
                   jax.ShapeDtypeStruct((Bsz, B_PAIRS, PAIR, B_VAL_DIM), F32)],
        scratch_shapes=[pltpu.VMEM((ns, B_PAIRS, PAIR, B_VAL_DIM), F32)],
        compiler_params=_params("parallel", "arbitrary"),
    )(pb, s0.reshape(Bsz, B_PAIRS, PAIR, B_VAL_DIM), prm['up'], prm['bias'], prm['nw'])
    return y, s.reshape(Bsz, B_HEADS, B_KEY_DIM, B_VAL_DIM)


ATT_QB = 2 * CHUNK
ATT_KB = LANE
ATT_NKB = (C_PAST_ROWS + ATT_QB) // ATT_KB
ATT_SCALE = C_HEAD_DIM ** -0.5


ATT_WIN = C_PAST_ROWS + ATT_QB


def _qkv_pad_kernel(n_pad, x_ref, w_ref, o_ref):
    r = pl.program_id(1)

    @pl.when(r < n_pad)
    def _():
        o_ref[...] = jnp.zeros_like(o_ref)

    @pl.when(r >= n_pad)
    def _():
        o_ref[...] = jnp.dot(x_ref[...].astype(BF16), w_ref[...], preferred_element_type=F32).astype(BF16)


def _qkv_padded(x, w):
    Bsz, T, D = x.shape
    N = w.shape[1]
    tm = next(t for t in (512, 256, 128) if T % t == 0)
    n_pad = C_PAST_ROWS // tm
    return pl.pallas_call(
        functools.partial(_qkv_pad_kernel, n_pad),
        grid=(Bsz, T // tm + n_pad),
        in_specs=[pl.BlockSpec((None, tm, D), lambda b, r: (b, jnp.maximum(r - n_pad, 0), 0)),
                  pl.BlockSpec((D, N), lambda b, r: (0, 0), pipeline_mode=pl.Buffered(1))],
        out_specs=pl.BlockSpec((None, tm, N), lambda b, r: (b, r, 0)),
        out_shape=jax.ShapeDtypeStruct((Bsz, C_PAST_ROWS + T, N), BF16),
        compiler_params=_params("parallel", "arbitrary"),
    )(x, w)


def _att_prompt_kernel(n_qb, q_ref, k_ref, v_ref, bias_ref, o_ref):
    g = pl.program_id(2)
    m0 = lax.broadcasted_iota(jnp.int32, (ATT_QB, PAIR), 1) < HALF
    col = lax.broadcasted_iota(jnp.int32, (2 * ATT_QB, ATT_WIN), 1)

    def run(window_has_padding):
        def body(t, carry):
            qi = g * n_qb + t
            r0 = pl.multiple_of(t * ATT_QB, ATT_QB)
            start = pl.multiple_of(qi * ATT_QB, ATT_QB)
            q = _stack_masked(q_ref[pl.ds(r0, ATT_QB), :], m0)
            s = _dg(q, k_ref[pl.ds(start, ATT_WIN), :], NT) + bias_ref[...]
            if window_has_padding:
                s = jnp.where(col >= C_PAST_ROWS - qi * ATT_QB, s, NEG_INF)
            m = s.max(-1, keepdims=True)
            p = jnp.exp(s - m)
            o = _dg(p.astype(BF16), v_ref[pl.ds(start, ATT_WIN), :], NN) / p.sum(-1, keepdims=True)
            o_ref[pl.ds(r0, ATT_QB), :] = jnp.where(m0, o[:ATT_QB], o[ATT_QB:]).astype(o_ref.dtype)
            return carry

        lax.fori_loop(0, n_qb, body, 0, unroll=True)

    first = g * (n_qb * ATT_QB) < C_PAST_ROWS
    pl.when(first)(lambda: run(True))
    pl.when(jnp.logical_not(first))(lambda: run(False))


def _attention_prompt(qkv, bias):
    Bsz, Tp, _ = qkv.shape
    T = Tp - C_PAST_ROWS
    nhp = C_HEADS // 2
    nblk = T // ATT_QB
    n_qb = next(n for n in (4, 2, 1) if nblk % n == 0)
    rows = n_qb * ATT_QB
    skip = C_PAST_ROWS // rows
    return pl.pallas_call(
        functools.partial(_att_prompt_kernel, n_qb),
        grid=(Bsz, nhp, T // rows),
        in_specs=[pl.BlockSpec((None, rows, LANE), lambda b, hp, g: (b, g + skip, hp)),
                  pl.BlockSpec((None, Tp, LANE), lambda b, hp, g: (b, 0, nhp + hp)),
                  pl.BlockSpec((None, Tp, LANE), lambda b, hp, g: (b, 0, 2 * nhp + hp)),
                  pl.BlockSpec((None, 2 * ATT_QB, ATT_WIN), lambda b, hp, g: (hp, 0, 0))],
        out_specs=pl.BlockSpec((None, rows, LANE), lambda b, hp, g: (b, g, hp)),
        out_shape=jax.ShapeDtypeStruct((Bsz, T, D_MODEL), BF16),
        compiler_params=_params("parallel", "parallel", "arbitrary"),
    )(qkv, qkv, qkv, bias)


def _att_sample_kernel(q_ref, kn_ref, vn_ref, kc_ref, vc_ref, bias_ref, o_ref):
    R = kc_ref.shape[1]
    for hh in range(2):
        hs = slice(hh * C_HEAD_DIM, (hh + 1) * C_HEAD_DIM)
        q = (q_ref[:, hs] * ATT_SCALE).astype(BF16)
        s_c = _dg(q, kc_ref[hh].astype(BF16), NT) + bias_ref[hh, :, 0:R]
        s_n = _dg(q, kn_ref[:, hs].astype(BF16), NT) + bias_ref[hh, :, R:]
        m = jnp.maximum(s_c.max(-1, keepdims=True), s_n.max(-1, keepdims=True))
        p_c = jnp.exp(s_c - m)
        p_n = jnp.exp(s_n - m)
        den = p_c.sum(-1, keepdims=True) + p_n.sum(-1, keepdims=True)
        acc = _dg(p_c.astype(BF16), vc_ref[hh].astype(BF16), NN) + _dg(p_n.astype(BF16), vn_ref[:, hs].astype(BF16), NN)
        o_ref[:, hs] = acc / den


def _attention_sample(qkv, cache_k, cache_v, bias):
    Bsz, T, _ = qkv.shape
    R = cache_k.shape[2]
    nhp = C_HEADS // 2
    cache = pl.BlockSpec((None, 2, R, C_HEAD_DIM), lambda b, hp: (b, hp, 0, 0))
    return pl.pallas_call(
        _att_sample_kernel,
        grid=(Bsz, nhp),
        in_specs=[pl.BlockSpec((None, T, LANE), lambda b, hp: (b, 0, hp)),
                  pl.BlockSpec((None, T, LANE), lambda b, hp: (b, 0, nhp + hp)),
                  pl.BlockSpec((None, T, LANE), lambda b, hp: (b, 0, 2 * nhp + hp)),
                  cache, cache,
                  pl.BlockSpec((2, T, R + T), lambda b, hp: (hp, 0, 0))],
        out_specs=pl.BlockSpec((None, T, LANE), lambda b, hp: (b, 0, hp)),
        out_shape=jax.ShapeDtypeStruct((Bsz, T, D_MODEL), F32),
        compiler_params=_params("parallel", "parallel"),
    )(qkv, qkv, qkv, cache_k, cache_v, bias)


def _rel_bias_tile(table, n_q, n_k, q_offset):
    period = n_q + n_k - 1
    m = np.arange(period)
    d = np.where(m < n_k, m, m - period)
    idx = np.clip(q_offset - d, -C_REL_CLIP, C_REL_CLIP) + C_REL_CLIP
    diag = table[:, idx]
    flat = jnp.tile(diag, (1, n_q))[:, :n_q * (period - 1)]
    return flat.reshape(table.shape[0], n_q, period - 1)[:, :, :n_k]


TM = 512


def _trunk(x, wkv0, shift0, gla0, cache_k, cache_v, P):
    Bsz, T, _ = x.shape
    M = Bsz * T
    xf = x.reshape(M, D_MODEL)
    wkv_o, shift_o, gla_o, k_o, v_o = [], [], [], [], []
    for layer in range(DEPTH):
        if layer % 2 == 0:
            e = layer // 2
            pa = _matmul(xf, P['w_in_a'][e], TM, A_PROJ).reshape(Bsz, T, A_PROJ)
            pb = _matmul(xf, P['w_in_b'][e], TM, B_PROJ_PAD).reshape(Bsz, T, B_PROJ_PAD)
            ya, wkv = _rwkv_mix(pa, shift0[e], wkv0[e], {k: v[e] for k, v in P['a'].items()})
            yb, gs = _gla_mix(pb, gla0[e], {k: v[e] for k, v in P['b'].items()})
            wkv_o.append(wkv)
            shift_o.append(pa[:, -1])
            gla_o.append(gs)
            pairs = [(ya.reshape(M, A_WIDTH), P['w_out_a'][e]), (yb.reshape(M, B_VWIDTH), P['w_out_b'][e])]
        else:
            o = layer // 2
            if cache_k is None:
                x3 = xf.reshape(Bsz, T, D_MODEL)
                att = _attention_prompt(_qkv_padded(x3, P['c_w_qkv_scaled'][o]), P['c_bias_prompt'][o])
                keep = min(C_PAST_ROWS, T)
                rows = _matmul(x3[:, T - keep:].reshape(Bsz * keep, D_MODEL), P['c_w_qkv'][o][:, D_MODEL:],
                               TM, D_MODEL)
                rows = rows.reshape(Bsz, keep, 2, C_HEADS, C_HEAD_DIM)
            else:
                qkv = _matmul(xf, P['c_w_qkv'][o], TM, D_MODEL).reshape(Bsz, T, 3 * D_MODEL)
                att = _attention_sample(qkv, cache_k[o], cache_v[o], P['c_bias_sample'][o])
                rows = qkv.reshape(Bsz, T, 3, C_HEADS, C_HEAD_DIM)[:, :, 1:]
            k_o.append(rows[:, :, 0].transpose(0, 2, 1, 3))
            v_o.append(rows[:, :, 1].transpose(0, 2, 1, 3))
            pairs = [(att.reshape(M, D_MODEL), P['c_w_o'][o])]
        xf = _mix_ffn(pairs, xf, P['ln1_w'][layer], P['ln1_b'][layer], P['ffn_w_in'][layer], P['ffn_w_out'][layer],
                      P['ln2_w'][layer], P['ln2_b'][layer], TM)
    return (xf.reshape(Bsz, T, D_MODEL), jnp.stack(wkv_o), jnp.stack(shift_o), jnp.stack(gla_o),
            jnp.stack(k_o), jnp.stack(v_o))


def _prepare(w_in_mix, a_mu, a_w0, a_w2, a_a0, a_a2, a_g2, a_k_k, a_k_a, a_r_k, a_ln_w, a_ln_b,
             b_alpha_up, b_alpha_bias, b_norm_w, w_out_mix, c_w_qkv, c_rel_bias, c_w_o,
             ln1_w, ln1_b, ln2_w, ln2_b, ffn_w_in, ffn_w_out, sample_len, cache_rows):
    kw, vw = B_KWIDTH, B_VWIDTH
    wb = w_in_mix[:, :, A_PROJ:]
    main = jnp.concatenate([wb[:, :, :2 * kw + vw], wb[:, :, 2 * kw + vw + B_GATE_LORA:]], axis=-1)
    lora = jnp.pad(wb[:, :, 2 * kw + vw:2 * kw + vw + B_GATE_LORA], ((0, 0), (0, 0), (0, B_LORA_PAD - B_GATE_LORA)))
    row3 = lambda t: t.reshape(t.shape[0], 1, -1)
    bias_prompt = jnp.stack([_rel_bias_tile(t, ATT_QB, ATT_WIN, C_PAST_ROWS) for t in c_rel_bias])
    rel = np.arange(ATT_WIN)[None, :] // CHUNK - np.arange(ATT_QB)[:, None] // CHUNK
    band = (rel >= 0) & (rel <= C_PAST_CHUNKS)
    bias_prompt = jnp.where(band, bias_prompt, NEG_INF)
    q_scale = jnp.concatenate([jnp.full((D_MODEL,), ATT_SCALE, F32), jnp.ones((2 * D_MODEL,), F32)])
    P = {
        'w_in_a': w_in_mix[:, :, :A_PROJ].astype(BF16),
        'w_in_b': jnp.concatenate([main, lora], axis=-1).astype(BF16),
        'a': dict(mu=row3(a_mu), w0=row3(a_w0), w2=a_w2, a0=row3(a_a0), a2=a_a2, g2=a_g2, k_k=row3(a_k_k),
                  k_a=row3(a_k_a), r_k=row3(a_r_k), ln_w=row3(a_ln_w), ln_b=row3(a_ln_b)),
        'b': dict(up=jnp.pad(b_alpha_up, ((0, 0), (0, B_LORA_PAD - B_GATE_LORA), (0, 0))),
                  bias=row3(b_alpha_bias), nw=row3(b_norm_w)),
        'w_out_a': w_out_mix[:, :A_WIDTH].astype(BF16),
        'w_out_b': w_out_mix[:, A_WIDTH:].astype(BF16),
        'c_w_qkv': c_w_qkv.astype(BF16),
        'c_w_qkv_scaled': (c_w_qkv * q_scale).astype(BF16),
        'c_w_o': c_w_o.astype(BF16),
        'c_bias_prompt': bias_prompt.reshape(bias_prompt.shape[0], C_HEADS // 2, 2 * ATT_QB, ATT_WIN),
        'c_bias_sample': jnp.stack([_rel_bias_tile(t, sample_len, cache_rows + sample_len, cache_rows)
                                    for t in c_rel_bias]),
        'ln1_w': row3(ln1_w), 'ln1_b': row3(ln1_b), 'ln2_w': row3(ln2_w), 'ln2_b': row3(ln2_b),
        'ffn_w_in': ffn_w_in.astype(BF16),
        'ffn_w_out': ffn_w_out.astype(BF16),
    }
    return P


def kernel(x_prompt, x_sample, state_a_wkv, state_a_shift, state_b_gla, cache_c_k, cache_c_v, w_in_mix, a_mu, a_w0, a_w2, a_a0, a_a2, a_g2, a_k_k, a_k_a, a_r_k, a_ln_w, a_ln_b, b_alpha_up, b_alpha_bias, b_norm_w, w_out_mix, c_w_qkv, c_rel_bias, c_w_o, ln1_w, ln1_b, ln2_w, ln2_b, ffn_w_in, ffn_w_out):
    P = _prepare(w_in_mix, a_mu, a_w0, a_w2, a_a0, a_a2, a_g2, a_k_k, a_k_a, a_r_k, a_ln_w, a_ln_b,
                 b_alpha_up, b_alpha_bias, b_norm_w, w_out_mix, c_w_qkv, c_rel_bias, c_w_o,
                 ln1_w, ln1_b, ln2_w, ln2_b, ffn_w_in, ffn_w_out, x_sample.shape[1], cache_c_k.shape[3])
    bp = x_prompt.shape[0]
    dt = x_prompt.dtype
    n_even = state_a_wkv.shape[0]
    wkv_zero = jnp.zeros((n_even, bp, A_HEADS, A_HEAD_DIM, A_HEAD_DIM), dt)
    shift_zero = jnp.zeros((n_even, bp, A_PROJ), dt)
    gla_zero = jnp.zeros((n_even, bp, B_HEADS, B_KEY_DIM, B_VAL_DIM), dt)
    y_prompt, p_wkv, p_shift, p_gla, p_k, p_v = _trunk(x_prompt, wkv_zero, shift_zero, gla_zero, None, None, P)
    y_sample, s_wkv, s_shift, s_gla, s_k, s_v = _trunk(
        x_sample, state_a_wkv, state_a_shift, state_b_gla, cache_c_k, cache_c_v, P)
    return (y_prompt, y_sample, p_wkv, p_shift, p_gla, p_k, p_v, s_wkv, s_shift, s_gla, s_k, s_v)
```

```python
import functools

import jax
import jax.numpy as jnp
import numpy as np
from jax import lax
from jax.experimental import pallas as pl
from jax.experimental.pallas import tpu as pltpu

F32 = jnp.float32
BF16 = jnp.bfloat16

D_MODEL = 1024
DEPTH = 4
CHUNK = 64
A_WIDTH = 512
A_HEAD_DIM = 64
A_HEADS = 8
A_DECAY_LORA = 64
A_ICL_LORA = 64
A_GATE_LORA = 128
A_PROJ = 3 * A_WIDTH + A_DECAY_LORA + A_ICL_LORA + A_GATE_LORA
A_NORM_EPS = 64e-5
B_VWIDTH = 512
B_KWIDTH = 256
B_HEADS = 4
B_KEY_DIM = 64
B_VAL_DIM = 128
B_GATE_LORA = 16
B_GATE_NORM = 16.0
B_LORA_PAD = 128
B_PROJ_PAD = 2 * B_KWIDTH + 2 * B_VWIDTH + B_LORA_PAD
C_HEAD_DIM = 64
C_HEADS = 16
C_PAST_CHUNKS = 8
C_PAST_ROWS = C_PAST_CHUNKS * CHUNK
C_REL_CLIP = 128
FFN_HIDDEN = 2816
LN_EPS = 1e-5
DEEPNORM_ALPHA = (2.0 * DEPTH) ** 0.25
NEG_INF = -1e30

LANE = 128
PAIR = LANE
HALF = PAIR // 2
A_PAIRS = A_HEADS // 2
B_PAIRS = B_HEADS // 2
VMEM_LIMIT = 56 * 1024 * 1024

NN = (((1,), (0,)), ((), ()))
NT = (((1,), (1,)), ((), ()))
TN = (((0,), (0,)), ((), ()))


def _params(*sem):
    return pltpu.CompilerParams(dimension_semantics=sem, vmem_limit_bytes=VMEM_LIMIT)


def _dg(a, b, dn):
    return lax.dot_general(a, b, dn, preferred_element_type=F32)


def _dot1(a, b, dn=NN):
    return _dg(a.astype(BF16), b.astype(BF16), dn)


def _split(a):
    hi = a.astype(BF16)
    lo = (a - hi.astype(F32)).astype(BF16)
    return hi, lo


def _dot3s(a, b, dn=NN):
    return _dg(a[0], b[0], dn) + _dg(a[0], b[1], dn) + _dg(a[1], b[0], dn)


def _dot3(a, b, dn=NN):
    return _dot3s(_split(a), _split(b), dn)


def _dot_xl(a_exact, b, dn=NN):
    bh, bl = _split(b)
    return _dg(a_exact, bh, dn) + _dg(a_exact, bl, dn)


def _dot_xr(a, b_exact, dn=NN):
    ah, al = _split(a)
    return _dg(ah, b_exact, dn) + _dg(al, b_exact, dn)


def _ln(z, w, b):
    mu = jnp.mean(z, -1, keepdims=True)
    d = z - mu
    var = jnp.mean(d * d, -1, keepdims=True)
    return d * lax.rsqrt(var + LN_EPS) * w + b


def _sigmoid(x):
    return 1.0 / (1.0 + jnp.exp(-x))


def _softplus(x):
    return jnp.maximum(x, 0.0) + jnp.log(1.0 + jnp.exp(-jnp.abs(x)))


def _stack_masked(x, m0):
    return jnp.concatenate([jnp.where(m0, x, 0.0), jnp.where(m0, 0.0, x)], axis=0)


def _fold(x):
    n = x.shape[0] // 2
    return x[:n] + x[n:]


def _mm_kernel(x_ref, w_ref, o_ref):
    o_ref[...] = jnp.dot(x_ref[...].astype(BF16), w_ref[...], preferred_element_type=F32)


def _matmul(x, w, tm, tn):
    M, K = x.shape
    N = w.shape[1]
    tm = min(tm, M)
    return pl.pallas_call(
        _mm_kernel,
        grid=(M // tm, N // tn),
        in_specs=[pl.BlockSpec((tm, K), lambda i, j: (i, 0)),
                  pl.BlockSpec((K, tn), lambda i, j: (0, j))],
        out_specs=pl.BlockSpec((tm, tn), lambda i, j: (i, j)),
        out_shape=jax.ShapeDtypeStruct((M, N), F32),
        compiler_params=_params("parallel", "arbitrary"),
    )(x, w)


def _mix_ffn_kernel(n_pairs, *refs):
    res_ref, lw1_ref, lb1_ref, wg_ref, wu_ref, wo_ref, lw2_ref, lb2_ref, o_ref = refs[2 * n_pairs:]
    acc = None
    for p in range(n_pairs):
        part = jnp.dot(refs[2 * p][...].astype(BF16), refs[2 * p + 1][...], preferred_element_type=F32)
        acc = part if acc is None else acc + part
    x = _ln(DEEPNORM_ALPHA * res_ref[...] + acc, lw1_ref[...], lb1_ref[...])
    xb = x.astype(BF16)
    g = jnp.dot(xb, wg_ref[...], preferred_element_type=F32)
    u = jnp.dot(xb, wu_ref[...], preferred_element_type=F32)
    act = (g * _sigmoid(g) * u).astype(BF16)
    y = jnp.dot(act, wo_ref[...], preferred_element_type=F32)
    o_ref[...] = _ln(DEEPNORM_ALPHA * x + y, lw2_ref[...], lb2_ref[...])


def _mix_ffn(pairs, res, lw1, lb1, w_in, w_out, lw2, lb2, tm):
    M = res.shape[0]
    tm = min(tm, M)
    resident = lambda shape, idx: pl.BlockSpec(shape, idx, pipeline_mode=pl.Buffered(1))
    vec = pl.BlockSpec((1, D_MODEL), lambda i: (0, 0))
    in_specs, args = [], []
    for a, w in pairs:
        K = a.shape[1]
        in_specs += [pl.BlockSpec((tm, K), lambda i: (i, 0)), resident((K, D_MODEL), lambda i: (0, 0))]
        args += [a, w]
    in_specs += [pl.BlockSpec((tm, D_MODEL), lambda i: (i, 0)), vec, vec,
                 resident((D_MODEL, FFN_HIDDEN), lambda i: (0, 0)),
                 resident((D_MODEL, FFN_HIDDEN), lambda i: (0, 1)),
                 resident((FFN_HIDDEN, D_MODEL), lambda i: (0, 0)), vec, vec]
    return pl.pallas_call(
        functools.partial(_mix_ffn_kernel, len(pairs)),
        grid=(M // tm,),
        in_specs=in_specs,
        out_specs=pl.BlockSpec((tm, D_MODEL), lambda i: (i, 0)),
        out_shape=jax.ShapeDtypeStruct((M, D_MODEL), F32),
        compiler_params=_params("parallel"),
    )(*args, res, lw1, lb1, w_in, w_in, w_out, lw2, lb2)


def _inv_unit_lower(ns, eye, nilpotency):
    size = eye.shape[0]
    xs = [eye - n for n in ns]
    ps = [_dot1(n, n) for n in ns]
    k = 2
    while k < nilpotency:
        k *= 2
        if k < nilpotency:
            rs = [_dot1(p, jnp.concatenate([x, p], axis=1)) for x, p in zip(xs, ps)]
            xs = [x + r[:, :size] for x, r in zip(xs, rs)]
            ps = [r[:, size:] for r in rs]
        else:
            xs = [x + _dot1(p, x) for x, p in zip(xs, ps)]
    return xs


def _segsum(x, seg):
    n = x.shape[1] // PAIR
    rows = x.shape[0]
    xs = jnp.concatenate([x[:, j * PAIR:(j + 1) * PAIR] for j in range(n)], axis=0)
    s = _dot_xr(xs, seg)
    return jnp.concatenate([s[j * rows:(j + 1) * rows] for j in range(n)], axis=1)


def _head_mask():
    hr = lax.broadcasted_iota(jnp.int32, (PAIR, PAIR), 0) // A_HEAD_DIM
    hc = lax.broadcasted_iota(jnp.int32, (PAIR, PAIR), 1) // A_HEAD_DIM
    return (hr == hc).astype(BF16)


def _rwkv_prep_kernel(n_chunks, pa_ref, prev_ref, shift_ref, mu_ref, w0_ref, w2_ref, a0_ref, a2_ref, g2_ref,
                      kk_ref, ka_ref, rk_ref,
                      mh_ref, ch_ref, qc_ref, oc_ref, g_ref, bonus_ref):
    i = pl.program_id(1)
    L = CHUNK
    pa = pa_ref[...]
    first = jnp.where(i == 0, shift_ref[...], prev_ref[7:8, :])
    rows = lax.broadcasted_iota(jnp.int32, pa.shape, 0)
    prev = jnp.where(rows == 0, first, pltpu.roll(pa, shift=1, axis=0))
    xs = pa + (prev - pa) * mu_ref[...]
    W = A_WIDTH
    r = xs[:, 0:W]
    k0 = xs[:, W:2 * W]
    v = xs[:, 2 * W:3 * W]
    xw = xs[:, 3 * W:3 * W + A_DECAY_LORA]
    xa = xs[:, 3 * W + A_DECAY_LORA:3 * W + A_DECAY_LORA + A_ICL_LORA]
    xg = xs[:, 3 * W + A_DECAY_LORA + A_ICL_LORA:]
    w = -_softplus(-(w0_ref[...] + _dot3(jnp.tanh(xw), w2_ref[...]))) - 0.5
    logw = -jnp.exp(w)
    a = _sigmoid(a0_ref[...] + _dot3(xa, a2_ref[...]))
    g_ref[...] = _dot3(_sigmoid(xg), g2_ref[...])
    seg = _head_mask()
    kk = k0 * kk_ref[...]
    kk = kk * lax.rsqrt(_segsum(kk * kk, seg) + 1e-12)
    k = k0 * (1.0 + (a - 1.0) * ka_ref[...])
    bonus_ref[...] = _segsum(r * k * rk_ref[...], seg) * v
    b = kk * a

    tri = (lax.broadcasted_iota(jnp.int32, (L, L), 0) >= lax.broadcasted_iota(jnp.int32, (L, L), 1)).astype(BF16)
    row = lax.broadcasted_iota(jnp.int32, (PAIR, PAIR), 0)
    col = lax.broadcasted_iota(jnp.int32, (PAIR, PAIR), 1)
    eye = (row == col).astype(F32)
    strict = row > col
    incl2 = (lax.broadcasted_iota(jnp.int32, (PAIR, 2 * PAIR), 0)
             >= lax.broadcasted_iota(jnp.int32, (PAIR, 2 * PAIR), 1) % PAIR)
    m0 = lax.broadcasted_iota(jnp.int32, (L, PAIR), 1) < HALF
    pairs = range(A_PAIRS)
    lanes = [slice(j * PAIR, (j + 1) * PAIR) for j in pairs]
    chains = [(c, j) for c in range(n_chunks) for j in pairs]
    every = range(len(chains))
    rows_of = [slice(c * L, (c + 1) * L) for c in range(n_chunks)]
    kkg, rg, kinv, binv, kd, bd, e_last = [], [], [], [], [], [], []
    for sl in rows_of:
        lw = logw[sl]
        cum = _dot_xl(tri, lw)
        c_last = cum[L - 1:L, :]
        e_neg = jnp.exp(-cum)
        e_rem = jnp.exp(c_last - cum)
        kkg.append(kk[sl] * jnp.exp(cum - lw))
        rg.append(r[sl] * jnp.exp(cum))
        kinv.append(k[sl] * e_neg)
        binv.append(b[sl] * e_neg)
        kd.append(k[sl] * e_rem)
        bd.append(b[sl] * e_rem)
        e_last.append(jnp.exp(c_last))
    sm = lambda xs_, c, j: _stack_masked(xs_[c][:, lanes[j]], m0).astype(BF16)
    kkg_b = [sm(kkg, c, j) for c, j in chains]
    v_b = [_stack_masked(v[rows_of[c], lanes[j]], m0).astype(BF16) for c, j in chains]
    bd_b = [sm(bd, c, j) for c, j in chains]
    kd_b = [sm(kd, c, j) for c, j in chains]
    rg_b = [sm(rg, c, j) for c, j in chains]
    kinv_b = [sm(kinv, c, j) for c, j in chains]
    binv_b = [sm(binv, c, j) for c, j in chains]
    aa = [_dg(jnp.concatenate([kkg_b[n], rg_b[n]], axis=0), jnp.concatenate([binv_b[n], kinv_b[n]], axis=0), NT)
          for n in every]
    a_kb = [jnp.where(strict, x[:PAIR, :PAIR], 0.0) for x in aa]
    a_kk = [jnp.where(strict, x[:PAIR, PAIR:], 0.0).astype(BF16) for x in aa]
    a_rbk = [jnp.where(incl2, x[PAIR:, :], 0.0).astype(BF16) for x in aa]
    t = _inv_unit_lower(a_kb, eye, L)
    av = [_dg(a_kk[n], v_b[n], NN).astype(BF16) for n in every]
    wu = [_dg(t[n].astype(BF16), jnp.concatenate([kkg_b[n], av[n]], axis=1), NN) for n in every]
    wu_b = [x.astype(BF16) for x in wu]
    mc = [_dg(bd_b[n], wu_b[n], TN) for n in every]
    kv = [_dg(kd_b[n], v_b[n], TN) for n in every]
    rhs = [jnp.concatenate([jnp.concatenate([wu_b[n][:, :PAIR], (-wu[n][:, PAIR:]).astype(BF16)], axis=1),
                            jnp.concatenate([jnp.zeros_like(v_b[n]), v_b[n]], axis=1)], axis=0) for n in every]
    qo = [_dg(a_rbk[n], rhs[n], NN) for n in every]
    for n, (c, j) in enumerate(chains):
        mh_ref[c, j] = (eye * e_last[c][:, lanes[j]] - mc[n][:, :PAIR]).astype(BF16)
        ch_ref[c, j] = kv[n] - mc[n][:, PAIR:]
        qc_ref[rows_of[c], lanes[j]] = (rg[c][:, lanes[j]] - _fold(qo[n][:, :PAIR])).astype(BF16)
        oc_ref[rows_of[c], lanes[j]] = _fold(qo[n][:, PAIR:])


def _rwkv_scan_kernel(n_seq, n_chunks, mh_ref, ch_ref, qc_ref, oc_ref, g_ref, bonus_ref, h0_ref, lnw_ref, lnb_ref,
                      y_ref, hout_ref, h_ref):
    i = pl.program_id(1)
    L = CHUNK

    @pl.when(i == 0)
    def _():
        h_ref[...] = h0_ref[...]

    seg = _head_mask()
    inv_n = 1.0 / A_HEAD_DIM
    lanes = [slice(j * PAIR, (j + 1) * PAIR) for j in range(A_PAIRS)]
    chains = [(s, j) for s in range(n_seq) for j in range(A_PAIRS)]
    hb = [h_ref[s, j] for s, j in chains]
    for c in range(n_chunks):
        sl = slice(c * L, (c + 1) * L)
        hb_b = [x.astype(BF16) for x in hb]
        o = [_dg(qc_ref[s, sl, lanes[j]], hb_b[n], NN) + oc_ref[s, sl, lanes[j]] for n, (s, j) in enumerate(chains)]
        hb = [_dg(mh_ref[s, c, j], hb_b[n], NN) + ch_ref[s, c, j] for n, (s, j) in enumerate(chains)]
        for s in range(n_seq):
            os_ = jnp.concatenate(o[s * A_PAIRS:(s + 1) * A_PAIRS], axis=1)
            mean = _segsum(os_, seg) * inv_n
            d = os_ - mean
            var = _segsum(d * d, seg) * inv_n
            on = d * lax.rsqrt(var + A_NORM_EPS) * lnw_ref[...] + lnb_ref[...]
            y_ref[s, sl, :] = ((on + bonus_ref[s, sl, :]) * g_ref[s, sl, :]).astype(y_ref.dtype)
    for n, (s, j) in enumerate(chains):
        h_ref[s, j] = hb[n]
    hout_ref[...] = h_ref[...]


def _rwkv_mix(pa, shift0, wkv0, prm):
    Bsz, T, _ = pa.shape
    n1 = next(n for n in (4, 2, 1) if (T // CHUNK) % n == 0)
    R1 = n1 * CHUNK
    W = A_WIDTH
    nblk = T // R1
    row = lambda n: pl.BlockSpec((1, n), lambda b, i: (0, 0))
    full = lambda s: pl.BlockSpec(s, lambda b, i: (0,) * len(s))
    act = lambda r, n: pl.BlockSpec((None, r, n), lambda b, i: (b, i, 0))
    mat = lambda n: pl.BlockSpec((None, n, A_PAIRS, PAIR, PAIR), lambda b, i: (b, i, 0, 0, 0))
    sub = R1 // 8
    mats = lambda dt: jax.ShapeDtypeStruct((Bsz, T // CHUNK, A_PAIRS, PAIR, PAIR), dt)
    acts = lambda dt: jax.ShapeDtypeStruct((Bsz, T, W), dt)
    mh, ch, qc, oc, g, bonus = pl.pallas_call(
        functools.partial(_rwkv_prep_kernel, n1),
        grid=(Bsz, nblk),
        in_specs=[act(R1, A_PROJ),
                  pl.BlockSpec((None, 8, A_PROJ), lambda b, i: (b, jnp.maximum(i * sub - 1, 0), 0)),
                  pl.BlockSpec((None, 1, A_PROJ), lambda b, i: (b, 0, 0)),
                  row(A_PROJ), row(W), full((A_DECAY_LORA, W)), row(W), full((A_ICL_LORA, W)),
                  full((A_GATE_LORA, W)), row(W), row(W), row(W)],
        out_specs=[mat(n1), mat(n1), act(R1, W), act(R1, W), act(R1, W), act(R1, W)],
        out_shape=[mats(BF16), mats(F32), acts(BF16), acts(F32), acts(F32), acts(F32)],
        compiler_params=_params("parallel", "parallel"),
    )(pa, pa, shift0[:, None, :], prm['mu'], prm['w0'], prm['w2'], prm['a0'], prm['a2'], prm['g2'],
      prm['k_k'], prm['k_a'], prm['r_k'])

    n2 = min(4, T // CHUNK)
    R2 = n2 * CHUNK
    hh = wkv0.transpose(0, 1, 3, 2).reshape(Bsz, A_PAIRS, 2, A_HEAD_DIM, A_HEAD_DIM)
    zero = jnp.zeros_like(hh[:, :, 0])
    h0 = jnp.concatenate([jnp.concatenate([hh[:, :, 0], zero], axis=-1),
                          jnp.concatenate([zero, hh[:, :, 1]], axis=-1)], axis=-2)
    ns = 2 if Bsz % 2 == 0 else 1
    state = pl.BlockSpec((ns, A_PAIRS, PAIR, PAIR), lambda b, i: (b, 0, 0, 0))
    act2 = lambda r, n: pl.BlockSpec((ns, r, n), lambda b, i: (b, i, 0))
    mat2 = lambda n: pl.BlockSpec((ns, n, A_PAIRS, PAIR, PAIR), lambda b, i: (b, i, 0, 0, 0))
    y, hout = pl.pallas_call(
        functools.partial(_rwkv_scan_kernel, ns, n2),
        grid=(Bsz // ns, T // R2),
        in_specs=[mat2(n2), mat2(n2), act2(R2, W), act2(R2, W), act2(R2, W), act2(R2, W), state,
                  row(W), row(W)],
        out_specs=[act2(R2, W), state],
        out_shape=[acts(BF16), jax.ShapeDtypeStruct((Bsz, A_PAIRS, PAIR, PAIR), F32)],
        scratch_shapes=[pltpu.VMEM((ns, A_PAIRS, PAIR, PAIR), F32)],
        compiler_params=_params("parallel", "arbitrary"),
    )(mh, ch, qc, oc, g, bonus, h0, prm['ln_w'], prm['ln_b'])
    hd = jnp.stack([hout[:, :, :HALF, :HALF], hout[:, :, HALF:, HALF:]], axis=2)
    wkv = hd.reshape(Bsz, A_HEADS, A_HEAD_DIM, A_HEAD_DIM).transpose(0, 1, 3, 2)
    return y, wkv


def _gla_kernel(n_seq, n_chunks, pb_ref, s0_ref, up_ref, bias_ref, nw_ref, y_ref, sout_ref, s_ref):
    i = pl.program_id(1)
    L = CHUNK

    @pl.when(i == 0)
    def _():
        s_ref[...] = s0_ref[...]

    kw, vw = B_KWIDTH, B_VWIDTH
    incl = lax.broadcasted_iota(jnp.int32, (L, L), 0) >= lax.broadcasted_iota(jnp.int32, (L, L), 1)
    tri = incl.astype(BF16)
    m0 = lax.broadcasted_iota(jnp.int32, (L, PAIR), 1) < HALF
    top = lax.broadcasted_iota(jnp.int32, (PAIR, B_VAL_DIM), 0) < HALF
    scale = B_KEY_DIM ** -0.5
    pairs = range(B_PAIRS)
    heads = range(B_HEADS)
    pl_ = lambda h: slice((h // 2) * PAIR, (h // 2 + 1) * PAIR)
    units = [(s, c) for c in range(n_chunks) for s in range(n_seq)]
    rows_of = lambda c: slice(c * L, (c + 1) * L)
    vslice = lambda h: slice(h * B_VAL_DIM, (h + 1) * B_VAL_DIM)
    z = [_dot3(pb_ref[s, rows_of(c), 2 * kw + 2 * vw:], up_ref[...]) + bias_ref[...] for s, c in units]
    bc = [_dot_xl(tri, -_softplus(-x) * (1.0 / B_GATE_NORM)) for x in z]
    b_last = [x[L - 1:L, :] for x in bc]
    q_dec, k_inv, k_state, e_last_t = [], [], [], []
    for n, (s, c) in enumerate(units):
        k = pb_ref[s, rows_of(c), kw:2 * kw]
        q_dec.append(pb_ref[s, rows_of(c), 0:kw] * scale * jnp.exp(bc[n]))
        k_inv.append((k * jnp.exp(-bc[n])).astype(BF16))
        k_state.append((k * jnp.exp(b_last[n] - bc[n])).astype(BF16))
        e_last_t.append(jnp.broadcast_to(jnp.exp(b_last[n]), (LANE, kw)).T)
    uh = [(n, h) for n in range(len(units)) for h in heads]
    vb = [pb_ref[units[n][0], rows_of(units[n][1]), 2 * kw + h * B_VAL_DIM:2 * kw + (h + 1) * B_VAL_DIM].astype(BF16)
          for n, h in uh]
    qm = [jnp.where(m0 if h % 2 == 0 else ~m0, q_dec[n][:, pl_(h)], 0.0).astype(BF16) for n, h in uh]
    scores = [jnp.where(incl, _dg(qm[i_], k_inv[n][:, pl_(h)], NT), 0.0).astype(BF16) for i_, (n, h) in enumerate(uh)]
    upd = [_dg(k_state[n][:, pl_(h)], vb[i_], TN) for i_, (n, h) in enumerate(uh)]
    intra = [_dg(scores[i_], vb[i_], NN) for i_ in range(len(uh))]
    states = [[s_ref[s, j] for j in pairs] for s in range(n_seq)]
    before = []
    for n, (s, c) in enumerate(units):
        before.append([x.astype(BF16) for x in states[s]])
        states[s] = [states[s][j] * e_last_t[n][j * PAIR:(j + 1) * PAIR, :]
                     + jnp.where(top, upd[n * B_HEADS + 2 * j], upd[n * B_HEADS + 2 * j + 1]) for j in pairs]
    o = [_dg(qm[i_], before[n][h // 2], NN) + intra[i_] for i_, (n, h) in enumerate(uh)]
    for i_, (n, h) in enumerate(uh):
        s, c = units[n]
        oh = o[i_] * lax.rsqrt(jnp.mean(o[i_] * o[i_], -1, keepdims=True) + LN_EPS) * nw_ref[...]
        rg_h = pb_ref[s, rows_of(c), 2 * kw + vw + h * B_VAL_DIM:2 * kw + vw + (h + 1) * B_VAL_DIM]
        y_ref[s, rows_of(c), vslice(h)] = (oh * (rg_h * _sigmoid(rg_h))).astype(y_ref.dtype)
    for s in range(n_seq):
        for j in pairs:
            s_ref[s, j] = states[s][j]
    sout_ref[...] = s_ref[...]


def _gla_mix(pb, s0, prm):
    Bsz, T, _ = pb.shape
    n = min(4, T // CHUNK)
    Rr = n * CHUNK
    ns = 2 if Bsz % 2 == 0 else 1
    st = pl.BlockSpec((ns, B_PAIRS, PAIR, B_VAL_DIM), lambda b, i: (b, 0, 0, 0))
    y, s = pl.pallas_call(
        functools.partial(_gla_kernel, ns, n),
        grid=(Bsz // ns, T // Rr),
        in_specs=[pl.BlockSpec((ns, Rr, B_PROJ_PAD), lambda b, i: (b, i, 0)), st,
                  pl.BlockSpec((B_LORA_PAD, B_KWIDTH), lambda b, i: (0, 0)),
                  pl.BlockSpec((1, B_KWIDTH), lambda b, i: (0, 0)),
                  pl.BlockSpec((1, B_VAL_DIM), lambda b, i: (0, 0))],
        out_specs=[pl.BlockSpec((ns, Rr, B_VWIDTH), lambda b, i: (b, i, 0)), st],
        out_shape=[jax.ShapeDtypeStruct((Bsz, T, B_VWIDTH), BF16),
                   jax.ShapeDtypeStruct((Bsz, B_PAIRS, PAIR, B_VAL_DIM), F32)],
        scratch_shapes=[pltpu.VMEM((ns, B_PAIRS, PAIR, B_VAL_DIM), F32)],
        compiler_params=_params("parallel", "arbitrary"),
    )(pb, s0.reshape(Bsz, B_PAIRS, PAIR, B_VAL_DIM), prm['up'], prm['bias'], prm['nw'])
    return y, s.reshape(Bsz, B_HEADS, B_KEY_DIM, B_VAL_DIM)


ATT_QB = 2 * CHUNK
ATT_KB = LANE
ATT_NKB = (C_PAST_ROWS + ATT_QB) // ATT_KB
ATT_SCALE = C_HEAD_DIM ** -0.5


ATT_WIN = C_PAST_ROWS + ATT_QB


def _qkv_pad_kernel(n_pad, x_ref, w_ref, o_ref):
    r = pl.program_id(1)

    @pl.when(r < n_pad)
    def _():
        o_ref[...] = jnp.zeros_like(o_ref)

    @pl.when(r >= n_pad)
    def _():
        o_ref[...] = jnp.dot(x_ref[...].astype(BF16), w_ref[...], preferred_element_type=F32).astype(BF16)


def _qkv_padded(x, w):
    Bsz, T, D = x.shape
    N = w.shape[1]
    tm = next(t for t in (512, 256, 128) if T % t == 0)
    n_pad = C_PAST_ROWS // tm
    return pl.pallas_call(
        functools.partial(_qkv_pad_kernel, n_pad),
        grid=(Bsz, T // tm + n_pad),
        in_specs=[pl.BlockSpec((None, tm, D), lambda b, r: (b, jnp.maximum(r - n_pad, 0), 0)),
                  pl.BlockSpec((D, N), lambda b, r: (0, 0), pipeline_mode=pl.Buffered(1))],
        out_specs=pl.BlockSpec((None, tm, N), lambda b, r: (b, r, 0)),
        out_shape=jax.ShapeDtypeStruct((Bsz, C_PAST_ROWS + T, N), BF16),
        compiler_params=_params("parallel", "arbitrary"),
    )(x, w)


def _att_prompt_kernel(n_qb, q_ref, k_ref, v_ref, bias_ref, o_ref):
    g = pl.program_id(2)
    m0 = lax.broadcasted_iota(jnp.int32, (ATT_QB, PAIR), 1) < HALF
    col = lax.broadcasted_iota(jnp.int32, (2 * ATT_QB, ATT_WIN), 1)

    def run(window_has_padding):
        def body(t, carry):
            qi = g * n_qb + t
            r0 = pl.multiple_of(t * ATT_QB, ATT_QB)
            start = pl.multiple_of(qi * ATT_QB, ATT_QB)
            q = _stack_masked(q_ref[pl.ds(r0, ATT_QB), :], m0)
            s = _dg(q, k_ref[pl.ds(start, ATT_WIN), :], NT) + bias_ref[...]
            if window_has_padding:
                s = jnp.where(col >= C_PAST_ROWS - qi * ATT_QB, s, NEG_INF)
            m = s.max(-1, keepdims=True)
            p = jnp.exp(s - m)
            o = _dg(p.astype(BF16), v_ref[pl.ds(start, ATT_WIN), :], NN) / p.sum(-1, keepdims=True)
            o_ref[pl.ds(r0, ATT_QB), :] = jnp.where(m0, o[:ATT_QB], o[ATT_QB:]).astype(o_ref.dtype)
            return carry

        lax.fori_loop(0, n_qb, body, 0, unroll=True)

    first = g * (n_qb * ATT_QB) < C_PAST_ROWS
    pl.when(first)(lambda: run(True))
    pl.when(jnp.logical_not(first))(lambda: run(False))


def _attention_prompt(qkv, bias):
    Bsz, Tp, _ = qkv.shape
    T = Tp - C_PAST_ROWS
    nhp = C_HEADS // 2
    nblk = T // ATT_QB
    n_qb = next(n for n in (4, 2, 1) if nblk % n == 0)
    rows = n_qb * ATT_QB
    skip = C_PAST_ROWS // rows
    return pl.pallas_call(
        functools.partial(_att_prompt_kernel, n_qb),
        grid=(Bsz, nhp, T // rows),
        in_specs=[pl.BlockSpec((None, rows, LANE), lambda b, hp, g: (b, g + skip, hp)),
                  pl.BlockSpec((None, Tp, LANE), lambda b, hp, g: (b, 0, nhp + hp)),
                  pl.BlockSpec((None, Tp, LANE), lambda b, hp, g: (b, 0, 2 * nhp + hp)),
                  pl.BlockSpec((None, 2 * ATT_QB, ATT_WIN), lambda b, hp, g: (hp, 0, 0))],
        out_specs=pl.BlockSpec((None, rows, LANE), lambda b, hp, g: (b, g, hp)),
        out_shape=jax.ShapeDtypeStruct((Bsz, T, D_MODEL), BF16),
        compiler_params=_params("parallel", "parallel", "arbitrary"),
    )(qkv, qkv, qkv, bias)


def _att_sample_kernel(q_ref, kn_ref, vn_ref, kc_ref, vc_ref, bias_ref, o_ref):
    R = kc_ref.shape[1]
    for hh in range(2):
        hs = slice(hh * C_HEAD_DIM, (hh + 1) * C_HEAD_DIM)
        q = (q_ref[:, hs] * ATT_SCALE).astype(BF16)
        s_c = _dg(q, kc_ref[hh].astype(BF16), NT) + bias_ref[hh, :, 0:R]
        s_n = _dg(q, kn_ref[:, hs].astype(BF16), NT) + bias_ref[hh, :, R:]
        m = jnp.maximum(s_c.max(-1, keepdims=True), s_n.max(-1, keepdims=True))
        p_c = jnp.exp(s_c - m)
        p_n = jnp.exp(s_n - m)
        den = p_c.sum(-1, keepdims=True) + p_n.sum(-1, keepdims=True)
        acc = _dg(p_c.astype(BF16), vc_ref[hh].astype(BF16), NN) + _dg(p_n.astype(BF16), vn_ref[:, hs].astype(BF16), NN)
        o_ref[:, hs] = acc / den


def _attention_sample(qkv, cache_k, cache_v, bias):
    Bsz, T, _ = qkv.shape
    R = cache_k.shape[2]
    nhp = C_HEADS // 2
    cache = pl.BlockSpec((None, 2, R, C_HEAD_DIM), lambda b, hp: (b, hp, 0, 0))
    return pl.pallas_call(
        _att_sample_kernel,
        grid=(Bsz, nhp),
        in_specs=[pl.BlockSpec((None, T, LANE), lambda b, hp: (b, 0, hp)),
                  pl.BlockSpec((None, T, LANE), lambda b, hp: (b, 0, nhp + hp)),
                  pl.BlockSpec((None, T, LANE), lambda b, hp: (b, 0, 2 * nhp + hp)),
                  cache, cache,
                  pl.BlockSpec((2, T, R + T), lambda b, hp: (hp, 0, 0))],
        out_specs=pl.BlockSpec((None, T, LANE), lambda b, hp: (b, 0, hp)),
        out_shape=jax.ShapeDtypeStruct((Bsz, T, D_MODEL), F32),
        compiler_params=_params("parallel", "parallel"),
    )(qkv, qkv, qkv, cache_k, cache_v, bias)


def _rel_bias_tile(table, n_q, n_k, q_offset):
    period = n_q + n_k - 1
    m = np.arange(period)
    d = np.where(m < n_k, m, m - period)
    idx = np.clip(q_offset - d, -C_REL_CLIP, C_REL_CLIP) + C_REL_CLIP
    diag = table[:, idx]
    flat = jnp.tile(diag, (1, n_q))[:, :n_q * (period - 1)]
    return flat.reshape(table.shape[0], n_q, period - 1)[:, :, :n_k]


TM = 512


def _trunk(x, wkv0, shift0, gla0, cache_k, cache_v, P):
    Bsz, T, _ = x.shape
    M = Bsz * T
    xf = x.reshape(M, D_MODEL)
    wkv_o, shift_o, gla_o, k_o, v_o = [], [], [], [], []
    for layer in range(DEPTH):
        if layer % 2 == 0:
            e = layer // 2
            pa = _matmul(xf, P['w_in_a'][e], TM, A_PROJ).reshape(Bsz, T, A_PROJ)
            pb = _matmul(xf, P['w_in_b'][e], TM, B_PROJ_PAD).reshape(Bsz, T, B_PROJ_PAD)
            ya, wkv = _rwkv_mix(pa, shift0[e], wkv0[e], {k: v[e] for k, v in P['a'].items()})
            yb, gs = _gla_mix(pb, gla0[e], {k: v[e] for k, v in P['b'].items()})
            wkv_o.append(wkv)
            shift_o.append(pa[:, -1])
            gla_o.append(gs)
            pairs = [(ya.reshape(M, A_WIDTH), P['w_out_a'][e]), (yb.reshape(M, B_VWIDTH), P['w_out_b'][e])]
        else:
            o = layer // 2
            if cache_k is None:
                x3 = xf.reshape(Bsz, T, D_MODEL)
                att = _attention_prompt(_qkv_padded(x3, P['c_w_qkv_scaled'][o]), P['c_bias_prompt'][o])
                keep = min(C_PAST_ROWS, T)
                rows = _matmul(x3[:, T - keep:].reshape(Bsz * keep, D_MODEL), P['c_w_qkv'][o][:, D_MODEL:],
                               TM, D_MODEL)
                rows = rows.reshape(Bsz, keep, 2, C_HEADS, C_HEAD_DIM)
            else:
                qkv = _matmul(xf, P['c_w_qkv'][o], TM, D_MODEL).reshape(Bsz, T, 3 * D_MODEL)
                att = _attention_sample(qkv, cache_k[o], cache_v[o], P['c_bias_sample'][o])
                rows = qkv.reshape(Bsz, T, 3, C_HEADS, C_HEAD_DIM)[:, :, 1:]
            k_o.append(rows[:, :, 0].transpose(0, 2, 1, 3))
            v_o.append(rows[:, :, 1].transpose(0, 2, 1, 3))
            pairs = [(att.reshape(M, D_MODEL), P['c_w_o'][o])]
        xf = _mix_ffn(pairs, xf, P['ln1_w'][layer], P['ln1_b'][layer], P['ffn_w_in'][layer], P['ffn_w_out'][layer],
                      P['ln2_w'][layer], P['ln2_b'][layer], TM)
    return (xf.reshape(Bsz, T, D_MODEL), jnp.stack(wkv_o), jnp.stack(shift_o), jnp.stack(gla_o),
            jnp.stack(k_o), jnp.stack(v_o))


def _prepare(w_in_mix, a_mu, a_w0, a_w2, a_a0, a_a2, a_g2, a_k_k, a_k_a, a_r_k, a_ln_w, a_ln_b,
             b_alpha_up, b_alpha_bias, b_norm_w, w_out_mix, c_w_qkv, c_rel_bias, c_w_o,
             ln1_w, ln1_b, ln2_w, ln2_b, ffn_w_in, ffn_w_out, sample_len, cache_rows):
    kw, vw = B_KWIDTH, B_VWIDTH
    wb = w_in_mix[:, :, A_PROJ:]
    main = jnp.concatenate([wb[:, :, :2 * kw + vw], wb[:, :, 2 * kw + vw + B_GATE_LORA:]], axis=-1)
    lora = jnp.pad(wb[:, :, 2 * kw + vw:2 * kw + vw + B_GATE_LORA], ((0, 0), (0, 0), (0, B_LORA_PAD - B_GATE_LORA)))
    row3 = lambda t: t.reshape(t.shape[0], 1, -1)
    bias_prompt = jnp.stack([_rel_bias_tile(t, ATT_QB, ATT_WIN, C_PAST_ROWS) for t in c_rel_bias])
    rel = np.arange(ATT_WIN)[None, :] // CHUNK - np.arange(ATT_QB)[:, None] // CHUNK
    band = (rel >= 0) & (rel <= C_PAST_CHUNKS)
    bias_prompt = jnp.where(band, bias_prompt, NEG_INF)
    q_scale = jnp.concatenate([jnp.full((D_MODEL,), ATT_SCALE, F32), jnp.ones((2 * D_MODEL,), F32)])
    P = {
        'w_in_a': w_in_mix[:, :, :A_PROJ].astype(BF16),
        'w_in_b': jnp.concatenate([main, lora], axis=-1).astype(BF16),
        'a': dict(mu=row3(a_mu), w0=row3(a_w0), w2=a_w2, a0=row3(a_a0), a2=a_a2, g2=a_g2, k_k=row3(a_k_k),
                  k_a=row3(a_k_a), r_k=row3(a_r_k), ln_w=row3(a_ln_w), ln_b=row3(a_ln_b)),
        'b': dict(up=jnp.pad(b_alpha_up, ((0, 0), (0, B_LORA_PAD - B_GATE_LORA), (0, 0))),
                  bias=row3(b_alpha_bias), nw=row3(b_norm_w)),
        'w_out_a': w_out_mix[:, :A_WIDTH].astype(BF16),
        'w_out_b': w_out_mix[:, A_WIDTH:].astype(BF16),
        'c_w_qkv': c_w_qkv.astype(BF16),
        'c_w_qkv_scaled': (c_w_qkv * q_scale).astype(BF16),
        'c_w_o': c_w_o.astype(BF16),
        'c_bias_prompt': bias_prompt.reshape(bias_prompt.shape[0], C_HEADS // 2, 2 * ATT_QB, ATT_WIN),
        'c_bias_sample': jnp.stack([_rel_bias_tile(t, sample_len, cache_rows + sample_len, cache_rows)
                                    for t in c_rel_bias]),
        'ln1_w': row3(ln1_w), 'ln1_b': row3(ln1_b), 'ln2_w': row3(ln2_w), 'ln2_b': row3(ln2_b),
        'ffn_w_in': ffn_w_in.astype(BF16),
        'ffn_w_out': ffn_w_out.astype(BF16),
    }
    return P


def kernel(x_prompt, x_sample, state_a_wkv, state_a_shift, state_b_gla, cache_c_k, cache_c_v, w_in_mix, a_mu, a_w0, a_w2, a_a0, a_a2, a_g2, a_k_k, a_k_a, a_r_k, a_ln_w, a_ln_b, b_alpha_up, b_alpha_bias, b_norm_w, w_out_mix, c_w_qkv, c_rel_bias, c_w_o, ln1_w, ln1_b, ln2_w, ln2_b, ffn_w_in, ffn_w_out):
    P = _prepare(w_in_mix, a_mu, a_w0, a_w2, a_a0, a_a2, a_g2, a_k_k, a_k_a, a_r_k, a_ln_w, a_ln_b,
                 b_alpha_up, b_alpha_bias, b_norm_w, w_out_mix, c_w_qkv, c_rel_bias, c_w_o,
                 ln1_w, ln1_b, ln2_w, ln2_b, ffn_w_in, ffn_w_out, x_sample.shape[1], cache_c_k.shape[3])
    bp = x_prompt.shape[0]
    dt = x_prompt.dtype
    n_even = state_a_wkv.shape[0]
    wkv_zero = jnp.zeros((n_even, bp, A_HEADS, A_HEAD_DIM, A_HEAD_DIM), dt)
    shift_zero = jnp.zeros((n_even, bp, A_PROJ), dt)
    gla_zero = jnp.zeros((n_even, bp, B_HEADS, B_KEY_DIM, B_VAL_DIM), dt)
    y_prompt, p_wkv, p_shift, p_gla, p_k, p_v = _trunk(x_prompt, wkv_zero, shift_zero, gla_zero, None, None, P)
    y_sample, s_wkv, s_shift, s_gla, s_k, s_v = _trunk(
        x_sample, state_a_wkv, state_a_shift, state_b_gla, cache_c_k, cache_c_v, P)
    return (y_prompt, y_sample, p_wkv, p_shift, p_gla, p_k, p_v, s_wkv, s_shift, s_gla, s_k, s_v)
```

```python
import functools

import jax
import jax.numpy as jnp
import numpy as np
from jax import lax
from jax.experimental import pallas as pl
from jax.experimental.pallas import tpu as pltpu

F32 = jnp.float32
BF16 = jnp.bfloat16

D_MODEL = 1024
DEPTH = 4
CHUNK = 64
A_WIDTH = 512
A_HEAD_DIM = 64
A_HEADS = 8
A_DECAY_LORA = 64
A_ICL_LORA = 64
A_GATE_LORA = 128
A_PROJ = 3 * A_WIDTH + A_DECAY_LORA + A_ICL_LORA + A_GATE_LORA
A_NORM_EPS = 64e-5
B_VWIDTH = 512
B_KWIDTH = 256
B_HEADS = 4
B_KEY_DIM = 64
B_VAL_DIM = 128
B_GATE_LORA = 16
B_GATE_NORM = 16.0
B_LORA_PAD = 128
B_PROJ_PAD = 2 * B_KWIDTH + 2 * B_VWIDTH + B_LORA_PAD
C_HEAD_DIM = 64
C_HEADS = 16
C_PAST_CHUNKS = 8
C_PAST_ROWS = C_PAST_CHUNKS * CHUNK
C_REL_CLIP = 128
FFN_HIDDEN = 2816
LN_EPS = 1e-5
DEEPNORM_ALPHA = (2.0 * DEPTH) ** 0.25
NEG_INF = -1e30

LANE = 128
PAIR = LANE
HALF = PAIR // 2
A_PAIRS = A_HEADS // 2
B_PAIRS = B_HEADS // 2
VMEM_LIMIT = 56 * 1024 * 1024

NN = (((1,), (0,)), ((), ()))
NT = (((1,), (1,)), ((), ()))
TN = (((0,), (0,)), ((), ()))


def _params(*sem):
    return pltpu.CompilerParams(dimension_semantics=sem, vmem_limit_bytes=VMEM_LIMIT)


def _dg(a, b, dn):
    return lax.dot_general(a, b, dn, preferred_element_type=F32)


def _dot1(a, b, dn=NN):
    return _dg(a.astype(BF16), b.astype(BF16), dn)


def _split(a):
    hi = a.astype(BF16)
    lo = (a - hi.astype(F32)).astype(BF16)
    return hi, lo


def _dot3s(a, b, dn=NN):
    return _dg(a[0], b[0], dn) + _dg(a[0], b[1], dn) + _dg(a[1], b[0], dn)


def _dot3(a, b, dn=NN):
    return _dot3s(_split(a), _split(b), dn)


def _dot_xl(a_exact, b, dn=NN):
    bh, bl = _split(b)
    return _dg(a_exact, bh, dn) + _dg(a_exact, bl, dn)


def _dot_xr(a, b_exact, dn=NN):
    ah, al = _split(a)
    return _dg(ah, b_exact, dn) + _dg(al, b_exact, dn)


def _ln(z, w, b):
    mu = jnp.mean(z, -1, keepdims=True)
    d = z - mu
    var = jnp.mean(d * d, -1, keepdims=True)
    return d * lax.rsqrt(var + LN_EPS) * w + b


def _sigmoid(x):
    return 1.0 / (1.0 + jnp.exp(-x))


def _softplus(x):
    return jnp.maximum(x, 0.0) + jnp.log(1.0 + jnp.exp(-jnp.abs(x)))


def _stack_masked(x, m0):
    return jnp.concatenate([jnp.where(m0, x, 0.0), jnp.where(m0, 0.0, x)], axis=0)


def _fold(x):
    n = x.shape[0] // 2
    return x[:n] + x[n:]


def _mm_kernel(x_ref, w_ref, o_ref):
    o_ref[...] = jnp.dot(x_ref[...].astype(BF16), w_ref[...], preferred_element_type=F32)


def _matmul(x, w, tm, tn):
    M, K = x.shape
    N = w.shape[1]
    tm = min(tm, M)
    return pl.pallas_call(
        _mm_kernel,
        grid=(M // tm, N // tn),
        in_specs=[pl.BlockSpec((tm, K), lambda i, j: (i, 0)),
                  pl.BlockSpec((K, tn), lambda i, j: (0, j))],
        out_specs=pl.BlockSpec((tm, tn), lambda i, j: (i, j)),
        out_shape=jax.ShapeDtypeStruct((M, N), F32),
        compiler_params=_params("parallel", "arbitrary"),
    )(x, w)


def _in_proj_kernel(x_ref, wa_ref, wb_ref, oa_ref, ob_ref):
    xb = x_ref[...].astype(BF16)
    oa_ref[...] = jnp.dot(xb, wa_ref[...], preferred_element_type=F32)
    ob_ref[...] = jnp.dot(xb, wb_ref[...], preferred_element_type=F32)


def _in_proj(x, wa, wb, tm):
    M, K = x.shape
    tm = min(tm, M)
    na, nb = wa.shape[1], wb.shape[1]
    resident = lambda n: pl.BlockSpec((K, n), lambda i: (0, 0), pipeline_mode=pl.Buffered(1))
    return pl.pallas_call(
        _in_proj_kernel,
        grid=(M // tm,),
        in_specs=[pl.BlockSpec((tm, K), lambda i: (i, 0)), resident(na), resident(nb)],
        out_specs=[pl.BlockSpec((tm, na), lambda i: (i, 0)), pl.BlockSpec((tm, nb), lambda i: (i, 0))],
        out_shape=[jax.ShapeDtypeStruct((M, na), F32), jax.ShapeDtypeStruct((M, nb), F32)],
        compiler_params=_params("parallel"),
    )(x, wa, wb)


def _mix_ffn_kernel(n_pairs, *refs):
    res_ref, lw1_ref, lb1_ref, wg_ref, wu_ref, wo_ref, lw2_ref, lb2_ref, o_ref = refs[2 * n_pairs:]
    acc = None
    for p in range(n_pairs):
        part = jnp.dot(refs[2 * p][...].astype(BF16), refs[2 * p + 1][...], preferred_element_type=F32)
        acc = part if acc is None else acc + part
    x = _ln(DEEPNORM_ALPHA * res_ref[...] + acc, lw1_ref[...], lb1_ref[...])
    xb = x.astype(BF16)
    g = jnp.dot(xb, wg_ref[...], preferred_element_type=F32)
    u = jnp.dot(xb, wu_ref[...], preferred_element_type=F32)
    act = (g * _sigmoid(g) * u).astype(BF16)
    y = jnp.dot(act, wo_ref[...], preferred_element_type=F32)
    o_ref[...] = _ln(DEEPNORM_ALPHA * x + y, lw2_ref[...], lb2_ref[...])


def _mix_ffn(pairs, res, lw1, lb1, w_in, w_out, lw2, lb2, tm):
    M = res.shape[0]
    tm = min(tm, M)
    resident = lambda shape, idx: pl.BlockSpec(shape, idx, pipeline_mode=pl.Buffered(1))
    vec = pl.BlockSpec((1, D_MODEL), lambda i: (0, 0))
    in_specs, args = [], []
    for a, w in pairs:
        K = a.shape[1]
        in_specs += [pl.BlockSpec((tm, K), lambda i: (i, 0)), resident((K, D_MODEL), lambda i: (0, 0))]
        args += [a, w]
    in_specs += [pl.BlockSpec((tm, D_MODEL), lambda i: (i, 0)), vec, vec,
                 resident((D_MODEL, FFN_HIDDEN), lambda i: (0, 0)),
                 resident((D_MODEL, FFN_HIDDEN), lambda i: (0, 1)),
                 resident((FFN_HIDDEN, D_MODEL), lambda i: (0, 0)), vec, vec]
    return pl.pallas_call(
        functools.partial(_mix_ffn_kernel, len(pairs)),
        grid=(M // tm,),
        in_specs=in_specs,
        out_specs=pl.BlockSpec((tm, D_MODEL), lambda i: (i, 0)),
        out_shape=jax.ShapeDtypeStruct((M, D_MODEL), F32),
        compiler_params=_params("parallel"),
    )(*args, res, lw1, lb1, w_in, w_in, w_out, lw2, lb2)


def _inv_unit_lower(ns, eye, nilpotency):
    size = eye.shape[0]
    xs = [eye - n for n in ns]
    ps = [_dot1(n, n) for n in ns]
    k = 2
    while k < nilpotency:
        k *= 2
        if k < nilpotency:
            rs = [_dot1(p, jnp.concatenate([x, p], axis=1)) for x, p in zip(xs, ps)]
            xs = [x + r[:, :size] for x, r in zip(xs, rs)]
            ps = [r[:, size:] for r in rs]
        else:
            xs = [x + _dot1(p, x) for x, p in zip(xs, ps)]
    return xs


def _segsum(x, seg):
    n = x.shape[1] // PAIR
    rows = x.shape[0]
    xs = jnp.concatenate([x[:, j * PAIR:(j + 1) * PAIR] for j in range(n)], axis=0)
    s = _dot_xr(xs, seg)
    return jnp.concatenate([s[j * rows:(j + 1) * rows] for j in range(n)], axis=1)


def _head_mask():
    hr = lax.broadcasted_iota(jnp.int32, (PAIR, PAIR), 0) // A_HEAD_DIM
    hc = lax.broadcasted_iota(jnp.int32, (PAIR, PAIR), 1) // A_HEAD_DIM
    return (hr == hc).astype(BF16)


def _rwkv_prep_kernel(n_chunks, pa_ref, prev_ref, shift_ref, mu_ref, w0_ref, w2_ref, a0_ref, a2_ref, g2_ref,
                      kk_ref, ka_ref, rk_ref,
                      mh_ref, ch_ref, qc_ref, oc_ref, g_ref, bonus_ref):
    i = pl.program_id(1)
    L = CHUNK
    pa = pa_ref[...]
    first = jnp.where(i == 0, shift_ref[...], prev_ref[7:8, :])
    rows = lax.broadcasted_iota(jnp.int32, pa.shape, 0)
    prev = jnp.where(rows == 0, first, pltpu.roll(pa, shift=1, axis=0))
    xs = pa + (prev - pa) * mu_ref[...]
    W = A_WIDTH
    r = xs[:, 0:W]
    k0 = xs[:, W:2 * W]
    v = xs[:, 2 * W:3 * W]
    xw = xs[:, 3 * W:3 * W + A_DECAY_LORA]
    xa = xs[:, 3 * W + A_DECAY_LORA:3 * W + A_DECAY_LORA + A_ICL_LORA]
    xg = xs[:, 3 * W + A_DECAY_LORA + A_ICL_LORA:]
    w = -_softplus(-(w0_ref[...] + _dot3(jnp.tanh(xw), w2_ref[...]))) - 0.5
    logw = -jnp.exp(w)
    a = _sigmoid(a0_ref[...] + _dot3(xa, a2_ref[...]))
    g_ref[...] = _dot3(_sigmoid(xg), g2_ref[...])
    seg = _head_mask()
    kk = k0 * kk_ref[...]
    kk = kk * lax.rsqrt(_segsum(kk * kk, seg) + 1e-12)
    k = k0 * (1.0 + (a - 1.0) * ka_ref[...])
    bonus_ref[...] = _segsum(r * k * rk_ref[...], seg) * v
    b = kk * a

    tri = (lax.broadcasted_iota(jnp.int32, (L, L), 0) >= lax.broadcasted_iota(jnp.int32, (L, L), 1)).astype(BF16)
    row = lax.broadcasted_iota(jnp.int32, (PAIR, PAIR), 0)
    col = lax.broadcasted_iota(jnp.int32, (PAIR, PAIR), 1)
    eye = (row == col).astype(F32)
    strict = row > col
    incl2 = (lax.broadcasted_iota(jnp.int32, (PAIR, 2 * PAIR), 0)
             >= lax.broadcasted_iota(jnp.int32, (PAIR, 2 * PAIR), 1) % PAIR)
    m0 = lax.broadcasted_iota(jnp.int32, (L, PAIR), 1) < HALF
    pairs = range(A_PAIRS)
    lanes = [slice(j * PAIR, (j + 1) * PAIR) for j in pairs]
    chains = [(c, j) for c in range(n_chunks) for j in pairs]
    every = range(len(chains))
    rows_of = [slice(c * L, (c + 1) * L) for c in range(n_chunks)]
    kkg, rg, kinv, binv, kd, bd, e_last = [], [], [], [], [], [], []
    for sl in rows_of:
        lw = logw[sl]
        cum = _dot_xl(tri, lw)
        c_last = cum[L - 1:L, :]
        e_neg = jnp.exp(-cum)
        e_rem = jnp.exp(c_last - cum)
        kkg.append(kk[sl] * jnp.exp(cum - lw))
        rg.append(r[sl] * jnp.exp(cum))
        kinv.append(k[sl] * e_neg)
        binv.append(b[sl] * e_neg)
        kd.append(k[sl] * e_rem)
        bd.append(b[sl] * e_rem)
        e_last.append(jnp.exp(c_last))
    sm = lambda xs_, c, j: _stack_masked(xs_[c][:, lanes[j]], m0).astype(BF16)
    kkg_b = [sm(kkg, c, j) for c, j in chains]
    v_b = [_stack_masked(v[rows_of[c], lanes[j]], m0).astype(BF16) for c, j in chains]
    bd_b = [sm(bd, c, j) for c, j in chains]
    kd_b = [sm(kd, c, j) for c, j in chains]
    rg_b = [sm(rg, c, j) for c, j in chains]
    kinv_b = [sm(kinv, c, j) for c, j in chains]
    binv_b = [sm(binv, c, j) for c, j in chains]
    aa = [_dg(jnp.concatenate([kkg_b[n], rg_b[n]], axis=0), jnp.concatenate([binv_b[n], kinv_b[n]], axis=0), NT)
          for n in every]
    a_kb = [jnp.where(strict, x[:PAIR, :PAIR], 0.0) for x in aa]
    a_kk = [jnp.where(strict, x[:PAIR, PAIR:], 0.0).astype(BF16) for x in aa]
    a_rbk = [jnp.where(incl2, x[PAIR:, :], 0.0).astype(BF16) for x in aa]
    t = _inv_unit_lower(a_kb, eye, L)
    av = [_dg(a_kk[n], v_b[n], NN).astype(BF16) for n in every]
    wu = [_dg(t[n].astype(BF16), jnp.concatenate([kkg_b[n], av[n]], axis=1), NN) for n in every]
    wu_b = [x.astype(BF16) for x in wu]
    mc = [_dg(bd_b[n], wu_b[n], TN) for n in every]
    kv = [_dg(kd_b[n], v_b[n], TN) for n in every]
    rhs = [jnp.concatenate([jnp.concatenate([wu_b[n][:, :PAIR], (-wu[n][:, PAIR:]).astype(BF16)], axis=1),
                            jnp.concatenate([jnp.zeros_like(v_b[n]), v_b[n]], axis=1)], axis=0) for n in every]
    qo = [_dg(a_rbk[n], rhs[n], NN) for n in every]
    for n, (c, j) in enumerate(chains):
        mh_ref[c, j] = (eye * e_last[c][:, lanes[j]] - mc[n][:, :PAIR]).astype(BF16)
        ch_ref[c, j] = kv[n] - mc[n][:, PAIR:]
        qc_ref[rows_of[c], lanes[j]] = (rg[c][:, lanes[j]] - _fold(qo[n][:, :PAIR])).astype(BF16)
        oc_ref[rows_of[c], lanes[j]] = _fold(qo[n][:, PAIR:])


def _rwkv_scan_kernel(n_seq, n_chunks, mh_ref, ch_ref, qc_ref, oc_ref, g_ref, bonus_ref, h0_ref, lnw_ref, lnb_ref,
                      y_ref, hout_ref, h_ref):
    i = pl.program_id(1)
    L = CHUNK

    @pl.when(i == 0)
    def _():
        h_ref[...] = h0_ref[...]

    seg = _head_mask()
    inv_n = 1.0 / A_HEAD_DIM
    lanes = [slice(j * PAIR, (j + 1) * PAIR) for j in range(A_PAIRS)]
    chains = [(s, j) for s in range(n_seq) for j in range(A_PAIRS)]
    hb = [h_ref[s, j] for s, j in chains]
    for c in range(n_chunks):
        sl = slice(c * L, (c + 1) * L)
        hb_b = [x.astype(BF16) for x in hb]
        o = [_dg(qc_ref[s, sl, lanes[j]], hb_b[n], NN) + oc_ref[s, sl, lanes[j]] for n, (s, j) in enumerate(chains)]
        hb = [_dg(mh_ref[s, c, j], hb_b[n], NN) + ch_ref[s, c, j] for n, (s, j) in enumerate(chains)]
        for s in range(n_seq):
            os_ = jnp.concatenate(o[s * A_PAIRS:(s + 1) * A_PAIRS], axis=1)
            mean = _segsum(os_, seg) * inv_n
            d = os_ - mean
            var = _segsum(d * d, seg) * inv_n
            on = d * lax.rsqrt(var + A_NORM_EPS) * lnw_ref[...] + lnb_ref[...]
            y_ref[s, sl, :] = ((on + bonus_ref[s, sl, :]) * g_ref[s, sl, :]).astype(y_ref.dtype)
    for n, (s, j) in enumerate(chains):
        h_ref[s, j] = hb[n]
    hout_ref[...] = h_ref[...]


def _rwkv_mix(pa, shift0, wkv0, prm):
    Bsz, T, _ = pa.shape
    n1 = next(n for n in (8, 4, 2, 1) if (T // CHUNK) % n == 0)
    R1 = n1 * CHUNK
    W = A_WIDTH
    nblk = T // R1
    row = lambda n: pl.BlockSpec((1, n), lambda b, i: (0, 0))
    full = lambda s: pl.BlockSpec(s, lambda b, i: (0,) * len(s))
    act = lambda r, n: pl.BlockSpec((None, r, n), lambda b, i: (b, i, 0))
    mat = lambda n: pl.BlockSpec((None, n, A_PAIRS, PAIR, PAIR), lambda b, i: (b, i, 0, 0, 0))
    sub = R1 // 8
    mats = lambda dt: jax.ShapeDtypeStruct((Bsz, T // CHUNK, A_PAIRS, PAIR, PAIR), dt)
    acts = lambda dt: jax.ShapeDtypeStruct((Bsz, T, W), dt)
    mh, ch, qc, oc, g, bonus = pl.pallas_call(
        functools.partial(_rwkv_prep_kernel, n1),
        grid=(Bsz, nblk),
        in_specs=[act(R1, A_PROJ),
                  pl.BlockSpec((None, 8, A_PROJ), lambda b, i: (b, jnp.maximum(i * sub - 1, 0), 0)),
                  pl.BlockSpec((None, 1, A_PROJ), lambda b, i: (b, 0, 0)),
                  row(A_PROJ), row(W), full((A_DECAY_LORA, W)), row(W), full((A_ICL_LORA, W)),
                  full((A_GATE_LORA, W)), row(W), row(W), row(W)],
        out_specs=[mat(n1), mat(n1), act(R1, W), act(R1, W), act(R1, W), act(R1, W)],
        out_shape=[mats(BF16), mats(F32), acts(BF16), acts(F32), acts(F32), acts(F32)],
        compiler_params=_params("parallel", "parallel"),
    )(pa, pa, shift0[:, None, :], prm['mu'], prm['w0'], prm['w2'], prm['a0'], prm['a2'], prm['g2'],
      prm['k_k'], prm['k_a'], prm['r_k'])

    n2 = min(4, T // CHUNK)
    R2 = n2 * CHUNK
    hh = wkv0.transpose(0, 1, 3, 2).reshape(Bsz, A_PAIRS, 2, A_HEAD_DIM, A_HEAD_DIM)
    zero = jnp.zeros_like(hh[:, :, 0])
    h0 = jnp.concatenate([jnp.concatenate([hh[:, :, 0], zero], axis=-1),
                          jnp.concatenate([zero, hh[:, :, 1]], axis=-1)], axis=-2)
    ns = 2 if Bsz % 2 == 0 else 1
    state = pl.BlockSpec((ns, A_PAIRS, PAIR, PAIR), lambda b, i: (b, 0, 0, 0))
    act2 = lambda r, n: pl.BlockSpec((ns, r, n), lambda b, i: (b, i, 0))
    mat2 = lambda n: pl.BlockSpec((ns, n, A_PAIRS, PAIR, PAIR), lambda b, i: (b, i, 0, 0, 0))
    y, hout = pl.pallas_call(
        functools.partial(_rwkv_scan_kernel, ns, n2),
        grid=(Bsz // ns, T // R2),
        in_specs=[mat2(n2), mat2(n2), act2(R2, W), act2(R2, W), act2(R2, W), act2(R2, W), state,
                  row(W), row(W)],
        out_specs=[act2(R2, W), state],
        out_shape=[acts(BF16), jax.ShapeDtypeStruct((Bsz, A_PAIRS, PAIR, PAIR), F32)],
        scratch_shapes=[pltpu.VMEM((ns, A_PAIRS, PAIR, PAIR), F32)],
        compiler_params=_params("parallel", "arbitrary"),
    )(mh, ch, qc, oc, g, bonus, h0, prm['ln_w'], prm['ln_b'])
    hd = jnp.stack([hout[:, :, :HALF, :HALF], hout[:, :, HALF:, HALF:]], axis=2)
    wkv = hd.reshape(Bsz, A_HEADS, A_HEAD_DIM, A_HEAD_DIM).transpose(0, 1, 3, 2)
    return y, wkv


def _gla_kernel(n_seq, n_chunks, pb_ref, s0_ref, up_ref, bias_ref, nw_ref, y_ref, sout_ref, s_ref):
    i = pl.program_id(1)
    L = CHUNK

    @pl.when(i == 0)
    def _():
        s_ref[...] = s0_ref[...]

    kw, vw = B_KWIDTH, B_VWIDTH
    incl = lax.broadcasted_iota(jnp.int32, (L, L), 0) >= lax.broadcasted_iota(jnp.int32, (L, L), 1)
    tri = incl.astype(BF16)
    m0 = lax.broadcasted_iota(jnp.int32, (L, PAIR), 1) < HALF
    top = lax.broadcasted_iota(jnp.int32, (PAIR, B_VAL_DIM), 0) < HALF
    scale = B_KEY_DIM ** -0.5
    pairs = range(B_PAIRS)
    heads = range(B_HEADS)
    pl_ = lambda h: slice((h // 2) * PAIR, (h // 2 + 1) * PAIR)
    units = [(s, c) for c in range(n_chunks) for s in range(n_seq)]
    rows_of = lambda c: slice(c * L, (c + 1) * L)
    vslice = lambda h: slice(h * B_VAL_DIM, (h + 1) * B_VAL_DIM)
    z = [_dot3(pb_ref[s, rows_of(c), 2 * kw + 2 * vw:], up_ref[...]) + bias_ref[...] for s, c in units]
    bc = [_dot_xl(tri, -_softplus(-x) * (1.0 / B_GATE_NORM)) for x in z]
    b_last = [x[L - 1:L, :] for x in bc]
    q_dec, k_inv, k_state, e_last_t = [], [], [], []
    for n, (s, c) in enumerate(units):
        k = pb_ref[s, rows_of(c), kw:2 * kw]
        q_dec.append(pb_ref[s, rows_of(c), 0:kw] * scale * jnp.exp(bc[n]))
        k_inv.append((k * jnp.exp(-bc[n])).astype(BF16))
        k_state.append((k * jnp.exp(b_last[n] - bc[n])).astype(BF16))
        e_last_t.append(jnp.broadcast_to(jnp.exp(b_last[n]), (LANE, kw)).T)
    uh = [(n, h) for n in range(len(units)) for h in heads]
    vb = [pb_ref[units[n][0], rows_of(units[n][1]), 2 * kw + h * B_VAL_DIM:2 * kw + (h + 1) * B_VAL_DIM].astype(BF16)
          for n, h in uh]
    qm = [jnp.where(m0 if h % 2 == 0 else ~m0, q_dec[n][:, pl_(h)], 0.0).astype(BF16) for n, h in uh]
    scores = [jnp.where(incl, _dg(qm[i_], k_inv[n][:, pl_(h)], NT), 0.0).astype(BF16) for i_, (n, h) in enumerate(uh)]
    upd = [_dg(k_state[n][:, pl_(h)], vb[i_], TN) for i_, (n, h) in enumerate(uh)]
    intra = [_dg(scores[i_], vb[i_], NN) for i_ in range(len(uh))]
    states = [[s_ref[s, j] for j in pairs] for s in range(n_seq)]
    before = []
    for n, (s, c) in enumerate(units):
        before.append([x.astype(BF16) for x in states[s]])
        states[s] = [states[s][j] * e_last_t[n][j * PAIR:(j + 1) * PAIR, :]
                     + jnp.where(top, upd[n * B_HEADS + 2 * j], upd[n * B_HEADS + 2 * j + 1]) for j in pairs]
    o = [_dg(qm[i_], before[n][h // 2], NN) + intra[i_] for i_, (n, h) in enumerate(uh)]
    for i_, (n, h) in enumerate(uh):
        s, c = units[n]
        oh = o[i_] * lax.rsqrt(jnp.mean(o[i_] * o[i_], -1, keepdims=True) + LN_EPS) * nw_ref[...]
        rg_h = pb_ref[s, rows_of(c), 2 * kw + vw + h * B_VAL_DIM:2 * kw + vw + (h + 1) * B_VAL_DIM]
        y_ref[s, rows_of(c), vslice(h)] = (oh * (rg_h * _sigmoid(rg_h))).astype(y_ref.dtype)
    for s in range(n_seq):
        for j in pairs:
            s_ref[s, j] = states[s][j]
    sout_ref[...] = s_ref[...]


def _gla_mix(pb, s0, prm):
    Bsz, T, _ = pb.shape
    n = min(4, T // CHUNK)
    Rr = n * CHUNK
    ns = 2 if Bsz % 2 == 0 else 1
    st = pl.BlockSpec((ns, B_PAIRS, PAIR, B_VAL_DIM), lambda b, i: (b, 0, 0, 0))
    y, s = pl.pallas_call(
        functools.partial(_gla_kernel, ns, n),
        grid=(Bsz // ns, T // Rr),
        in_specs=[pl.BlockSpec((ns, Rr, B_PROJ_PAD), lambda b, i: (b, i, 0)), st,
                  pl.BlockSpec((B_LORA_PAD, B_KWIDTH), lambda b, i: (0, 0)),
                  pl.BlockSpec((1, B_KWIDTH), lambda b, i: (0, 0)),
                  pl.BlockSpec((1, B_VAL_DIM), lambda b, i: (0, 0))],
        out_specs=[pl.BlockSpec((ns, Rr, B_VWIDTH), lambda b, i: (b, i, 0)), st],
        out_shape=[jax.ShapeDtypeStruct((Bsz, T, B_VWIDTH), BF16),
                   jax.ShapeDtypeStruct((Bsz, B_PAIRS, PAIR, B_VAL_DIM), F32)],
        scratch_shapes=[pltpu.VMEM((ns, B_PAIRS, PAIR, B_VAL_DIM), F32)],
        compiler_params=_params("parallel", "arbitrary"),
    )(pb, s0.reshape(Bsz, B_PAIRS, PAIR, B_VAL_DIM), prm['up'], prm['bias'], prm['nw'])
    return y, s.reshape(Bsz, B_HEADS, B_KEY_DIM, B_VAL_DIM)


ATT_QB = 2 * CHUNK
ATT_SCALE = C_HEAD_DIM ** -0.5
LOG2E = 1.4426950408889634


ATT_WIN = C_PAST_ROWS + ATT_QB


def _qkv_pad_kernel(n_pad, x_ref, w_ref, o_ref):
    r = pl.program_id(1)

    @pl.when(r < n_pad)
    def _():
        o_ref[...] = jnp.zeros_like(o_ref)

    @pl.when(r >= n_pad)
    def _():
        o_ref[...] = jnp.dot(x_ref[...].astype(BF16), w_ref[...], preferred_element_type=F32).astype(BF16)


def _qkv_padded(x, w):
    Bsz, T, D = x.shape
    N = w.shape[1]
    tm = next(t for t in (512, 256, 128) if T % t == 0)
    n_pad = C_PAST_ROWS // tm
    return pl.pallas_call(
        functools.partial(_qkv_pad_kernel, n_pad),
        grid=(Bsz, T // tm + n_pad),
        in_specs=[pl.BlockSpec((None, tm, D), lambda b, r: (b, jnp.maximum(r - n_pad, 0), 0)),
                  pl.BlockSpec((D, N), lambda b, r: (0, 0), pipeline_mode=pl.Buffered(1))],
        out_specs=pl.BlockSpec((None, tm, N), lambda b, r: (b, r, 0)),
        out_shape=jax.ShapeDtypeStruct((Bsz, C_PAST_ROWS + T, N), BF16),
        compiler_params=_params("parallel", "arbitrary"),
    )(x, w)


def _kv_tail(x, w, keep):
    Bsz, T, D = x.shape
    tm = next(t for t in (512, 256, 128, 64) if keep % t == 0 and T % t == 0)
    first = (T - keep) // tm
    return pl.pallas_call(
        _mm_kernel,
        grid=(Bsz, keep // tm, 2),
        in_specs=[pl.BlockSpec((None, tm, D), lambda b, r, j: (b, first + r, 0)),
                  pl.BlockSpec((D, D), lambda b, r, j: (0, j + 1))],
        out_specs=pl.BlockSpec((None, tm, D), lambda b, r, j: (b, r, j)),
        out_shape=jax.ShapeDtypeStruct((Bsz, keep, 2 * D), F32),
        compiler_params=_params("parallel", "parallel", "arbitrary"),
    )(x, w)


def _att_prompt_kernel(n_qb, q_ref, k_ref, v_ref, bias_ref, o_ref):
    g = pl.program_id(2)
    m0 = lax.broadcasted_iota(jnp.int32, (ATT_QB, PAIR), 1) < HALF
    col = lax.broadcasted_iota(jnp.int32, (2 * ATT_QB, ATT_WIN), 1)

    def run(window_has_padding):
        blocks = range(n_qb)
        rows = [slice(t * ATT_QB, (t + 1) * ATT_QB) for t in blocks]
        wins = [pl.ds(pl.multiple_of((g * n_qb + t) * ATT_QB, ATT_QB), ATT_WIN) for t in blocks]
        q = [_stack_masked(q_ref[rows[t], :], m0) for t in blocks]
        s = [_dg(q[t], k_ref[wins[t], :], NT) + bias_ref[...] for t in blocks]
        if window_has_padding:
            s = [jnp.where(col >= C_PAST_ROWS - (g * n_qb + t) * ATT_QB, s[t], NEG_INF) for t in blocks]
        p = [jnp.exp2(x - x.max(-1, keepdims=True)) for x in s]
        o = [_dg(p[t].astype(BF16), v_ref[wins[t], :], NN) / p[t].sum(-1, keepdims=True) for t in blocks]
        for t in blocks:
            o_ref[rows[t], :] = jnp.where(m0, o[t][:ATT_QB], o[t][ATT_QB:]).astype(o_ref.dtype)

    first = g * (n_qb * ATT_QB) < C_PAST_ROWS
    pl.when(first)(lambda: run(True))
    pl.when(jnp.logical_not(first))(lambda: run(False))


def _attention_prompt(qkv, bias):
    Bsz, Tp, _ = qkv.shape
    T = Tp - C_PAST_ROWS
    nhp = C_HEADS // 2
    nblk = T // ATT_QB
    n_qb = next(n for n in (4, 2, 1) if nblk % n == 0)
    rows = n_qb * ATT_QB
    skip = C_PAST_ROWS // rows
    return pl.pallas_call(
        functools.partial(_att_prompt_kernel, n_qb),
        grid=(Bsz, nhp, T // rows),
        in_specs=[pl.BlockSpec((None, rows, LANE), lambda b, hp, g: (b, g + skip, hp)),
                  pl.BlockSpec((None, Tp, LANE), lambda b, hp, g: (b, 0, nhp + hp)),
                  pl.BlockSpec((None, Tp, LANE), lambda b, hp, g: (b, 0, 2 * nhp + hp)),
                  pl.BlockSpec((None, 2 * ATT_QB, ATT_WIN), lambda b, hp, g: (hp, 0, 0))],
        out_specs=pl.BlockSpec((None, rows, LANE), lambda b, hp, g: (b, g, hp)),
        out_shape=jax.ShapeDtypeStruct((Bsz, T, D_MODEL), BF16),
        compiler_params=_params("parallel", "parallel", "arbitrary"),
    )(qkv, qkv, qkv, bias)


def _att_sample_kernel(qkv_ref, kc_ref, vc_ref, bias_ref, o_ref):
    R = kc_ref.shape[1]
    heads = range(C_HEADS)
    col = lambda part, h: slice(part * D_MODEL + h * C_HEAD_DIM, part * D_MODEL + (h + 1) * C_HEAD_DIM)
    q = [(qkv_ref[:, col(0, h)] * ATT_SCALE).astype(BF16) for h in heads]
    s_c = [_dg(q[h], kc_ref[h].astype(BF16), NT) + bias_ref[h, :, 0:R] for h in heads]
    s_n = [_dg(q[h], qkv_ref[:, col(1, h)].astype(BF16), NT) + bias_ref[h, :, R:] for h in heads]
    m = [jnp.maximum(s_c[h].max(-1, keepdims=True), s_n[h].max(-1, keepdims=True)) for h in heads]
    p_c = [jnp.exp(s_c[h] - m[h]) for h in heads]
    p_n = [jnp.exp(s_n[h] - m[h]) for h in heads]
    acc = [_dg(p_c[h].astype(BF16), vc_ref[h].astype(BF16), NN)
           + _dg(p_n[h].astype(BF16), qkv_ref[:, col(2, h)].astype(BF16), NN) for h in heads]
    for h in heads:
        den = p_c[h].sum(-1, keepdims=True) + p_n[h].sum(-1, keepdims=True)
        o_ref[:, col(0, h)] = (acc[h] / den).astype(o_ref.dtype)


def _attention_sample(qkv, cache_k, cache_v, bias):
    Bsz, T, _ = qkv.shape
    R = cache_k.shape[2]
    cache = pl.BlockSpec((None, C_HEADS, R, C_HEAD_DIM), lambda b: (b, 0, 0, 0))
    return pl.pallas_call(
        _att_sample_kernel,
        grid=(Bsz,),
        in_specs=[pl.BlockSpec((None, T, 3 * D_MODEL), lambda b: (b, 0, 0)), cache, cache,
                  pl.BlockSpec((C_HEADS, T, R + T), lambda b: (0, 0, 0))],
        out_specs=pl.BlockSpec((None, T, D_MODEL), lambda b: (b, 0, 0)),
        out_shape=jax.ShapeDtypeStruct((Bsz, T, D_MODEL), BF16),
        compiler_params=_params("parallel"),
    )(qkv, cache_k, cache_v, bias)


def _rel_bias_tile(table, n_q, n_k, q_offset):
    period = n_q + n_k - 1
    m = np.arange(period)
    d = np.where(m < n_k, m, m - period)
    idx = np.clip(q_offset - d, -C_REL_CLIP, C_REL_CLIP) + C_REL_CLIP
    diag = table[:, idx]
    flat = jnp.tile(diag, (1, n_q))[:, :n_q * (period - 1)]
    return flat.reshape(table.shape[0], n_q, period - 1)[:, :, :n_k]


TM = 512


def _trunk(x, wkv0, shift0, gla0, cache_k, cache_v, P):
    Bsz, T, _ = x.shape
    M = Bsz * T
    xf = x.reshape(M, D_MODEL)
    wkv_o, shift_o, gla_o, k_o, v_o = [], [], [], [], []
    for layer in range(DEPTH):
        if layer % 2 == 0:
            e = layer // 2
            pa, pb = _in_proj(xf, P['w_in_a'][e], P['w_in_b'][e], TM)
            pa = pa.reshape(Bsz, T, A_PROJ)
            pb = pb.reshape(Bsz, T, B_PROJ_PAD)
            ya, wkv = _rwkv_mix(pa, shift0[e], wkv0[e], {k: v[e] for k, v in P['a'].items()})
            yb, gs = _gla_mix(pb, gla0[e], {k: v[e] for k, v in P['b'].items()})
            wkv_o.append(wkv)
            shift_o.append(pa[:, -1])
            gla_o.append(gs)
            pairs = [(ya.reshape(M, A_WIDTH), P['w_out_a'][e]), (yb.reshape(M, B_VWIDTH), P['w_out_b'][e])]
        else:
            o = layer // 2
            if cache_k is None:
                x3 = xf.reshape(Bsz, T, D_MODEL)
                att = _attention_prompt(_qkv_padded(x3, P['c_w_qkv_scaled'][o]), P['c_bias_prompt'][o])
                keep = min(C_PAST_ROWS, T)
                rows = _kv_tail(x3, P['c_w_qkv'][o], keep).reshape(Bsz, keep, 2, C_HEADS, C_HEAD_DIM)
            else:
                qkv = _matmul(xf, P['c_w_qkv'][o], TM, D_MODEL).reshape(Bsz, T, 3 * D_MODEL)
                att = _attention_sample(qkv, cache_k[o], cache_v[o], P['c_bias_sample'][o])
                rows = qkv.reshape(Bsz, T, 3, C_HEADS, C_HEAD_DIM)[:, :, 1:]
            k_o.append(rows[:, :, 0].transpose(0, 2, 1, 3))
            v_o.append(rows[:, :, 1].transpose(0, 2, 1, 3))
            pairs = [(att.reshape(M, D_MODEL), P['c_w_o'][o])]
        xf = _mix_ffn(pairs, xf, P['ln1_w'][layer], P['ln1_b'][layer], P['ffn_w_in'][layer], P['ffn_w_out'][layer],
                      P['ln2_w'][layer], P['ln2_b'][layer], TM)
    return (xf.reshape(Bsz, T, D_MODEL), jnp.stack(wkv_o), jnp.stack(shift_o), jnp.stack(gla_o),
            jnp.stack(k_o), jnp.stack(v_o))


def _prepare(w_in_mix, a_mu, a_w0, a_w2, a_a0, a_a2, a_g2, a_k_k, a_k_a, a_r_k, a_ln_w, a_ln_b,
             b_alpha_up, b_alpha_bias, b_norm_w, w_out_mix, c_w_qkv, c_rel_bias, c_w_o,
             ln1_w, ln1_b, ln2_w, ln2_b, ffn_w_in, ffn_w_out, sample_len, cache_rows):
    kw, vw = B_KWIDTH, B_VWIDTH
    wb = w_in_mix[:, :, A_PROJ:]
    main = jnp.concatenate([wb[:, :, :2 * kw + vw], wb[:, :, 2 * kw + vw + B_GATE_LORA:]], axis=-1)
    lora = jnp.pad(wb[:, :, 2 * kw + vw:2 * kw + vw + B_GATE_LORA], ((0, 0), (0, 0), (0, B_LORA_PAD - B_GATE_LORA)))
    row3 = lambda t: t.reshape(t.shape[0], 1, -1)
    bias_prompt = jnp.stack([_rel_bias_tile(t, ATT_QB, ATT_WIN, C_PAST_ROWS) for t in c_rel_bias])
    rel = np.arange(ATT_WIN)[None, :] // CHUNK - np.arange(ATT_QB)[:, None] // CHUNK
    band = (rel >= 0) & (rel <= C_PAST_CHUNKS)
    bias_prompt = jnp.where(band, bias_prompt * LOG2E, NEG_INF)
    q_scale = jnp.concatenate([jnp.full((D_MODEL,), ATT_SCALE * LOG2E, F32), jnp.ones((2 * D_MODEL,), F32)])
    P = {
        'w_in_a': w_in_mix[:, :, :A_PROJ].astype(BF16),
        'w_in_b': jnp.concatenate([main, lora], axis=-1).astype(BF16),
        'a': dict(mu=row3(a_mu), w0=row3(a_w0), w2=a_w2, a0=row3(a_a0), a2=a_a2, g2=a_g2, k_k=row3(a_k_k),
                  k_a=row3(a_k_a), r_k=row3(a_r_k), ln_w=row3(a_ln_w), ln_b=row3(a_ln_b)),
        'b': dict(up=jnp.pad(b_alpha_up, ((0, 0), (0, B_LORA_PAD - B_GATE_LORA), (0, 0))),
                  bias=row3(b_alpha_bias), nw=row3(b_norm_w)),
        'w_out_a': w_out_mix[:, :A_WIDTH].astype(BF16),
        'w_out_b': w_out_mix[:, A_WIDTH:].astype(BF16),
        'c_w_qkv': c_w_qkv.astype(BF16),
        'c_w_qkv_scaled': (c_w_qkv * q_scale).astype(BF16),
        'c_w_o': c_w_o.astype(BF16),
        'c_bias_prompt': bias_prompt.reshape(bias_prompt.shape[0], C_HEADS // 2, 2 * ATT_QB, ATT_WIN),
        'c_bias_sample': jnp.stack([_rel_bias_tile(t, sample_len, cache_rows + sample_len, cache_rows)
                                    for t in c_rel_bias]),
        'ln1_w': row3(ln1_w), 'ln1_b': row3(ln1_b), 'ln2_w': row3(ln2_w), 'ln2_b': row3(ln2_b),
        'ffn_w_in': ffn_w_in.astype(BF16),
        'ffn_w_out': ffn_w_out.astype(BF16),
    }
    return P


def kernel(x_prompt, x_sample, state_a_wkv, state_a_shift, state_b_gla, cache_c_k, cache_c_v, w_in_mix, a_mu, a_w0, a_w2, a_a0, a_a2, a_g2, a_k_k, a_k_a, a_r_k, a_ln_w, a_ln_b, b_alpha_up, b_alpha_bias, b_norm_w, w_out_mix, c_w_qkv, c_rel_bias, c_w_o, ln1_w, ln1_b, ln2_w, ln2_b, ffn_w_in, ffn_w_out):
    P = _prepare(w_in_mix, a_mu, a_w0, a_w2, a_a0, a_a2, a_g2, a_k_k, a_k_a, a_r_k, a_ln_w, a_ln_b,
                 b_alpha_up, b_alpha_bias, b_norm_w, w_out_mix, c_w_qkv, c_rel_bias, c_w_o,
                 ln1_w, ln1_b, ln2_w, ln2_b, ffn_w_in, ffn_w_out, x_sample.shape[1], cache_c_k.shape[3])
    bp = x_prompt.shape[0]
    dt = x_prompt.dtype
    n_even = state_a_wkv.shape[0]
    wkv_zero = jnp.zeros((n_even, bp, A_HEADS, A_HEAD_DIM, A_HEAD_DIM), dt)
    shift_zero = jnp.zeros((n_even, bp, A_PROJ), dt)
    gla_zero = jnp.zeros((n_even, bp, B_HEADS, B_KEY_DIM, B_VAL_DIM), dt)
    y_prompt, p_wkv, p_shift, p_gla, p_k, p_v = _trunk(x_prompt, wkv_zero, shift_zero, gla_zero, None, None, P)
    y_sample, s_wkv, s_shift, s_gla, s_k, s_v = _trunk(
        x_sample, state_a_wkv, state_a_shift, state_b_gla, cache_c_k, cache_c_v, P)
    return (y_prompt, y_sample, p_wkv, p_shift, p_gla, p_k, p_v, s_wkv, s_shift, s_gla, s_k, s_v)
```

```python
import functools

import jax
import jax.numpy as jnp
import numpy as np
from jax import lax
from jax.experimental import pallas as pl
from jax.experimental.pallas import tpu as pltpu

F32 = jnp.float32
BF16 = jnp.bfloat16

D_MODEL = 1024
DEPTH = 4
CHUNK = 64
A_WIDTH = 512
A_HEAD_DIM = 64
A_HEADS = 8
A_DECAY_LORA = 64
A_ICL_LORA = 64
A_GATE_LORA = 128
A_PROJ = 3 * A_WIDTH + A_DECAY_LORA + A_ICL_LORA + A_GATE_LORA
A_NORM_EPS = 64e-5
B_VWIDTH = 512
B_KWIDTH = 256
B_HEADS = 4
B_KEY_DIM = 64
B_VAL_DIM = 128
B_GATE_LORA = 16
B_GATE_NORM = 16.0
B_LORA_PAD = 128
B_PROJ_PAD = 2 * B_KWIDTH + 2 * B_VWIDTH + B_LORA_PAD
C_HEAD_DIM = 64
C_HEADS = 16
C_PAST_CHUNKS = 8
C_PAST_ROWS = C_PAST_CHUNKS * CHUNK
C_REL_CLIP = 128
FFN_HIDDEN = 2816
LN_EPS = 1e-5
DEEPNORM_ALPHA = (2.0 * DEPTH) ** 0.25
NEG_INF = -1e30

LANE = 128
PAIR = LANE
HALF = PAIR // 2
A_PAIRS = A_HEADS // 2
B_PAIRS = B_HEADS // 2
VMEM_LIMIT = 56 * 1024 * 1024

NN = (((1,), (0,)), ((), ()))
NT = (((1,), (1,)), ((), ()))
TN = (((0,), (0,)), ((), ()))


def _params(*sem):
    return pltpu.CompilerParams(dimension_semantics=sem, vmem_limit_bytes=VMEM_LIMIT)


def _dg(a, b, dn):
    return lax.dot_general(a, b, dn, preferred_element_type=F32)


def _dot1(a, b, dn=NN):
    return _dg(a.astype(BF16), b.astype(BF16), dn)


def _split(a):
    hi = a.astype(BF16)
    lo = (a - hi.astype(F32)).astype(BF16)
    return hi, lo


def _dot3s(a, b, dn=NN):
    return _dg(a[0], b[0], dn) + _dg(a[0], b[1], dn) + _dg(a[1], b[0], dn)


def _dot3(a, b, dn=NN):
    return _dot3s(_split(a), _split(b), dn)


def _dot_xl(a_exact, b, dn=NN):
    bh, bl = _split(b)
    return _dg(a_exact, bh, dn) + _dg(a_exact, bl, dn)


def _dot_xr(a, b_exact, dn=NN):
    ah, al = _split(a)
    return _dg(ah, b_exact, dn) + _dg(al, b_exact, dn)


def _ln(z, w, b):
    mu = jnp.mean(z, -1, keepdims=True)
    d = z - mu
    var = jnp.mean(d * d, -1, keepdims=True)
    return d * lax.rsqrt(var + LN_EPS) * w + b


def _sigmoid(x):
    return 1.0 / (1.0 + jnp.exp(-x))


def _softplus(x):
    return jnp.maximum(x, 0.0) + jnp.log(1.0 + jnp.exp(-jnp.abs(x)))


def _stack_masked(x, m0):
    return jnp.concatenate([jnp.where(m0, x, 0.0), jnp.where(m0, 0.0, x)], axis=0)


def _fold(x):
    n = x.shape[0] // 2
    return x[:n] + x[n:]


def _mm_kernel(x_ref, w_ref, o_ref):
    o_ref[...] = jnp.dot(x_ref[...].astype(BF16), w_ref[...], preferred_element_type=F32)


def _matmul(x, w, tm, tn):
    M, K = x.shape
    N = w.shape[1]
    tm = min(tm, M)
    return pl.pallas_call(
        _mm_kernel,
        grid=(M // tm, N // tn),
        in_specs=[pl.BlockSpec((tm, K), lambda i, j: (i, 0)),
                  pl.BlockSpec((K, tn), lambda i, j: (0, j))],
        out_specs=pl.BlockSpec((tm, tn), lambda i, j: (i, j)),
        out_shape=jax.ShapeDtypeStruct((M, N), F32),
        compiler_params=_params("parallel", "arbitrary"),
    )(x, w)


def _in_proj_kernel(x_ref, wa_ref, wb_ref, oa_ref, ob_ref):
    xb = x_ref[...].astype(BF16)
    oa_ref[...] = jnp.dot(xb, wa_ref[...], preferred_element_type=F32)
    ob_ref[...] = jnp.dot(xb, wb_ref[...], preferred_element_type=F32)


def _in_proj(x, wa, wb, tm):
    M, K = x.shape
    tm = min(tm, M)
    na, nb = wa.shape[1], wb.shape[1]
    resident = lambda n: pl.BlockSpec((K, n), lambda i: (0, 0), pipeline_mode=pl.Buffered(1))
    return pl.pallas_call(
        _in_proj_kernel,
        grid=(M // tm,),
        in_specs=[pl.BlockSpec((tm, K), lambda i: (i, 0)), resident(na), resident(nb)],
        out_specs=[pl.BlockSpec((tm, na), lambda i: (i, 0)), pl.BlockSpec((tm, nb), lambda i: (i, 0))],
        out_shape=[jax.ShapeDtypeStruct((M, na), F32), jax.ShapeDtypeStruct((M, nb), F32)],
        compiler_params=_params("parallel"),
    )(x, wa, wb)


def _mix_ffn_kernel(n_pairs, *refs):
    res_ref, lw1_ref, lb1_ref, wg_ref, wu_ref, wo_ref, lw2_ref, lb2_ref, o_ref = refs[2 * n_pairs:]
    acc = None
    for p in range(n_pairs):
        part = jnp.dot(refs[2 * p][...].astype(BF16), refs[2 * p + 1][...], preferred_element_type=F32)
        acc = part if acc is None else acc + part
    x = _ln(DEEPNORM_ALPHA * res_ref[...] + acc, lw1_ref[...], lb1_ref[...])
    xb = x.astype(BF16)
    g = jnp.dot(xb, wg_ref[...], preferred_element_type=F32)
    u = jnp.dot(xb, wu_ref[...], preferred_element_type=F32)
    act = (g * _sigmoid(g) * u).astype(BF16)
    y = jnp.dot(act, wo_ref[...], preferred_element_type=F32)
    o_ref[...] = _ln(DEEPNORM_ALPHA * x + y, lw2_ref[...], lb2_ref[...])


def _mix_ffn(pairs, res, lw1, lb1, w_in, w_out, lw2, lb2, tm):
    M = res.shape[0]
    tm = min(tm, M)
    resident = lambda shape, idx: pl.BlockSpec(shape, idx, pipeline_mode=pl.Buffered(1))
    vec = pl.BlockSpec((1, D_MODEL), lambda i: (0, 0))
    in_specs, args = [], []
    for a, w in pairs:
        K = a.shape[1]
        in_specs += [pl.BlockSpec((tm, K), lambda i: (i, 0)), resident((K, D_MODEL), lambda i: (0, 0))]
        args += [a, w]
    in_specs += [pl.BlockSpec((tm, D_MODEL), lambda i: (i, 0)), vec, vec,
                 resident((D_MODEL, FFN_HIDDEN), lambda i: (0, 0)),
                 resident((D_MODEL, FFN_HIDDEN), lambda i: (0, 1)),
                 resident((FFN_HIDDEN, D_MODEL), lambda i: (0, 0)), vec, vec]
    return pl.pallas_call(
        functools.partial(_mix_ffn_kernel, len(pairs)),
        grid=(M // tm,),
        in_specs=in_specs,
        out_specs=pl.BlockSpec((tm, D_MODEL), lambda i: (i, 0)),
        out_shape=jax.ShapeDtypeStruct((M, D_MODEL), F32),
        compiler_params=_params("parallel"),
    )(*args, res, lw1, lb1, w_in, w_in, w_out, lw2, lb2)


def _inv_unit_lower(ns, eye, nilpotency):
    size = eye.shape[0]
    xs = [eye - n for n in ns]
    ps = [_dot1(n, n) for n in ns]
    k = 2
    while k < nilpotency:
        k *= 2
        if k < nilpotency:
            rs = [_dot1(p, jnp.concatenate([x, p], axis=1)) for x, p in zip(xs, ps)]
            xs = [x + r[:, :size] for x, r in zip(xs, rs)]
            ps = [r[:, size:] for r in rs]
        else:
            xs = [x + _dot1(p, x) for x, p in zip(xs, ps)]
    return xs


def _segsum(x, seg):
    n = x.shape[1] // PAIR
    rows = x.shape[0]
    xs = jnp.concatenate([x[:, j * PAIR:(j + 1) * PAIR] for j in range(n)], axis=0)
    s = _dot_xr(xs, seg)
    return jnp.concatenate([s[j * rows:(j + 1) * rows] for j in range(n)], axis=1)


def _head_mask():
    hr = lax.broadcasted_iota(jnp.int32, (PAIR, PAIR), 0) // A_HEAD_DIM
    hc = lax.broadcasted_iota(jnp.int32, (PAIR, PAIR), 1) // A_HEAD_DIM
    return (hr == hc).astype(BF16)


def _rwkv_prep_kernel(n_chunks, pa_ref, prev_ref, shift_ref, mu_ref, w0_ref, w2_ref, a0_ref, a2_ref, g2_ref,
                      kk_ref, ka_ref, rk_ref,
                      mh_ref, ch_ref, qc_ref, oc_ref, g_ref, bonus_ref):
    i = pl.program_id(1)
    L = CHUNK
    pa = pa_ref[...]
    first = jnp.where(i == 0, shift_ref[...], prev_ref[7:8, :])
    rows = lax.broadcasted_iota(jnp.int32, pa.shape, 0)
    prev = jnp.where(rows == 0, first, pltpu.roll(pa, shift=1, axis=0))
    xs = pa + (prev - pa) * mu_ref[...]
    W = A_WIDTH
    r = xs[:, 0:W]
    k0 = xs[:, W:2 * W]
    v = xs[:, 2 * W:3 * W]
    xw = xs[:, 3 * W:3 * W + A_DECAY_LORA]
    xa = xs[:, 3 * W + A_DECAY_LORA:3 * W + A_DECAY_LORA + A_ICL_LORA]
    xg = xs[:, 3 * W + A_DECAY_LORA + A_ICL_LORA:]
    w = -_softplus(-(w0_ref[...] + _dot3(jnp.tanh(xw), w2_ref[...]))) - 0.5
    logw = -jnp.exp(w)
    a = _sigmoid(a0_ref[...] + _dot3(xa, a2_ref[...]))
    g_ref[...] = _dot3(_sigmoid(xg), g2_ref[...])
    seg = _head_mask()
    kk = k0 * kk_ref[...]
    kk = kk * lax.rsqrt(_segsum(kk * kk, seg) + 1e-12)
    k = k0 * (1.0 + (a - 1.0) * ka_ref[...])
    bonus_ref[...] = _segsum(r * k * rk_ref[...], seg) * v
    b = kk * a

    tri = (lax.broadcasted_iota(jnp.int32, (L, L), 0) >= lax.broadcasted_iota(jnp.int32, (L, L), 1)).astype(BF16)
    row = lax.broadcasted_iota(jnp.int32, (PAIR, PAIR), 0)
    col = lax.broadcasted_iota(jnp.int32, (PAIR, PAIR), 1)
    eye = (row == col).astype(F32)
    strict = row > col
    incl2 = (lax.broadcasted_iota(jnp.int32, (PAIR, 2 * PAIR), 0)
             >= lax.broadcasted_iota(jnp.int32, (PAIR, 2 * PAIR), 1) % PAIR)
    m0 = lax.broadcasted_iota(jnp.int32, (L, PAIR), 1) < HALF
    pairs = range(A_PAIRS)
    lanes = [slice(j * PAIR, (j + 1) * PAIR) for j in pairs]
    chains = [(c, j) for c in range(n_chunks) for j in pairs]
    every = range(len(chains))
    rows_of = [slice(c * L, (c + 1) * L) for c in range(n_chunks)]
    kkg, rg, kinv, binv, kd, bd, e_last = [], [], [], [], [], [], []
    for sl in rows_of:
        lw = logw[sl]
        cum = _dot_xl(tri, lw)
        c_last = cum[L - 1:L, :]
        e_neg = jnp.exp(-cum)
        e_rem = jnp.exp(c_last - cum)
        kkg.append(kk[sl] * jnp.exp(cum - lw))
        rg.append(r[sl] * jnp.exp(cum))
        kinv.append(k[sl] * e_neg)
        binv.append(b[sl] * e_neg)
        kd.append(k[sl] * e_rem)
        bd.append(b[sl] * e_rem)
        e_last.append(jnp.exp(c_last))
    sm = lambda xs_, c, j: _stack_masked(xs_[c][:, lanes[j]].astype(BF16), m0)
    kkg_b = [sm(kkg, c, j) for c, j in chains]
    v_b = [_stack_masked(v[rows_of[c], lanes[j]].astype(BF16), m0) for c, j in chains]
    bd_b = [sm(bd, c, j) for c, j in chains]
    kd_b = [sm(kd, c, j) for c, j in chains]
    rg_b = [sm(rg, c, j) for c, j in chains]
    kinv_b = [sm(kinv, c, j) for c, j in chains]
    binv_b = [sm(binv, c, j) for c, j in chains]
    aa = [_dg(jnp.concatenate([kkg_b[n], rg_b[n]], axis=0), jnp.concatenate([binv_b[n], kinv_b[n]], axis=0), NT)
          for n in every]
    a_kb = [jnp.where(strict, x[:PAIR, :PAIR], 0.0) for x in aa]
    a_kk = [jnp.where(strict, x[:PAIR, PAIR:], 0.0).astype(BF16) for x in aa]
    a_rbk = [jnp.where(incl2, x[PAIR:, :], 0.0).astype(BF16) for x in aa]
    t = _inv_unit_lower(a_kb, eye, L)
    av = [_dg(a_kk[n], v_b[n], NN).astype(BF16) for n in every]
    wu = [_dg(t[n].astype(BF16), jnp.concatenate([kkg_b[n], av[n]], axis=1), NN) for n in every]
    wu_b = [x.astype(BF16) for x in wu]
    mc = [_dg(bd_b[n], wu_b[n], TN) for n in every]
    kv = [_dg(kd_b[n], v_b[n], TN) for n in every]
    rhs = [jnp.concatenate([jnp.concatenate([wu_b[n][:, :PAIR], (-wu[n][:, PAIR:]).astype(BF16)], axis=1),
                            jnp.concatenate([jnp.zeros_like(v_b[n]), v_b[n]], axis=1)], axis=0) for n in every]
    qo = [_dg(a_rbk[n], rhs[n], NN) for n in every]
    for n, (c, j) in enumerate(chains):
        mh_ref[c, j] = (eye * e_last[c][:, lanes[j]] - mc[n][:, :PAIR]).astype(BF16)
        ch_ref[c, j] = kv[n] - mc[n][:, PAIR:]
        qc_ref[rows_of[c], lanes[j]] = (rg[c][:, lanes[j]] - _fold(qo[n][:, :PAIR])).astype(BF16)
        oc_ref[rows_of[c], lanes[j]] = _fold(qo[n][:, PAIR:])


def _rwkv_scan_kernel(n_seq, n_chunks, mh_ref, ch_ref, qc_ref, oc_ref, g_ref, bonus_ref, h0_ref, lnw_ref, lnb_ref,
                      y_ref, hout_ref, h_ref):
    i = pl.program_id(1)
    L = CHUNK

    @pl.when(i == 0)
    def _():
        h_ref[...] = h0_ref[...]

    seg = _head_mask()
    inv_n = 1.0 / A_HEAD_DIM
    lanes = [slice(j * PAIR, (j + 1) * PAIR) for j in range(A_PAIRS)]
    chains = [(s, j) for s in range(n_seq) for j in range(A_PAIRS)]
    hb = [h_ref[s, j] for s, j in chains]
    for c in range(n_chunks):
        sl = slice(c * L, (c + 1) * L)
        hb_b = [x.astype(BF16) for x in hb]
        o = [_dg(qc_ref[s, sl, lanes[j]], hb_b[n], NN) + oc_ref[s, sl, lanes[j]] for n, (s, j) in enumerate(chains)]
        hb = [_dg(mh_ref[s, c, j], hb_b[n], NN) + ch_ref[s, c, j] for n, (s, j) in enumerate(chains)]
        for s in range(n_seq):
            os_ = jnp.concatenate(o[s * A_PAIRS:(s + 1) * A_PAIRS], axis=1)
            mean = _segsum(os_, seg) * inv_n
            d = os_ - mean
            var = _segsum(d * d, seg) * inv_n
            on = d * lax.rsqrt(var + A_NORM_EPS) * lnw_ref[...] + lnb_ref[...]
            y_ref[s, sl, :] = ((on + bonus_ref[s, sl, :]) * g_ref[s, sl, :]).astype(y_ref.dtype)
    for n, (s, j) in enumerate(chains):
        h_ref[s, j] = hb[n]
    hout_ref[...] = h_ref[...]


def _rwkv_mix(pa, shift0, wkv0, prm):
    Bsz, T, _ = pa.shape
    n1 = next(n for n in (8, 4, 2, 1) if (T // CHUNK) % n == 0)
    R1 = n1 * CHUNK
    W = A_WIDTH
    nblk = T // R1
    row = lambda n: pl.BlockSpec((1, n), lambda b, i: (0, 0))
    full = lambda s: pl.BlockSpec(s, lambda b, i: (0,) * len(s))
    act = lambda r, n: pl.BlockSpec((None, r, n), lambda b, i: (b, i, 0))
    mat = lambda n: pl.BlockSpec((None, n, A_PAIRS, PAIR, PAIR), lambda b, i: (b, i, 0, 0, 0))
    sub = R1 // 8
    mats = lambda dt: jax.ShapeDtypeStruct((Bsz, T // CHUNK, A_PAIRS, PAIR, PAIR), dt)
    acts = lambda dt: jax.ShapeDtypeStruct((Bsz, T, W), dt)
    mh, ch, qc, oc, g, bonus = pl.pallas_call(
        functools.partial(_rwkv_prep_kernel, n1),
        grid=(Bsz, nblk),
        in_specs=[act(R1, A_PROJ),
                  pl.BlockSpec((None, 8, A_PROJ), lambda b, i: (b, jnp.maximum(i * sub - 1, 0), 0)),
                  pl.BlockSpec((None, 1, A_PROJ), lambda b, i: (b, 0, 0)),
                  row(A_PROJ), row(W), full((A_DECAY_LORA, W)), row(W), full((A_ICL_LORA, W)),
                  full((A_GATE_LORA, W)), row(W), row(W), row(W)],
        out_specs=[mat(n1), mat(n1), act(R1, W), act(R1, W), act(R1, W), act(R1, W)],
        out_shape=[mats(BF16), mats(F32), acts(BF16), acts(F32), acts(F32), acts(F32)],
        compiler_params=_params("parallel", "parallel"),
    )(pa, pa, shift0[:, None, :], prm['mu'], prm['w0'], prm['w2'], prm['a0'], prm['a2'], prm['g2'],
      prm['k_k'], prm['k_a'], prm['r_k'])

    n2 = min(4, T // CHUNK)
    R2 = n2 * CHUNK
    hh = wkv0.transpose(0, 1, 3, 2).reshape(Bsz, A_PAIRS, 2, A_HEAD_DIM, A_HEAD_DIM)
    zero = jnp.zeros_like(hh[:, :, 0])
    h0 = jnp.concatenate([jnp.concatenate([hh[:, :, 0], zero], axis=-1),
                          jnp.concatenate([zero, hh[:, :, 1]], axis=-1)], axis=-2)
    ns = next(n for n in (4, 2, 1) if Bsz % n == 0)
    state = pl.BlockSpec((ns, A_PAIRS, PAIR, PAIR), lambda b, i: (b, 0, 0, 0))
    act2 = lambda r, n: pl.BlockSpec((ns, r, n), lambda b, i: (b, i, 0))
    mat2 = lambda n: pl.BlockSpec((ns, n, A_PAIRS, PAIR, PAIR), lambda b, i: (b, i, 0, 0, 0))
    y, hout = pl.pallas_call(
        functools.partial(_rwkv_scan_kernel, ns, n2),
        grid=(Bsz // ns, T // R2),
        in_specs=[mat2(n2), mat2(n2), act2(R2, W), act2(R2, W), act2(R2, W), act2(R2, W), state,
                  row(W), row(W)],
        out_specs=[act2(R2, W), state],
        out_shape=[acts(BF16), jax.ShapeDtypeStruct((Bsz, A_PAIRS, PAIR, PAIR), F32)],
        scratch_shapes=[pltpu.VMEM((ns, A_PAIRS, PAIR, PAIR), F32)],
        compiler_params=_params("parallel", "arbitrary"),
    )(mh, ch, qc, oc, g, bonus, h0, prm['ln_w'], prm['ln_b'])
    hd = jnp.stack([hout[:, :, :HALF, :HALF], hout[:, :, HALF:, HALF:]], axis=2)
    wkv = hd.reshape(Bsz, A_HEADS, A_HEAD_DIM, A_HEAD_DIM).transpose(0, 1, 3, 2)
    return y, wkv


def _gla_kernel(n_seq, n_chunks, pb_ref, s0_ref, up_ref, bias_ref, nw_ref, y_ref, sout_ref, s_ref):
    i = pl.program_id(1)
    L = CHUNK

    @pl.when(i == 0)
    def _():
        s_ref[...] = s0_ref[...]

    kw, vw = B_KWIDTH, B_VWIDTH
    incl = lax.broadcasted_iota(jnp.int32, (L, L), 0) >= lax.broadcasted_iota(jnp.int32, (L, L), 1)
    tri = incl.astype(BF16)
    m0 = lax.broadcasted_iota(jnp.int32, (L, PAIR), 1) < HALF
    top = lax.broadcasted_iota(jnp.int32, (PAIR, B_VAL_DIM), 0) < HALF
    scale = B_KEY_DIM ** -0.5
    pairs = range(B_PAIRS)
    heads = range(B_HEADS)
    pl_ = lambda h: slice((h // 2) * PAIR, (h // 2 + 1) * PAIR)
    units = [(s, c) for c in range(n_chunks) for s in range(n_seq)]
    rows_of = lambda c: slice(c * L, (c + 1) * L)
    vslice = lambda h: slice(h * B_VAL_DIM, (h + 1) * B_VAL_DIM)
    z = [_dot3(pb_ref[s, rows_of(c), 2 * kw + 2 * vw:], up_ref[...]) + bias_ref[...] for s, c in units]
    bc = [_dot_xl(tri, -_softplus(-x) * (1.0 / B_GATE_NORM)) for x in z]
    b_last = [x[L - 1:L, :] for x in bc]
    q_dec, k_inv, k_state, e_last_t = [], [], [], []
    for n, (s, c) in enumerate(units):
        k = pb_ref[s, rows_of(c), kw:2 * kw]
        q_dec.append(pb_ref[s, rows_of(c), 0:kw] * scale * jnp.exp(bc[n]))
        k_inv.append((k * jnp.exp(-bc[n])).astype(BF16))
        k_state.append((k * jnp.exp(b_last[n] - bc[n])).astype(BF16))
        e_last_t.append(jnp.broadcast_to(jnp.exp(b_last[n]), (LANE, kw)).T)
    uh = [(n, h) for n in range(len(units)) for h in heads]
    vb = [pb_ref[units[n][0], rows_of(units[n][1]), 2 * kw + h * B_VAL_DIM:2 * kw + (h + 1) * B_VAL_DIM].astype(BF16)
          for n, h in uh]
    qm = [jnp.where(m0 if h % 2 == 0 else ~m0, q_dec[n][:, pl_(h)], 0.0).astype(BF16) for n, h in uh]
    scores = [jnp.where(incl, _dg(qm[i_], k_inv[n][:, pl_(h)], NT), 0.0).astype(BF16) for i_, (n, h) in enumerate(uh)]
    upd = [_dg(k_state[n][:, pl_(h)], vb[i_], TN) for i_, (n, h) in enumerate(uh)]
    intra = [_dg(scores[i_], vb[i_], NN) for i_ in range(len(uh))]
    states = [[s_ref[s, j] for j in pairs] for s in range(n_seq)]
    before = []
    for n, (s, c) in enumerate(units):
        before.append([x.astype(BF16) for x in states[s]])
        states[s] = [states[s][j] * e_last_t[n][j * PAIR:(j + 1) * PAIR, :]
                     + jnp.where(top, upd[n * B_HEADS + 2 * j], upd[n * B_HEADS + 2 * j + 1]) for j in pairs]
    o = [_dg(qm[i_], before[n][h // 2], NN) + intra[i_] for i_, (n, h) in enumerate(uh)]
    for i_, (n, h) in enumerate(uh):
        s, c = units[n]
        oh = o[i_] * lax.rsqrt(jnp.mean(o[i_] * o[i_], -1, keepdims=True) + LN_EPS) * nw_ref[...]
        rg_h = pb_ref[s, rows_of(c), 2 * kw + vw + h * B_VAL_DIM:2 * kw + vw + (h + 1) * B_VAL_DIM]
        y_ref[s, rows_of(c), vslice(h)] = (oh * (rg_h * _sigmoid(rg_h))).astype(y_ref.dtype)
    for s in range(n_seq):
        for j in pairs:
            s_ref[s, j] = states[s][j]
    sout_ref[...] = s_ref[...]


def _gla_mix(pb, s0, prm):
    Bsz, T, _ = pb.shape
    n = min(4, T // CHUNK)
    Rr = n * CHUNK
    ns = next(n for n in (4, 2, 1) if Bsz % n == 0)
    st = pl.BlockSpec((ns, B_PAIRS, PAIR, B_VAL_DIM), lambda b, i: (b, 0, 0, 0))
    y, s = pl.pallas_call(
        functools.partial(_gla_kernel, ns, n),
        grid=(Bsz // ns, T // Rr),
        in_specs=[pl.BlockSpec((ns, Rr, B_PROJ_PAD), lambda b, i: (b, i, 0)), st,
                  pl.BlockSpec((B_LORA_PAD, B_KWIDTH), lambda b, i: (0, 0)),
                  pl.BlockSpec((1, B_KWIDTH), lambda b, i: (0, 0)),
                  pl.BlockSpec((1, B_VAL_DIM), lambda b, i: (0, 0))],
        out_specs=[pl.BlockSpec((ns, Rr, B_VWIDTH), lambda b, i: (b, i, 0)), st],
        out_shape=[jax.ShapeDtypeStruct((Bsz, T, B_VWIDTH), BF16),
                   jax.ShapeDtypeStruct((Bsz, B_PAIRS, PAIR, B_VAL_DIM), F32)],
        scratch_shapes=[pltpu.VMEM((ns, B_PAIRS, PAIR, B_VAL_DIM), F32)],
        compiler_params=_params("parallel", "arbitrary"),
    )(pb, s0.reshape(Bsz, B_PAIRS, PAIR, B_VAL_DIM), prm['up'], prm['bias'], prm['nw'])
    return y, s.reshape(Bsz, B_HEADS, B_KEY_DIM, B_VAL_DIM)


ATT_QB = 2 * CHUNK
ATT_SCALE = C_HEAD_DIM ** -0.5
LOG2E = 1.4426950408889634


ATT_WIN = C_PAST_ROWS + ATT_QB


def _qkv_pad_kernel(n_pad, x_ref, w_ref, o_ref):
    r = pl.program_id(1)

    @pl.when(r < n_pad)
    def _():
        o_ref[...] = jnp.zeros_like(o_ref)

    @pl.when(r >= n_pad)
    def _():
        y = jnp.dot(x_ref[...].astype(BF16), w_ref[...], preferred_element_type=F32).astype(BF16)
        qk = 2 * D_MODEL
        o_ref[:, :qk] = y[:, :qk]
        ones = jnp.ones((y.shape[0], PAIR), BF16)
        for hp in range(C_HEADS // 2):
            o_ref[:, qk + 2 * hp * PAIR:qk + (2 * hp + 1) * PAIR] = y[:, qk + hp * PAIR:qk + (hp + 1) * PAIR]
            o_ref[:, qk + (2 * hp + 1) * PAIR:qk + (2 * hp + 2) * PAIR] = ones


def _qkv_padded(x, w):
    Bsz, T, D = x.shape
    N = w.shape[1]
    n_out = N + D
    tm = next(t for t in (512, 256, 128) if T % t == 0)
    n_pad = C_PAST_ROWS // tm
    return pl.pallas_call(
        functools.partial(_qkv_pad_kernel, n_pad),
        grid=(Bsz, T // tm + n_pad),
        in_specs=[pl.BlockSpec((None, tm, D), lambda b, r: (b, jnp.maximum(r - n_pad, 0), 0)),
                  pl.BlockSpec((D, N), lambda b, r: (0, 0), pipeline_mode=pl.Buffered(1))],
        out_specs=pl.BlockSpec((None, tm, n_out), lambda b, r: (b, r, 0)),
        out_shape=jax.ShapeDtypeStruct((Bsz, C_PAST_ROWS + T, n_out), BF16),
        compiler_params=_params("parallel", "arbitrary"),
    )(x, w)


def _kv_tail(x, w, keep):
    Bsz, T, D = x.shape
    tm = next(t for t in (512, 256, 128, 64) if keep % t == 0 and T % t == 0)
    first = (T - keep) // tm
    return pl.pallas_call(
        _mm_kernel,
        grid=(Bsz, keep // tm, 2),
        in_specs=[pl.BlockSpec((None, tm, D), lambda b, r, j: (b, first + r, 0)),
                  pl.BlockSpec((D, D), lambda b, r, j: (0, j + 1))],
        out_specs=pl.BlockSpec((None, tm, D), lambda b, r, j: (b, r, j)),
        out_shape=jax.ShapeDtypeStruct((Bsz, keep, 2 * D), F32),
        compiler_params=_params("parallel", "parallel", "arbitrary"),
    )(x, w)


def _att_prompt_kernel(n_qb, q_ref, k_ref, v_ref, bias_ref, o_ref):
    g = pl.program_id(2)
    m0 = lax.broadcasted_iota(jnp.int32, (ATT_QB, PAIR), 1) < HALF
    col = lax.broadcasted_iota(jnp.int32, (2 * ATT_QB, ATT_WIN), 1)

    def run(window_has_padding):
        blocks = range(n_qb)
        rows = [slice(t * ATT_QB, (t + 1) * ATT_QB) for t in blocks]
        wins = [pl.ds(pl.multiple_of((g * n_qb + t) * ATT_QB, ATT_QB), ATT_WIN) for t in blocks]
        q = [_stack_masked(q_ref[rows[t], :], m0) for t in blocks]
        s = [_dg(q[t], k_ref[wins[t], :], NT) + bias_ref[...] for t in blocks]
        if window_has_padding:
            s = [jnp.where(col >= C_PAST_ROWS - (g * n_qb + t) * ATT_QB, s[t], NEG_INF) for t in blocks]
        p = [jnp.exp2(x - x.max(-1, keepdims=True)) for x in s]
        pv = [_dg(p[t].astype(BF16), v_ref[wins[t], :], NN) for t in blocks]
        o = [x[:, :PAIR] / x[:, PAIR:] for x in pv]
        for t in blocks:
            o_ref[rows[t], :] = jnp.where(m0, o[t][:ATT_QB], o[t][ATT_QB:]).astype(o_ref.dtype)

    first = g * (n_qb * ATT_QB) < C_PAST_ROWS
    pl.when(first)(lambda: run(True))
    pl.when(jnp.logical_not(first))(lambda: run(False))


def _attention_prompt(qkv, bias):
    Bsz, Tp, _ = qkv.shape
    T = Tp - C_PAST_ROWS
    nhp = C_HEADS // 2
    nblk = T // ATT_QB
    n_qb = next(n for n in (4, 2, 1) if nblk % n == 0)
    rows = n_qb * ATT_QB
    skip = C_PAST_ROWS // rows
    return pl.pallas_call(
        functools.partial(_att_prompt_kernel, n_qb),
        grid=(Bsz, nhp, T // rows),
        in_specs=[pl.BlockSpec((None, rows, LANE), lambda b, hp, g: (b, g + skip, hp)),
                  pl.BlockSpec((None, Tp, LANE), lambda b, hp, g: (b, 0, nhp + hp)),
                  pl.BlockSpec((None, Tp, 2 * LANE), lambda b, hp, g: (b, 0, nhp + hp)),
                  pl.BlockSpec((None, 2 * ATT_QB, ATT_WIN), lambda b, hp, g: (hp, 0, 0))],
        out_specs=pl.BlockSpec((None, rows, LANE), lambda b, hp, g: (b, g, hp)),
        out_shape=jax.ShapeDtypeStruct((Bsz, T, D_MODEL), BF16),
        compiler_params=_params("parallel", "parallel", "arbitrary"),
    )(qkv, qkv, qkv, bias)


def _att_sample_kernel(qkv_ref, kc_ref, vc_ref, bias_ref, o_ref):
    R = kc_ref.shape[1]
    heads = range(C_HEADS)
    col = lambda part, h: slice(part * D_MODEL + h * C_HEAD_DIM, part * D_MODEL + (h + 1) * C_HEAD_DIM)
    q = [(qkv_ref[:, col(0, h)] * ATT_SCALE).astype(BF16) for h in heads]
    s_c = [_dg(q[h], kc_ref[h].astype(BF16), NT) + bias_ref[h, :, 0:R] for h in heads]
    s_n = [_dg(q[h], qkv_ref[:, col(1, h)].astype(BF16), NT) + bias_ref[h, :, R:] for h in heads]
    m = [jnp.maximum(s_c[h].max(-1, keepdims=True), s_n[h].max(-1, keepdims=True)) for h in heads]
    p_c = [jnp.exp(s_c[h] - m[h]) for h in heads]
    p_n = [jnp.exp(s_n[h] - m[h]) for h in heads]
    acc = [_dg(p_c[h].astype(BF16), vc_ref[h].astype(BF16), NN)
           + _dg(p_n[h].astype(BF16), qkv_ref[:, col(2, h)].astype(BF16), NN) for h in heads]
    for h in heads:
        den = p_c[h].sum(-1, keepdims=True) + p_n[h].sum(-1, keepdims=True)
        o_ref[:, col(0, h)] = (acc[h] / den).astype(o_ref.dtype)


def _attention_sample(qkv, cache_k, cache_v, bias):
    Bsz, T, _ = qkv.shape
    R = cache_k.shape[2]
    cache = pl.BlockSpec((None, C_HEADS, R, C_HEAD_DIM), lambda b: (b, 0, 0, 0))
    return pl.pallas_call(
        _att_sample_kernel,
        grid=(Bsz,),
        in_specs=[pl.BlockSpec((None, T, 3 * D_MODEL), lambda b: (b, 0, 0)), cache, cache,
                  pl.BlockSpec((C_HEADS, T, R + T), lambda b: (0, 0, 0))],
        out_specs=pl.BlockSpec((None, T, D_MODEL), lambda b: (b, 0, 0)),
        out_shape=jax.ShapeDtypeStruct((Bsz, T, D_MODEL), BF16),
        compiler_params=_params("parallel"),
    )(qkv, cache_k, cache_v, bias)


def _rel_bias_tile(table, n_q, n_k, q_offset):
    period = n_q + n_k - 1
    m = np.arange(period)
    d = np.where(m < n_k, m, m - period)
    idx = np.clip(q_offset - d, -C_REL_CLIP, C_REL_CLIP) + C_REL_CLIP
    diag = table[:, idx]
    flat = jnp.tile(diag, (1, n_q))[:, :n_q * (period - 1)]
    return flat.reshape(table.shape[0], n_q, period - 1)[:, :, :n_k]


TM = 512


def _trunk(x, wkv0, shift0, gla0, cache_k, cache_v, P):
    Bsz, T, _ = x.shape
    M = Bsz * T
    xf = x.reshape(M, D_MODEL)
    wkv_o, shift_o, gla_o, k_o, v_o = [], [], [], [], []
    for layer in range(DEPTH):
        if layer % 2 == 0:
            e = layer // 2
            pa, pb = _in_proj(xf, P['w_in_a'][e], P['w_in_b'][e], TM)
            pa = pa.reshape(Bsz, T, A_PROJ)
            pb = pb.reshape(Bsz, T, B_PROJ_PAD)
            ya, wkv = _rwkv_mix(pa, shift0[e], wkv0[e], {k: v[e] for k, v in P['a'].items()})
            yb, gs = _gla_mix(pb, gla0[e], {k: v[e] for k, v in P['b'].items()})
            wkv_o.append(wkv)
            shift_o.append(pa[:, -1])
            gla_o.append(gs)
            pairs = [(ya.reshape(M, A_WIDTH), P['w_out_a'][e]), (yb.reshape(M, B_VWIDTH), P['w_out_b'][e])]
        else:
            o = layer // 2
            if cache_k is None:
                x3 = xf.reshape(Bsz, T, D_MODEL)
                att = _attention_prompt(_qkv_padded(x3, P['c_w_qkv_scaled'][o]), P['c_bias_prompt'][o])
                keep = min(C_PAST_ROWS, T)
                rows = _kv_tail(x3, P['c_w_qkv'][o], keep).reshape(Bsz, keep, 2, C_HEADS, C_HEAD_DIM)
            else:
                qkv = _matmul(xf, P['c_w_qkv'][o], TM, D_MODEL).reshape(Bsz, T, 3 * D_MODEL)
                att = _attention_sample(qkv, cache_k[o], cache_v[o], P['c_bias_sample'][o])
                rows = qkv.reshape(Bsz, T, 3, C_HEADS, C_HEAD_DIM)[:, :, 1:]
            k_o.append(rows[:, :, 0].transpose(0, 2, 1, 3))
            v_o.append(rows[:, :, 1].transpose(0, 2, 1, 3))
            pairs = [(att.reshape(M, D_MODEL), P['c_w_o'][o])]
        xf = _mix_ffn(pairs, xf, P['ln1_w'][layer], P['ln1_b'][layer], P['ffn_w_in'][layer], P['ffn_w_out'][layer],
                      P['ln2_w'][layer], P['ln2_b'][layer], TM)
    return (xf.reshape(Bsz, T, D_MODEL), jnp.stack(wkv_o), jnp.stack(shift_o), jnp.stack(gla_o),
            jnp.stack(k_o), jnp.stack(v_o))


def _prepare(w_in_mix, a_mu, a_w0, a_w2, a_a0, a_a2, a_g2, a_k_k, a_k_a, a_r_k, a_ln_w, a_ln_b,
             b_alpha_up, b_alpha_bias, b_norm_w, w_out_mix, c_w_qkv, c_rel_bias, c_w_o,
             ln1_w, ln1_b, ln2_w, ln2_b, ffn_w_in, ffn_w_out, sample_len, cache_rows):
    kw, vw = B_KWIDTH, B_VWIDTH
    wb = w_in_mix[:, :, A_PROJ:]
    main = jnp.concatenate([wb[:, :, :2 * kw + vw], wb[:, :, 2 * kw + vw + B_GATE_LORA:]], axis=-1)
    lora = jnp.pad(wb[:, :, 2 * kw + vw:2 * kw + vw + B_GATE_LORA], ((0, 0), (0, 0), (0, B_LORA_PAD - B_GATE_LORA)))
    row3 = lambda t: t.reshape(t.shape[0], 1, -1)
    bias_prompt = jnp.stack([_rel_bias_tile(t, ATT_QB, ATT_WIN, C_PAST_ROWS) for t in c_rel_bias])
    rel = np.arange(ATT_WIN)[None, :] // CHUNK - np.arange(ATT_QB)[:, None] // CHUNK
    band = (rel >= 0) & (rel <= C_PAST_CHUNKS)
    bias_prompt = jnp.where(band, bias_prompt * LOG2E, NEG_INF)
    q_scale = jnp.concatenate([jnp.full((D_MODEL,), ATT_SCALE * LOG2E, F32), jnp.ones((2 * D_MODEL,), F32)])
    P = {
        'w_in_a': w_in_mix[:, :, :A_PROJ].astype(BF16),
        'w_in_b': jnp.concatenate([main, lora], axis=-1).astype(BF16),
        'a': dict(mu=row3(a_mu), w0=row3(a_w0), w2=a_w2, a0=row3(a_a0), a2=a_a2, g2=a_g2, k_k=row3(a_k_k),
                  k_a=row3(a_k_a), r_k=row3(a_r_k), ln_w=row3(a_ln_w), ln_b=row3(a_ln_b)),
        'b': dict(up=jnp.pad(b_alpha_up, ((0, 0), (0, B_LORA_PAD - B_GATE_LORA), (0, 0))),
                  bias=row3(b_alpha_bias), nw=row3(b_norm_w)),
        'w_out_a': w_out_mix[:, :A_WIDTH].astype(BF16),
        'w_out_b': w_out_mix[:, A_WIDTH:].astype(BF16),
        'c_w_qkv': c_w_qkv.astype(BF16),
        'c_w_qkv_scaled': (c_w_qkv * q_scale).astype(BF16),
        'c_w_o': c_w_o.astype(BF16),
        'c_bias_prompt': bias_prompt.reshape(bias_prompt.shape[0], C_HEADS // 2, 2 * ATT_QB, ATT_WIN),
        'c_bias_sample': jnp.stack([_rel_bias_tile(t, sample_len, cache_rows + sample_len, cache_rows)
                                    for t in c_rel_bias]),
        'ln1_w': row3(ln1_w), 'ln1_b': row3(ln1_b), 'ln2_w': row3(ln2_w), 'ln2_b': row3(ln2_b),
        'ffn_w_in': ffn_w_in.astype(BF16),
        'ffn_w_out': ffn_w_out.astype(BF16),
    }
    return P


def kernel(x_prompt, x_sample, state_a_wkv, state_a_shift, state_b_gla, cache_c_k, cache_c_v, w_in_mix, a_mu, a_w0, a_w2, a_a0, a_a2, a_g2, a_k_k, a_k_a, a_r_k, a_ln_w, a_ln_b, b_alpha_up, b_alpha_bias, b_norm_w, w_out_mix, c_w_qkv, c_rel_bias, c_w_o, ln1_w, ln1_b, ln2_w, ln2_b, ffn_w_in, ffn_w_out):
    P = _prepare(w_in_mix, a_mu, a_w0, a_w2, a_a0, a_a2, a_g2, a_k_k, a_k_a, a_r_k, a_ln_w, a_ln_b,
                 b_alpha_up, b_alpha_bias, b_norm_w, w_out_mix, c_w_qkv, c_rel_bias, c_w_o,
                 ln1_w, ln1_b, ln2_w, ln2_b, ffn_w_in, ffn_w_out, x_sample.shape[1], cache_c_k.shape[3])
    bp = x_prompt.shape[0]
    dt = x_prompt.dtype
    n_even = state_a_wkv.shape[0]
    wkv_zero = jnp.zeros((n_even, bp, A_HEADS, A_HEAD_DIM, A_HEAD_DIM), dt)
    shift_zero = jnp.zeros((n_even, bp, A_PROJ), dt)
    gla_zero = jnp.zeros((n_even, bp, B_HEADS, B_KEY_DIM, B_VAL_DIM), dt)
    y_prompt, p_wkv, p_shift, p_gla, p_k, p_v = _trunk(x_prompt, wkv_zero, shift_zero, gla_zero, None, None, P)
    y_sample, s_wkv, s_shift, s_gla, s_k, s_v = _trunk(
        x_sample, state_a_wkv, state_a_shift, state_b_gla, cache_c_k, cache_c_v, P)
    return (y_prompt, y_sample, p_wkv, p_shift, p_gla, p_k, p_v, s_wkv, s_shift, s_gla, s_k, s_v)
```

```python
import functools

import jax
import jax.numpy as jnp
import numpy as np
from jax import lax
from jax.experimental import pallas as pl
from jax.experimental.pallas import tpu as pltpu

F32 = jnp.float32
BF16 = jnp.bfloat16

D_MODEL = 1024
DEPTH = 4
CHUNK = 64
A_WIDTH = 512
A_HEAD_DIM = 64
A_HEADS = 8
A_DECAY_LORA = 64
A_ICL_LORA = 64
A_GATE_LORA = 128
A_PROJ = 3 * A_WIDTH + A_DECAY_LORA + A_ICL_LORA + A_GATE_LORA
A_NORM_EPS = 64e-5
B_VWIDTH = 512
B_KWIDTH = 256
B_HEADS = 4
B_KEY_DIM = 64
B_VAL_DIM = 128
B_GATE_LORA = 16
B_GATE_NORM = 16.0
B_LORA_PAD = 128
B_PROJ_PAD = 2 * B_KWIDTH + 2 * B_VWIDTH + B_LORA_PAD
C_HEAD_DIM = 64
C_HEADS = 16
C_PAST_CHUNKS = 8
C_PAST_ROWS = C_PAST_CHUNKS * CHUNK
C_REL_CLIP = 128
FFN_HIDDEN = 2816
LN_EPS = 1e-5
DEEPNORM_ALPHA = (2.0 * DEPTH) ** 0.25
NEG_INF = -1e30

LANE = 128
PAIR = LANE
HALF = PAIR // 2
A_PAIRS = A_HEADS // 2
B_PAIRS = B_HEADS // 2
VMEM_LIMIT = 56 * 1024 * 1024

NN = (((1,), (0,)), ((), ()))
NT = (((1,), (1,)), ((), ()))
TN = (((0,), (0,)), ((), ()))


def _params(*sem):
    return pltpu.CompilerParams(dimension_semantics=sem, vmem_limit_bytes=VMEM_LIMIT)


def _dg(a, b, dn):
    return lax.dot_general(a, b, dn, preferred_element_type=F32)


def _dot1(a, b, dn=NN):
    return _dg(a.astype(BF16), b.astype(BF16), dn)


def _split(a):
    hi = a.astype(BF16)
    lo = (a - hi.astype(F32)).astype(BF16)
    return hi, lo


def _dot3s(a, b, dn=NN):
    return _dg(a[0], b[0], dn) + _dg(a[0], b[1], dn) + _dg(a[1], b[0], dn)


def _dot3(a, b, dn=NN):
    return _dot3s(_split(a), _split(b), dn)


def _dot_xl(a_exact, b, dn=NN):
    bh, bl = _split(b)
    return _dg(a_exact, bh, dn) + _dg(a_exact, bl, dn)


def _dot_xr(a, b_exact, dn=NN):
    ah, al = _split(a)
    return _dg(ah, b_exact, dn) + _dg(al, b_exact, dn)


def _ln(z, w, b):
    mu = jnp.mean(z, -1, keepdims=True)
    d = z - mu
    var = jnp.mean(d * d, -1, keepdims=True)
    return d * lax.rsqrt(var + LN_EPS) * w + b


def _sigmoid(x):
    return 1.0 / (1.0 + jnp.exp(-x))


def _softplus(x):
    return jnp.maximum(x, 0.0) + jnp.log(1.0 + jnp.exp(-jnp.abs(x)))


def _stack_masked(x, m0):
    return jnp.concatenate([jnp.where(m0, x, 0.0), jnp.where(m0, 0.0, x)], axis=0)


def _fold(x):
    n = x.shape[0] // 2
    return x[:n] + x[n:]


def _mm_kernel(x_ref, w_ref, o_ref):
    o_ref[...] = jnp.dot(x_ref[...].astype(BF16), w_ref[...], preferred_element_type=F32)


def _matmul(x, w, tm, tn):
    M, K = x.shape
    N = w.shape[1]
    tm = min(tm, M)
    return pl.pallas_call(
        _mm_kernel,
        grid=(M // tm, N // tn),
        in_specs=[pl.BlockSpec((tm, K), lambda i, j: (i, 0)),
                  pl.BlockSpec((K, tn), lambda i, j: (0, j))],
        out_specs=pl.BlockSpec((tm, tn), lambda i, j: (i, j)),
        out_shape=jax.ShapeDtypeStruct((M, N), F32),
        compiler_params=_params("parallel", "arbitrary"),
    )(x, w)


def _in_proj_kernel(x_ref, wa_ref, wb_ref, oa_ref, ob_ref):
    xb = x_ref[...].astype(BF16)
    oa_ref[...] = jnp.dot(xb, wa_ref[...], preferred_element_type=F32)
    ob_ref[...] = jnp.dot(xb, wb_ref[...], preferred_element_type=F32)


def _in_proj(x, wa, wb, tm):
    M, K = x.shape
    tm = min(tm, M)
    na, nb = wa.shape[1], wb.shape[1]
    resident = lambda n: pl.BlockSpec((K, n), lambda i: (0, 0), pipeline_mode=pl.Buffered(1))
    return pl.pallas_call(
        _in_proj_kernel,
        grid=(M // tm,),
        in_specs=[pl.BlockSpec((tm, K), lambda i: (i, 0)), resident(na), resident(nb)],
        out_specs=[pl.BlockSpec((tm, na), lambda i: (i, 0)), pl.BlockSpec((tm, nb), lambda i: (i, 0))],
        out_shape=[jax.ShapeDtypeStruct((M, na), F32), jax.ShapeDtypeStruct((M, nb), F32)],
        compiler_params=_params("parallel"),
    )(x, wa, wb)


def _mix_ffn_kernel(n_pairs, *refs):
    res_ref, lw1_ref, lb1_ref, wg_ref, wu_ref, wo_ref, lw2_ref, lb2_ref, o_ref = refs[2 * n_pairs:]
    tm = o_ref.shape[0]
    n_half = 2 if tm % 32 == 0 else 1
    halves = [slice(i * (tm // n_half), (i + 1) * (tm // n_half)) for i in range(n_half)]
    mm = lambda a, w: jnp.dot(a, w, preferred_element_type=F32)
    acc = [mm(refs[0][h, :].astype(BF16), refs[1][...]) for h in halves]
    for p in range(1, n_pairs):
        acc = [acc[i] + mm(refs[2 * p][h, :].astype(BF16), refs[2 * p + 1][...]) for i, h in enumerate(halves)]
    x = [_ln(DEEPNORM_ALPHA * res_ref[h, :] + acc[i], lw1_ref[...], lb1_ref[...]) for i, h in enumerate(halves)]
    xb = [v.astype(BF16) for v in x]
    g = [mm(v, wg_ref[...]) for v in xb]
    u = [mm(v, wu_ref[...]) for v in xb]
    act = [(g[i] * _sigmoid(g[i]) * u[i]).astype(BF16) for i in range(n_half)]
    y = [mm(v, wo_ref[...]) for v in act]
    for i, h in enumerate(halves):
        o_ref[h, :] = _ln(DEEPNORM_ALPHA * x[i] + y[i], lw2_ref[...], lb2_ref[...])


def _mix_ffn(pairs, res, lw1, lb1, w_in, w_out, lw2, lb2, tm):
    M = res.shape[0]
    tm = min(tm, M)
    resident = lambda shape, idx: pl.BlockSpec(shape, idx, pipeline_mode=pl.Buffered(1))
    vec = pl.BlockSpec((1, D_MODEL), lambda i: (0, 0))
    in_specs, args = [], []
    for a, w in pairs:
        K = a.shape[1]
        in_specs += [pl.BlockSpec((tm, K), lambda i: (i, 0)), resident((K, D_MODEL), lambda i: (0, 0))]
        args += [a, w]
    in_specs += [pl.BlockSpec((tm, D_MODEL), lambda i: (i, 0)), vec, vec,
                 resident((D_MODEL, FFN_HIDDEN), lambda i: (0, 0)),
                 resident((D_MODEL, FFN_HIDDEN), lambda i: (0, 1)),
                 resident((FFN_HIDDEN, D_MODEL), lambda i: (0, 0)), vec, vec]
    return pl.pallas_call(
        functools.partial(_mix_ffn_kernel, len(pairs)),
        grid=(M // tm,),
        in_specs=in_specs,
        out_specs=pl.BlockSpec((tm, D_MODEL), lambda i: (i, 0)),
        out_shape=jax.ShapeDtypeStruct((M, D_MODEL), F32),
        compiler_params=_params("parallel"),
    )(*args, res, lw1, lb1, w_in, w_in, w_out, lw2, lb2)


def _inv_unit_lower(ns, eye, nilpotency):
    size = eye.shape[0]
    xs = [eye - n for n in ns]
    ps = [_dot1(n, n) for n in ns]
    k = 2
    while k < nilpotency:
        k *= 2
        if k < nilpotency:
            rs = [_dot1(p, jnp.concatenate([x, p], axis=1)) for x, p in zip(xs, ps)]
            xs = [x + r[:, :size] for x, r in zip(xs, rs)]
            ps = [r[:, size:] for r in rs]
        else:
            xs = [x + _dot1(p, x) for x, p in zip(xs, ps)]
    return xs


def _segsum(x, seg):
    n = x.shape[1] // PAIR
    rows = x.shape[0]
    xs = jnp.concatenate([x[:, j * PAIR:(j + 1) * PAIR] for j in range(n)], axis=0)
    s = _dot_xr(xs, seg)
    return jnp.concatenate([s[j * rows:(j + 1) * rows] for j in range(n)], axis=1)


def _head_mask():
    hr = lax.broadcasted_iota(jnp.int32, (PAIR, PAIR), 0) // A_HEAD_DIM
    hc = lax.broadcasted_iota(jnp.int32, (PAIR, PAIR), 1) // A_HEAD_DIM
    return (hr == hc).astype(BF16)


def _rwkv_prep_kernel(n_chunks, pa_ref, prev_ref, shift_ref, mu_ref, w0_ref, w2_ref, a0_ref, a2_ref, g2_ref,
                      kk_ref, ka_ref, rk_ref,
                      mh_ref, ch_ref, qc_ref, oc_ref, g_ref, bonus_ref):
    i = pl.program_id(1)
    L = CHUNK
    pa = pa_ref[...]
    first = jnp.where(i == 0, shift_ref[...], prev_ref[7:8, :])
    rows = lax.broadcasted_iota(jnp.int32, pa.shape, 0)
    prev = jnp.where(rows == 0, first, pltpu.roll(pa, shift=1, axis=0))
    xs = pa + (prev - pa) * mu_ref[...]
    W = A_WIDTH
    r = xs[:, 0:W]
    k0 = xs[:, W:2 * W]
    v = xs[:, 2 * W:3 * W]
    xw = xs[:, 3 * W:3 * W + A_DECAY_LORA]
    xa = xs[:, 3 * W + A_DECAY_LORA:3 * W + A_DECAY_LORA + A_ICL_LORA]
    xg = xs[:, 3 * W + A_DECAY_LORA + A_ICL_LORA:]
    w = -_softplus(-(w0_ref[...] + _dot3(jnp.tanh(xw), w2_ref[...]))) - 0.5
    logw = -jnp.exp(w)
    a = _sigmoid(a0_ref[...] + _dot3(xa, a2_ref[...]))
    g_ref[...] = _dot3(_sigmoid(xg), g2_ref[...])
    seg = _head_mask()
    kk = k0 * kk_ref[...]
    kk = kk * lax.rsqrt(_segsum(kk * kk, seg) + 1e-12)
    k = k0 * (1.0 + (a - 1.0) * ka_ref[...])
    bonus_ref[...] = _segsum(r * k * rk_ref[...], seg) * v
    b = kk * a

    tri = (lax.broadcasted_iota(jnp.int32, (L, L), 0) >= lax.broadcasted_iota(jnp.int32, (L, L), 1)).astype(BF16)
    row = lax.broadcasted_iota(jnp.int32, (PAIR, PAIR), 0)
    col = lax.broadcasted_iota(jnp.int32, (PAIR, PAIR), 1)
    eye = (row == col).astype(F32)
    strict = row > col
    incl2 = (lax.broadcasted_iota(jnp.int32, (PAIR, 2 * PAIR), 0)
             >= lax.broadcasted_iota(jnp.int32, (PAIR, 2 * PAIR), 1) % PAIR)
    m0 = lax.broadcasted_iota(jnp.int32, (L, PAIR), 1) < HALF
    pairs = range(A_PAIRS)
    lanes = [slice(j * PAIR, (j + 1) * PAIR) for j in pairs]
    chains = [(c, j) for c in range(n_chunks) for j in pairs]
    every = range(len(chains))
    rows_of = [slice(c * L, (c + 1) * L) for c in range(n_chunks)]
    kkg, rg, kinv, binv, kd, bd, e_last = [], [], [], [], [], [], []
    for sl in rows_of:
        lw = logw[sl]
        cum = _dot_xl(tri, lw)
        c_last = cum[L - 1:L, :]
        e_neg = jnp.exp(-cum)
        e_rem = jnp.exp(c_last - cum)
        kkg.append(kk[sl] * jnp.exp(cum - lw))
        rg.append(r[sl] * jnp.exp(cum))
        kinv.append(k[sl] * e_neg)
        binv.append(b[sl] * e_neg)
        kd.append(k[sl] * e_rem)
        bd.append(b[sl] * e_rem)
        e_last.append(jnp.exp(c_last))
    sm = lambda xs_, c, j: _stack_masked(xs_[c][:, lanes[j]].astype(BF16), m0)
    kkg_b = [sm(kkg, c, j) for c, j in chains]
    v_b = [_stack_masked(v[rows_of[c], lanes[j]].astype(BF16), m0) for c, j in chains]
    bd_b = [sm(bd, c, j) for c, j in chains]
    kd_b = [sm(kd, c, j) for c, j in chains]
    rg_b = [sm(rg, c, j) for c, j in chains]
    kinv_b = [sm(kinv, c, j) for c, j in chains]
    binv_b = [sm(binv, c, j) for c, j in chains]
    aa = [_dg(jnp.concatenate([kkg_b[n], rg_b[n]], axis=0), jnp.concatenate([binv_b[n], kinv_b[n]], axis=0), NT)
          for n in every]
    a_kb = [jnp.where(strict, x[:PAIR, :PAIR], 0.0) for x in aa]
    a_kk = [jnp.where(strict, x[:PAIR, PAIR:], 0.0).astype(BF16) for x in aa]
    a_rbk = [jnp.where(incl2, x[PAIR:, :], 0.0).astype(BF16) for x in aa]
    t = _inv_unit_lower(a_kb, eye, L)
    av = [_dg(a_kk[n], v_b[n], NN).astype(BF16) for n in every]
    wu = [_dg(t[n].astype(BF16), jnp.concatenate([kkg_b[n], av[n]], axis=1), NN) for n in every]
    wu_b = [x.astype(BF16) for x in wu]
    mc = [_dg(bd_b[n], wu_b[n], TN) for n in every]
    kv = [_dg(kd_b[n], v_b[n], TN) for n in every]
    rhs = [jnp.concatenate([jnp.concatenate([wu_b[n][:, :PAIR], (-wu[n][:, PAIR:]).astype(BF16)], axis=1),
                            jnp.concatenate([jnp.zeros_like(v_b[n]), v_b[n]], axis=1)], axis=0) for n in every]
    qo = [_dg(a_rbk[n], rhs[n], NN) for n in every]
    for n, (c, j) in enumerate(chains):
        mh_ref[c, j] = (eye * e_last[c][:, lanes[j]] - mc[n][:, :PAIR]).astype(BF16)
        ch_ref[c, j] = kv[n] - mc[n][:, PAIR:]
        qc_ref[rows_of[c], lanes[j]] = (rg[c][:, lanes[j]] - _fold(qo[n][:, :PAIR])).astype(BF16)
        oc_ref[rows_of[c], lanes[j]] = _fold(qo[n][:, PAIR:])


def _rwkv_scan_kernel(n_seq, n_chunks, mh_ref, ch_ref, qc_ref, oc_ref, g_ref, bonus_ref, h0_ref, lnw_ref, lnb_ref,
                      y_ref, hout_ref, h_ref):
    i = pl.program_id(1)
    L = CHUNK

    @pl.when(i == 0)
    def _():
        h_ref[...] = h0_ref[...]

    seg = _head_mask()
    inv_n = 1.0 / A_HEAD_DIM
    lanes = [slice(j * PAIR, (j + 1) * PAIR) for j in range(A_PAIRS)]
    chains = [(s, j) for s in range(n_seq) for j in range(A_PAIRS)]
    hb = [h_ref[s, j] for s, j in chains]
    for c in range(n_chunks):
        sl = slice(c * L, (c + 1) * L)
        hb_b = [x.astype(BF16) for x in hb]
        o = [_dg(qc_ref[s, sl, lanes[j]], hb_b[n], NN) + oc_ref[s, sl, lanes[j]] for n, (s, j) in enumerate(chains)]
        hb = [_dg(mh_ref[s, c, j], hb_b[n], NN) + ch_ref[s, c, j] for n, (s, j) in enumerate(chains)]
        for s in range(n_seq):
            os_ = jnp.concatenate(o[s * A_PAIRS:(s + 1) * A_PAIRS], axis=1)
            mean = _segsum(os_, seg) * inv_n
            d = os_ - mean
            var = _segsum(d * d, seg) * inv_n
            on = d * lax.rsqrt(var + A_NORM_EPS) * lnw_ref[...] + lnb_ref[...]
            y_ref[s, sl, :] = ((on + bonus_ref[s, sl, :]) * g_ref[s, sl, :]).astype(y_ref.dtype)
    for n, (s, j) in enumerate(chains):
        h_ref[s, j] = hb[n]
    hout_ref[...] = h_ref[...]


def _rwkv_mix(pa, shift0, wkv0, prm):
    Bsz, T, _ = pa.shape
    n1 = next(n for n in (8, 4, 2, 1) if (T // CHUNK) % n == 0)
    R1 = n1 * CHUNK
    W = A_WIDTH
    nblk = T // R1
    row = lambda n: pl.BlockSpec((1, n), lambda b, i: (0, 0))
    full = lambda s: pl.BlockSpec(s, lambda b, i: (0,) * len(s))
    act = lambda r, n: pl.BlockSpec((None, r, n), lambda b, i: (b, i, 0))
    mat = lambda n: pl.BlockSpec((None, n, A_PAIRS, PAIR, PAIR), lambda b, i: (b, i, 0, 0, 0))
    sub = R1 // 8
    mats = lambda dt: jax.ShapeDtypeStruct((Bsz, T // CHUNK, A_PAIRS, PAIR, PAIR), dt)
    acts = lambda dt: jax.ShapeDtypeStruct((Bsz, T, W), dt)
    mh, ch, qc, oc, g, bonus = pl.pallas_call(
        functools.partial(_rwkv_prep_kernel, n1),
        grid=(Bsz, nblk),
        in_specs=[act(R1, A_PROJ),
                  pl.BlockSpec((None, 8, A_PROJ), lambda b, i: (b, jnp.maximum(i * sub - 1, 0), 0)),
                  pl.BlockSpec((None, 1, A_PROJ), lambda b, i: (b, 0, 0)),
                  row(A_PROJ), row(W), full((A_DECAY_LORA, W)), row(W), full((A_ICL_LORA, W)),
                  full((A_GATE_LORA, W)), row(W), row(W), row(W)],
        out_specs=[mat(n1), mat(n1), act(R1, W), act(R1, W), act(R1, W), act(R1, W)],
        out_shape=[mats(BF16), mats(F32), acts(BF16), acts(F32), acts(F32), acts(F32)],
        compiler_params=_params("parallel", "parallel"),
    )(pa, pa, shift0[:, None, :], prm['mu'], prm['w0'], prm['w2'], prm['a0'], prm['a2'], prm['g2'],
      prm['k_k'], prm['k_a'], prm['r_k'])

    n2 = min(4, T // CHUNK)
    R2 = n2 * CHUNK
    hh = wkv0.transpose(0, 1, 3, 2).reshape(Bsz, A_PAIRS, 2, A_HEAD_DIM, A_HEAD_DIM)
    zero = jnp.zeros_like(hh[:, :, 0])
    h0 = jnp.concatenate([jnp.concatenate([hh[:, :, 0], zero], axis=-1),
                          jnp.concatenate([zero, hh[:, :, 1]], axis=-1)], axis=-2)
    ns = next(n for n in (4, 2, 1) if Bsz % n == 0)
    state = pl.BlockSpec((ns, A_PAIRS, PAIR, PAIR), lambda b, i: (b, 0, 0, 0))
    act2 = lambda r, n: pl.BlockSpec((ns, r, n), lambda b, i: (b, i, 0))
    mat2 = lambda n: pl.BlockSpec((ns, n, A_PAIRS, PAIR, PAIR), lambda b, i: (b, i, 0, 0, 0))
    y, hout = pl.pallas_call(
        functools.partial(_rwkv_scan_kernel, ns, n2),
        grid=(Bsz // ns, T // R2),
        in_specs=[mat2(n2), mat2(n2), act2(R2, W), act2(R2, W), act2(R2, W), act2(R2, W), state,
                  row(W), row(W)],
        out_specs=[act2(R2, W), state],
        out_shape=[acts(BF16), jax.ShapeDtypeStruct((Bsz, A_PAIRS, PAIR, PAIR), F32)],
        scratch_shapes=[pltpu.VMEM((ns, A_PAIRS, PAIR, PAIR), F32)],
        compiler_params=_params("parallel", "arbitrary"),
    )(mh, ch, qc, oc, g, bonus, h0, prm['ln_w'], prm['ln_b'])
    hd = jnp.stack([hout[:, :, :HALF, :HALF], hout[:, :, HALF:, HALF:]], axis=2)
    wkv = hd.reshape(Bsz, A_HEADS, A_HEAD_DIM, A_HEAD_DIM).transpose(0, 1, 3, 2)
    return y, wkv


def _gla_kernel(n_seq, n_chunks, pb_ref, s0_ref, up_ref, bias_ref, nw_ref, y_ref, sout_ref, s_ref):
    i = pl.program_id(1)
    L = CHUNK

    @pl.when(i == 0)
    def _():
        s_ref[...] = s0_ref[...]

    kw, vw = B_KWIDTH, B_VWIDTH
    incl = lax.broadcasted_iota(jnp.int32, (L, L), 0) >= lax.broadcasted_iota(jnp.int32, (L, L), 1)
    tri = incl.astype(BF16)
    m0 = lax.broadcasted_iota(jnp.int32, (L, PAIR), 1) < HALF
    top = lax.broadcasted_iota(jnp.int32, (PAIR, B_VAL_DIM), 0) < HALF
    scale = B_KEY_DIM ** -0.5
    pairs = range(B_PAIRS)
    heads = range(B_HEADS)
    pl_ = lambda h: slice((h // 2) * PAIR, (h // 2 + 1) * PAIR)
    units = [(s, c) for c in range(n_chunks) for s in range(n_seq)]
    rows_of = lambda c: slice(c * L, (c + 1) * L)
    vslice = lambda h: slice(h * B_VAL_DIM, (h + 1) * B_VAL_DIM)
    z = [_dot3(pb_ref[s, rows_of(c), 2 * kw + 2 * vw:], up_ref[...]) + bias_ref[...] for s, c in units]
    bc = [_dot_xl(tri, -_softplus(-x) * (1.0 / B_GATE_NORM)) for x in z]
    b_last = [x[L - 1:L, :] for x in bc]
    q_dec, k_inv, k_state, e_last_t = [], [], [], []
    for n, (s, c) in enumerate(units):
        k = pb_ref[s, rows_of(c), kw:2 * kw]
        q_dec.append(pb_ref[s, rows_of(c), 0:kw] * scale * jnp.exp(bc[n]))
        k_inv.append((k * jnp.exp(-bc[n])).astype(BF16))
        k_state.append((k * jnp.exp(b_last[n] - bc[n])).astype(BF16))
        e_last_t.append(jnp.broadcast_to(jnp.exp(b_last[n]), (LANE, kw)).T)
    uh = [(n, h) for n in range(len(units)) for h in heads]
    vb = [pb_ref[units[n][0], rows_of(units[n][1]), 2 * kw + h * B_VAL_DIM:2 * kw + (h + 1) * B_VAL_DIM].astype(BF16)
          for n, h in uh]
    qm = [jnp.where(m0 if h % 2 == 0 else ~m0, q_dec[n][:, pl_(h)], 0.0).astype(BF16) for n, h in uh]
    scores = [jnp.where(incl, _dg(qm[i_], k_inv[n][:, pl_(h)], NT), 0.0).astype(BF16) for i_, (n, h) in enumerate(uh)]
    upd = [_dg(k_state[n][:, pl_(h)], vb[i_], TN) for i_, (n, h) in enumerate(uh)]
    intra = [_dg(scores[i_], vb[i_], NN) for i_ in range(len(uh))]
    states = [[s_ref[s, j] for j in pairs] for s in range(n_seq)]
    before = []
    for n, (s, c) in enumerate(units):
        before.append([x.astype(BF16) for x in states[s]])
        states[s] = [states[s][j] * e_last_t[n][j * PAIR:(j + 1) * PAIR, :]
                     + jnp.where(top, upd[n * B_HEADS + 2 * j], upd[n * B_HEADS + 2 * j + 1]) for j in pairs]
    o = [_dg(qm[i_], before[n][h // 2], NN) + intra[i_] for i_, (n, h) in enumerate(uh)]
    for i_, (n, h) in enumerate(uh):
        s, c = units[n]
        oh = o[i_] * lax.rsqrt(jnp.mean(o[i_] * o[i_], -1, keepdims=True) + LN_EPS) * nw_ref[...]
        rg_h = pb_ref[s, rows_of(c), 2 * kw + vw + h * B_VAL_DIM:2 * kw + vw + (h + 1) * B_VAL_DIM]
        y_ref[s, rows_of(c), vslice(h)] = (oh * (rg_h * _sigmoid(rg_h))).astype(y_ref.dtype)
    for s in range(n_seq):
        for j in pairs:
            s_ref[s, j] = states[s][j]
    sout_ref[...] = s_ref[...]


def _gla_mix(pb, s0, prm):
    Bsz, T, _ = pb.shape
    n = min(4, T // CHUNK)
    Rr = n * CHUNK
    ns = next(n for n in (4, 2, 1) if Bsz % n == 0)
    st = pl.BlockSpec((ns, B_PAIRS, PAIR, B_VAL_DIM), lambda b, i: (b, 0, 0, 0))
    y, s = pl.pallas_call(
        functools.partial(_gla_kernel, ns, n),
        grid=(Bsz // ns, T // Rr),
        in_specs=[pl.BlockSpec((ns, Rr, B_PROJ_PAD), lambda b, i: (b, i, 0)), st,
                  pl.BlockSpec((B_LORA_PAD, B_KWIDTH), lambda b, i: (0, 0)),
                  pl.BlockSpec((1, B_KWIDTH), lambda b, i: (0, 0)),
                  pl.BlockSpec((1, B_VAL_DIM), lambda b, i: (0, 0))],
        out_specs=[pl.BlockSpec((ns, Rr, B_VWIDTH), lambda b, i: (b, i, 0)), st],
        out_shape=[jax.ShapeDtypeStruct((Bsz, T, B_VWIDTH), BF16),
                   jax.ShapeDtypeStruct((Bsz, B_PAIRS, PAIR, B_VAL_DIM), F32)],
        scratch_shapes=[pltpu.VMEM((ns, B_PAIRS, PAIR, B_VAL_DIM), F32)],
        compiler_params=_params("parallel", "arbitrary"),
    )(pb, s0.reshape(Bsz, B_PAIRS, PAIR, B_VAL_DIM), prm['up'], prm['bias'], prm['nw'])
    return y, s.reshape(Bsz, B_HEADS, B_KEY_DIM, B_VAL_DIM)


ATT_QB = 2 * CHUNK
ATT_SCALE = C_HEAD_DIM ** -0.5
LOG2E = 1.4426950408889634


ATT_WIN = C_PAST_ROWS + ATT_QB
ATT_MAX_QB = 8
ATT_PAD = ATT_MAX_QB * ATT_QB


def _qkv_pad_kernel(n_pad, x_ref, w_ref, o_ref):
    r = pl.program_id(1)

    @pl.when(r < n_pad)
    def _():
        o_ref[...] = jnp.zeros_like(o_ref)

    @pl.when(r >= n_pad)
    def _():
        y = jnp.dot(x_ref[...].astype(BF16), w_ref[...], preferred_element_type=F32).astype(BF16)
        qk = 2 * D_MODEL
        o_ref[:, :qk] = y[:, :qk]
        ones = jnp.ones((y.shape[0], PAIR), BF16)
        for hp in range(C_HEADS // 2):
            o_ref[:, qk + 2 * hp * PAIR:qk + (2 * hp + 1) * PAIR] = y[:, qk + hp * PAIR:qk + (hp + 1) * PAIR]
            o_ref[:, qk + (2 * hp + 1) * PAIR:qk + (2 * hp + 2) * PAIR] = ones


def _qkv_padded(x, w):
    Bsz, T, D = x.shape
    N = w.shape[1]
    n_out = N + D
    tm = next(t for t in (512, 256, 128) if T % t == 0)
    n_pad = ATT_PAD // tm
    return pl.pallas_call(
        functools.partial(_qkv_pad_kernel, n_pad),
        grid=(Bsz, T // tm + n_pad),
        in_specs=[pl.BlockSpec((None, tm, D), lambda b, r: (b, jnp.maximum(r - n_pad, 0), 0)),
                  pl.BlockSpec((D, N), lambda b, r: (0, 0), pipeline_mode=pl.Buffered(1))],
        out_specs=pl.BlockSpec((None, tm, n_out), lambda b, r: (b, r, 0)),
        out_shape=jax.ShapeDtypeStruct((Bsz, ATT_PAD + T, n_out), BF16),
        compiler_params=_params("parallel", "arbitrary"),
    )(x, w)


def _kv_tail(x, w, keep):
    Bsz, T, D = x.shape
    tm = next(t for t in (512, 256, 128, 64) if keep % t == 0 and T % t == 0)
    first = (T - keep) // tm
    return pl.pallas_call(
        _mm_kernel,
        grid=(Bsz, keep // tm, 2),
        in_specs=[pl.BlockSpec((None, tm, D), lambda b, r, j: (b, first + r, 0)),
                  pl.BlockSpec((D, D), lambda b, r, j: (0, j + 1))],
        out_specs=pl.BlockSpec((None, tm, D), lambda b, r, j: (b, r, j)),
        out_shape=jax.ShapeDtypeStruct((Bsz, keep, 2 * D), F32),
        compiler_params=_params("parallel", "parallel", "arbitrary"),
    )(x, w)


def _att_prompt_kernel(n_qb, q_ref, k_ref, v_ref, bias_ref, o_ref):
    g = pl.program_id(2)
    m0 = lax.broadcasted_iota(jnp.int32, (ATT_QB, PAIR), 1) < HALF
    col = lax.broadcasted_iota(jnp.int32, (2 * ATT_QB, ATT_WIN), 1)

    def run(window_has_padding):
        blocks = range(n_qb)
        rows = [slice(t * ATT_QB, (t + 1) * ATT_QB) for t in blocks]
        wins = [pl.ds(pl.multiple_of(ATT_PAD - C_PAST_ROWS + (g * n_qb + t) * ATT_QB, ATT_QB), ATT_WIN)
                for t in blocks]
        q = [_stack_masked(q_ref[rows[t], :], m0) for t in blocks]
        s = [_dg(q[t], k_ref[wins[t], :], NT) + bias_ref[...] for t in blocks]
        if window_has_padding:
            s = [jnp.where(col >= C_PAST_ROWS - (g * n_qb + t) * ATT_QB, s[t], NEG_INF) for t in blocks]
        p = [jnp.exp2(x - x.max(-1, keepdims=True)) for x in s]
        pv = [_dg(p[t].astype(BF16), v_ref[wins[t], :], NN) for t in blocks]
        o = [x[:, :PAIR] / x[:, PAIR:] for x in pv]
        for t in blocks:
            o_ref[rows[t], :] = jnp.where(m0, o[t][:ATT_QB], o[t][ATT_QB:]).astype(o_ref.dtype)

    first = g * (n_qb * ATT_QB) < C_PAST_ROWS
    pl.when(first)(lambda: run(True))
    pl.when(jnp.logical_not(first))(lambda: run(False))


def _attention_prompt(qkv, bias):
    Bsz, Tp, _ = qkv.shape
    T = Tp - ATT_PAD
    nhp = C_HEADS // 2
    nblk = T // ATT_QB
    n_qb = next(n for n in (ATT_MAX_QB, 4, 2, 1) if nblk % n == 0)
    rows = n_qb * ATT_QB
    skip = ATT_PAD // rows
    return pl.pallas_call(
        functools.partial(_att_prompt_kernel, n_qb),
        grid=(Bsz, nhp, T // rows),
        in_specs=[pl.BlockSpec((None, rows, LANE), lambda b, hp, g: (b, g + skip, hp)),
                  pl.BlockSpec((None, Tp, LANE), lambda b, hp, g: (b, 0, nhp + hp)),
                  pl.BlockSpec((None, Tp, 2 * LANE), lambda b, hp, g: (b, 0, nhp + hp)),
                  pl.BlockSpec((None, 2 * ATT_QB, ATT_WIN), lambda b, hp, g: (hp, 0, 0))],
        out_specs=pl.BlockSpec((None, rows, LANE), lambda b, hp, g: (b, g, hp)),
        out_shape=jax.ShapeDtypeStruct((Bsz, T, D_MODEL), BF16),
        compiler_params=_params("parallel", "parallel", "arbitrary"),
    )(qkv, qkv, qkv, bias)


def _att_sample_kernel(qkv_ref, kc_ref, vc_ref, bias_ref, o_ref):
    R = kc_ref.shape[1]
    heads = range(C_HEADS)
    col = lambda part, h: slice(part * D_MODEL + h * C_HEAD_DIM, part * D_MODEL + (h + 1) * C_HEAD_DIM)
    q = [(qkv_ref[:, col(0, h)] * ATT_SCALE).astype(BF16) for h in heads]
    s_c = [_dg(q[h], kc_ref[h].astype(BF16), NT) + bias_ref[h, :, 0:R] for h in heads]
    s_n = [_dg(q[h], qkv_ref[:, col(1, h)].astype(BF16), NT) + bias_ref[h, :, R:] for h in heads]
    m = [jnp.maximum(s_c[h].max(-1, keepdims=True), s_n[h].max(-1, keepdims=True)) for h in heads]
    p_c = [jnp.exp(s_c[h] - m[h]) for h in heads]
    p_n = [jnp.exp(s_n[h] - m[h]) for h in heads]
    acc = [_dg(p_c[h].astype(BF16), vc_ref[h].astype(BF16), NN)
           + _dg(p_n[h].astype(BF16), qkv_ref[:, col(2, h)].astype(BF16), NN) for h in heads]
    for h in heads:
        den = p_c[h].sum(-1, keepdims=True) + p_n[h].sum(-1, keepdims=True)
        o_ref[:, col(0, h)] = (acc[h] / den).astype(o_ref.dtype)


def _attention_sample(qkv, cache_k, cache_v, bias):
    Bsz, T, _ = qkv.shape
    R = cache_k.shape[2]
    cache = pl.BlockSpec((None, C_HEADS, R, C_HEAD_DIM), lambda b: (b, 0, 0, 0))
    return pl.pallas_call(
        _att_sample_kernel,
        grid=(Bsz,),
        in_specs=[pl.BlockSpec((None, T, 3 * D_MODEL), lambda b: (b, 0, 0)), cache, cache,
                  pl.BlockSpec((C_HEADS, T, R + T), lambda b: (0, 0, 0))],
        out_specs=pl.BlockSpec((None, T, D_MODEL), lambda b: (b, 0, 0)),
        out_shape=jax.ShapeDtypeStruct((Bsz, T, D_MODEL), BF16),
        compiler_params=_params("parallel"),
    )(qkv, cache_k, cache_v, bias)


def _rel_bias_tile(table, n_q, n_k, q_offset):
    period = n_q + n_k - 1
    m = np.arange(period)
    d = np.where(m < n_k, m, m - period)
    idx = np.clip(q_offset - d, -C_REL_CLIP, C_REL_CLIP) + C_REL_CLIP
    diag = table[:, idx]
    flat = jnp.tile(diag, (1, n_q))[:, :n_q * (period - 1)]
    return flat.reshape(table.shape[0], n_q, period - 1)[:, :, :n_k]


TM = 512


def _trunk(x, wkv0, shift0, gla0, cache_k, cache_v, P):
    Bsz, T, _ = x.shape
    M = Bsz * T
    xf = x.reshape(M, D_MODEL)
    wkv_o, shift_o, gla_o, k_o, v_o = [], [], [], [], []
    for layer in range(DEPTH):
        if layer % 2 == 0:
            e = layer // 2
            pa, pb = _in_proj(xf, P['w_in_a'][e], P['w_in_b'][e], TM)
            pa = pa.reshape(Bsz, T, A_PROJ)
            pb = pb.reshape(Bsz, T, B_PROJ_PAD)
            ya, wkv = _rwkv_mix(pa, shift0[e], wkv0[e], {k: v[e] for k, v in P['a'].items()})
            yb, gs = _gla_mix(pb, gla0[e], {k: v[e] for k, v in P['b'].items()})
            wkv_o.append(wkv)
            shift_o.append(pa[:, -1])
            gla_o.append(gs)
            pairs = [(ya.reshape(M, A_WIDTH), P['w_out_a'][e]), (yb.reshape(M, B_VWIDTH), P['w_out_b'][e])]
        else:
            o = layer // 2
            if cache_k is None:
                x3 = xf.reshape(Bsz, T, D_MODEL)
                att = _attention_prompt(_qkv_padded(x3, P['c_w_qkv_scaled'][o]), P['c_bias_prompt'][o])
                keep = min(C_PAST_ROWS, T)
                rows = _kv_tail(x3, P['c_w_qkv'][o], keep).reshape(Bsz, keep, 2, C_HEADS, C_HEAD_DIM)
            else:
                qkv = _matmul(xf, P['c_w_qkv'][o], TM, D_MODEL).reshape(Bsz, T, 3 * D_MODEL)
                att = _attention_sample(qkv, cache_k[o], cache_v[o], P['c_bias_sample'][o])
                rows = qkv.reshape(Bsz, T, 3, C_HEADS, C_HEAD_DIM)[:, :, 1:]
            k_o.append(rows[:, :, 0].transpose(0, 2, 1, 3))
            v_o.append(rows[:, :, 1].transpose(0, 2, 1, 3))
            pairs = [(att.reshape(M, D_MODEL), P['c_w_o'][o])]
        xf = _mix_ffn(pairs, xf, P['ln1_w'][layer], P['ln1_b'][layer], P['ffn_w_in'][layer], P['ffn_w_out'][layer],
                      P['ln2_w'][layer], P['ln2_b'][layer], TM)
    return (xf.reshape(Bsz, T, D_MODEL), jnp.stack(wkv_o), jnp.stack(shift_o), jnp.stack(gla_o),
            jnp.stack(k_o), jnp.stack(v_o))


def _prepare(w_in_mix, a_mu, a_w0, a_w2, a_a0, a_a2, a_g2, a_k_k, a_k_a, a_r_k, a_ln_w, a_ln_b,
             b_alpha_up, b_alpha_bias, b_norm_w, w_out_mix, c_w_qkv, c_rel_bias, c_w_o,
             ln1_w, ln1_b, ln2_w, ln2_b, ffn_w_in, ffn_w_out, sample_len, cache_rows):
    kw, vw = B_KWIDTH, B_VWIDTH
    wb = w_in_mix[:, :, A_PROJ:]
    main = jnp.concatenate([wb[:, :, :2 * kw + vw], wb[:, :, 2 * kw + vw + B_GATE_LORA:]], axis=-1)
    lora = jnp.pad(wb[:, :, 2 * kw + vw:2 * kw + vw + B_GATE_LORA], ((0, 0), (0, 0), (0, B_LORA_PAD - B_GATE_LORA)))
    row3 = lambda t: t.reshape(t.shape[0], 1, -1)
    bias_prompt = jnp.stack([_rel_bias_tile(t, ATT_QB, ATT_WIN, C_PAST_ROWS) for t in c_rel_bias])
    rel = np.arange(ATT_WIN)[None, :] // CHUNK - np.arange(ATT_QB)[:, None] // CHUNK
    band = (rel >= 0) & (rel <= C_PAST_CHUNKS)
    bias_prompt = jnp.where(band, bias_prompt * LOG2E, NEG_INF)
    q_scale = jnp.concatenate([jnp.full((D_MODEL,), ATT_SCALE * LOG2E, F32), jnp.ones((2 * D_MODEL,), F32)])
    P = {
        'w_in_a': w_in_mix[:, :, :A_PROJ].astype(BF16),
        'w_in_b': jnp.concatenate([main, lora], axis=-1).astype(BF16),
        'a': dict(mu=row3(a_mu), w0=row3(a_w0), w2=a_w2, a0=row3(a_a0), a2=a_a2, g2=a_g2, k_k=row3(a_k_k),
                  k_a=row3(a_k_a), r_k=row3(a_r_k), ln_w=row3(a_ln_w), ln_b=row3(a_ln_b)),
        'b': dict(up=jnp.pad(b_alpha_up, ((0, 0), (0, B_LORA_PAD - B_GATE_LORA), (0, 0))),
                  bias=row3(b_alpha_bias), nw=row3(b_norm_w)),
        'w_out_a': w_out_mix[:, :A_WIDTH].astype(BF16),
        'w_out_b': w_out_mix[:, A_WIDTH:].astype(BF16),
        'c_w_qkv': c_w_qkv.astype(BF16),
        'c_w_qkv_scaled': (c_w_qkv * q_scale).astype(BF16),
        'c_w_o': c_w_o.astype(BF16),
        'c_bias_prompt': bias_prompt.reshape(bias_prompt.shape[0], C_HEADS // 2, 2 * ATT_QB, ATT_WIN),
        'c_bias_sample': jnp.stack([_rel_bias_tile(t, sample_len, cache_rows + sample_len, cache_rows)
                                    for t in c_rel_bias]),
        'ln1_w': row3(ln1_w), 'ln1_b': row3(ln1_b), 'ln2_w': row3(ln2_w), 'ln2_b': row3(ln2_b),
        'ffn_w_in': ffn_w_in.astype(BF16),
        'ffn_w_out': ffn_w_out.astype(BF16),
    }
    return P


def kernel(x_prompt, x_sample, state_a_wkv, state_a_shift, state_b_gla, cache_c_k, cache_c_v, w_in_mix, a_mu, a_w0, a_w2, a_a0, a_a2, a_g2, a_k_k, a_k_a, a_r_k, a_ln_w, a_ln_b, b_alpha_up, b_alpha_bias, b_norm_w, w_out_mix, c_w_qkv, c_rel_bias, c_w_o, ln1_w, ln1_b, ln2_w, ln2_b, ffn_w_in, ffn_w_out):
    P = _prepare(w_in_mix, a_mu, a_w0, a_w2, a_a0, a_a2, a_g2, a_k_k, a_k_a, a_r_k, a_ln_w, a_ln_b,
                 b_alpha_up, b_alpha_bias, b_norm_w, w_out_mix, c_w_qkv, c_rel_bias, c_w_o,
                 ln1_w, ln1_b, ln2_w, ln2_b, ffn_w_in, ffn_w_out, x_sample.shape[1], cache_c_k.shape[3])
    bp = x_prompt.shape[0]
    dt = x_prompt.dtype
    n_even = state_a_wkv.shape[0]
    wkv_zero = jnp.zeros((n_even, bp, A_HEADS, A_HEAD_DIM, A_HEAD_DIM), dt)
    shift_zero = jnp.zeros((n_even, bp, A_PROJ), dt)
    gla_zero = jnp.zeros((n_even, bp, B_HEADS, B_KEY_DIM, B_VAL_DIM), dt)
    y_prompt, p_wkv, p_shift, p_gla, p_k, p_v = _trunk(x_prompt, wkv_zero, shift_zero, gla_zero, None, None, P)
    y_sample, s_wkv, s_shift, s_gla, s_k, s_v = _trunk(
        x_sample, state_a_wkv, state_a_shift, state_b_gla, cache_c_k, cache_c_v, P)
    return (y_prompt, y_sample, p_wkv, p_shift, p_gla, p_k, p_v, s_wkv, s_shift, s_gla, s_k, s_v)
```

```python
import functools

import jax
import jax.numpy as jnp
import numpy as np
from jax import lax
from jax.experimental import pallas as pl
from jax.experimental.pallas import tpu as pltpu

F32 = jnp.float32
BF16 = jnp.bfloat16

D_MODEL = 1024
DEPTH = 4
CHUNK = 64
A_WIDTH = 512
A_HEAD_DIM = 64
A_HEADS = 8
A_DECAY_LORA = 64
A_ICL_LORA = 64
A_GATE_LORA = 128
A_PROJ = 3 * A_WIDTH + A_DECAY_LORA + A_ICL_LORA + A_GATE_LORA
A_NORM_EPS = 64e-5
B_VWIDTH = 512
B_KWIDTH = 256
B_HEADS = 4
B_KEY_DIM = 64
B_VAL_DIM = 128
B_GATE_LORA = 16
B_GATE_NORM = 16.0
B_LORA_PAD = 128
B_PROJ_PAD = 2 * B_KWIDTH + 2 * B_VWIDTH + B_LORA_PAD
C_HEAD_DIM = 64
C_HEADS = 16
C_PAST_CHUNKS = 8
C_PAST_ROWS = C_PAST_CHUNKS * CHUNK
C_REL_CLIP = 128
FFN_HIDDEN = 2816
LN_EPS = 1e-5
DEEPNORM_ALPHA = (2.0 * DEPTH) ** 0.25
NEG_INF = -1e30

LANE = 128
PAIR = LANE
HALF = PAIR // 2
A_PAIRS = A_HEADS // 2
B_PAIRS = B_HEADS // 2
VMEM_LIMIT = 56 * 1024 * 1024

NN = (((1,), (0,)), ((), ()))
NT = (((1,), (1,)), ((), ()))
TN = (((0,), (0,)), ((), ()))


def _params(*sem):
    return pltpu.CompilerParams(dimension_semantics=sem, vmem_limit_bytes=VMEM_LIMIT)


def _dg(a, b, dn):
    return lax.dot_general(a, b, dn, preferred_element_type=F32)


def _dot1(a, b, dn=NN):
    return _dg(a.astype(BF16), b.astype(BF16), dn)


def _split(a):
    hi = a.astype(BF16)
    lo = (a - hi.astype(F32)).astype(BF16)
    return hi, lo


def _dot3s(a, b, dn=NN):
    return _dg(a[0], b[0], dn) + _dg(a[0], b[1], dn) + _dg(a[1], b[0], dn)


def _dot3(a, b, dn=NN):
    return _dot3s(_split(a), _split(b), dn)


def _dot_xl(a_exact, b, dn=NN):
    bh, bl = _split(b)
    return _dg(a_exact, bh, dn) + _dg(a_exact, bl, dn)


def _dot_xr(a, b_exact, dn=NN):
    ah, al = _split(a)
    return _dg(ah, b_exact, dn) + _dg(al, b_exact, dn)


def _ln(z, w, b):
    mu = jnp.mean(z, -1, keepdims=True)
    d = z - mu
    var = jnp.mean(d * d, -1, keepdims=True)
    return d * lax.rsqrt(var + LN_EPS) * w + b


def _sigmoid(x):
    return 1.0 / (1.0 + jnp.exp(-x))


def _softplus(x):
    return jnp.maximum(x, 0.0) + jnp.log(1.0 + jnp.exp(-jnp.abs(x)))


def _stack_masked(x, m0):
    return jnp.concatenate([jnp.where(m0, x, 0.0), jnp.where(m0, 0.0, x)], axis=0)


def _fold(x):
    n = x.shape[0] // 2
    return x[:n] + x[n:]


def _mm_kernel(x_ref, w_ref, o_ref):
    o_ref[...] = jnp.dot(x_ref[...].astype(BF16), w_ref[...], preferred_element_type=F32)


def _matmul(x, w, tm, tn):
    M, K = x.shape
    N = w.shape[1]
    tm = min(tm, M)
    return pl.pallas_call(
        _mm_kernel,
        grid=(M // tm, N // tn),
        in_specs=[pl.BlockSpec((tm, K), lambda i, j: (i, 0)),
                  pl.BlockSpec((K, tn), lambda i, j: (0, j))],
        out_specs=pl.BlockSpec((tm, tn), lambda i, j: (i, j)),
        out_shape=jax.ShapeDtypeStruct((M, N), F32),
        compiler_params=_params("parallel", "arbitrary"),
    )(x, w)


def _in_proj_kernel(x_ref, wa_ref, wb_ref, oa_ref, ob_ref):
    xb = x_ref[...].astype(BF16)
    oa_ref[...] = jnp.dot(xb, wa_ref[...], preferred_element_type=F32)
    ob_ref[...] = jnp.dot(xb, wb_ref[...], preferred_element_type=F32)


def _in_proj(x, wa, wb, e, tm):
    M, K = x.shape
    tm = min(tm, M)
    na, nb = wa.shape[2], wb.shape[2]
    resident = lambda n: pl.BlockSpec((None, K, n), lambda i: (e, 0, 0), pipeline_mode=pl.Buffered(1))
    return pl.pallas_call(
        _in_proj_kernel,
        grid=(M // tm,),
        in_specs=[pl.BlockSpec((tm, K), lambda i: (i, 0)), resident(na), resident(nb)],
        out_specs=[pl.BlockSpec((tm, na), lambda i: (i, 0)), pl.BlockSpec((tm, nb), lambda i: (i, 0))],
        out_shape=[jax.ShapeDtypeStruct((M, na), F32), jax.ShapeDtypeStruct((M, nb), F32)],
        compiler_params=_params("parallel"),
    )(x, wa, wb)


def _mix_ffn_kernel(n_pairs, *refs):
    res_ref, lw1_ref, lb1_ref, wg_ref, wu_ref, wo_ref, lw2_ref, lb2_ref, o_ref = refs[2 * n_pairs:]
    tm = o_ref.shape[0]
    n_half = 2 if tm % 32 == 0 else 1
    halves = [slice(i * (tm // n_half), (i + 1) * (tm // n_half)) for i in range(n_half)]
    mm = lambda a, w: jnp.dot(a, w, preferred_element_type=F32)
    acc = [mm(refs[0][h, :].astype(BF16), refs[1][...]) for h in halves]
    for p in range(1, n_pairs):
        acc = [acc[i] + mm(refs[2 * p][h, :].astype(BF16), refs[2 * p + 1][...]) for i, h in enumerate(halves)]
    x = [_ln(DEEPNORM_ALPHA * res_ref[h, :] + acc[i], lw1_ref[...], lb1_ref[...]) for i, h in enumerate(halves)]
    xb = [v.astype(BF16) for v in x]
    g = [mm(v, wg_ref[...]) for v in xb]
    u = [mm(v, wu_ref[...]) for v in xb]
    act = [(g[i] * _sigmoid(g[i]) * u[i]).astype(BF16) for i in range(n_half)]
    y = [mm(v, wo_ref[...]) for v in act]
    for i, h in enumerate(halves):
        o_ref[h, :] = _ln(DEEPNORM_ALPHA * x[i] + y[i], lw2_ref[...], lb2_ref[...])


def _mix_ffn(pairs, res, lw1, lb1, w_in, w_out, lw2, lb2, layer, tm):
    M = res.shape[0]
    tm = min(tm, M)
    resident = lambda shape, idx: pl.BlockSpec((None,) + shape, idx, pipeline_mode=pl.Buffered(1))
    vec = pl.BlockSpec((None, 1, D_MODEL), lambda i: (layer, 0, 0))
    in_specs, args = [], []
    for a, w, n in pairs:
        K = a.shape[1]
        in_specs += [pl.BlockSpec((tm, K), lambda i: (i, 0)), resident((K, D_MODEL), lambda i, n=n: (n, 0, 0))]
        args += [a, w]
    in_specs += [pl.BlockSpec((tm, D_MODEL), lambda i: (i, 0)), vec, vec,
                 resident((D_MODEL, FFN_HIDDEN), lambda i: (layer, 0, 0)),
                 resident((D_MODEL, FFN_HIDDEN), lambda i: (layer, 0, 1)),
                 resident((FFN_HIDDEN, D_MODEL), lambda i: (layer, 0, 0)), vec, vec]
    return pl.pallas_call(
        functools.partial(_mix_ffn_kernel, len(pairs)),
        grid=(M // tm,),
        in_specs=in_specs,
        out_specs=pl.BlockSpec((tm, D_MODEL), lambda i: (i, 0)),
        out_shape=jax.ShapeDtypeStruct((M, D_MODEL), F32),
        compiler_params=_params("parallel"),
    )(*args, res, lw1, lb1, w_in, w_in, w_out, lw2, lb2)


def _inv_unit_lower(ns, eye, nilpotency):
    size = eye.shape[0]
    xs = [eye - n for n in ns]
    ps = [_dot1(n, n) for n in ns]
    k = 2
    while k < nilpotency:
        k *= 2
        if k < nilpotency:
            rs = [_dot1(p, jnp.concatenate([x, p], axis=1)) for x, p in zip(xs, ps)]
            xs = [x + r[:, :size] for x, r in zip(xs, rs)]
            ps = [r[:, size:] for r in rs]
        else:
            xs = [x + _dot1(p, x) for x, p in zip(xs, ps)]
    return xs


def _segsum(x, seg):
    n = x.shape[1] // PAIR
    rows = x.shape[0]
    xs = jnp.concatenate([x[:, j * PAIR:(j + 1) * PAIR] for j in range(n)], axis=0)
    s = _dot_xr(xs, seg)
    return jnp.concatenate([s[j * rows:(j + 1) * rows] for j in range(n)], axis=1)


def _head_mask():
    hr = lax.broadcasted_iota(jnp.int32, (PAIR, PAIR), 0) // A_HEAD_DIM
    hc = lax.broadcasted_iota(jnp.int32, (PAIR, PAIR), 1) // A_HEAD_DIM
    return (hr == hc).astype(BF16)


def _rwkv_prep_kernel(n_chunks, pa_ref, prev_ref, shift_ref, mu_ref, w0_ref, w2_ref, a0_ref, a2_ref, g2_ref,
                      kk_ref, ka_ref, rk_ref,
                      mh_ref, ch_ref, qc_ref, oc_ref, g_ref, bonus_ref):
    i = pl.program_id(1)
    L = CHUNK
    pa = pa_ref[...]
    first = jnp.where(i == 0, shift_ref[...], prev_ref[7:8, :])
    rows = lax.broadcasted_iota(jnp.int32, pa.shape, 0)
    prev = jnp.where(rows == 0, first, pltpu.roll(pa, shift=1, axis=0))
    xs = pa + (prev - pa) * mu_ref[...]
    W = A_WIDTH
    r = xs[:, 0:W]
    k0 = xs[:, W:2 * W]
    v = xs[:, 2 * W:3 * W]
    xw = xs[:, 3 * W:3 * W + A_DECAY_LORA]
    xa = xs[:, 3 * W + A_DECAY_LORA:3 * W + A_DECAY_LORA + A_ICL_LORA]
    xg = xs[:, 3 * W + A_DECAY_LORA + A_ICL_LORA:]
    w = -_softplus(-(w0_ref[...] + _dot3(jnp.tanh(xw), w2_ref[...]))) - 0.5
    logw = -jnp.exp(w)
    a = _sigmoid(a0_ref[...] + _dot3(xa, a2_ref[...]))
    g_ref[...] = _dot3(_sigmoid(xg), g2_ref[...])
    seg = _head_mask()
    kk = k0 * kk_ref[...]
    kk = kk * lax.rsqrt(_segsum(kk * kk, seg) + 1e-12)
    k = k0 * (1.0 + (a - 1.0) * ka_ref[...])
    bonus_ref[...] = _segsum(r * k * rk_ref[...], seg) * v
    b = kk * a

    tri = (lax.broadcasted_iota(jnp.int32, (L, L), 0) >= lax.broadcasted_iota(jnp.int32, (L, L), 1)).astype(BF16)
    row = lax.broadcasted_iota(jnp.int32, (PAIR, PAIR), 0)
    col = lax.broadcasted_iota(jnp.int32, (PAIR, PAIR), 1)
    eye = (row == col).astype(F32)
    strict = row > col
    incl2 = (lax.broadcasted_iota(jnp.int32, (PAIR, 2 * PAIR), 0)
             >= lax.broadcasted_iota(jnp.int32, (PAIR, 2 * PAIR), 1) % PAIR)
    m0 = lax.broadcasted_iota(jnp.int32, (L, PAIR), 1) < HALF
    pairs = range(A_PAIRS)
    lanes = [slice(j * PAIR, (j + 1) * PAIR) for j in pairs]
    chains = [(c, j) for c in range(n_chunks) for j in pairs]
    every = range(len(chains))
    rows_of = [slice(c * L, (c + 1) * L) for c in range(n_chunks)]
    kkg, rg, kinv, binv, kd, bd, e_last = [], [], [], [], [], [], []
    for sl in rows_of:
        lw = logw[sl]
        cum = _dot_xl(tri, lw)
        c_last = cum[L - 1:L, :]
        e_neg = jnp.exp(-cum)
        e_rem = jnp.exp(c_last - cum)
        kkg.append(kk[sl] * jnp.exp(cum - lw))
        rg.append(r[sl] * jnp.exp(cum))
        kinv.append(k[sl] * e_neg)
        binv.append(b[sl] * e_neg)
        kd.append(k[sl] * e_rem)
        bd.append(b[sl] * e_rem)
        e_last.append(jnp.exp(c_last))
    sm = lambda xs_, c, j: _stack_masked(xs_[c][:, lanes[j]].astype(BF16), m0)
    kkg_b = [sm(kkg, c, j) for c, j in chains]
    v_b = [_stack_masked(v[rows_of[c], lanes[j]].astype(BF16), m0) for c, j in chains]
    bd_b = [sm(bd, c, j) for c, j in chains]
    kd_b = [sm(kd, c, j) for c, j in chains]
    rg_b = [sm(rg, c, j) for c, j in chains]
    kinv_b = [sm(kinv, c, j) for c, j in chains]
    binv_b = [sm(binv, c, j) for c, j in chains]
    aa = [_dg(jnp.concatenate([kkg_b[n], rg_b[n]], axis=0), jnp.concatenate([binv_b[n], kinv_b[n]], axis=0), NT)
          for n in every]
    a_kb = [jnp.where(strict, x[:PAIR, :PAIR], 0.0) for x in aa]
    a_kk = [jnp.where(strict, x[:PAIR, PAIR:], 0.0).astype(BF16) for x in aa]
    a_rbk = [jnp.where(incl2, x[PAIR:, :], 0.0).astype(BF16) for x in aa]
    t = _inv_unit_lower(a_kb, eye, L)
    av = [_dg(a_kk[n], v_b[n], NN).astype(BF16) for n in every]
    wu = [_dg(t[n].astype(BF16), jnp.concatenate([kkg_b[n], av[n]], axis=1), NN) for n in every]
    wu_b = [x.astype(BF16) for x in wu]
    mc = [_dg(bd_b[n], wu_b[n], TN) for n in every]
    kv = [_dg(kd_b[n], v_b[n], TN) for n in every]
    rhs = [jnp.concatenate([jnp.concatenate([wu_b[n][:, :PAIR], (-wu[n][:, PAIR:]).astype(BF16)], axis=1),
                            jnp.concatenate([jnp.zeros_like(v_b[n]), v_b[n]], axis=1)], axis=0) for n in every]
    qo = [_dg(a_rbk[n], rhs[n], NN) for n in every]
    for n, (c, j) in enumerate(chains):
        mh_ref[c, j] = (eye * e_last[c][:, lanes[j]] - mc[n][:, :PAIR]).astype(BF16)
        ch_ref[c, j] = kv[n] - mc[n][:, PAIR:]
        qc_ref[rows_of[c], lanes[j]] = (rg[c][:, lanes[j]] - _fold(qo[n][:, :PAIR])).astype(BF16)
        oc_ref[rows_of[c], lanes[j]] = _fold(qo[n][:, PAIR:])


def _rwkv_scan_kernel(n_seq, n_chunks, mh_ref, ch_ref, qc_ref, oc_ref, g_ref, bonus_ref, h0_ref, lnw_ref, lnb_ref,
                      y_ref, hout_ref, h_ref):
    i = pl.program_id(1)
    L = CHUNK

    @pl.when(i == 0)
    def _():
        h_ref[...] = h0_ref[...]

    seg = _head_mask()
    inv_n = 1.0 / A_HEAD_DIM
    lanes = [slice(j * PAIR, (j + 1) * PAIR) for j in range(A_PAIRS)]
    chains = [(s, j) for s in range(n_seq) for j in range(A_PAIRS)]
    hb = [h_ref[s, j] for s, j in chains]
    for c in range(n_chunks):
        sl = slice(c * L, (c + 1) * L)
        hb_b = [x.astype(BF16) for x in hb]
        o = [_dg(qc_ref[s, sl, lanes[j]], hb_b[n], NN) + oc_ref[s, sl, lanes[j]] for n, (s, j) in enumerate(chains)]
        hb = [_dg(mh_ref[s, c, j], hb_b[n], NN) + ch_ref[s, c, j] for n, (s, j) in enumerate(chains)]
        for s in range(n_seq):
            os_ = jnp.concatenate(o[s * A_PAIRS:(s + 1) * A_PAIRS], axis=1)
            mean = _segsum(os_, seg) * inv_n
            d = os_ - mean
            var = _segsum(d * d, seg) * inv_n
            on = d * lax.rsqrt(var + A_NORM_EPS) * lnw_ref[...] + lnb_ref[...]
            y_ref[s, sl, :] = ((on + bonus_ref[s, sl, :]) * g_ref[s, sl, :]).astype(y_ref.dtype)
    for n, (s, j) in enumerate(chains):
        h_ref[s, j] = hb[n]
    hout_ref[...] = h_ref[...]


def _rwkv_mix(pa, shift0, wkv0, prm):
    Bsz, T, _ = pa.shape
    n1 = next(n for n in (8, 4, 2, 1) if (T // CHUNK) % n == 0)
    R1 = n1 * CHUNK
    W = A_WIDTH
    nblk = T // R1
    row = lambda n: pl.BlockSpec((1, n), lambda b, i: (0, 0))
    full = lambda s: pl.BlockSpec(s, lambda b, i: (0,) * len(s))
    act = lambda r, n: pl.BlockSpec((None, r, n), lambda b, i: (b, i, 0))
    mat = lambda n: pl.BlockSpec((None, n, A_PAIRS, PAIR, PAIR), lambda b, i: (b, i, 0, 0, 0))
    sub = R1 // 8
    mats = lambda dt: jax.ShapeDtypeStruct((Bsz, T // CHUNK, A_PAIRS, PAIR, PAIR), dt)
    acts = lambda dt: jax.ShapeDtypeStruct((Bsz, T, W), dt)
    mh, ch, qc, oc, g, bonus = pl.pallas_call(
        functools.partial(_rwkv_prep_kernel, n1),
        grid=(Bsz, nblk),
        in_specs=[act(R1, A_PROJ),
                  pl.BlockSpec((None, 8, A_PROJ), lambda b, i: (b, jnp.maximum(i * sub - 1, 0), 0)),
                  pl.BlockSpec((None, 1, A_PROJ), lambda b, i: (b, 0, 0)),
                  row(A_PROJ), row(W), full((A_DECAY_LORA, W)), row(W), full((A_ICL_LORA, W)),
                  full((A_GATE_LORA, W)), row(W), row(W), row(W)],
        out_specs=[mat(n1), mat(n1), act(R1, W), act(R1, W), act(R1, W), act(R1, W)],
        out_shape=[mats(BF16), mats(F32), acts(BF16), acts(F32), acts(F32), acts(F32)],
        compiler_params=_params("parallel", "parallel"),
    )(pa, pa, shift0[:, None, :], prm['mu'], prm['w0'], prm['w2'], prm['a0'], prm['a2'], prm['g2'],
      prm['k_k'], prm['k_a'], prm['r_k'])

    n2 = min(4, T // CHUNK)
    R2 = n2 * CHUNK
    hh = wkv0.transpose(0, 1, 3, 2).reshape(Bsz, A_PAIRS, 2, A_HEAD_DIM, A_HEAD_DIM)
    zero = jnp.zeros_like(hh[:, :, 0])
    h0 = jnp.concatenate([jnp.concatenate([hh[:, :, 0], zero], axis=-1),
                          jnp.concatenate([zero, hh[:, :, 1]], axis=-1)], axis=-2)
    ns = next(n for n in (4, 2, 1) if Bsz % n == 0)
    state = pl.BlockSpec((ns, A_PAIRS, PAIR, PAIR), lambda b, i: (b, 0, 0, 0))
    act2 = lambda r, n: pl.BlockSpec((ns, r, n), lambda b, i: (b, i, 0))
    mat2 = lambda n: pl.BlockSpec((ns, n, A_PAIRS, PAIR, PAIR), lambda b, i: (b, i, 0, 0, 0))
    y, hout = pl.pallas_call(
        functools.partial(_rwkv_scan_kernel, ns, n2),
        grid=(Bsz // ns, T // R2),
        in_specs=[mat2(n2), mat2(n2), act2(R2, W), act2(R2, W), act2(R2, W), act2(R2, W), state,
                  row(W), row(W)],
        out_specs=[act2(R2, W), state],
        out_shape=[acts(BF16), jax.ShapeDtypeStruct((Bsz, A_PAIRS, PAIR, PAIR), F32)],
        scratch_shapes=[pltpu.VMEM((ns, A_PAIRS, PAIR, PAIR), F32)],
        compiler_params=_params("parallel", "arbitrary"),
    )(mh, ch, qc, oc, g, bonus, h0, prm['ln_w'], prm['ln_b'])
    hd = jnp.stack([hout[:, :, :HALF, :HALF], hout[:, :, HALF:, HALF:]], axis=2)
    wkv = hd.reshape(Bsz, A_HEADS, A_HEAD_DIM, A_HEAD_DIM).transpose(0, 1, 3, 2)
    return y, wkv


def _gla_kernel(n_seq, n_chunks, pb_ref, s0_ref, up_ref, bias_ref, nw_ref, y_ref, sout_ref, s_ref):
    i = pl.program_id(1)
    L = CHUNK

    @pl.when(i == 0)
    def _():
        s_ref[...] = s0_ref[...]

    kw, vw = B_KWIDTH, B_VWIDTH
    incl = lax.broadcasted_iota(jnp.int32, (L, L), 0) >= lax.broadcasted_iota(jnp.int32, (L, L), 1)
    tri = incl.astype(BF16)
    m0 = lax.broadcasted_iota(jnp.int32, (L, PAIR), 1) < HALF
    top = lax.broadcasted_iota(jnp.int32, (PAIR, B_VAL_DIM), 0) < HALF
    scale = B_KEY_DIM ** -0.5
    pairs = range(B_PAIRS)
    heads = range(B_HEADS)
    pl_ = lambda h: slice((h // 2) * PAIR, (h // 2 + 1) * PAIR)
    units = [(s, c) for c in range(n_chunks) for s in range(n_seq)]
    rows_of = lambda c: slice(c * L, (c + 1) * L)
    vslice = lambda h: slice(h * B_VAL_DIM, (h + 1) * B_VAL_DIM)
    z = [_dot3(pb_ref[s, rows_of(c), 2 * kw + 2 * vw:], up_ref[...]) + bias_ref[...] for s, c in units]
    bc = [_dot_xl(tri, -_softplus(-x) * (1.0 / B_GATE_NORM)) for x in z]
    b_last = [x[L - 1:L, :] for x in bc]
    q_dec, k_inv, k_state, e_last_t = [], [], [], []
    for n, (s, c) in enumerate(units):
        k = pb_ref[s, rows_of(c), kw:2 * kw]
        q_dec.append(pb_ref[s, rows_of(c), 0:kw] * scale * jnp.exp(bc[n]))
        k_inv.append((k * jnp.exp(-bc[n])).astype(BF16))
        k_state.append((k * jnp.exp(b_last[n] - bc[n])).astype(BF16))
        e_last_t.append(jnp.broadcast_to(jnp.exp(b_last[n]), (LANE, kw)).T)
    uh = [(n, h) for n in range(len(units)) for h in heads]
    vb = [pb_ref[units[n][0], rows_of(units[n][1]), 2 * kw + h * B_VAL_DIM:2 * kw + (h + 1) * B_VAL_DIM].astype(BF16)
          for n, h in uh]
    qm = [jnp.where(m0 if h % 2 == 0 else ~m0, q_dec[n][:, pl_(h)], 0.0).astype(BF16) for n, h in uh]
    scores = [jnp.where(incl, _dg(qm[i_], k_inv[n][:, pl_(h)], NT), 0.0).astype(BF16) for i_, (n, h) in enumerate(uh)]
    upd = [_dg(k_state[n][:, pl_(h)], vb[i_], TN) for i_, (n, h) in enumerate(uh)]
    intra = [_dg(scores[i_], vb[i_], NN) for i_ in range(len(uh))]
    states = [[s_ref[s, j] for j in pairs] for s in range(n_seq)]
    before = []
    for n, (s, c) in enumerate(units):
        before.append([x.astype(BF16) for x in states[s]])
        states[s] = [states[s][j] * e_last_t[n][j * PAIR:(j + 1) * PAIR, :]
                     + jnp.where(top, upd[n * B_HEADS + 2 * j], upd[n * B_HEADS + 2 * j + 1]) for j in pairs]
    o = [_dg(qm[i_], before[n][h // 2], NN) + intra[i_] for i_, (n, h) in enumerate(uh)]
    for i_, (n, h) in enumerate(uh):
        s, c = units[n]
        oh = o[i_] * lax.rsqrt(jnp.mean(o[i_] * o[i_], -1, keepdims=True) + LN_EPS) * nw_ref[...]
        rg_h = pb_ref[s, rows_of(c), 2 * kw + vw + h * B_VAL_DIM:2 * kw + vw + (h + 1) * B_VAL_DIM]
        y_ref[s, rows_of(c), vslice(h)] = (oh * (rg_h * _sigmoid(rg_h))).astype(y_ref.dtype)
    for s in range(n_seq):
        for j in pairs:
            s_ref[s, j] = states[s][j]
    sout_ref[...] = s_ref[...]


def _gla_mix(pb, s0, prm):
    Bsz, T, _ = pb.shape
    n = min(4, T // CHUNK)
    Rr = n * CHUNK
    ns = next(n for n in (4, 2, 1) if Bsz % n == 0)
    st = pl.BlockSpec((ns, B_PAIRS, PAIR, B_VAL_DIM), lambda b, i: (b, 0, 0, 0))
    y, s = pl.pallas_call(
        functools.partial(_gla_kernel, ns, n),
        grid=(Bsz // ns, T // Rr),
        in_specs=[pl.BlockSpec((ns, Rr, B_PROJ_PAD), lambda b, i: (b, i, 0)), st,
                  pl.BlockSpec((B_LORA_PAD, B_KWIDTH), lambda b, i: (0, 0)),
                  pl.BlockSpec((1, B_KWIDTH), lambda b, i: (0, 0)),
                  pl.BlockSpec((1, B_VAL_DIM), lambda b, i: (0, 0))],
        out_specs=[pl.BlockSpec((ns, Rr, B_VWIDTH), lambda b, i: (b, i, 0)), st],
        out_shape=[jax.ShapeDtypeStruct((Bsz, T, B_VWIDTH), BF16),
                   jax.ShapeDtypeStruct((Bsz, B_PAIRS, PAIR, B_VAL_DIM), F32)],
        scratch_shapes=[pltpu.VMEM((ns, B_PAIRS, PAIR, B_VAL_DIM), F32)],
        compiler_params=_params("parallel", "arbitrary"),
    )(pb, s0.reshape(Bsz, B_PAIRS, PAIR, B_VAL_DIM), prm['up'], prm['bias'], prm['nw'])
    return y, s.reshape(Bsz, B_HEADS, B_KEY_DIM, B_VAL_DIM)


ATT_QB = 2 * CHUNK
ATT_SCALE = C_HEAD_DIM ** -0.5
LOG2E = 1.4426950408889634


ATT_WIN = C_PAST_ROWS + ATT_QB
ATT_MAX_QB = 8
ATT_PAD = ATT_MAX_QB * ATT_QB


def _qkv_pad_kernel(n_pad, x_ref, w_ref, o_ref):
    r = pl.program_id(1)

    @pl.when(r < n_pad)
    def _():
        o_ref[...] = jnp.zeros_like(o_ref)

    @pl.when(r >= n_pad)
    def _():
        y = jnp.dot(x_ref[...].astype(BF16), w_ref[...], preferred_element_type=F32).astype(BF16)
        qk = 2 * D_MODEL
        o_ref[:, :qk] = y[:, :qk]
        ones = jnp.ones((y.shape[0], PAIR), BF16)
        for hp in range(C_HEADS // 2):
            o_ref[:, qk + 2 * hp * PAIR:qk + (2 * hp + 1) * PAIR] = y[:, qk + hp * PAIR:qk + (hp + 1) * PAIR]
            o_ref[:, qk + (2 * hp + 1) * PAIR:qk + (2 * hp + 2) * PAIR] = ones


def _qkv_padded(x, w, o):
    Bsz, T, D = x.shape
    N = w.shape[2]
    n_out = N + D
    tm = next(t for t in (512, 256, 128) if T % t == 0)
    n_pad = ATT_PAD // tm
    return pl.pallas_call(
        functools.partial(_qkv_pad_kernel, n_pad),
        grid=(Bsz, T // tm + n_pad),
        in_specs=[pl.BlockSpec((None, tm, D), lambda b, r: (b, jnp.maximum(r - n_pad, 0), 0)),
                  pl.BlockSpec((None, D, N), lambda b, r: (o, 0, 0), pipeline_mode=pl.Buffered(1))],
        out_specs=pl.BlockSpec((None, tm, n_out), lambda b, r: (b, r, 0)),
        out_shape=jax.ShapeDtypeStruct((Bsz, ATT_PAD + T, n_out), BF16),
        compiler_params=_params("parallel", "arbitrary"),
    )(x, w)


def _kv_tail(x, w, keep):
    Bsz, T, D = x.shape
    tm = next(t for t in (512, 256, 128, 64) if keep % t == 0 and T % t == 0)
    first = (T - keep) // tm
    return pl.pallas_call(
        _mm_kernel,
        grid=(Bsz, keep // tm, 2),
        in_specs=[pl.BlockSpec((None, tm, D), lambda b, r, j: (b, first + r, 0)),
                  pl.BlockSpec((D, D), lambda b, r, j: (0, j + 1))],
        out_specs=pl.BlockSpec((None, tm, D), lambda b, r, j: (b, r, j)),
        out_shape=jax.ShapeDtypeStruct((Bsz, keep, 2 * D), F32),
        compiler_params=_params("parallel", "parallel", "arbitrary"),
    )(x, w)


def _att_prompt_kernel(n_qb, q_ref, k_ref, v_ref, bias_ref, o_ref):
    g = pl.program_id(2)
    m0 = lax.broadcasted_iota(jnp.int32, (ATT_QB, PAIR), 1) < HALF
    col = lax.broadcasted_iota(jnp.int32, (2 * ATT_QB, ATT_WIN), 1)

    def run(window_has_padding):
        blocks = range(n_qb)
        rows = [slice(t * ATT_QB, (t + 1) * ATT_QB) for t in blocks]
        wins = [pl.ds(pl.multiple_of(ATT_PAD - C_PAST_ROWS + (g * n_qb + t) * ATT_QB, ATT_QB), ATT_WIN)
                for t in blocks]
        q = [_stack_masked(q_ref[rows[t], :], m0) for t in blocks]
        s = [_dg(q[t], k_ref[wins[t], :], NT) + bias_ref[...] for t in blocks]
        if window_has_padding:
            s = [jnp.where(col >= C_PAST_ROWS - (g * n_qb + t) * ATT_QB, s[t], NEG_INF) for t in blocks]
        p = [jnp.exp2(x - x.max(-1, keepdims=True)) for x in s]
        pv = [_dg(p[t].astype(BF16), v_ref[wins[t], :], NN) for t in blocks]
        o = [x[:, :PAIR] / x[:, PAIR:] for x in pv]
        for t in blocks:
            o_ref[rows[t], :] = jnp.where(m0, o[t][:ATT_QB], o[t][ATT_QB:]).astype(o_ref.dtype)

    first = g * (n_qb * ATT_QB) < C_PAST_ROWS
    pl.when(first)(lambda: run(True))
    pl.when(jnp.logical_not(first))(lambda: run(False))


def _attention_prompt(qkv, bias):
    Bsz, Tp, _ = qkv.shape
    T = Tp - ATT_PAD
    nhp = C_HEADS // 2
    nblk = T // ATT_QB
    n_qb = next(n for n in (ATT_MAX_QB, 4, 2, 1) if nblk % n == 0)
    rows = n_qb * ATT_QB
    skip = ATT_PAD // rows
    return pl.pallas_call(
        functools.partial(_att_prompt_kernel, n_qb),
        grid=(Bsz, nhp, T // rows),
        in_specs=[pl.BlockSpec((None, rows, LANE), lambda b, hp, g: (b, g + skip, hp)),
                  pl.BlockSpec((None, Tp, LANE), lambda b, hp, g: (b, 0, nhp + hp)),
                  pl.BlockSpec((None, Tp, 2 * LANE), lambda b, hp, g: (b, 0, nhp + hp)),
                  pl.BlockSpec((None, 2 * ATT_QB, ATT_WIN), lambda b, hp, g: (hp, 0, 0))],
        out_specs=pl.BlockSpec((None, rows, LANE), lambda b, hp, g: (b, g, hp)),
        out_shape=jax.ShapeDtypeStruct((Bsz, T, D_MODEL), BF16),
        compiler_params=_params("parallel", "parallel", "arbitrary"),
    )(qkv, qkv, qkv, bias)


def _att_sample_kernel(qkv_ref, kc_ref, vc_ref, bias_ref, o_ref):
    R = kc_ref.shape[1]
    heads = range(C_HEADS)
    col = lambda part, h: slice(part * D_MODEL + h * C_HEAD_DIM, part * D_MODEL + (h + 1) * C_HEAD_DIM)
    q = [(qkv_ref[:, col(0, h)] * ATT_SCALE).astype(BF16) for h in heads]
    s_c = [_dg(q[h], kc_ref[h].astype(BF16), NT) + bias_ref[h, :, 0:R] for h in heads]
    s_n = [_dg(q[h], qkv_ref[:, col(1, h)].astype(BF16), NT) + bias_ref[h, :, R:] for h in heads]
    m = [jnp.maximum(s_c[h].max(-1, keepdims=True), s_n[h].max(-1, keepdims=True)) for h in heads]
    p_c = [jnp.exp(s_c[h] - m[h]) for h in heads]
    p_n = [jnp.exp(s_n[h] - m[h]) for h in heads]
    acc = [_dg(p_c[h].astype(BF16), vc_ref[h].astype(BF16), NN)
           + _dg(p_n[h].astype(BF16), qkv_ref[:, col(2, h)].astype(BF16), NN) for h in heads]
    for h in heads:
        den = p_c[h].sum(-1, keepdims=True) + p_n[h].sum(-1, keepdims=True)
        o_ref[:, col(0, h)] = (acc[h] / den).astype(o_ref.dtype)


def _attention_sample(qkv, cache_k, cache_v, bias, o):
    Bsz, T, _ = qkv.shape
    R = cache_k.shape[3]
    cache = pl.BlockSpec((None, None, C_HEADS, R, C_HEAD_DIM), lambda b: (o, b, 0, 0, 0))
    return pl.pallas_call(
        _att_sample_kernel,
        grid=(Bsz,),
        in_specs=[pl.BlockSpec((None, T, 3 * D_MODEL), lambda b: (b, 0, 0)), cache, cache,
                  pl.BlockSpec((C_HEADS, T, R + T), lambda b: (0, 0, 0))],
        out_specs=pl.BlockSpec((None, T, D_MODEL), lambda b: (b, 0, 0)),
        out_shape=jax.ShapeDtypeStruct((Bsz, T, D_MODEL), BF16),
        compiler_params=_params("parallel"),
    )(qkv, cache_k, cache_v, bias)


def _rel_bias_tile(table, n_q, n_k, q_offset):
    period = n_q + n_k - 1
    m = np.arange(period)
    d = np.where(m < n_k, m, m - period)
    idx = np.clip(q_offset - d, -C_REL_CLIP, C_REL_CLIP) + C_REL_CLIP
    diag = table[:, idx]
    flat = jnp.tile(diag, (1, n_q))[:, :n_q * (period - 1)]
    return flat.reshape(table.shape[0], n_q, period - 1)[:, :, :n_k]


TM = 512


def _trunk(x, wkv0, shift0, gla0, cache_k, cache_v, P):
    Bsz, T, _ = x.shape
    M = Bsz * T
    xf = x.reshape(M, D_MODEL)
    wkv_o, shift_o, gla_o, k_o, v_o = [], [], [], [], []
    for layer in range(DEPTH):
        if layer % 2 == 0:
            e = layer // 2
            pa, pb = _in_proj(xf, P['w_in_a'], P['w_in_b'], e, TM)
            pa = pa.reshape(Bsz, T, A_PROJ)
            pb = pb.reshape(Bsz, T, B_PROJ_PAD)
            ya, wkv = _rwkv_mix(pa, shift0[e], wkv0[e], {k: v[e] for k, v in P['a'].items()})
            yb, gs = _gla_mix(pb, gla0[e], {k: v[e] for k, v in P['b'].items()})
            wkv_o.append(wkv)
            shift_o.append(pa[:, -1])
            gla_o.append(gs)
            pairs = [(ya.reshape(M, A_WIDTH), P['w_out_a'], e), (yb.reshape(M, B_VWIDTH), P['w_out_b'], e)]
        else:
            o = layer // 2
            if cache_k is None:
                x3 = xf.reshape(Bsz, T, D_MODEL)
                att = _attention_prompt(_qkv_padded(x3, P['c_w_qkv_scaled'], o), P['c_bias_prompt'][o])
                keep = min(C_PAST_ROWS, T)
                rows = _kv_tail(x3, P['c_w_qkv'][o], keep).reshape(Bsz, keep, 2, C_HEADS, C_HEAD_DIM)
            else:
                qkv = _matmul(xf, P['c_w_qkv'][o], TM, D_MODEL).reshape(Bsz, T, 3 * D_MODEL)
                att = _attention_sample(qkv, cache_k, cache_v, P['c_bias_sample'][o], o)
                rows = qkv.reshape(Bsz, T, 3, C_HEADS, C_HEAD_DIM)[:, :, 1:]
            k_o.append(rows[:, :, 0].transpose(0, 2, 1, 3))
            v_o.append(rows[:, :, 1].transpose(0, 2, 1, 3))
            pairs = [(att.reshape(M, D_MODEL), P['c_w_o'], o)]
        xf = _mix_ffn(pairs, xf, P['ln1_w'], P['ln1_b'], P['ffn_w_in'], P['ffn_w_out'], P['ln2_w'], P['ln2_b'],
                      layer, TM)
    return (xf.reshape(Bsz, T, D_MODEL), jnp.stack(wkv_o), jnp.stack(shift_o), jnp.stack(gla_o),
            jnp.stack(k_o), jnp.stack(v_o))


def _prepare(w_in_mix, a_mu, a_w0, a_w2, a_a0, a_a2, a_g2, a_k_k, a_k_a, a_r_k, a_ln_w, a_ln_b,
             b_alpha_up, b_alpha_bias, b_norm_w, w_out_mix, c_w_qkv, c_rel_bias, c_w_o,
             ln1_w, ln1_b, ln2_w, ln2_b, ffn_w_in, ffn_w_out, sample_len, cache_rows):
    kw, vw = B_KWIDTH, B_VWIDTH
    wb = w_in_mix[:, :, A_PROJ:]
    main = jnp.concatenate([wb[:, :, :2 * kw + vw], wb[:, :, 2 * kw + vw + B_GATE_LORA:]], axis=-1)
    lora = jnp.pad(wb[:, :, 2 * kw + vw:2 * kw + vw + B_GATE_LORA], ((0, 0), (0, 0), (0, B_LORA_PAD - B_GATE_LORA)))
    row3 = lambda t: t.reshape(t.shape[0], 1, -1)
    bias_prompt = jnp.stack([_rel_bias_tile(t, ATT_QB, ATT_WIN, C_PAST_ROWS) for t in c_rel_bias])
    rel = np.arange(ATT_WIN)[None, :] // CHUNK - np.arange(ATT_QB)[:, None] // CHUNK
    band = (rel >= 0) & (rel <= C_PAST_CHUNKS)
    bias_prompt = jnp.where(band, bias_prompt * LOG2E, NEG_INF)
    q_scale = jnp.concatenate([jnp.full((D_MODEL,), ATT_SCALE * LOG2E, F32), jnp.ones((2 * D_MODEL,), F32)])
    P = {
        'w_in_a': w_in_mix[:, :, :A_PROJ].astype(BF16),
        'w_in_b': jnp.concatenate([main, lora], axis=-1).astype(BF16),
        'a': dict(mu=row3(a_mu), w0=row3(a_w0), w2=a_w2, a0=row3(a_a0), a2=a_a2, g2=a_g2, k_k=row3(a_k_k),
                  k_a=row3(a_k_a), r_k=row3(a_r_k), ln_w=row3(a_ln_w), ln_b=row3(a_ln_b)),
        'b': dict(up=jnp.pad(b_alpha_up, ((0, 0), (0, B_LORA_PAD - B_GATE_LORA), (0, 0))),
                  bias=row3(b_alpha_bias), nw=row3(b_norm_w)),
        'w_out_a': w_out_mix[:, :A_WIDTH].astype(BF16),
        'w_out_b': w_out_mix[:, A_WIDTH:].astype(BF16),
        'c_w_qkv': c_w_qkv.astype(BF16),
        'c_w_qkv_scaled': (c_w_qkv * q_scale).astype(BF16),
        'c_w_o': c_w_o.astype(BF16),
        'c_bias_prompt': bias_prompt.reshape(bias_prompt.shape[0], C_HEADS // 2, 2 * ATT_QB, ATT_WIN),
        'c_bias_sample': jnp.stack([_rel_bias_tile(t, sample_len, cache_rows + sample_len, cache_rows)
                                    for t in c_rel_bias]),
        'ln1_w': row3(ln1_w), 'ln1_b': row3(ln1_b), 'ln2_w': row3(ln2_w), 'ln2_b': row3(ln2_b),
        'ffn_w_in': ffn_w_in.astype(BF16),
        'ffn_w_out': ffn_w_out.astype(BF16),
    }
    return P


def kernel(x_prompt, x_sample, state_a_wkv, state_a_shift, state_b_gla, cache_c_k, cache_c_v, w_in_mix, a_mu, a_w0, a_w2, a_a0, a_a2, a_g2, a_k_k, a_k_a, a_r_k, a_ln_w, a_ln_b, b_alpha_up, b_alpha_bias, b_norm_w, w_out_mix, c_w_qkv, c_rel_bias, c_w_o, ln1_w, ln1_b, ln2_w, ln2_b, ffn_w_in, ffn_w_out):
    P = _prepare(w_in_mix, a_mu, a_w0, a_w2, a_a0, a_a2, a_g2, a_k_k, a_k_a, a_r_k, a_ln_w, a_ln_b,
                 b_alpha_up, b_alpha_bias, b_norm_w, w_out_mix, c_w_qkv, c_rel_bias, c_w_o,
                 ln1_w, ln1_b, ln2_w, ln2_b, ffn_w_in, ffn_w_out, x_sample.shape[1], cache_c_k.shape[3])
    bp = x_prompt.shape[0]
    dt = x_prompt.dtype
    n_even = state_a_wkv.shape[0]
    wkv_zero = jnp.zeros((n_even, bp, A_HEADS, A_HEAD_DIM, A_HEAD_DIM), dt)
    shift_zero = jnp.zeros((n_even, bp, A_PROJ), dt)
    gla_zero = jnp.zeros((n_even, bp, B_HEADS, B_KEY_DIM, B_VAL_DIM), dt)
    y_prompt, p_wkv, p_shift, p_gla, p_k, p_v = _trunk(x_prompt, wkv_zero, shift_zero, gla_zero, None, None, P)
    y_sample, s_wkv, s_shift, s_gla, s_k, s_v = _trunk(
        x_sample, state_a_wkv, state_a_shift, state_b_gla, cache_c_k, cache_c_v, P)
    return (y_prompt, y_sample, p_wkv, p_shift, p_gla, p_k, p_v, s_wkv, s_shift, s_gla, s_k, s_v)
```

```python
import functools

import jax
import jax.numpy as jnp
import numpy as np
from jax import lax
from jax.experimental import pallas as pl
from jax.experimental.pallas import tpu as pltpu

F32 = jnp.float32
BF16 = jnp.bfloat16

D_MODEL = 1024
DEPTH = 4
CHUNK = 64
A_WIDTH = 512
A_HEAD_DIM = 64
A_HEADS = 8
A_DECAY_LORA = 64
A_ICL_LORA = 64
A_GATE_LORA = 128
A_PROJ = 3 * A_WIDTH + A_DECAY_LORA + A_ICL_LORA + A_GATE_LORA
A_NORM_EPS = 64e-5
B_VWIDTH = 512
B_KWIDTH = 256
B_HEADS = 4
B_KEY_DIM = 64
B_VAL_DIM = 128
B_GATE_LORA = 16
B_GATE_NORM = 16.0
B_LORA_PAD = 128
B_PROJ_PAD = 2 * B_KWIDTH + 2 * B_VWIDTH + B_LORA_PAD
C_HEAD_DIM = 64
C_HEADS = 16
C_PAST_CHUNKS = 8
C_PAST_ROWS = C_PAST_CHUNKS * CHUNK
C_REL_CLIP = 128
FFN_HIDDEN = 2816
LN_EPS = 1e-5
DEEPNORM_ALPHA = (2.0 * DEPTH) ** 0.25
NEG_INF = -1e30

LANE = 128
PAIR = LANE
HALF = PAIR // 2
A_PAIRS = A_HEADS // 2
B_PAIRS = B_HEADS // 2
VMEM_LIMIT = 56 * 1024 * 1024

NN = (((1,), (0,)), ((), ()))
NT = (((1,), (1,)), ((), ()))
TN = (((0,), (0,)), ((), ()))


def _params(*sem):
    return pltpu.CompilerParams(dimension_semantics=sem, vmem_limit_bytes=VMEM_LIMIT)


def _dg(a, b, dn):
    return lax.dot_general(a, b, dn, preferred_element_type=F32)


def _dot1(a, b, dn=NN):
    return _dg(a.astype(BF16), b.astype(BF16), dn)


def _split(a):
    hi = a.astype(BF16)
    lo = (a - hi.astype(F32)).astype(BF16)
    return hi, lo


def _dot3s(a, b, dn=NN):
    return _dg(a[0], b[0], dn) + _dg(a[0], b[1], dn) + _dg(a[1], b[0], dn)


def _dot3(a, b, dn=NN):
    return _dot3s(_split(a), _split(b), dn)


def _dot_xl(a_exact, b, dn=NN):
    bh, bl = _split(b)
    return _dg(a_exact, bh, dn) + _dg(a_exact, bl, dn)


def _dot_xr(a, b_exact, dn=NN):
    ah, al = _split(a)
    return _dg(ah, b_exact, dn) + _dg(al, b_exact, dn)


def _ln(z, w, b):
    mu = jnp.mean(z, -1, keepdims=True)
    d = z - mu
    var = jnp.mean(d * d, -1, keepdims=True)
    return d * lax.rsqrt(var + LN_EPS) * w + b


def _sigmoid(x):
    return 1.0 / (1.0 + jnp.exp(-x))


def _softplus(x):
    return jnp.maximum(x, 0.0) + jnp.log(1.0 + jnp.exp(-jnp.abs(x)))


def _stack_masked(x, m0):
    return jnp.concatenate([jnp.where(m0, x, 0.0), jnp.where(m0, 0.0, x)], axis=0)


def _fold(x):
    n = x.shape[0] // 2
    return x[:n] + x[n:]


def _mm_kernel(x_ref, w_ref, o_ref):
    o_ref[...] = jnp.dot(x_ref[...].astype(BF16), w_ref[...], preferred_element_type=F32)


def _matmul(x, w, tm, tn):
    M, K = x.shape
    N = w.shape[1]
    tm = min(tm, M)
    return pl.pallas_call(
        _mm_kernel,
        grid=(M // tm, N // tn),
        in_specs=[pl.BlockSpec((tm, K), lambda i, j: (i, 0)),
                  pl.BlockSpec((K, tn), lambda i, j: (0, j))],
        out_specs=pl.BlockSpec((tm, tn), lambda i, j: (i, j)),
        out_shape=jax.ShapeDtypeStruct((M, N), F32),
        compiler_params=_params("parallel", "arbitrary"),
    )(x, w)


def _in_proj_kernel(x_ref, wa_ref, wb_ref, oa_ref, ob_ref):
    xb = x_ref[...].astype(BF16)
    oa_ref[...] = jnp.dot(xb, wa_ref[...], preferred_element_type=F32)
    ob_ref[...] = jnp.dot(xb, wb_ref[...], preferred_element_type=F32)


def _in_proj(x, wa, wb, e, tm):
    M, K = x.shape
    tm = min(tm, M)
    na, nb = wa.shape[2], wb.shape[2]
    resident = lambda n: pl.BlockSpec((None, K, n), lambda i: (e, 0, 0), pipeline_mode=pl.Buffered(1))
    return pl.pallas_call(
        _in_proj_kernel,
        grid=(M // tm,),
        in_specs=[pl.BlockSpec((tm, K), lambda i: (i, 0)), resident(na), resident(nb)],
        out_specs=[pl.BlockSpec((tm, na), lambda i: (i, 0)), pl.BlockSpec((tm, nb), lambda i: (i, 0))],
        out_shape=[jax.ShapeDtypeStruct((M, na), F32), jax.ShapeDtypeStruct((M, nb), F32)],
        compiler_params=_params("parallel"),
    )(x, wa, wb)


def _mix_ffn_kernel(n_pairs, *refs):
    res_ref, lw1_ref, lb1_ref, wg_ref, wu_ref, wo_ref, lw2_ref, lb2_ref, o_ref = refs[2 * n_pairs:]
    tm = o_ref.shape[0]
    n_half = 2 if tm % 32 == 0 else 1
    halves = [slice(i * (tm // n_half), (i + 1) * (tm // n_half)) for i in range(n_half)]
    mm = lambda a, w: jnp.dot(a, w, preferred_element_type=F32)
    acc = [mm(refs[0][h, :].astype(BF16), refs[1][...]) for h in halves]
    for p in range(1, n_pairs):
        acc = [acc[i] + mm(refs[2 * p][h, :].astype(BF16), refs[2 * p + 1][...]) for i, h in enumerate(halves)]
    x = [_ln(DEEPNORM_ALPHA * res_ref[h, :] + acc[i], lw1_ref[...], lb1_ref[...]) for i, h in enumerate(halves)]
    xb = [v.astype(BF16) for v in x]
    g = [mm(v, wg_ref[...]) for v in xb]
    u = [mm(v, wu_ref[...]) for v in xb]
    act = [(g[i] * _sigmoid(g[i]) * u[i]).astype(BF16) for i in range(n_half)]
    y = [mm(v, wo_ref[...]) for v in act]
    for i, h in enumerate(halves):
        o_ref[h, :] = _ln(DEEPNORM_ALPHA * x[i] + y[i], lw2_ref[...], lb2_ref[...])


def _mix_ffn(pairs, res, lw1, lb1, w_in, w_out, lw2, lb2, layer, tm):
    M = res.shape[0]
    tm = min(tm, M)
    resident = lambda shape, idx: pl.BlockSpec((None,) + shape, idx, pipeline_mode=pl.Buffered(1))
    vec = pl.BlockSpec((None, 1, D_MODEL), lambda i: (layer, 0, 0))
    in_specs, args = [], []
    for a, w, n in pairs:
        K = a.shape[1]
        in_specs += [pl.BlockSpec((tm, K), lambda i: (i, 0)), resident((K, D_MODEL), lambda i, n=n: (n, 0, 0))]
        args += [a, w]
    in_specs += [pl.BlockSpec((tm, D_MODEL), lambda i: (i, 0)), vec, vec,
                 resident((D_MODEL, FFN_HIDDEN), lambda i: (layer, 0, 0)),
                 resident((D_MODEL, FFN_HIDDEN), lambda i: (layer, 0, 1)),
                 resident((FFN_HIDDEN, D_MODEL), lambda i: (layer, 0, 0)), vec, vec]
    return pl.pallas_call(
        functools.partial(_mix_ffn_kernel, len(pairs)),
        grid=(M // tm,),
        in_specs=in_specs,
        out_specs=pl.BlockSpec((tm, D_MODEL), lambda i: (i, 0)),
        out_shape=jax.ShapeDtypeStruct((M, D_MODEL), F32),
        compiler_params=_params("parallel"),
    )(*args, res, lw1, lb1, w_in, w_in, w_out, lw2, lb2)


def _inv_unit_lower(ns, eye, nilpotency):
    size = eye.shape[0]
    xs = [eye - n for n in ns]
    ps = [_dot1(n, n) for n in ns]
    k = 2
    while k < nilpotency:
        k *= 2
        if k < nilpotency:
            rs = [_dot1(p, jnp.concatenate([x, p], axis=1)) for x, p in zip(xs, ps)]
            xs = [x + r[:, :size] for x, r in zip(xs, rs)]
            ps = [r[:, size:] for r in rs]
        else:
            xs = [x + _dot1(p, x) for x, p in zip(xs, ps)]
    return xs


def _segsum(x, seg):
    n = x.shape[1] // PAIR
    rows = x.shape[0]
    xs = jnp.concatenate([x[:, j * PAIR:(j + 1) * PAIR] for j in range(n)], axis=0)
    s = _dot_xr(xs, seg)
    return jnp.concatenate([s[j * rows:(j + 1) * rows] for j in range(n)], axis=1)


def _head_mask():
    hr = lax.broadcasted_iota(jnp.int32, (PAIR, PAIR), 0) // A_HEAD_DIM
    hc = lax.broadcasted_iota(jnp.int32, (PAIR, PAIR), 1) // A_HEAD_DIM
    return (hr == hc).astype(BF16)


def _rwkv_prep_kernel(n_chunks, pa_ref, prev_ref, shift_ref, mu_ref, w0_ref, w2_ref, a0_ref, a2_ref, g2_ref,
                      kk_ref, ka_ref, rk_ref,
                      mh_ref, ch_ref, qc_ref, oc_ref, g_ref, bonus_ref):
    i = pl.program_id(1)
    L = CHUNK
    pa = pa_ref[...]
    first = jnp.where(i == 0, shift_ref[...], prev_ref[7:8, :])
    rows = lax.broadcasted_iota(jnp.int32, pa.shape, 0)
    prev = jnp.where(rows == 0, first, pltpu.roll(pa, shift=1, axis=0))
    xs = pa + (prev - pa) * mu_ref[...]
    W = A_WIDTH
    r = xs[:, 0:W]
    k0 = xs[:, W:2 * W]
    v = xs[:, 2 * W:3 * W]
    xw = xs[:, 3 * W:3 * W + A_DECAY_LORA]
    xa = xs[:, 3 * W + A_DECAY_LORA:3 * W + A_DECAY_LORA + A_ICL_LORA]
    xg = xs[:, 3 * W + A_DECAY_LORA + A_ICL_LORA:]
    w = -_softplus(-(w0_ref[...] + _dot3(jnp.tanh(xw), w2_ref[...]))) - 0.5
    logw = -jnp.exp(w)
    a = _sigmoid(a0_ref[...] + _dot3(xa, a2_ref[...]))
    g_ref[...] = _dot3(_sigmoid(xg), g2_ref[...])
    seg = _head_mask()
    kk = k0 * kk_ref[...]
    kk = kk * lax.rsqrt(_segsum(kk * kk, seg) + 1e-12)
    k = k0 * (1.0 + (a - 1.0) * ka_ref[...])
    bonus_ref[...] = _segsum(r * k * rk_ref[...], seg) * v
    b = kk * a

    tri = (lax.broadcasted_iota(jnp.int32, (L, L), 0) >= lax.broadcasted_iota(jnp.int32, (L, L), 1)).astype(BF16)
    row = lax.broadcasted_iota(jnp.int32, (PAIR, PAIR), 0)
    col = lax.broadcasted_iota(jnp.int32, (PAIR, PAIR), 1)
    eye = (row == col).astype(F32)
    strict = row > col
    incl2 = (lax.broadcasted_iota(jnp.int32, (PAIR, 2 * PAIR), 0)
             >= lax.broadcasted_iota(jnp.int32, (PAIR, 2 * PAIR), 1) % PAIR)
    m0 = lax.broadcasted_iota(jnp.int32, (L, PAIR), 1) < HALF
    pairs = range(A_PAIRS)
    lanes = [slice(j * PAIR, (j + 1) * PAIR) for j in pairs]
    chains = [(c, j) for c in range(n_chunks) for j in pairs]
    every = range(len(chains))
    rows_of = [slice(c * L, (c + 1) * L) for c in range(n_chunks)]
    kkg, rg, kinv, binv, kd, bd, e_last = [], [], [], [], [], [], []
    for sl in rows_of:
        lw = logw[sl]
        cum = _dot_xl(tri, lw)
        c_last = cum[L - 1:L, :]
        e_neg = jnp.exp(-cum)
        e_rem = jnp.exp(c_last - cum)
        kkg.append(kk[sl] * jnp.exp(cum - lw))
        rg.append(r[sl] * jnp.exp(cum))
        kinv.append(k[sl] * e_neg)
        binv.append(b[sl] * e_neg)
        kd.append(k[sl] * e_rem)
        bd.append(b[sl] * e_rem)
        e_last.append(jnp.exp(c_last))
    sm = lambda xs_, c, j: _stack_masked(xs_[c][:, lanes[j]].astype(BF16), m0)
    kkg_b = [sm(kkg, c, j) for c, j in chains]
    v_b = [_stack_masked(v[rows_of[c], lanes[j]].astype(BF16), m0) for c, j in chains]
    bd_b = [sm(bd, c, j) for c, j in chains]
    kd_b = [sm(kd, c, j) for c, j in chains]
    rg_b = [sm(rg, c, j) for c, j in chains]
    kinv_b = [sm(kinv, c, j) for c, j in chains]
    binv_b = [sm(binv, c, j) for c, j in chains]
    aa = [_dg(jnp.concatenate([kkg_b[n], rg_b[n]], axis=0), jnp.concatenate([binv_b[n], kinv_b[n]], axis=0), NT)
          for n in every]
    a_kb = [jnp.where(strict, x[:PAIR, :PAIR], 0.0) for x in aa]
    a_kk = [jnp.where(strict, x[:PAIR, PAIR:], 0.0).astype(BF16) for x in aa]
    a_rbk = [jnp.where(incl2, x[PAIR:, :], 0.0).astype(BF16) for x in aa]
    t = _inv_unit_lower(a_kb, eye, L)
    av = [_dg(a_kk[n], v_b[n], NN).astype(BF16) for n in every]
    wu = [_dg(t[n].astype(BF16), jnp.concatenate([kkg_b[n], av[n]], axis=1), NN) for n in every]
    wu_b = [x.astype(BF16) for x in wu]
    mc = [_dg(bd_b[n], wu_b[n], TN) for n in every]
    kv = [_dg(kd_b[n], v_b[n], TN) for n in every]
    rhs = [jnp.concatenate([jnp.concatenate([wu_b[n][:, :PAIR], (-wu[n][:, PAIR:]).astype(BF16)], axis=1),
                            jnp.concatenate([jnp.zeros_like(v_b[n]), v_b[n]], axis=1)], axis=0) for n in every]
    qo = [_dg(a_rbk[n], rhs[n], NN) for n in every]
    for n, (c, j) in enumerate(chains):
        mh_ref[c, j] = (eye * e_last[c][:, lanes[j]] - mc[n][:, :PAIR]).astype(BF16)
        ch_ref[c, j] = kv[n] - mc[n][:, PAIR:]
        qc_ref[rows_of[c], lanes[j]] = (rg[c][:, lanes[j]] - _fold(qo[n][:, :PAIR])).astype(BF16)
        oc_ref[rows_of[c], lanes[j]] = _fold(qo[n][:, PAIR:])


def _rwkv_scan_kernel(n_seq, n_chunks, mh_ref, ch_ref, qc_ref, oc_ref, g_ref, bonus_ref, h0_ref, lnw_ref, lnb_ref,
                      y_ref, hout_ref, h_ref):
    i = pl.program_id(1)
    L = CHUNK

    @pl.when(i == 0)
    def _():
        h_ref[...] = h0_ref[...]

    seg = _head_mask()
    inv_n = 1.0 / A_HEAD_DIM
    lanes = [slice(j * PAIR, (j + 1) * PAIR) for j in range(A_PAIRS)]
    chains = [(s, j) for s in range(n_seq) for j in range(A_PAIRS)]
    hb = [h_ref[s, j] for s, j in chains]
    for c in range(n_chunks):
        sl = slice(c * L, (c + 1) * L)
        hb_b = [x.astype(BF16) for x in hb]
        o = [_dg(qc_ref[s, sl, lanes[j]], hb_b[n], NN) + oc_ref[s, sl, lanes[j]] for n, (s, j) in enumerate(chains)]
        hb = [_dg(mh_ref[s, c, j], hb_b[n], NN) + ch_ref[s, c, j] for n, (s, j) in enumerate(chains)]
        for s in range(n_seq):
            os_ = jnp.concatenate(o[s * A_PAIRS:(s + 1) * A_PAIRS], axis=1)
            mean = _segsum(os_, seg) * inv_n
            d = os_ - mean
            var = _segsum(d * d, seg) * inv_n
            on = d * lax.rsqrt(var + A_NORM_EPS) * lnw_ref[...] + lnb_ref[...]
            y_ref[s, sl, :] = ((on + bonus_ref[s, sl, :]) * g_ref[s, sl, :]).astype(y_ref.dtype)
    for n, (s, j) in enumerate(chains):
        h_ref[s, j] = hb[n]
    hout_ref[...] = h_ref[...]


def _rwkv_mix(pa, shift0, wkv0, prm):
    Bsz, T, _ = pa.shape
    n1 = next(n for n in (8, 4, 2, 1) if (T // CHUNK) % n == 0)
    R1 = n1 * CHUNK
    W = A_WIDTH
    nblk = T // R1
    row = lambda n: pl.BlockSpec((1, n), lambda b, i: (0, 0))
    full = lambda s: pl.BlockSpec(s, lambda b, i: (0,) * len(s))
    act = lambda r, n: pl.BlockSpec((None, r, n), lambda b, i: (b, i, 0))
    mat = lambda n: pl.BlockSpec((None, n, A_PAIRS, PAIR, PAIR), lambda b, i: (b, i, 0, 0, 0))
    sub = R1 // 8
    mats = lambda dt: jax.ShapeDtypeStruct((Bsz, T // CHUNK, A_PAIRS, PAIR, PAIR), dt)
    acts = lambda dt: jax.ShapeDtypeStruct((Bsz, T, W), dt)
    mh, ch, qc, oc, g, bonus = pl.pallas_call(
        functools.partial(_rwkv_prep_kernel, n1),
        grid=(Bsz, nblk),
        in_specs=[act(R1, A_PROJ),
                  pl.BlockSpec((None, 8, A_PROJ), lambda b, i: (b, jnp.maximum(i * sub - 1, 0), 0)),
                  pl.BlockSpec((None, 1, A_PROJ), lambda b, i: (b, 0, 0)),
                  row(A_PROJ), row(W), full((A_DECAY_LORA, W)), row(W), full((A_ICL_LORA, W)),
                  full((A_GATE_LORA, W)), row(W), row(W), row(W)],
        out_specs=[mat(n1), mat(n1), act(R1, W), act(R1, W), act(R1, W), act(R1, W)],
        out_shape=[mats(BF16), mats(F32), acts(BF16), acts(F32), acts(F32), acts(F32)],
        compiler_params=_params("parallel", "parallel"),
    )(pa, pa, shift0[:, None, :], prm['mu'], prm['w0'], prm['w2'], prm['a0'], prm['a2'], prm['g2'],
      prm['k_k'], prm['k_a'], prm['r_k'])

    n2 = min(4, T // CHUNK)
    R2 = n2 * CHUNK
    hh = wkv0.transpose(0, 1, 3, 2).reshape(Bsz, A_PAIRS, 2, A_HEAD_DIM, A_HEAD_DIM)
    zero = jnp.zeros_like(hh[:, :, 0])
    h0 = jnp.concatenate([jnp.concatenate([hh[:, :, 0], zero], axis=-1),
                          jnp.concatenate([zero, hh[:, :, 1]], axis=-1)], axis=-2)
    ns = next(n for n in (4, 2, 1) if Bsz % n == 0)
    state = pl.BlockSpec((ns, A_PAIRS, PAIR, PAIR), lambda b, i: (b, 0, 0, 0))
    act2 = lambda r, n: pl.BlockSpec((ns, r, n), lambda b, i: (b, i, 0))
    mat2 = lambda n: pl.BlockSpec((ns, n, A_PAIRS, PAIR, PAIR), lambda b, i: (b, i, 0, 0, 0))
    y, hout = pl.pallas_call(
        functools.partial(_rwkv_scan_kernel, ns, n2),
        grid=(Bsz // ns, T // R2),
        in_specs=[mat2(n2), mat2(n2), act2(R2, W), act2(R2, W), act2(R2, W), act2(R2, W), state,
                  row(W), row(W)],
        out_specs=[act2(R2, W), state],
        out_shape=[acts(BF16), jax.ShapeDtypeStruct((Bsz, A_PAIRS, PAIR, PAIR), F32)],
        scratch_shapes=[pltpu.VMEM((ns, A_PAIRS, PAIR, PAIR), F32)],
        compiler_params=_params("parallel", "arbitrary"),
    )(mh, ch, qc, oc, g, bonus, h0, prm['ln_w'], prm['ln_b'])
    hd = jnp.stack([hout[:, :, :HALF, :HALF], hout[:, :, HALF:, HALF:]], axis=2)
    wkv = hd.reshape(Bsz, A_HEADS, A_HEAD_DIM, A_HEAD_DIM).transpose(0, 1, 3, 2)
    return y, wkv


def _gla_kernel(n_seq, n_chunks, pb_ref, s0_ref, up_ref, bias_ref, nw_ref, y_ref, sout_ref, s_ref):
    i = pl.program_id(1)
    L = CHUNK

    @pl.when(i == 0)
    def _():
        s_ref[...] = s0_ref[...]

    kw, vw = B_KWIDTH, B_VWIDTH
    incl = lax.broadcasted_iota(jnp.int32, (L, L), 0) >= lax.broadcasted_iota(jnp.int32, (L, L), 1)
    tri = incl.astype(BF16)
    m0 = lax.broadcasted_iota(jnp.int32, (L, PAIR), 1) < HALF
    top = lax.broadcasted_iota(jnp.int32, (PAIR, B_VAL_DIM), 0) < HALF
    scale = B_KEY_DIM ** -0.5
    pairs = range(B_PAIRS)
    heads = range(B_HEADS)
    pl_ = lambda h: slice((h // 2) * PAIR, (h // 2 + 1) * PAIR)
    units = [(s, c) for c in range(n_chunks) for s in range(n_seq)]
    rows_of = lambda c: slice(c * L, (c + 1) * L)
    vslice = lambda h: slice(h * B_VAL_DIM, (h + 1) * B_VAL_DIM)
    z = [_dot3(pb_ref[s, rows_of(c), 2 * kw + 2 * vw:], up_ref[...]) + bias_ref[...] for s, c in units]
    bc = [_dot_xl(tri, -_softplus(-x) * (1.0 / B_GATE_NORM)) for x in z]
    b_last = [x[L - 1:L, :] for x in bc]
    q_dec, k_inv, k_state, e_last_t = [], [], [], []
    for n, (s, c) in enumerate(units):
        k = pb_ref[s, rows_of(c), kw:2 * kw]
        q_dec.append(pb_ref[s, rows_of(c), 0:kw] * scale * jnp.exp(bc[n]))
        k_inv.append((k * jnp.exp(-bc[n])).astype(BF16))
        k_state.append((k * jnp.exp(b_last[n] - bc[n])).astype(BF16))
        e_last_t.append(jnp.broadcast_to(jnp.exp(b_last[n]), (LANE, kw)).T)
    uh = [(n, h) for n in range(len(units)) for h in heads]
    vb = [pb_ref[units[n][0], rows_of(units[n][1]), 2 * kw + h * B_VAL_DIM:2 * kw + (h + 1) * B_VAL_DIM].astype(BF16)
          for n, h in uh]
    qm = [jnp.where(m0 if h % 2 == 0 else ~m0, q_dec[n][:, pl_(h)], 0.0).astype(BF16) for n, h in uh]
    scores = [jnp.where(incl, _dg(qm[i_], k_inv[n][:, pl_(h)], NT), 0.0).astype(BF16) for i_, (n, h) in enumerate(uh)]
    upd = [_dg(k_state[n][:, pl_(h)], vb[i_], TN) for i_, (n, h) in enumerate(uh)]
    intra = [_dg(scores[i_], vb[i_], NN) for i_ in range(len(uh))]
    states = [[s_ref[s, j] for j in pairs] for s in range(n_seq)]
    before = []
    for n, (s, c) in enumerate(units):
        before.append([x.astype(BF16) for x in states[s]])
        states[s] = [states[s][j] * e_last_t[n][j * PAIR:(j + 1) * PAIR, :]
                     + jnp.where(top, upd[n * B_HEADS + 2 * j], upd[n * B_HEADS + 2 * j + 1]) for j in pairs]
    o = [_dg(qm[i_], before[n][h // 2], NN) + intra[i_] for i_, (n, h) in enumerate(uh)]
    for i_, (n, h) in enumerate(uh):
        s, c = units[n]
        oh = o[i_] * lax.rsqrt(jnp.mean(o[i_] * o[i_], -1, keepdims=True) + LN_EPS) * nw_ref[...]
        rg_h = pb_ref[s, rows_of(c), 2 * kw + vw + h * B_VAL_DIM:2 * kw + vw + (h + 1) * B_VAL_DIM]
        y_ref[s, rows_of(c), vslice(h)] = (oh * (rg_h * _sigmoid(rg_h))).astype(y_ref.dtype)
    for s in range(n_seq):
        for j in pairs:
            s_ref[s, j] = states[s][j]
    sout_ref[...] = s_ref[...]


def _gla_mix(pb, s0, prm):
    Bsz, T, _ = pb.shape
    n = min(4, T // CHUNK)
    Rr = n * CHUNK
    ns = next(n for n in (4, 2, 1) if Bsz % n == 0)
    st = pl.BlockSpec((ns, B_PAIRS, PAIR, B_VAL_DIM), lambda b, i: (b, 0, 0, 0))
    y, s = pl.pallas_call(
        functools.partial(_gla_kernel, ns, n),
        grid=(Bsz // ns, T // Rr),
        in_specs=[pl.BlockSpec((ns, Rr, B_PROJ_PAD), lambda b, i: (b, i, 0)), st,
                  pl.BlockSpec((B_LORA_PAD, B_KWIDTH), lambda b, i: (0, 0)),
                  pl.BlockSpec((1, B_KWIDTH), lambda b, i: (0, 0)),
                  pl.BlockSpec((1, B_VAL_DIM), lambda b, i: (0, 0))],
        out_specs=[pl.BlockSpec((ns, Rr, B_VWIDTH), lambda b, i: (b, i, 0)), st],
        out_shape=[jax.ShapeDtypeStruct((Bsz, T, B_VWIDTH), BF16),
                   jax.ShapeDtypeStruct((Bsz, B_PAIRS, PAIR, B_VAL_DIM), F32)],
        scratch_shapes=[pltpu.VMEM((ns, B_PAIRS, PAIR, B_VAL_DIM), F32)],
        compiler_params=_params("parallel", "arbitrary"),
    )(pb, s0.reshape(Bsz, B_PAIRS, PAIR, B_VAL_DIM), prm['up'], prm['bias'], prm['nw'])
    return y, s.reshape(Bsz, B_HEADS, B_KEY_DIM, B_VAL_DIM)


ATT_QB = 2 * CHUNK
ATT_SCALE = C_HEAD_DIM ** -0.5
LOG2E = 1.4426950408889634


ATT_WIN = C_PAST_ROWS + ATT_QB
ATT_MAX_QB = 8
ATT_PAD = ATT_MAX_QB * ATT_QB


def _qkv_pad_kernel(n_pad, x_ref, w_ref, o_ref):
    r = pl.program_id(1)

    @pl.when(r < n_pad)
    def _():
        o_ref[...] = jnp.zeros_like(o_ref)

    @pl.when(r >= n_pad)
    def _():
        y = jnp.dot(x_ref[...].astype(BF16), w_ref[...], preferred_element_type=F32).astype(BF16)
        qk = 2 * D_MODEL
        o_ref[:, :qk] = y[:, :qk]
        ones = jnp.ones((y.shape[0], PAIR), BF16)
        for hp in range(C_HEADS // 2):
            o_ref[:, qk + 2 * hp * PAIR:qk + (2 * hp + 1) * PAIR] = y[:, qk + hp * PAIR:qk + (hp + 1) * PAIR]
            o_ref[:, qk + (2 * hp + 1) * PAIR:qk + (2 * hp + 2) * PAIR] = ones


def _qkv_padded(x, w, o):
    Bsz, T, D = x.shape
    N = w.shape[2]
    n_out = N + D
    tm = next(t for t in (512, 256, 128) if T % t == 0)
    n_pad = ATT_PAD // tm
    return pl.pallas_call(
        functools.partial(_qkv_pad_kernel, n_pad),
        grid=(Bsz, T // tm + n_pad),
        in_specs=[pl.BlockSpec((None, tm, D), lambda b, r: (b, jnp.maximum(r - n_pad, 0), 0)),
                  pl.BlockSpec((None, D, N), lambda b, r: (o, 0, 0), pipeline_mode=pl.Buffered(1))],
        out_specs=pl.BlockSpec((None, tm, n_out), lambda b, r: (b, r, 0)),
        out_shape=jax.ShapeDtypeStruct((Bsz, ATT_PAD + T, n_out), BF16),
        compiler_params=_params("parallel", "arbitrary"),
    )(x, w)


def _kv_tail_kernel(x_ref, wt_ref, o_ref):
    o_ref[...] = _dg(wt_ref[...], x_ref[...].astype(BF16), NT)


def _kv_tail(x, w_t, o, keep):
    Bsz, T, D = x.shape
    tm = next(t for t in (512, 256, 128) if keep % t == 0 and T % t == 0)
    first = (T - keep) // tm
    return pl.pallas_call(
        _kv_tail_kernel,
        grid=(Bsz, keep // tm, 2),
        in_specs=[pl.BlockSpec((None, tm, D), lambda b, r, j: (b, first + r, 0)),
                  pl.BlockSpec((None, D, D), lambda b, r, j: (o, j, 0))],
        out_specs=pl.BlockSpec((None, None, D, tm), lambda b, r, j: (b, j, 0, r)),
        out_shape=jax.ShapeDtypeStruct((Bsz, 2, D, keep), F32),
        compiler_params=_params("parallel", "parallel", "arbitrary"),
    )(x, w_t)


def _att_prompt_kernel(n_qb, q_ref, k_ref, v_ref, bias_ref, o_ref):
    g = pl.program_id(2)
    m0 = lax.broadcasted_iota(jnp.int32, (ATT_QB, PAIR), 1) < HALF
    col = lax.broadcasted_iota(jnp.int32, (2 * ATT_QB, ATT_WIN), 1)

    def run(window_has_padding):
        blocks = range(n_qb)
        rows = [slice(t * ATT_QB, (t + 1) * ATT_QB) for t in blocks]
        wins = [pl.ds(pl.multiple_of(ATT_PAD - C_PAST_ROWS + (g * n_qb + t) * ATT_QB, ATT_QB), ATT_WIN)
                for t in blocks]
        q = [_stack_masked(q_ref[rows[t], :], m0) for t in blocks]
        s = [_dg(q[t], k_ref[wins[t], :], NT) + bias_ref[...] for t in blocks]
        if window_has_padding:
            s = [jnp.where(col >= C_PAST_ROWS - (g * n_qb + t) * ATT_QB, s[t], NEG_INF) for t in blocks]
        p = [jnp.exp2(x - x.max(-1, keepdims=True)) for x in s]
        pv = [_dg(p[t].astype(BF16), v_ref[wins[t], :], NN) for t in blocks]
        o = [x[:, :PAIR] / x[:, PAIR:] for x in pv]
        for t in blocks:
            o_ref[rows[t], :] = jnp.where(m0, o[t][:ATT_QB], o[t][ATT_QB:]).astype(o_ref.dtype)

    first = g * (n_qb * ATT_QB) < C_PAST_ROWS
    pl.when(first)(lambda: run(True))
    pl.when(jnp.logical_not(first))(lambda: run(False))


def _attention_prompt(qkv, bias):
    Bsz, Tp, _ = qkv.shape
    T = Tp - ATT_PAD
    nhp = C_HEADS // 2
    nblk = T // ATT_QB
    n_qb = next(n for n in (ATT_MAX_QB, 4, 2, 1) if nblk % n == 0)
    rows = n_qb * ATT_QB
    skip = ATT_PAD // rows
    return pl.pallas_call(
        functools.partial(_att_prompt_kernel, n_qb),
        grid=(Bsz, nhp, T // rows),
        in_specs=[pl.BlockSpec((None, rows, LANE), lambda b, hp, g: (b, g + skip, hp)),
                  pl.BlockSpec((None, Tp, LANE), lambda b, hp, g: (b, 0, nhp + hp)),
                  pl.BlockSpec((None, Tp, 2 * LANE), lambda b, hp, g: (b, 0, nhp + hp)),
                  pl.BlockSpec((None, 2 * ATT_QB, ATT_WIN), lambda b, hp, g: (hp, 0, 0))],
        out_specs=pl.BlockSpec((None, rows, LANE), lambda b, hp, g: (b, g, hp)),
        out_shape=jax.ShapeDtypeStruct((Bsz, T, D_MODEL), BF16),
        compiler_params=_params("parallel", "parallel", "arbitrary"),
    )(qkv, qkv, qkv, bias)


def _att_sample_kernel(qkv_ref, kc_ref, vc_ref, bias_ref, o_ref):
    R = kc_ref.shape[2]
    heads = range(C_HEADS)
    col = lambda part, h: slice(part * D_MODEL + h * C_HEAD_DIM, part * D_MODEL + (h + 1) * C_HEAD_DIM)
    q = [(qkv_ref[:, col(0, h)] * ATT_SCALE).astype(BF16) for h in heads]
    s_c = [_dg(q[h], kc_ref[h].astype(BF16), NN) + bias_ref[h, :, 0:R] for h in heads]
    s_n = [_dg(q[h], qkv_ref[:, col(1, h)].astype(BF16), NT) + bias_ref[h, :, R:] for h in heads]
    m = [jnp.maximum(s_c[h].max(-1, keepdims=True), s_n[h].max(-1, keepdims=True)) for h in heads]
    p_c = [jnp.exp(s_c[h] - m[h]) for h in heads]
    p_n = [jnp.exp(s_n[h] - m[h]) for h in heads]
    acc = [_dg(p_c[h].astype(BF16), vc_ref[h].astype(BF16), NT)
           + _dg(p_n[h].astype(BF16), qkv_ref[:, col(2, h)].astype(BF16), NN) for h in heads]
    for h in heads:
        den = p_c[h].sum(-1, keepdims=True) + p_n[h].sum(-1, keepdims=True)
        o_ref[:, col(0, h)] = (acc[h] / den).astype(o_ref.dtype)


def _attention_sample(qkv, cache_k, cache_v, bias, o):
    Bsz, T, _ = qkv.shape
    R = cache_k.shape[4]
    cache = pl.BlockSpec((None, None, C_HEADS, C_HEAD_DIM, R), lambda b: (o, b, 0, 0, 0))
    return pl.pallas_call(
        _att_sample_kernel,
        grid=(Bsz,),
        in_specs=[pl.BlockSpec((None, T, 3 * D_MODEL), lambda b: (b, 0, 0)), cache, cache,
                  pl.BlockSpec((C_HEADS, T, R + T), lambda b: (0, 0, 0))],
        out_specs=pl.BlockSpec((None, T, D_MODEL), lambda b: (b, 0, 0)),
        out_shape=jax.ShapeDtypeStruct((Bsz, T, D_MODEL), BF16),
        compiler_params=_params("parallel"),
    )(qkv, cache_k, cache_v, bias)


def _rel_bias_tile(table, n_q, n_k, q_offset):
    period = n_q + n_k - 1
    m = np.arange(period)
    d = np.where(m < n_k, m, m - period)
    idx = np.clip(q_offset - d, -C_REL_CLIP, C_REL_CLIP) + C_REL_CLIP
    diag = table[:, idx]
    flat = jnp.tile(diag, (1, n_q))[:, :n_q * (period - 1)]
    return flat.reshape(table.shape[0], n_q, period - 1)[:, :, :n_k]


TM = 512


def _trunk(x, wkv0, shift0, gla0, cache_k, cache_v, P):
    Bsz, T, _ = x.shape
    M = Bsz * T
    xf = x.reshape(M, D_MODEL)
    wkv_o, shift_o, gla_o, k_o, v_o = [], [], [], [], []
    for layer in range(DEPTH):
        if layer % 2 == 0:
            e = layer // 2
            pa, pb = _in_proj(xf, P['w_in_a'], P['w_in_b'], e, TM)
            pa = pa.reshape(Bsz, T, A_PROJ)
            pb = pb.reshape(Bsz, T, B_PROJ_PAD)
            ya, wkv = _rwkv_mix(pa, shift0[e], wkv0[e], {k: v[e] for k, v in P['a'].items()})
            yb, gs = _gla_mix(pb, gla0[e], {k: v[e] for k, v in P['b'].items()})
            wkv_o.append(wkv)
            shift_o.append(pa[:, -1])
            gla_o.append(gs)
            pairs = [(ya.reshape(M, A_WIDTH), P['w_out_a'], e), (yb.reshape(M, B_VWIDTH), P['w_out_b'], e)]
        else:
            o = layer // 2
            if cache_k is None:
                x3 = xf.reshape(Bsz, T, D_MODEL)
                att = _attention_prompt(_qkv_padded(x3, P['c_w_qkv_scaled'], o), P['c_bias_prompt'][o])
                keep = min(C_PAST_ROWS, T)
                rows_t = _kv_tail(x3, P['c_w_kv_t'], o, keep).reshape(Bsz, 2, C_HEADS, C_HEAD_DIM, keep)
                k_o.append(jnp.swapaxes(rows_t[:, 0], -1, -2))
                v_o.append(jnp.swapaxes(rows_t[:, 1], -1, -2))
            else:
                qkv = _matmul(xf, P['c_w_qkv'][o], TM, D_MODEL).reshape(Bsz, T, 3 * D_MODEL)
                att = _attention_sample(qkv, jnp.swapaxes(cache_k, -1, -2), jnp.swapaxes(cache_v, -1, -2),
                                        P['c_bias_sample'][o], o)
                rows = qkv.reshape(Bsz, T, 3, C_HEADS, C_HEAD_DIM)
                k_o.append(rows[:, :, 1].transpose(0, 2, 1, 3))
                v_o.append(rows[:, :, 2].transpose(0, 2, 1, 3))
            pairs = [(att.reshape(M, D_MODEL), P['c_w_o'], o)]
        xf = _mix_ffn(pairs, xf, P['ln1_w'], P['ln1_b'], P['ffn_w_in'], P['ffn_w_out'], P['ln2_w'], P['ln2_b'],
                      layer, TM)
    return (xf.reshape(Bsz, T, D_MODEL), jnp.stack(wkv_o), jnp.stack(shift_o), jnp.stack(gla_o),
            jnp.stack(k_o), jnp.stack(v_o))


def _prepare(w_in_mix, a_mu, a_w0, a_w2, a_a0, a_a2, a_g2, a_k_k, a_k_a, a_r_k, a_ln_w, a_ln_b,
             b_alpha_up, b_alpha_bias, b_norm_w, w_out_mix, c_w_qkv, c_rel_bias, c_w_o,
             ln1_w, ln1_b, ln2_w, ln2_b, ffn_w_in, ffn_w_out, sample_len, cache_rows):
    kw, vw = B_KWIDTH, B_VWIDTH
    wb = w_in_mix[:, :, A_PROJ:]
    main = jnp.concatenate([wb[:, :, :2 * kw + vw], wb[:, :, 2 * kw + vw + B_GATE_LORA:]], axis=-1)
    lora = jnp.pad(wb[:, :, 2 * kw + vw:2 * kw + vw + B_GATE_LORA], ((0, 0), (0, 0), (0, B_LORA_PAD - B_GATE_LORA)))
    row3 = lambda t: t.reshape(t.shape[0], 1, -1)
    bias_prompt = jnp.stack([_rel_bias_tile(t, ATT_QB, ATT_WIN, C_PAST_ROWS) for t in c_rel_bias])
    rel = np.arange(ATT_WIN)[None, :] // CHUNK - np.arange(ATT_QB)[:, None] // CHUNK
    band = (rel >= 0) & (rel <= C_PAST_CHUNKS)
    bias_prompt = jnp.where(band, bias_prompt * LOG2E, NEG_INF)
    q_scale = jnp.concatenate([jnp.full((D_MODEL,), ATT_SCALE * LOG2E, F32), jnp.ones((2 * D_MODEL,), F32)])
    P = {
        'w_in_a': w_in_mix[:, :, :A_PROJ].astype(BF16),
        'w_in_b': jnp.concatenate([main, lora], axis=-1).astype(BF16),
        'a': dict(mu=row3(a_mu), w0=row3(a_w0), w2=a_w2, a0=row3(a_a0), a2=a_a2, g2=a_g2, k_k=row3(a_k_k),
                  k_a=row3(a_k_a), r_k=row3(a_r_k), ln_w=row3(a_ln_w), ln_b=row3(a_ln_b)),
        'b': dict(up=jnp.pad(b_alpha_up, ((0, 0), (0, B_LORA_PAD - B_GATE_LORA), (0, 0))),
                  bias=row3(b_alpha_bias), nw=row3(b_norm_w)),
        'w_out_a': w_out_mix[:, :A_WIDTH].astype(BF16),
        'w_out_b': w_out_mix[:, A_WIDTH:].astype(BF16),
        'c_w_qkv': c_w_qkv.astype(BF16),
        'c_w_qkv_scaled': (c_w_qkv * q_scale).astype(BF16),
        'c_w_kv_t': jnp.swapaxes(c_w_qkv[:, :, D_MODEL:], 1, 2).astype(BF16),
        'c_w_o': c_w_o.astype(BF16),
        'c_bias_prompt': bias_prompt.reshape(bias_prompt.shape[0], C_HEADS // 2, 2 * ATT_QB, ATT_WIN),
        'c_bias_sample': jnp.stack([_rel_bias_tile(t, sample_len, cache_rows + sample_len, cache_rows)
                                    for t in c_rel_bias]),
        'ln1_w': row3(ln1_w), 'ln1_b': row3(ln1_b), 'ln2_w': row3(ln2_w), 'ln2_b': row3(ln2_b),
        'ffn_w_in': ffn_w_in.astype(BF16),
        'ffn_w_out': ffn_w_out.astype(BF16),
    }
    return P


def kernel(x_prompt, x_sample, state_a_wkv, state_a_shift, state_b_gla, cache_c_k, cache_c_v, w_in_mix, a_mu, a_w0, a_w2, a_a0, a_a2, a_g2, a_k_k, a_k_a, a_r_k, a_ln_w, a_ln_b, b_alpha_up, b_alpha_bias, b_norm_w, w_out_mix, c_w_qkv, c_rel_bias, c_w_o, ln1_w, ln1_b, ln2_w, ln2_b, ffn_w_in, ffn_w_out):
    P = _prepare(w_in_mix, a_mu, a_w0, a_w2, a_a0, a_a2, a_g2, a_k_k, a_k_a, a_r_k, a_ln_w, a_ln_b,
                 b_alpha_up, b_alpha_bias, b_norm_w, w_out_mix, c_w_qkv, c_rel_bias, c_w_o,
                 ln1_w, ln1_b, ln2_w, ln2_b, ffn_w_in, ffn_w_out, x_sample.shape[1], cache_c_k.shape[3])
    bp = x_prompt.shape[0]
    dt = x_prompt.dtype
    n_even = state_a_wkv.shape[0]
    wkv_zero = jnp.zeros((n_even, bp, A_HEADS, A_HEAD_DIM, A_HEAD_DIM), dt)
    shift_zero = jnp.zeros((n_even, bp, A_PROJ), dt)
    gla_zero = jnp.zeros((n_even, bp, B_HEADS, B_KEY_DIM, B_VAL_DIM), dt)
    y_prompt, p_wkv, p_shift, p_gla, p_k, p_v = _trunk(x_prompt, wkv_zero, shift_zero, gla_zero, None, None, P)
    y_sample, s_wkv, s_shift, s_gla, s_k, s_v = _trunk(
        x_sample, state_a_wkv, state_a_shift, state_b_gla, cache_c_k, cache_c_v, P)
    return (y_prompt, y_sample, p_wkv, p_shift, p_gla, p_k, p_v, s_wkv, s_shift, s_gla, s_k, s_v)
```

```python
import functools

import jax
import jax.numpy as jnp
import numpy as np
from jax import lax
from jax.experimental import pallas as pl
from jax.experimental.pallas import tpu as pltpu

F32 = jnp.float32
BF16 = jnp.bfloat16

D_MODEL = 1024
DEPTH = 4
CHUNK = 64
A_WIDTH = 512
A_HEAD_DIM = 64
A_HEADS = 8
A_DECAY_LORA = 64
A_ICL_LORA = 64
A_GATE_LORA = 128
A_PROJ = 3 * A_WIDTH + A_DECAY_LORA + A_ICL_LORA + A_GATE_LORA
A_NORM_EPS = 64e-5
B_VWIDTH = 512
B_KWIDTH = 256
B_HEADS = 4
B_KEY_DIM = 64
B_VAL_DIM = 128
B_GATE_LORA = 16
B_GATE_NORM = 16.0
B_LORA_PAD = 128
B_PROJ_PAD = 2 * B_KWIDTH + 2 * B_VWIDTH + B_LORA_PAD
C_HEAD_DIM = 64
C_HEADS = 16
C_PAST_CHUNKS = 8
C_PAST_ROWS = C_PAST_CHUNKS * CHUNK
C_REL_CLIP = 128
FFN_HIDDEN = 2816
LN_EPS = 1e-5
DEEPNORM_ALPHA = (2.0 * DEPTH) ** 0.25
NEG_INF = -1e30

LANE = 128
PAIR = LANE
HALF = PAIR // 2
A_PAIRS = A_HEADS // 2
B_PAIRS = B_HEADS // 2
VMEM_LIMIT = 56 * 1024 * 1024

NN = (((1,), (0,)), ((), ()))
NT = (((1,), (1,)), ((), ()))
TN = (((0,), (0,)), ((), ()))


def _params(*sem):
    return pltpu.CompilerParams(dimension_semantics=sem, vmem_limit_bytes=VMEM_LIMIT)


def _dg(a, b, dn):
    return lax.dot_general(a, b, dn, preferred_element_type=F32)


def _dot1(a, b, dn=NN):
    return _dg(a.astype(BF16), b.astype(BF16), dn)


def _split(a):
    hi = a.astype(BF16)
    lo = (a - hi.astype(F32)).astype(BF16)
    return hi, lo


def _dot3s(a, b, dn=NN):
    return _dg(a[0], b[0], dn) + _dg(a[0], b[1], dn) + _dg(a[1], b[0], dn)


def _dot3(a, b, dn=NN):
    return _dot3s(_split(a), _split(b), dn)


def _dot_xl(a_exact, b, dn=NN):
    bh, bl = _split(b)
    return _dg(a_exact, bh, dn) + _dg(a_exact, bl, dn)


def _dot_xr(a, b_exact, dn=NN):
    ah, al = _split(a)
    return _dg(ah, b_exact, dn) + _dg(al, b_exact, dn)


def _ln(z, w, b):
    mu = jnp.mean(z, -1, keepdims=True)
    d = z - mu
    var = jnp.mean(d * d, -1, keepdims=True)
    return d * lax.rsqrt(var + LN_EPS) * w + b


def _sigmoid(x):
    return 1.0 / (1.0 + jnp.exp(-x))


def _softplus(x):
    return jnp.maximum(x, 0.0) + jnp.log(1.0 + jnp.exp(-jnp.abs(x)))


def _stack_masked(x, m0):
    return jnp.concatenate([jnp.where(m0, x, 0.0), jnp.where(m0, 0.0, x)], axis=0)


def _fold(x):
    n = x.shape[0] // 2
    return x[:n] + x[n:]


def _mm_kernel(x_ref, w_ref, o_ref):
    o_ref[...] = jnp.dot(x_ref[...].astype(BF16), w_ref[...], preferred_element_type=F32)


def _matmul(x, w, tm, tn):
    M, K = x.shape
    N = w.shape[1]
    tm = next(t for t in (tm, 256, 128, 64) if M % t == 0)
    return pl.pallas_call(
        _mm_kernel,
        grid=(M // tm, N // tn),
        in_specs=[pl.BlockSpec((tm, K), lambda i, j: (i, 0)),
                  pl.BlockSpec((K, tn), lambda i, j: (0, j))],
        out_specs=pl.BlockSpec((tm, tn), lambda i, j: (i, j)),
        out_shape=jax.ShapeDtypeStruct((M, N), F32),
        compiler_params=_params("parallel", "arbitrary"),
    )(x, w)


def _in_proj_kernel(x_ref, wa_ref, wb_ref, oa_ref, ob_ref):
    xb = x_ref[...].astype(BF16)
    oa_ref[...] = jnp.dot(xb, wa_ref[...], preferred_element_type=F32)
    ob_ref[...] = jnp.dot(xb, wb_ref[...], preferred_element_type=F32)


def _in_proj(x, wa, wb, e, tm):
    M, K = x.shape
    tm = next(t for t in (tm, 512, 256, 128, 64) if M % t == 0)
    na, nb = wa.shape[2], wb.shape[2]
    resident = lambda n: pl.BlockSpec((None, K, n), lambda i: (e, 0, 0), pipeline_mode=pl.Buffered(1))
    return pl.pallas_call(
        _in_proj_kernel,
        grid=(M // tm,),
        in_specs=[pl.BlockSpec((tm, K), lambda i: (i, 0)), resident(na), resident(nb)],
        out_specs=[pl.BlockSpec((tm, na), lambda i: (i, 0)), pl.BlockSpec((tm, nb), lambda i: (i, 0))],
        out_shape=[jax.ShapeDtypeStruct((M, na), F32), jax.ShapeDtypeStruct((M, nb), F32)],
        compiler_params=_params("parallel"),
    )(x, wa, wb)


def _mix_ffn_kernel(n_pairs, *refs):
    res_ref, lw1_ref, lb1_ref, wg_ref, wu_ref, wo_ref, lw2_ref, lb2_ref, o_ref = refs[2 * n_pairs:]
    tm = o_ref.shape[0]
    n_half = 2 if tm % 32 == 0 else 1
    halves = [slice(i * (tm // n_half), (i + 1) * (tm // n_half)) for i in range(n_half)]
    mm = lambda a, w: jnp.dot(a, w, preferred_element_type=F32)
    acc = [mm(refs[0][h, :].astype(BF16), refs[1][...]) for h in halves]
    for p in range(1, n_pairs):
        acc = [acc[i] + mm(refs[2 * p][h, :].astype(BF16), refs[2 * p + 1][...]) for i, h in enumerate(halves)]
    x = [_ln(DEEPNORM_ALPHA * res_ref[h, :] + acc[i], lw1_ref[...], lb1_ref[...]) for i, h in enumerate(halves)]
    xb = [v.astype(BF16) for v in x]
    g = [mm(v, wg_ref[...]) for v in xb]
    u = [mm(v, wu_ref[...]) for v in xb]
    act = [(g[i] * _sigmoid(g[i]) * u[i]).astype(BF16) for i in range(n_half)]
    y = [mm(v, wo_ref[...]) for v in act]
    for i, h in enumerate(halves):
        o_ref[h, :] = _ln(DEEPNORM_ALPHA * x[i] + y[i], lw2_ref[...], lb2_ref[...])


def _mix_ffn(pairs, res, lw1, lb1, w_in, w_out, lw2, lb2, layer, tm):
    M = res.shape[0]
    tm = next(t for t in (tm, 256, 128, 64) if M % t == 0)
    resident = lambda shape, idx: pl.BlockSpec((None,) + shape, idx, pipeline_mode=pl.Buffered(1))
    vec = pl.BlockSpec((None, 1, D_MODEL), lambda i: (layer, 0, 0))
    in_specs, args = [], []
    for a, w, n in pairs:
        K = a.shape[1]
        in_specs += [pl.BlockSpec((tm, K), lambda i: (i, 0)), resident((K, D_MODEL), lambda i, n=n: (n, 0, 0))]
        args += [a, w]
    in_specs += [pl.BlockSpec((tm, D_MODEL), lambda i: (i, 0)), vec, vec,
                 resident((D_MODEL, FFN_HIDDEN), lambda i: (layer, 0, 0)),
                 resident((D_MODEL, FFN_HIDDEN), lambda i: (layer, 0, 1)),
                 resident((FFN_HIDDEN, D_MODEL), lambda i: (layer, 0, 0)), vec, vec]
    return pl.pallas_call(
        functools.partial(_mix_ffn_kernel, len(pairs)),
        grid=(M // tm,),
        in_specs=in_specs,
        out_specs=pl.BlockSpec((tm, D_MODEL), lambda i: (i, 0)),
        out_shape=jax.ShapeDtypeStruct((M, D_MODEL), F32),
        compiler_params=_params("parallel"),
    )(*args, res, lw1, lb1, w_in, w_in, w_out, lw2, lb2)


def _inv_unit_lower(ns, eye, nilpotency):
    size = eye.shape[0]
    xs = [eye - n for n in ns]
    ps = [_dot1(n, n) for n in ns]
    k = 2
    while k < nilpotency:
        k *= 2
        if k < nilpotency:
            rs = [_dot1(p, jnp.concatenate([x, p], axis=1)) for x, p in zip(xs, ps)]
            xs = [x + r[:, :size] for x, r in zip(xs, rs)]
            ps = [r[:, size:] for r in rs]
        else:
            xs = [x + _dot1(p, x) for x, p in zip(xs, ps)]
    return xs


def _segsum(x, seg, split=False):
    n = x.shape[1] // PAIR
    rows = x.shape[0]
    xs = jnp.concatenate([x[:, j * PAIR:(j + 1) * PAIR] for j in range(n)], axis=0)
    s = _dot_xr(xs, seg) if split else _dot1(xs, seg)
    return jnp.concatenate([s[j * rows:(j + 1) * rows] for j in range(n)], axis=1)


def _head_mask():
    hr = lax.broadcasted_iota(jnp.int32, (PAIR, PAIR), 0) // A_HEAD_DIM
    hc = lax.broadcasted_iota(jnp.int32, (PAIR, PAIR), 1) // A_HEAD_DIM
    return (hr == hc).astype(BF16)


def _rwkv_prep_kernel(n_chunks, pa_ref, prev_ref, shift_ref, mu_ref, w0_ref, w2_ref, a0_ref, a2_ref, g2_ref,
                      kk_ref, ka_ref, rk_ref,
                      mh_ref, ch_ref, qc_ref, oc_ref, g_ref, bonus_ref):
    i = pl.program_id(1)
    L = CHUNK
    pa = pa_ref[...]
    first = jnp.where(i == 0, shift_ref[...], prev_ref[7:8, :])
    rows = lax.broadcasted_iota(jnp.int32, pa.shape, 0)
    prev = jnp.where(rows == 0, first, pltpu.roll(pa, shift=1, axis=0))
    xs = pa + (prev - pa) * mu_ref[...]
    W = A_WIDTH
    r = xs[:, 0:W]
    k0 = xs[:, W:2 * W]
    v = xs[:, 2 * W:3 * W]
    xw = xs[:, 3 * W:3 * W + A_DECAY_LORA]
    xa = xs[:, 3 * W + A_DECAY_LORA:3 * W + A_DECAY_LORA + A_ICL_LORA]
    xg = xs[:, 3 * W + A_DECAY_LORA + A_ICL_LORA:]
    w = -_softplus(-(w0_ref[...] + _dot3(jnp.tanh(xw), w2_ref[...]))) - 0.5
    logw = -jnp.exp(w)
    a = _sigmoid(a0_ref[...] + _dot1(xa, a2_ref[...]))
    g_ref[...] = _dot1(_sigmoid(xg), g2_ref[...])
    seg = _head_mask()
    kk = k0 * kk_ref[...]
    kk = kk * lax.rsqrt(_segsum(kk * kk, seg) + 1e-12)
    k = k0 * (1.0 + (a - 1.0) * ka_ref[...])
    bonus_ref[...] = _segsum(r * k * rk_ref[...], seg) * v
    b = kk * a

    tri = (lax.broadcasted_iota(jnp.int32, (L, L), 0) >= lax.broadcasted_iota(jnp.int32, (L, L), 1)).astype(BF16)
    row = lax.broadcasted_iota(jnp.int32, (PAIR, PAIR), 0)
    col = lax.broadcasted_iota(jnp.int32, (PAIR, PAIR), 1)
    eye = (row == col).astype(F32)
    strict = row > col
    incl2 = (lax.broadcasted_iota(jnp.int32, (PAIR, 2 * PAIR), 0)
             >= lax.broadcasted_iota(jnp.int32, (PAIR, 2 * PAIR), 1) % PAIR)
    m0 = lax.broadcasted_iota(jnp.int32, (L, PAIR), 1) < HALF
    pairs = range(A_PAIRS)
    lanes = [slice(j * PAIR, (j + 1) * PAIR) for j in pairs]
    chains = [(c, j) for c in range(n_chunks) for j in pairs]
    every = range(len(chains))
    rows_of = [slice(c * L, (c + 1) * L) for c in range(n_chunks)]
    kkg, rg, kinv, binv, kd, bd, e_last = [], [], [], [], [], [], []
    for sl in rows_of:
        lw = logw[sl]
        cum = _dot_xl(tri, lw)
        c_last = cum[L - 1:L, :]
        e_neg = jnp.exp(-cum)
        e_rem = jnp.exp(c_last - cum)
        kkg.append(kk[sl] * jnp.exp(cum - lw))
        rg.append(r[sl] * jnp.exp(cum))
        kinv.append(k[sl] * e_neg)
        binv.append(b[sl] * e_neg)
        kd.append(k[sl] * e_rem)
        bd.append(b[sl] * e_rem)
        e_last.append(jnp.exp(c_last))
    sm = lambda xs_, c, j: _stack_masked(xs_[c][:, lanes[j]].astype(BF16), m0)
    kkg_b = [sm(kkg, c, j) for c, j in chains]
    v_b = [_stack_masked(v[rows_of[c], lanes[j]].astype(BF16), m0) for c, j in chains]
    bd_b = [sm(bd, c, j) for c, j in chains]
    kd_b = [sm(kd, c, j) for c, j in chains]
    rg_b = [sm(rg, c, j) for c, j in chains]
    kinv_b = [sm(kinv, c, j) for c, j in chains]
    binv_b = [sm(binv, c, j) for c, j in chains]
    aa = [_dg(jnp.concatenate([kkg_b[n], rg_b[n]], axis=0), jnp.concatenate([binv_b[n], kinv_b[n]], axis=0), NT)
          for n in every]
    a_kb = [jnp.where(strict, x[:PAIR, :PAIR], 0.0) for x in aa]
    a_kk = [jnp.where(strict, x[:PAIR, PAIR:], 0.0).astype(BF16) for x in aa]
    a_rbk = [jnp.where(incl2, x[PAIR:, :], 0.0).astype(BF16) for x in aa]
    t = _inv_unit_lower(a_kb, eye, L)
    av = [_dg(a_kk[n], v_b[n], NN).astype(BF16) for n in every]
    wu = [_dg(t[n].astype(BF16), jnp.concatenate([kkg_b[n], av[n]], axis=1), NN) for n in every]
    wu_b = [x.astype(BF16) for x in wu]
    mc = [_dg(bd_b[n], wu_b[n], TN) for n in every]
    kv = [_dg(kd_b[n], v_b[n], TN) for n in every]
    rhs = [jnp.concatenate([jnp.concatenate([wu_b[n][:, :PAIR], (-wu[n][:, PAIR:]).astype(BF16)], axis=1),
                            jnp.concatenate([jnp.zeros_like(v_b[n]), v_b[n]], axis=1)], axis=0) for n in every]
    qo = [_dg(a_rbk[n], rhs[n], NN) for n in every]
    for n, (c, j) in enumerate(chains):
        mh_ref[c, j] = (eye * e_last[c][:, lanes[j]] - mc[n][:, :PAIR]).astype(BF16)
        ch_ref[c, j] = kv[n] - mc[n][:, PAIR:]
        qc_ref[rows_of[c], lanes[j]] = (rg[c][:, lanes[j]] - _fold(qo[n][:, :PAIR])).astype(BF16)
        oc_ref[rows_of[c], lanes[j]] = _fold(qo[n][:, PAIR:])


def _rwkv_scan_kernel(n_seq, n_chunks, mh_ref, ch_ref, qc_ref, oc_ref, g_ref, bonus_ref, h0_ref, lnw_ref, lnb_ref,
                      y_ref, hout_ref, h_ref):
    i = pl.program_id(1)
    L = CHUNK

    @pl.when(i == 0)
    def _():
        h_ref[...] = h0_ref[...]

    seg = _head_mask()
    inv_n = 1.0 / A_HEAD_DIM
    lanes = [slice(j * PAIR, (j + 1) * PAIR) for j in range(A_PAIRS)]
    chains = [(s, j) for s in range(n_seq) for j in range(A_PAIRS)]
    hb = [h_ref[s, j] for s, j in chains]
    for c in range(n_chunks):
        sl = slice(c * L, (c + 1) * L)
        hb_b = [x.astype(BF16) for x in hb]
        o = [_dg(qc_ref[s, sl, lanes[j]], hb_b[n], NN) + oc_ref[s, sl, lanes[j]] for n, (s, j) in enumerate(chains)]
        hb = [_dg(mh_ref[s, c, j], hb_b[n], NN) + ch_ref[s, c, j] for n, (s, j) in enumerate(chains)]
        for s in range(n_seq):
            os_ = jnp.concatenate(o[s * A_PAIRS:(s + 1) * A_PAIRS], axis=1)
            mean = _segsum(os_, seg, split=True) * inv_n
            d = os_ - mean
            var = _segsum(d * d, seg, split=True) * inv_n
            on = d * lax.rsqrt(var + A_NORM_EPS) * lnw_ref[...] + lnb_ref[...]
            y_ref[s, sl, :] = ((on + bonus_ref[s, sl, :]) * g_ref[s, sl, :]).astype(y_ref.dtype)
    for n, (s, j) in enumerate(chains):
        h_ref[s, j] = hb[n]
    hout_ref[...] = h_ref[...]


def _rwkv_mix(pa, shift0, wkv0, prm):
    Bsz, T, _ = pa.shape
    n1 = next(n for n in (8, 4, 2, 1) if (T // CHUNK) % n == 0)
    R1 = n1 * CHUNK
    W = A_WIDTH
    nblk = T // R1
    row = lambda n: pl.BlockSpec((1, n), lambda b, i: (0, 0))
    full = lambda s: pl.BlockSpec(s, lambda b, i: (0,) * len(s))
    act = lambda r, n: pl.BlockSpec((None, r, n), lambda b, i: (b, i, 0))
    mat = lambda n: pl.BlockSpec((None, n, A_PAIRS, PAIR, PAIR), lambda b, i: (b, i, 0, 0, 0))
    sub = R1 // 8
    mats = lambda dt: jax.ShapeDtypeStruct((Bsz, T // CHUNK, A_PAIRS, PAIR, PAIR), dt)
    acts = lambda dt: jax.ShapeDtypeStruct((Bsz, T, W), dt)
    mh, ch, qc, oc, g, bonus = pl.pallas_call(
        functools.partial(_rwkv_prep_kernel, n1),
        grid=(Bsz, nblk),
        in_specs=[act(R1, A_PROJ),
                  pl.BlockSpec((None, 8, A_PROJ), lambda b, i: (b, jnp.maximum(i * sub - 1, 0), 0)),
                  pl.BlockSpec((None, 1, A_PROJ), lambda b, i: (b, 0, 0)),
                  row(A_PROJ), row(W), full((A_DECAY_LORA, W)), row(W), full((A_ICL_LORA, W)),
                  full((A_GATE_LORA, W)), row(W), row(W), row(W)],
        out_specs=[mat(n1), mat(n1), act(R1, W), act(R1, W), act(R1, W), act(R1, W)],
        out_shape=[mats(BF16), mats(F32), acts(BF16), acts(F32), acts(F32), acts(F32)],
        compiler_params=_params("parallel", "parallel"),
    )(pa, pa, shift0[:, None, :], prm['mu'], prm['w0'], prm['w2'], prm['a0'], prm['a2'], prm['g2'],
      prm['k_k'], prm['k_a'], prm['r_k'])

    n2 = min(4, T // CHUNK)
    R2 = n2 * CHUNK
    hh = wkv0.transpose(0, 1, 3, 2).reshape(Bsz, A_PAIRS, 2, A_HEAD_DIM, A_HEAD_DIM)
    zero = jnp.zeros_like(hh[:, :, 0])
    h0 = jnp.concatenate([jnp.concatenate([hh[:, :, 0], zero], axis=-1),
                          jnp.concatenate([zero, hh[:, :, 1]], axis=-1)], axis=-2)
    ns = next(n for n in (4, 2, 1) if Bsz % n == 0)
    state = pl.BlockSpec((ns, A_PAIRS, PAIR, PAIR), lambda b, i: (b, 0, 0, 0))
    act2 = lambda r, n: pl.BlockSpec((ns, r, n), lambda b, i: (b, i, 0))
    mat2 = lambda n: pl.BlockSpec((ns, n, A_PAIRS, PAIR, PAIR), lambda b, i: (b, i, 0, 0, 0))
    y, hout = pl.pallas_call(
        functools.partial(_rwkv_scan_kernel, ns, n2),
        grid=(Bsz // ns, T // R2),
        in_specs=[mat2(n2), mat2(n2), act2(R2, W), act2(R2, W), act2(R2, W), act2(R2, W), state,
                  row(W), row(W)],
        out_specs=[act2(R2, W), state],
        out_shape=[acts(BF16), jax.ShapeDtypeStruct((Bsz, A_PAIRS, PAIR, PAIR), F32)],
        scratch_shapes=[pltpu.VMEM((ns, A_PAIRS, PAIR, PAIR), F32)],
        compiler_params=_params("parallel", "arbitrary"),
    )(mh, ch, qc, oc, g, bonus, h0, prm['ln_w'], prm['ln_b'])
    hd = jnp.stack([hout[:, :, :HALF, :HALF], hout[:, :, HALF:, HALF:]], axis=2)
    wkv = hd.reshape(Bsz, A_HEADS, A_HEAD_DIM, A_HEAD_DIM).transpose(0, 1, 3, 2)
    return y, wkv


def _gla_kernel(n_seq, n_chunks, pb_ref, s0_ref, up_ref, bias_ref, nw_ref, y_ref, sout_ref, s_ref):
    i = pl.program_id(1)
    L = CHUNK

    @pl.when(i == 0)
    def _():
        s_ref[...] = s0_ref[...]

    kw, vw = B_KWIDTH, B_VWIDTH
    incl = lax.broadcasted_iota(jnp.int32, (L, L), 0) >= lax.broadcasted_iota(jnp.int32, (L, L), 1)
    tri = incl.astype(BF16)
    m0 = lax.broadcasted_iota(jnp.int32, (L, PAIR), 1) < HALF
    top = lax.broadcasted_iota(jnp.int32, (PAIR, B_VAL_DIM), 0) < HALF
    scale = B_KEY_DIM ** -0.5
    pairs = range(B_PAIRS)
    heads = range(B_HEADS)
    pl_ = lambda h: slice((h // 2) * PAIR, (h // 2 + 1) * PAIR)
    units = [(s, c) for c in range(n_chunks) for s in range(n_seq)]
    rows_of = lambda c: slice(c * L, (c + 1) * L)
    vslice = lambda h: slice(h * B_VAL_DIM, (h + 1) * B_VAL_DIM)
    z = [_dot3(pb_ref[s, rows_of(c), 2 * kw + 2 * vw:], up_ref[...]) + bias_ref[...] for s, c in units]
    bc = [_dot_xl(tri, -_softplus(-x) * (1.0 / B_GATE_NORM)) for x in z]
    b_last = [x[L - 1:L, :] for x in bc]
    q_dec, k_inv, k_state, e_last_t = [], [], [], []
    for n, (s, c) in enumerate(units):
        k = pb_ref[s, rows_of(c), kw:2 * kw]
        q_dec.append(pb_ref[s, rows_of(c), 0:kw] * scale * jnp.exp(bc[n]))
        k_inv.append((k * jnp.exp(-bc[n])).astype(BF16))
        k_state.append((k * jnp.exp(b_last[n] - bc[n])).astype(BF16))
        e_last_t.append(jnp.broadcast_to(jnp.exp(b_last[n]), (LANE, kw)).T)
    uh = [(n, h) for n in range(len(units)) for h in heads]
    vb = [pb_ref[units[n][0], rows_of(units[n][1]), 2 * kw + h * B_VAL_DIM:2 * kw + (h + 1) * B_VAL_DIM].astype(BF16)
          for n, h in uh]
    qm = [jnp.where(m0 if h % 2 == 0 else ~m0, q_dec[n][:, pl_(h)], 0.0).astype(BF16) for n, h in uh]
    scores = [jnp.where(incl, _dg(qm[i_], k_inv[n][:, pl_(h)], NT), 0.0).astype(BF16) for i_, (n, h) in enumerate(uh)]
    upd = [_dg(k_state[n][:, pl_(h)], vb[i_], TN) for i_, (n, h) in enumerate(uh)]
    intra = [_dg(scores[i_], vb[i_], NN) for i_ in range(len(uh))]
    states = [[s_ref[s, j] for j in pairs] for s in range(n_seq)]
    before = []
    for n, (s, c) in enumerate(units):
        before.append([x.astype(BF16) for x in states[s]])
        states[s] = [states[s][j] * e_last_t[n][j * PAIR:(j + 1) * PAIR, :]
                     + jnp.where(top, upd[n * B_HEADS + 2 * j], upd[n * B_HEADS + 2 * j + 1]) for j in pairs]
    o = [_dg(qm[i_], before[n][h // 2], NN) + intra[i_] for i_, (n, h) in enumerate(uh)]
    for i_, (n, h) in enumerate(uh):
        s, c = units[n]
        oh = o[i_] * lax.rsqrt(jnp.mean(o[i_] * o[i_], -1, keepdims=True) + LN_EPS) * nw_ref[...]
        rg_h = pb_ref[s, rows_of(c), 2 * kw + vw + h * B_VAL_DIM:2 * kw + vw + (h + 1) * B_VAL_DIM]
        y_ref[s, rows_of(c), vslice(h)] = (oh * (rg_h * _sigmoid(rg_h))).astype(y_ref.dtype)
    for s in range(n_seq):
        for j in pairs:
            s_ref[s, j] = states[s][j]
    sout_ref[...] = s_ref[...]


def _gla_mix(pb, s0, prm):
    Bsz, T, _ = pb.shape
    n = min(4, T // CHUNK)
    Rr = n * CHUNK
    ns = next(n for n in (4, 2, 1) if Bsz % n == 0)
    st = pl.BlockSpec((ns, B_PAIRS, PAIR, B_VAL_DIM), lambda b, i: (b, 0, 0, 0))
    y, s = pl.pallas_call(
        functools.partial(_gla_kernel, ns, n),
        grid=(Bsz // ns, T // Rr),
        in_specs=[pl.BlockSpec((ns, Rr, B_PROJ_PAD), lambda b, i: (b, i, 0)), st,
                  pl.BlockSpec((B_LORA_PAD, B_KWIDTH), lambda b, i: (0, 0)),
                  pl.BlockSpec((1, B_KWIDTH), lambda b, i: (0, 0)),
                  pl.BlockSpec((1, B_VAL_DIM), lambda b, i: (0, 0))],
        out_specs=[pl.BlockSpec((ns, Rr, B_VWIDTH), lambda b, i: (b, i, 0)), st],
        out_shape=[jax.ShapeDtypeStruct((Bsz, T, B_VWIDTH), BF16),
                   jax.ShapeDtypeStruct((Bsz, B_PAIRS, PAIR, B_VAL_DIM), F32)],
        scratch_shapes=[pltpu.VMEM((ns, B_PAIRS, PAIR, B_VAL_DIM), F32)],
        compiler_params=_params("parallel", "arbitrary"),
    )(pb, s0.reshape(Bsz, B_PAIRS, PAIR, B_VAL_DIM), prm['up'], prm['bias'], prm['nw'])
    return y, s.reshape(Bsz, B_HEADS, B_KEY_DIM, B_VAL_DIM)


ATT_QB = 2 * CHUNK
ATT_SCALE = C_HEAD_DIM ** -0.5
LOG2E = 1.4426950408889634


ATT_WIN = C_PAST_ROWS + ATT_QB
ATT_MAX_QB = 8
ATT_PAD = ATT_MAX_QB * ATT_QB


def _qkv_pad_kernel(n_pad, x_ref, w_ref, o_ref):
    r = pl.program_id(1)

    @pl.when(r < n_pad)
    def _():
        o_ref[...] = jnp.zeros_like(o_ref)

    @pl.when(r >= n_pad)
    def _():
        y = jnp.dot(x_ref[...].astype(BF16), w_ref[...], preferred_element_type=F32).astype(BF16)
        qk = 2 * D_MODEL
        o_ref[:, :qk] = y[:, :qk]
        ones = jnp.ones((y.shape[0], PAIR), BF16)
        for hp in range(C_HEADS // 2):
            o_ref[:, qk + 2 * hp * PAIR:qk + (2 * hp + 1) * PAIR] = y[:, qk + hp * PAIR:qk + (hp + 1) * PAIR]
            o_ref[:, qk + (2 * hp + 1) * PAIR:qk + (2 * hp + 2) * PAIR] = ones


def _qkv_padded(x, w, o):
    Bsz, T, D = x.shape
    N = w.shape[2]
    n_out = N + D
    tm = next(t for t in (TM_PROJ, 512, 256, 128) if T % t == 0)
    n_pad = ATT_PAD // tm
    return pl.pallas_call(
        functools.partial(_qkv_pad_kernel, n_pad),
        grid=(Bsz, T // tm + n_pad),
        in_specs=[pl.BlockSpec((None, tm, D), lambda b, r: (b, jnp.maximum(r - n_pad, 0), 0)),
                  pl.BlockSpec((None, D, N), lambda b, r: (o, 0, 0), pipeline_mode=pl.Buffered(1))],
        out_specs=pl.BlockSpec((None, tm, n_out), lambda b, r: (b, r, 0)),
        out_shape=jax.ShapeDtypeStruct((Bsz, ATT_PAD + T, n_out), BF16),
        compiler_params=_params("parallel", "arbitrary"),
    )(x, w)


def _kv_tail_kernel(x_ref, wt_ref, o_ref):
    o_ref[...] = _dg(wt_ref[...], x_ref[...].astype(BF16), NT)


def _kv_tail(x, w_t, o, keep):
    Bsz, T, D = x.shape
    tm = next(t for t in (512, 256, 128) if keep % t == 0 and T % t == 0)
    first = (T - keep) // tm
    return pl.pallas_call(
        _kv_tail_kernel,
        grid=(Bsz, keep // tm, 2),
        in_specs=[pl.BlockSpec((None, tm, D), lambda b, r, j: (b, first + r, 0)),
                  pl.BlockSpec((None, D, D), lambda b, r, j: (o, j, 0))],
        out_specs=pl.BlockSpec((None, None, D, tm), lambda b, r, j: (b, j, 0, r)),
        out_shape=jax.ShapeDtypeStruct((Bsz, 2, D, keep), F32),
        compiler_params=_params("parallel", "parallel", "arbitrary"),
    )(x, w_t)


def _att_prompt_kernel(n_qb, q_ref, k_ref, v_ref, bias_ref, o_ref):
    g = pl.program_id(2)
    m0 = lax.broadcasted_iota(jnp.int32, (ATT_QB, PAIR), 1) < HALF
    col = lax.broadcasted_iota(jnp.int32, (2 * ATT_QB, ATT_WIN), 1)

    def run(window_has_padding):
        blocks = range(n_qb)
        rows = [slice(t * ATT_QB, (t + 1) * ATT_QB) for t in blocks]
        wins = [pl.ds(pl.multiple_of(ATT_PAD - C_PAST_ROWS + (g * n_qb + t) * ATT_QB, ATT_QB), ATT_WIN)
                for t in blocks]
        q = [_stack_masked(q_ref[rows[t], :], m0) for t in blocks]
        s = [_dg(q[t], k_ref[wins[t], :], NT) + bias_ref[...] for t in blocks]
        if window_has_padding:
            s = [jnp.where(col >= C_PAST_ROWS - (g * n_qb + t) * ATT_QB, s[t], NEG_INF) for t in blocks]
        p = [jnp.exp2(x - x.max(-1, keepdims=True)) for x in s]
        pv = [_dg(p[t].astype(BF16), v_ref[wins[t], :], NN) for t in blocks]
        o = [x[:, :PAIR] / x[:, PAIR:] for x in pv]
        for t in blocks:
            o_ref[rows[t], :] = jnp.where(m0, o[t][:ATT_QB], o[t][ATT_QB:]).astype(o_ref.dtype)

    first = g * (n_qb * ATT_QB) < C_PAST_ROWS
    pl.when(first)(lambda: run(True))
    pl.when(jnp.logical_not(first))(lambda: run(False))


def _attention_prompt(qkv, bias):
    Bsz, Tp, _ = qkv.shape
    T = Tp - ATT_PAD
    nhp = C_HEADS // 2
    nblk = T // ATT_QB
    n_qb = next(n for n in (ATT_MAX_QB, 4, 2, 1) if nblk % n == 0)
    rows = n_qb * ATT_QB
    skip = ATT_PAD // rows
    return pl.pallas_call(
        functools.partial(_att_prompt_kernel, n_qb),
        grid=(Bsz, nhp, T // rows),
        in_specs=[pl.BlockSpec((None, rows, LANE), lambda b, hp, g: (b, g + skip, hp)),
                  pl.BlockSpec((None, Tp, LANE), lambda b, hp, g: (b, 0, nhp + hp)),
                  pl.BlockSpec((None, Tp, 2 * LANE), lambda b, hp, g: (b, 0, nhp + hp)),
                  pl.BlockSpec((None, 2 * ATT_QB, ATT_WIN), lambda b, hp, g: (hp, 0, 0))],
        out_specs=pl.BlockSpec((None, rows, LANE), lambda b, hp, g: (b, g, hp)),
        out_shape=jax.ShapeDtypeStruct((Bsz, T, D_MODEL), BF16),
        compiler_params=_params("parallel", "parallel", "arbitrary"),
    )(qkv, qkv, qkv, bias)


def _att_sample_kernel(qkv_ref, kc_ref, vc_ref, bias_ref, o_ref):
    R = kc_ref.shape[2]
    heads = range(C_HEADS)
    col = lambda part, h: slice(part * D_MODEL + h * C_HEAD_DIM, part * D_MODEL + (h + 1) * C_HEAD_DIM)
    q = [(qkv_ref[:, col(0, h)] * ATT_SCALE).astype(BF16) for h in heads]
    s_c = [_dg(q[h], kc_ref[h].astype(BF16), NN) + bias_ref[h, :, 0:R] for h in heads]
    s_n = [_dg(q[h], qkv_ref[:, col(1, h)].astype(BF16), NT) + bias_ref[h, :, R:] for h in heads]
    m = [jnp.maximum(s_c[h].max(-1, keepdims=True), s_n[h].max(-1, keepdims=True)) for h in heads]
    p_c = [jnp.exp(s_c[h] - m[h]) for h in heads]
    p_n = [jnp.exp(s_n[h] - m[h]) for h in heads]
    acc = [_dg(p_c[h].astype(BF16), vc_ref[h].astype(BF16), NT)
           + _dg(p_n[h].astype(BF16), qkv_ref[:, col(2, h)].astype(BF16), NN) for h in heads]
    for h in heads:
        den = p_c[h].sum(-1, keepdims=True) + p_n[h].sum(-1, keepdims=True)
        o_ref[:, col(0, h)] = (acc[h] / den).astype(o_ref.dtype)


def _attention_sample(qkv, cache_k, cache_v, bias, o):
    Bsz, T, _ = qkv.shape
    R = cache_k.shape[4]
    cache = pl.BlockSpec((None, None, C_HEADS, C_HEAD_DIM, R), lambda b: (o, b, 0, 0, 0))
    return pl.pallas_call(
        _att_sample_kernel,
        grid=(Bsz,),
        in_specs=[pl.BlockSpec((None, T, 3 * D_MODEL), lambda b: (b, 0, 0)), cache, cache,
                  pl.BlockSpec((C_HEADS, T, R + T), lambda b: (0, 0, 0))],
        out_specs=pl.BlockSpec((None, T, D_MODEL), lambda b: (b, 0, 0)),
        out_shape=jax.ShapeDtypeStruct((Bsz, T, D_MODEL), BF16),
        compiler_params=_params("parallel"),
    )(qkv, cache_k, cache_v, bias)


def _rel_bias_tile(table, n_q, n_k, q_offset):
    period = n_q + n_k - 1
    m = np.arange(period)
    d = np.where(m < n_k, m, m - period)
    idx = np.clip(q_offset - d, -C_REL_CLIP, C_REL_CLIP) + C_REL_CLIP
    diag = table[:, idx]
    flat = jnp.tile(diag, (1, n_q))[:, :n_q * (period - 1)]
    return flat.reshape(table.shape[0], n_q, period - 1)[:, :, :n_k]


TM = 512
TM_PROJ = 1024


def _trunk(x, wkv0, shift0, gla0, cache_k, cache_v, P):
    Bsz, T, _ = x.shape
    M = Bsz * T
    xf = x.reshape(M, D_MODEL)
    wkv_o, shift_o, gla_o, k_o, v_o = [], [], [], [], []
    for layer in range(DEPTH):
        if layer % 2 == 0:
            e = layer // 2
            pa, pb = _in_proj(xf, P['w_in_a'], P['w_in_b'], e, TM_PROJ)
            pa = pa.reshape(Bsz, T, A_PROJ)
            pb = pb.reshape(Bsz, T, B_PROJ_PAD)
            ya, wkv = _rwkv_mix(pa, shift0[e], wkv0[e], {k: v[e] for k, v in P['a'].items()})
            yb, gs = _gla_mix(pb, gla0[e], {k: v[e] for k, v in P['b'].items()})
            wkv_o.append(wkv)
            shift_o.append(pa[:, -1])
            gla_o.append(gs)
            pairs = [(ya.reshape(M, A_WIDTH), P['w_out_a'], e), (yb.reshape(M, B_VWIDTH), P['w_out_b'], e)]
        else:
            o = layer // 2
            if cache_k is None:
                x3 = xf.reshape(Bsz, T, D_MODEL)
                att = _attention_prompt(_qkv_padded(x3, P['c_w_qkv_scaled'], o), P['c_bias_prompt'][o])
                keep = min(C_PAST_ROWS, T)
                rows_t = _kv_tail(x3, P['c_w_kv_t'], o, keep).reshape(Bsz, 2, C_HEADS, C_HEAD_DIM, keep)
                k_o.append(jnp.swapaxes(rows_t[:, 0], -1, -2))
                v_o.append(jnp.swapaxes(rows_t[:, 1], -1, -2))
            else:
                qkv = _matmul(xf, P['c_w_qkv'][o], TM, D_MODEL).reshape(Bsz, T, 3 * D_MODEL)
                att = _attention_sample(qkv, jnp.swapaxes(cache_k, -1, -2), jnp.swapaxes(cache_v, -1, -2),
                                        P['c_bias_sample'][o], o)
                rows = qkv.reshape(Bsz, T, 3, C_HEADS, C_HEAD_DIM)
                k_o.append(rows[:, :, 1].transpose(0, 2, 1, 3))
                v_o.append(rows[:, :, 2].transpose(0, 2, 1, 3))
            pairs = [(att.reshape(M, D_MODEL), P['c_w_o'], o)]
        xf = _mix_ffn(pairs, xf, P['ln1_w'], P['ln1_b'], P['ffn_w_in'], P['ffn_w_out'], P['ln2_w'], P['ln2_b'],
                      layer, TM)
    return (xf.reshape(Bsz, T, D_MODEL), jnp.stack(wkv_o), jnp.stack(shift_o), jnp.stack(gla_o),
            jnp.stack(k_o), jnp.stack(v_o))


def _prepare(w_in_mix, a_mu, a_w0, a_w2, a_a0, a_a2, a_g2, a_k_k, a_k_a, a_r_k, a_ln_w, a_ln_b,
             b_alpha_up, b_alpha_bias, b_norm_w, w_out_mix, c_w_qkv, c_rel_bias, c_w_o,
             ln1_w, ln1_b, ln2_w, ln2_b, ffn_w_in, ffn_w_out, sample_len, cache_rows):
    kw, vw = B_KWIDTH, B_VWIDTH
    wb = w_in_mix[:, :, A_PROJ:]
    main = jnp.concatenate([wb[:, :, :2 * kw + vw], wb[:, :, 2 * kw + vw + B_GATE_LORA:]], axis=-1)
    lora = jnp.pad(wb[:, :, 2 * kw + vw:2 * kw + vw + B_GATE_LORA], ((0, 0), (0, 0), (0, B_LORA_PAD - B_GATE_LORA)))
    row3 = lambda t: t.reshape(t.shape[0], 1, -1)
    bias_prompt = jnp.stack([_rel_bias_tile(t, ATT_QB, ATT_WIN, C_PAST_ROWS) for t in c_rel_bias])
    rel = np.arange(ATT_WIN)[None, :] // CHUNK - np.arange(ATT_QB)[:, None] // CHUNK
    band = (rel >= 0) & (rel <= C_PAST_CHUNKS)
    bias_prompt = jnp.where(band, bias_prompt * LOG2E, NEG_INF)
    q_scale = jnp.concatenate([jnp.full((D_MODEL,), ATT_SCALE * LOG2E, F32), jnp.ones((2 * D_MODEL,), F32)])
    P = {
        'w_in_a': w_in_mix[:, :, :A_PROJ].astype(BF16),
        'w_in_b': jnp.concatenate([main, lora], axis=-1).astype(BF16),
        'a': dict(mu=row3(a_mu), w0=row3(a_w0), w2=a_w2, a0=row3(a_a0), a2=a_a2, g2=a_g2, k_k=row3(a_k_k),
                  k_a=row3(a_k_a), r_k=row3(a_r_k), ln_w=row3(a_ln_w), ln_b=row3(a_ln_b)),
        'b': dict(up=jnp.pad(b_alpha_up, ((0, 0), (0, B_LORA_PAD - B_GATE_LORA), (0, 0))),
                  bias=row3(b_alpha_bias), nw=row3(b_norm_w)),
        'w_out_a': w_out_mix[:, :A_WIDTH].astype(BF16),
        'w_out_b': w_out_mix[:, A_WIDTH:].astype(BF16),
        'c_w_qkv': c_w_qkv.astype(BF16),
        'c_w_qkv_scaled': (c_w_qkv * q_scale).astype(BF16),
        'c_w_kv_t': jnp.swapaxes(c_w_qkv[:, :, D_MODEL:], 1, 2).astype(BF16),
        'c_w_o': c_w_o.astype(BF16),
        'c_bias_prompt': bias_prompt.reshape(bias_prompt.shape[0], C_HEADS // 2, 2 * ATT_QB, ATT_WIN),
        'c_bias_sample': jnp.stack([_rel_bias_tile(t, sample_len, cache_rows + sample_len, cache_rows)
                                    for t in c_rel_bias]),
        'ln1_w': row3(ln1_w), 'ln1_b': row3(ln1_b), 'ln2_w': row3(ln2_w), 'ln2_b': row3(ln2_b),
        'ffn_w_in': ffn_w_in.astype(BF16),
        'ffn_w_out': ffn_w_out.astype(BF16),
    }
    return P


def kernel(x_prompt, x_sample, state_a_wkv, state_a_shift, state_b_gla, cache_c_k, cache_c_v, w_in_mix, a_mu, a_w0, a_w2, a_a0, a_a2, a_g2, a_k_k, a_k_a, a_r_k, a_ln_w, a_ln_b, b_alpha_up, b_alpha_bias, b_norm_w, w_out_mix, c_w_qkv, c_rel_bias, c_w_o, ln1_w, ln1_b, ln2_w, ln2_b, ffn_w_in, ffn_w_out):
    P = _prepare(w_in_mix, a_mu, a_w0, a_w2, a_a0, a_a2, a_g2, a_k_k, a_k_a, a_r_k, a_ln_w, a_ln_b,
                 b_alpha_up, b_alpha_bias, b_norm_w, w_out_mix, c_w_qkv, c_rel_bias, c_w_o,
                 ln1_w, ln1_b, ln2_w, ln2_b, ffn_w_in, ffn_w_out, x_sample.shape[1], cache_c_k.shape[3])
    bp = x_prompt.shape[0]
    dt = x_prompt.dtype
    n_even = state_a_wkv.shape[0]
    wkv_zero = jnp.zeros((n_even, bp, A_HEADS, A_HEAD_DIM, A_HEAD_DIM), dt)
    shift_zero = jnp.zeros((n_even, bp, A_PROJ), dt)
    gla_zero = jnp.zeros((n_even, bp, B_HEADS, B_KEY_DIM, B_VAL_DIM), dt)
    y_prompt, p_wkv, p_shift, p_gla, p_k, p_v = _trunk(x_prompt, wkv_zero, shift_zero, gla_zero, None, None, P)
    y_sample, s_wkv, s_shift, s_gla, s_k, s_v = _trunk(
        x_sample, state_a_wkv, state_a_shift, state_b_gla, cache_c_k, cache_c_v, P)
    return (y_prompt, y_sample, p_wkv, p_shift, p_gla, p_k, p_v, s_wkv, s_shift, s_gla, s_k, s_v)
```

```python
import functools

import jax
import jax.numpy as jnp
import numpy as np
from jax import lax
from jax.experimental import pallas as pl
from jax.experimental.pallas import tpu as pltpu

F32 = jnp.float32
BF16 = jnp.bfloat16

D_MODEL = 1024
DEPTH = 4
CHUNK = 64
A_WIDTH = 512
A_HEAD_DIM = 64
A_HEADS = 8
A_DECAY_LORA = 64
A_ICL_LORA = 64
A_GATE_LORA = 128
A_PROJ = 3 * A_WIDTH + A_DECAY_LORA + A_ICL_LORA + A_GATE_LORA
A_NORM_EPS = 64e-5
B_VWIDTH = 512
B_KWIDTH = 256
B_HEADS = 4
B_KEY_DIM = 64
B_VAL_DIM = 128
B_GATE_LORA = 16
B_GATE_NORM = 16.0
B_LORA_PAD = 128
B_PROJ_PAD = 2 * B_KWIDTH + 2 * B_VWIDTH + B_LORA_PAD
C_HEAD_DIM = 64
C_HEADS = 16
C_PAST_CHUNKS = 8
C_PAST_ROWS = C_PAST_CHUNKS * CHUNK
C_REL_CLIP = 128
FFN_HIDDEN = 2816
LN_EPS = 1e-5
DEEPNORM_ALPHA = (2.0 * DEPTH) ** 0.25
NEG_INF = -1e30

LANE = 128
SUBLANE = 8
BF16_ROWS = 2 * SUBLANE
PAIR = LANE
HALF = PAIR // 2
A_PAIRS = A_HEADS // 2
B_PAIRS = B_HEADS // 2
VMEM_LIMIT = 56 * 1024 * 1024

NN = (((1,), (0,)), ((), ()))
NT = (((1,), (1,)), ((), ()))
TN = (((0,), (0,)), ((), ()))


def _params(*sem):
    return pltpu.CompilerParams(dimension_semantics=sem, vmem_limit_bytes=VMEM_LIMIT)


def _dg(a, b, dn):
    return lax.dot_general(a, b, dn, preferred_element_type=F32)


def _dot1(a, b, dn=NN):
    return _dg(a.astype(BF16), b.astype(BF16), dn)


def _split(a):
    hi = a.astype(BF16)
    lo = (a - hi.astype(F32)).astype(BF16)
    return hi, lo


def _dot3s(a, b, dn=NN):
    return _dg(a[0], b[0], dn) + _dg(a[0], b[1], dn) + _dg(a[1], b[0], dn)


def _dot3(a, b, dn=NN):
    return _dot3s(_split(a), _split(b), dn)


def _dot_xl(a_exact, b, dn=NN):
    bh, bl = _split(b)
    return _dg(a_exact, bh, dn) + _dg(a_exact, bl, dn)


def _dot_xr(a, b_exact, dn=NN):
    ah, al = _split(a)
    return _dg(ah, b_exact, dn) + _dg(al, b_exact, dn)


def _ln(z, w, b):
    mu = jnp.mean(z, -1, keepdims=True)
    d = z - mu
    var = jnp.mean(d * d, -1, keepdims=True)
    return d * lax.rsqrt(var + LN_EPS) * w + b


def _sigmoid(x):
    return 1.0 / (1.0 + jnp.exp(-x))


def _softplus(x):
    return jnp.maximum(x, 0.0) + jnp.log(1.0 + jnp.exp(-jnp.abs(x)))


def _stack_masked(x, m0):
    return jnp.concatenate([jnp.where(m0, x, 0.0), jnp.where(m0, 0.0, x)], axis=0)


def _fold(x):
    n = x.shape[0] // 2
    return x[:n] + x[n:]


def _mm_kernel(x_ref, w_ref, o_ref):
    o_ref[...] = jnp.dot(x_ref[...].astype(BF16), w_ref[...], preferred_element_type=F32)


def _matmul(x, w, tm, tn):
    M, K = x.shape
    N = w.shape[1]
    tm = next(t for t in (tm, 256, 128, 64) if M % t == 0)
    return pl.pallas_call(
        _mm_kernel,
        grid=(M // tm, N // tn),
        in_specs=[pl.BlockSpec((tm, K), lambda i, j: (i, 0)),
                  pl.BlockSpec((K, tn), lambda i, j: (0, j))],
        out_specs=pl.BlockSpec((tm, tn), lambda i, j: (i, j)),
        out_shape=jax.ShapeDtypeStruct((M, N), F32),
        compiler_params=_params("parallel", "arbitrary"),
    )(x, w)


def _in_proj_kernel(x_ref, wa_ref, wb_ref, oa_ref, ob_ref):
    xb = x_ref[...].astype(BF16)
    oa_ref[...] = jnp.dot(xb, wa_ref[...], preferred_element_type=F32)
    ob_ref[...] = jnp.dot(xb, wb_ref[...], preferred_element_type=F32)


def _in_proj(x, wa, wb, e, tm):
    M, K = x.shape
    tm = next(t for t in (tm, 512, 256, 128, 64) if M % t == 0)
    na, nb = wa.shape[2], wb.shape[2]
    resident = lambda n: pl.BlockSpec((None, K, n), lambda i: (e, 0, 0), pipeline_mode=pl.Buffered(1))
    return pl.pallas_call(
        _in_proj_kernel,
        grid=(M // tm,),
        in_specs=[pl.BlockSpec((tm, K), lambda i: (i, 0)), resident(na), resident(nb)],
        out_specs=[pl.BlockSpec((tm, na), lambda i: (i, 0)), pl.BlockSpec((tm, nb), lambda i: (i, 0))],
        out_shape=[jax.ShapeDtypeStruct((M, na), F32), jax.ShapeDtypeStruct((M, nb), F32)],
        compiler_params=_params("parallel"),
    )(x, wa, wb)


def _mix_ffn_kernel(n_pairs, *refs):
    res_ref, lw1_ref, lb1_ref, wg_ref, wu_ref, wo_ref, lw2_ref, lb2_ref, o_ref = refs[2 * n_pairs:]
    tm = o_ref.shape[0]
    n_half = 2 if tm % (2 * BF16_ROWS) == 0 else 1
    halves = [slice(i * (tm // n_half), (i + 1) * (tm // n_half)) for i in range(n_half)]
    mm = lambda a, w: jnp.dot(a, w, preferred_element_type=F32)
    acc = [mm(refs[0][h, :].astype(BF16), refs[1][...]) for h in halves]
    for p in range(1, n_pairs):
        acc = [acc[i] + mm(refs[2 * p][h, :].astype(BF16), refs[2 * p + 1][...]) for i, h in enumerate(halves)]
    x = [_ln(DEEPNORM_ALPHA * res_ref[h, :] + acc[i], lw1_ref[...], lb1_ref[...]) for i, h in enumerate(halves)]
    xb = [v.astype(BF16) for v in x]
    g = [mm(v, wg_ref[...]) for v in xb]
    u = [mm(v, wu_ref[...]) for v in xb]
    act = [(g[i] * _sigmoid(g[i]) * u[i]).astype(BF16) for i in range(n_half)]
    y = [mm(v, wo_ref[...]) for v in act]
    for i, h in enumerate(halves):
        o_ref[h, :] = _ln(DEEPNORM_ALPHA * x[i] + y[i], lw2_ref[...], lb2_ref[...])


def _mix_ffn(pairs, res, lw1, lb1, w_in, w_out, lw2, lb2, layer, tm):
    M = res.shape[0]
    tm = next(t for t in (tm, 256, 128, 64) if M % t == 0)
    resident = lambda shape, idx: pl.BlockSpec((None,) + shape, idx, pipeline_mode=pl.Buffered(1))
    vec = pl.BlockSpec((None, 1, D_MODEL), lambda i: (layer, 0, 0))
    in_specs, args = [], []
    for a, w, n in pairs:
        K = a.shape[1]
        in_specs += [pl.BlockSpec((tm, K), lambda i: (i, 0)), resident((K, D_MODEL), lambda i, n=n: (n, 0, 0))]
        args += [a, w]
    in_specs += [pl.BlockSpec((tm, D_MODEL), lambda i: (i, 0)), vec, vec,
                 resident((D_MODEL, FFN_HIDDEN), lambda i: (layer, 0, 0)),
                 resident((D_MODEL, FFN_HIDDEN), lambda i: (layer, 0, 1)),
                 resident((FFN_HIDDEN, D_MODEL), lambda i: (layer, 0, 0)), vec, vec]
    return pl.pallas_call(
        functools.partial(_mix_ffn_kernel, len(pairs)),
        grid=(M // tm,),
        in_specs=in_specs,
        out_specs=pl.BlockSpec((tm, D_MODEL), lambda i: (i, 0)),
        out_shape=jax.ShapeDtypeStruct((M, D_MODEL), F32),
        compiler_params=_params("parallel"),
    )(*args, res, lw1, lb1, w_in, w_in, w_out, lw2, lb2)


def _inv_unit_lower(ns, eye, nilpotency):
    size = eye.shape[0]
    xs = [eye - n for n in ns]
    ps = [_dot1(n, n) for n in ns]
    k = 2
    while k < nilpotency:
        k *= 2
        if k < nilpotency:
            rs = [_dot1(p, jnp.concatenate([x, p], axis=1)) for x, p in zip(xs, ps)]
            xs = [x + r[:, :size] for x, r in zip(xs, rs)]
            ps = [r[:, size:] for r in rs]
        else:
            xs = [x + _dot1(p, x) for x, p in zip(xs, ps)]
    return xs


def _segsum(x, seg, split=False):
    n = x.shape[1] // PAIR
    rows = x.shape[0]
    xs = jnp.concatenate([x[:, j * PAIR:(j + 1) * PAIR] for j in range(n)], axis=0)
    s = _dot_xr(xs, seg) if split else _dot1(xs, seg)
    return jnp.concatenate([s[j * rows:(j + 1) * rows] for j in range(n)], axis=1)


def _head_mask():
    hr = lax.broadcasted_iota(jnp.int32, (PAIR, PAIR), 0) // A_HEAD_DIM
    hc = lax.broadcasted_iota(jnp.int32, (PAIR, PAIR), 1) // A_HEAD_DIM
    return (hr == hc).astype(BF16)


def _rwkv_prep_kernel(n_chunks, pa_ref, prev_ref, shift_ref, mu_ref, w0_ref, w2_ref, a0_ref, a2_ref, g2_ref,
                      kk_ref, ka_ref, rk_ref,
                      mh_ref, ch_ref, qc_ref, oc_ref, g_ref, bonus_ref):
    i = pl.program_id(1)
    L = CHUNK
    pa = pa_ref[...]
    first = jnp.where(i == 0, shift_ref[...], prev_ref[SUBLANE - 1:SUBLANE, :])
    rows = lax.broadcasted_iota(jnp.int32, pa.shape, 0)
    prev = jnp.where(rows == 0, first, pltpu.roll(pa, shift=1, axis=0))
    xs = pa + (prev - pa) * mu_ref[...]
    W = A_WIDTH
    r = xs[:, 0:W]
    k0 = xs[:, W:2 * W]
    v = xs[:, 2 * W:3 * W]
    xw = xs[:, 3 * W:3 * W + A_DECAY_LORA]
    xa = xs[:, 3 * W + A_DECAY_LORA:3 * W + A_DECAY_LORA + A_ICL_LORA]
    xg = xs[:, 3 * W + A_DECAY_LORA + A_ICL_LORA:]
    w = -_softplus(-(w0_ref[...] + _dot3(jnp.tanh(xw), w2_ref[...]))) - 0.5
    logw = -jnp.exp(w)
    a = _sigmoid(a0_ref[...] + _dot1(xa, a2_ref[...]))
    g_ref[...] = _dot1(_sigmoid(xg), g2_ref[...])
    seg = _head_mask()
    kk = k0 * kk_ref[...]
    kk = kk * lax.rsqrt(_segsum(kk * kk, seg) + 1e-12)
    k = k0 * (1.0 + (a - 1.0) * ka_ref[...])
    bonus_ref[...] = _segsum(r * k * rk_ref[...], seg) * v
    b = kk * a

    tri = (lax.broadcasted_iota(jnp.int32, (L, L), 0) >= lax.broadcasted_iota(jnp.int32, (L, L), 1)).astype(BF16)
    row = lax.broadcasted_iota(jnp.int32, (PAIR, PAIR), 0)
    col = lax.broadcasted_iota(jnp.int32, (PAIR, PAIR), 1)
    eye = (row == col).astype(F32)
    strict = row > col
    incl2 = (lax.broadcasted_iota(jnp.int32, (PAIR, 2 * PAIR), 0)
             >= lax.broadcasted_iota(jnp.int32, (PAIR, 2 * PAIR), 1) % PAIR)
    m0 = lax.broadcasted_iota(jnp.int32, (L, PAIR), 1) < HALF
    pairs = range(A_PAIRS)
    lanes = [slice(j * PAIR, (j + 1) * PAIR) for j in pairs]
    chains = [(c, j) for c in range(n_chunks) for j in pairs]
    every = range(len(chains))
    rows_of = [slice(c * L, (c + 1) * L) for c in range(n_chunks)]
    kkg, rg, kinv, binv, kd, bd, e_last = [], [], [], [], [], [], []
    for sl in rows_of:
        lw = logw[sl]
        cum = _dot_xl(tri, lw)
        c_last = cum[L - 1:L, :]
        e_neg = jnp.exp(-cum)
        e_rem = jnp.exp(c_last - cum)
        kkg.append(kk[sl] * jnp.exp(cum - lw))
        rg.append(r[sl] * jnp.exp(cum))
        kinv.append(k[sl] * e_neg)
        binv.append(b[sl] * e_neg)
        kd.append(k[sl] * e_rem)
        bd.append(b[sl] * e_rem)
        e_last.append(jnp.exp(c_last))
    sm = lambda xs_, c, j: _stack_masked(xs_[c][:, lanes[j]].astype(BF16), m0)
    kkg_b = [sm(kkg, c, j) for c, j in chains]
    v_b = [_stack_masked(v[rows_of[c], lanes[j]].astype(BF16), m0) for c, j in chains]
    bd_b = [sm(bd, c, j) for c, j in chains]
    kd_b = [sm(kd, c, j) for c, j in chains]
    rg_b = [sm(rg, c, j) for c, j in chains]
    kinv_b = [sm(kinv, c, j) for c, j in chains]
    binv_b = [sm(binv, c, j) for c, j in chains]
    aa = [_dg(jnp.concatenate([kkg_b[n], rg_b[n]], axis=0), jnp.concatenate([binv_b[n], kinv_b[n]], axis=0), NT)
          for n in every]
    a_kb = [jnp.where(strict, x[:PAIR, :PAIR], 0.0) for x in aa]
    a_kk = [jnp.where(strict, x[:PAIR, PAIR:], 0.0).astype(BF16) for x in aa]
    a_rbk = [jnp.where(incl2, x[PAIR:, :], 0.0).astype(BF16) for x in aa]
    t = _inv_unit_lower(a_kb, eye, L)
    av = [_dg(a_kk[n], v_b[n], NN).astype(BF16) for n in every]
    wu = [_dg(t[n].astype(BF16), jnp.concatenate([kkg_b[n], av[n]], axis=1), NN) for n in every]
    wu_b = [x.astype(BF16) for x in wu]
    mc = [_dg(bd_b[n], wu_b[n], TN) for n in every]
    kv = [_dg(kd_b[n], v_b[n], TN) for n in every]
    rhs = [jnp.concatenate([jnp.concatenate([wu_b[n][:, :PAIR], (-wu[n][:, PAIR:]).astype(BF16)], axis=1),
                            jnp.concatenate([jnp.zeros_like(v_b[n]), v_b[n]], axis=1)], axis=0) for n in every]
    qo = [_dg(a_rbk[n], rhs[n], NN) for n in every]
    for n, (c, j) in enumerate(chains):
        mh_ref[c, j] = (eye * e_last[c][:, lanes[j]] - mc[n][:, :PAIR]).astype(BF16)
        ch_ref[c, j] = kv[n] - mc[n][:, PAIR:]
        qc_ref[rows_of[c], lanes[j]] = (rg[c][:, lanes[j]] - _fold(qo[n][:, :PAIR])).astype(BF16)
        oc_ref[rows_of[c], lanes[j]] = _fold(qo[n][:, PAIR:])


def _rwkv_scan_kernel(n_seq, n_chunks, mh_ref, ch_ref, qc_ref, oc_ref, g_ref, bonus_ref, h0_ref, lnw_ref, lnb_ref,
                      y_ref, hout_ref, h_ref):
    i = pl.program_id(1)
    L = CHUNK

    @pl.when(i == 0)
    def _():
        h_ref[...] = h0_ref[...]

    seg = _head_mask()
    inv_n = 1.0 / A_HEAD_DIM
    lanes = [slice(j * PAIR, (j + 1) * PAIR) for j in range(A_PAIRS)]
    chains = [(s, j) for s in range(n_seq) for j in range(A_PAIRS)]
    hb = [h_ref[s, j] for s, j in chains]
    for c in range(n_chunks):
        sl = slice(c * L, (c + 1) * L)
        hb_b = [x.astype(BF16) for x in hb]
        o = [_dg(qc_ref[s, sl, lanes[j]], hb_b[n], NN) + oc_ref[s, sl, lanes[j]] for n, (s, j) in enumerate(chains)]
        hb = [_dg(mh_ref[s, c, j], hb_b[n], NN) + ch_ref[s, c, j] for n, (s, j) in enumerate(chains)]
        for s in range(n_seq):
            os_ = jnp.concatenate(o[s * A_PAIRS:(s + 1) * A_PAIRS], axis=1)
            mean = _segsum(os_, seg, split=True) * inv_n
            d = os_ - mean
            var = _segsum(d * d, seg, split=True) * inv_n
            on = d * lax.rsqrt(var + A_NORM_EPS) * lnw_ref[...] + lnb_ref[...]
            y_ref[s, sl, :] = ((on + bonus_ref[s, sl, :]) * g_ref[s, sl, :]).astype(y_ref.dtype)
    for n, (s, j) in enumerate(chains):
        h_ref[s, j] = hb[n]
    hout_ref[...] = h_ref[...]


def _rwkv_mix(pa, shift0, wkv0, prm):
    Bsz, T, _ = pa.shape
    n1 = next(n for n in (8, 4, 2, 1) if (T // CHUNK) % n == 0)
    R1 = n1 * CHUNK
    W = A_WIDTH
    nblk = T // R1
    row = lambda n: pl.BlockSpec((1, n), lambda b, i: (0, 0))
    full = lambda s: pl.BlockSpec(s, lambda b, i: (0,) * len(s))
    act = lambda r, n: pl.BlockSpec((None, r, n), lambda b, i: (b, i, 0))
    mat = lambda n: pl.BlockSpec((None, n, A_PAIRS, PAIR, PAIR), lambda b, i: (b, i, 0, 0, 0))
    sub = R1 // SUBLANE
    mats = lambda dt: jax.ShapeDtypeStruct((Bsz, T // CHUNK, A_PAIRS, PAIR, PAIR), dt)
    acts = lambda dt: jax.ShapeDtypeStruct((Bsz, T, W), dt)
    mh, ch, qc, oc, g, bonus = pl.pallas_call(
        functools.partial(_rwkv_prep_kernel, n1),
        grid=(Bsz, nblk),
        in_specs=[act(R1, A_PROJ),
                  pl.BlockSpec((None, SUBLANE, A_PROJ), lambda b, i: (b, jnp.maximum(i * sub - 1, 0), 0)),
                  pl.BlockSpec((None, 1, A_PROJ), lambda b, i: (b, 0, 0)),
                  row(A_PROJ), row(W), full((A_DECAY_LORA, W)), row(W), full((A_ICL_LORA, W)),
                  full((A_GATE_LORA, W)), row(W), row(W), row(W)],
        out_specs=[mat(n1), mat(n1), act(R1, W), act(R1, W), act(R1, W), act(R1, W)],
        out_shape=[mats(BF16), mats(F32), acts(BF16), acts(F32), acts(F32), acts(F32)],
        compiler_params=_params("parallel", "parallel"),
    )(pa, pa, shift0[:, None, :], prm['mu'], prm['w0'], prm['w2'], prm['a0'], prm['a2'], prm['g2'],
      prm['k_k'], prm['k_a'], prm['r_k'])

    n2 = min(4, T // CHUNK)
    R2 = n2 * CHUNK
    hh = wkv0.transpose(0, 1, 3, 2).reshape(Bsz, A_PAIRS, 2, A_HEAD_DIM, A_HEAD_DIM)
    zero = jnp.zeros_like(hh[:, :, 0])
    h0 = jnp.concatenate([jnp.concatenate([hh[:, :, 0], zero], axis=-1),
                          jnp.concatenate([zero, hh[:, :, 1]], axis=-1)], axis=-2)
    ns = next(n for n in (4, 2, 1) if Bsz % n == 0)
    state = pl.BlockSpec((ns, A_PAIRS, PAIR, PAIR), lambda b, i: (b, 0, 0, 0))
    act2 = lambda r, n: pl.BlockSpec((ns, r, n), lambda b, i: (b, i, 0))
    mat2 = lambda n: pl.BlockSpec((ns, n, A_PAIRS, PAIR, PAIR), lambda b, i: (b, i, 0, 0, 0))
    y, hout = pl.pallas_call(
        functools.partial(_rwkv_scan_kernel, ns, n2),
        grid=(Bsz // ns, T // R2),
        in_specs=[mat2(n2), mat2(n2), act2(R2, W), act2(R2, W), act2(R2, W), act2(R2, W), state,
                  row(W), row(W)],
        out_specs=[act2(R2, W), state],
        out_shape=[acts(BF16), jax.ShapeDtypeStruct((Bsz, A_PAIRS, PAIR, PAIR), F32)],
        scratch_shapes=[pltpu.VMEM((ns, A_PAIRS, PAIR, PAIR), F32)],
        compiler_params=_params("parallel", "arbitrary"),
    )(mh, ch, qc, oc, g, bonus, h0, prm['ln_w'], prm['ln_b'])
    hd = jnp.stack([hout[:, :, :HALF, :HALF], hout[:, :, HALF:, HALF:]], axis=2)
    wkv = hd.reshape(Bsz, A_HEADS, A_HEAD_DIM, A_HEAD_DIM).transpose(0, 1, 3, 2)
    return y, wkv


def _gla_kernel(n_seq, n_chunks, pb_ref, s0_ref, up_ref, bias_ref, nw_ref, y_ref, sout_ref, s_ref):
    i = pl.program_id(1)
    L = CHUNK

    @pl.when(i == 0)
    def _():
        s_ref[...] = s0_ref[...]

    kw, vw = B_KWIDTH, B_VWIDTH
    incl = lax.broadcasted_iota(jnp.int32, (L, L), 0) >= lax.broadcasted_iota(jnp.int32, (L, L), 1)
    tri = incl.astype(BF16)
    m0 = lax.broadcasted_iota(jnp.int32, (L, PAIR), 1) < HALF
    top = lax.broadcasted_iota(jnp.int32, (PAIR, B_VAL_DIM), 0) < HALF
    scale = B_KEY_DIM ** -0.5
    pairs = range(B_PAIRS)
    heads = range(B_HEADS)
    pl_ = lambda h: slice((h // 2) * PAIR, (h // 2 + 1) * PAIR)
    units = [(s, c) for c in range(n_chunks) for s in range(n_seq)]
    rows_of = lambda c: slice(c * L, (c + 1) * L)
    vslice = lambda h: slice(h * B_VAL_DIM, (h + 1) * B_VAL_DIM)
    z = [_dot3(pb_ref[s, rows_of(c), 2 * kw + 2 * vw:], up_ref[...]) + bias_ref[...] for s, c in units]
    bc = [_dot_xl(tri, -_softplus(-x) * (1.0 / B_GATE_NORM)) for x in z]
    b_last = [x[L - 1:L, :] for x in bc]
    q_dec, k_inv, k_state, e_last_t = [], [], [], []
    for n, (s, c) in enumerate(units):
        k = pb_ref[s, rows_of(c), kw:2 * kw]
        q_dec.append(pb_ref[s, rows_of(c), 0:kw] * scale * jnp.exp(bc[n]))
        k_inv.append((k * jnp.exp(-bc[n])).astype(BF16))
        k_state.append((k * jnp.exp(b_last[n] - bc[n])).astype(BF16))
        e_last_t.append(jnp.broadcast_to(jnp.exp(b_last[n]), (LANE, kw)).T)
    uh = [(n, h) for n in range(len(units)) for h in heads]
    vb = [pb_ref[units[n][0], rows_of(units[n][1]), 2 * kw + h * B_VAL_DIM:2 * kw + (h + 1) * B_VAL_DIM].astype(BF16)
          for n, h in uh]
    qm = [jnp.where(m0 if h % 2 == 0 else ~m0, q_dec[n][:, pl_(h)], 0.0).astype(BF16) for n, h in uh]
    scores = [jnp.where(incl, _dg(qm[i_], k_inv[n][:, pl_(h)], NT), 0.0).astype(BF16) for i_, (n, h) in enumerate(uh)]
    upd = [_dg(k_state[n][:, pl_(h)], vb[i_], TN) for i_, (n, h) in enumerate(uh)]
    intra = [_dg(scores[i_], vb[i_], NN) for i_ in range(len(uh))]
    states = [[s_ref[s, j] for j in pairs] for s in range(n_seq)]
    before = []
    for n, (s, c) in enumerate(units):
        before.append([x.astype(BF16) for x in states[s]])
        states[s] = [states[s][j] * e_last_t[n][j * PAIR:(j + 1) * PAIR, :]
                     + jnp.where(top, upd[n * B_HEADS + 2 * j], upd[n * B_HEADS + 2 * j + 1]) for j in pairs]
    o = [_dg(qm[i_], before[n][h // 2], NN) + intra[i_] for i_, (n, h) in enumerate(uh)]
    for i_, (n, h) in enumerate(uh):
        s, c = units[n]
        oh = o[i_] * lax.rsqrt(jnp.mean(o[i_] * o[i_], -1, keepdims=True) + LN_EPS) * nw_ref[...]
        rg_h = pb_ref[s, rows_of(c), 2 * kw + vw + h * B_VAL_DIM:2 * kw + vw + (h + 1) * B_VAL_DIM]
        y_ref[s, rows_of(c), vslice(h)] = (oh * (rg_h * _sigmoid(rg_h))).astype(y_ref.dtype)
    for s in range(n_seq):
        for j in pairs:
            s_ref[s, j] = states[s][j]
    sout_ref[...] = s_ref[...]


def _gla_mix(pb, s0, prm):
    Bsz, T, _ = pb.shape
    n = min(4, T // CHUNK)
    Rr = n * CHUNK
    ns = next(n for n in (4, 2, 1) if Bsz % n == 0)
    st = pl.BlockSpec((ns, B_PAIRS, PAIR, B_VAL_DIM), lambda b, i: (b, 0, 0, 0))
    y, s = pl.pallas_call(
        functools.partial(_gla_kernel, ns, n),
        grid=(Bsz // ns, T // Rr),
        in_specs=[pl.BlockSpec((ns, Rr, B_PROJ_PAD), lambda b, i: (b, i, 0)), st,
                  pl.BlockSpec((B_LORA_PAD, B_KWIDTH), lambda b, i: (0, 0)),
                  pl.BlockSpec((1, B_KWIDTH), lambda b, i: (0, 0)),
                  pl.BlockSpec((1, B_VAL_DIM), lambda b, i: (0, 0))],
        out_specs=[pl.BlockSpec((ns, Rr, B_VWIDTH), lambda b, i: (b, i, 0)), st],
        out_shape=[jax.ShapeDtypeStruct((Bsz, T, B_VWIDTH), BF16),
                   jax.ShapeDtypeStruct((Bsz, B_PAIRS, PAIR, B_VAL_DIM), F32)],
        scratch_shapes=[pltpu.VMEM((ns, B_PAIRS, PAIR, B_VAL_DIM), F32)],
        compiler_params=_params("parallel", "arbitrary"),
    )(pb, s0.reshape(Bsz, B_PAIRS, PAIR, B_VAL_DIM), prm['up'], prm['bias'], prm['nw'])
    return y, s.reshape(Bsz, B_HEADS, B_KEY_DIM, B_VAL_DIM)


ATT_QB = 2 * CHUNK
ATT_SCALE = C_HEAD_DIM ** -0.5
LOG2E = 1.4426950408889634


ATT_WIN = C_PAST_ROWS + ATT_QB
ATT_MAX_QB = 32
ATT_PAD = ATT_MAX_QB * ATT_QB


def _qkv_pad_kernel(n_pad, x_ref, w_ref, o_ref):
    r = pl.program_id(1)

    @pl.when(r < n_pad)
    def _():
        o_ref[...] = jnp.zeros_like(o_ref)

    @pl.when(r >= n_pad)
    def _():
        y = jnp.dot(x_ref[...].astype(BF16), w_ref[...], preferred_element_type=F32).astype(BF16)
        qk = 2 * D_MODEL
        o_ref[:, :qk] = y[:, :qk]
        ones = jnp.ones((y.shape[0], PAIR), BF16)
        for hp in range(C_HEADS // 2):
            o_ref[:, qk + 2 * hp * PAIR:qk + (2 * hp + 1) * PAIR] = y[:, qk + hp * PAIR:qk + (hp + 1) * PAIR]
            o_ref[:, qk + (2 * hp + 1) * PAIR:qk + (2 * hp + 2) * PAIR] = ones


def _qkv_padded(x, w, o):
    Bsz, T, D = x.shape
    N = w.shape[2]
    n_out = N + D
    tm = next(t for t in (TM_PROJ, 512, 256, 128) if T % t == 0)
    n_pad = ATT_PAD // tm
    return pl.pallas_call(
        functools.partial(_qkv_pad_kernel, n_pad),
        grid=(Bsz, T // tm + n_pad),
        in_specs=[pl.BlockSpec((None, tm, D), lambda b, r: (b, jnp.maximum(r - n_pad, 0), 0)),
                  pl.BlockSpec((None, D, N), lambda b, r: (o, 0, 0), pipeline_mode=pl.Buffered(1))],
        out_specs=pl.BlockSpec((None, tm, n_out), lambda b, r: (b, r, 0)),
        out_shape=jax.ShapeDtypeStruct((Bsz, ATT_PAD + T, n_out), BF16),
        compiler_params=_params("parallel", "arbitrary"),
    )(x, w)


def _kv_tail_kernel(x_ref, wt_ref, o_ref):
    o_ref[...] = _dg(wt_ref[...], x_ref[...].astype(BF16), NT)


def _kv_tail(x, w_t, o, keep):
    Bsz, T, D = x.shape
    tm = next(t for t in (512, 256, 128) if keep % t == 0 and T % t == 0)
    first = (T - keep) // tm
    return pl.pallas_call(
        _kv_tail_kernel,
        grid=(Bsz, keep // tm, 2),
        in_specs=[pl.BlockSpec((None, tm, D), lambda b, r, j: (b, first + r, 0)),
                  pl.BlockSpec((None, D, D), lambda b, r, j: (o, j, 0))],
        out_specs=pl.BlockSpec((None, None, D, tm), lambda b, r, j: (b, j, 0, r)),
        out_shape=jax.ShapeDtypeStruct((Bsz, 2, D, keep), F32),
        compiler_params=_params("parallel", "parallel", "arbitrary"),
    )(x, w_t)


def _att_prompt_kernel(n_qb, q_ref, k_ref, v_ref, bias_ref, o_ref):
    g = pl.program_id(2)
    m0 = lax.broadcasted_iota(jnp.int32, (ATT_QB, PAIR), 1) < HALF
    col = lax.broadcasted_iota(jnp.int32, (2 * ATT_QB, ATT_WIN), 1)

    def run(window_has_padding):
        blocks = range(n_qb)
        rows = [slice(t * ATT_QB, (t + 1) * ATT_QB) for t in blocks]
        wins = [pl.ds(pl.multiple_of(ATT_PAD - C_PAST_ROWS + (g * n_qb + t) * ATT_QB, ATT_QB), ATT_WIN)
                for t in blocks]
        q = [_stack_masked(q_ref[rows[t], :], m0) for t in blocks]
        s = [_dg(q[t], k_ref[wins[t], :], NT) + bias_ref[...] for t in blocks]
        if window_has_padding:
            s = [jnp.where(col >= C_PAST_ROWS - (g * n_qb + t) * ATT_QB, s[t], NEG_INF) for t in blocks]
        p = [jnp.exp2(x - x.max(-1, keepdims=True)) for x in s]
        pv = [_dg(p[t].astype(BF16), v_ref[wins[t], :], NN) for t in blocks]
        o = [x[:, :PAIR] / x[:, PAIR:] for x in pv]
        for t in blocks:
            o_ref[rows[t], :] = jnp.where(m0, o[t][:ATT_QB], o[t][ATT_QB:]).astype(o_ref.dtype)

    first = g * (n_qb * ATT_QB) < C_PAST_ROWS
    pl.when(first)(lambda: run(True))
    pl.when(jnp.logical_not(first))(lambda: run(False))


def _attention_prompt(qkv, bias):
    Bsz, Tp, _ = qkv.shape
    T = Tp - ATT_PAD
    nhp = C_HEADS // 2
    nblk = T // ATT_QB
    n_qb = next(n for n in (ATT_MAX_QB, 16, 8, 4, 2, 1) if nblk % n == 0)
    rows = n_qb * ATT_QB
    skip = ATT_PAD // rows
    return pl.pallas_call(
        functools.partial(_att_prompt_kernel, n_qb),
        grid=(Bsz, nhp, T // rows),
        in_specs=[pl.BlockSpec((None, rows, LANE), lambda b, hp, g: (b, g + skip, hp)),
                  pl.BlockSpec((None, Tp, LANE), lambda b, hp, g: (b, 0, nhp + hp)),
                  pl.BlockSpec((None, Tp, 2 * LANE), lambda b, hp, g: (b, 0, nhp + hp)),
                  pl.BlockSpec((None, 2 * ATT_QB, ATT_WIN), lambda b, hp, g: (hp, 0, 0))],
        out_specs=pl.BlockSpec((None, rows, LANE), lambda b, hp, g: (b, g, hp)),
        out_shape=jax.ShapeDtypeStruct((Bsz, T, D_MODEL), BF16),
        compiler_params=_params("parallel", "parallel", "arbitrary"),
    )(qkv, qkv, qkv, bias)


def _att_sample_kernel(qkv_ref, kc_ref, vc_ref, bias_ref, o_ref):
    R = kc_ref.shape[2]
    heads = range(C_HEADS)
    col = lambda part, h: slice(part * D_MODEL + h * C_HEAD_DIM, part * D_MODEL + (h + 1) * C_HEAD_DIM)
    q = [(qkv_ref[:, col(0, h)] * ATT_SCALE).astype(BF16) for h in heads]
    s_c = [_dg(q[h], kc_ref[h].astype(BF16), NN) + bias_ref[h, :, 0:R] for h in heads]
    s_n = [_dg(q[h], qkv_ref[:, col(1, h)].astype(BF16), NT) + bias_ref[h, :, R:] for h in heads]
    m = [jnp.maximum(s_c[h].max(-1, keepdims=True), s_n[h].max(-1, keepdims=True)) for h in heads]
    p_c = [jnp.exp(s_c[h] - m[h]) for h in heads]
    p_n = [jnp.exp(s_n[h] - m[h]) for h in heads]
    acc = [_dg(p_c[h].astype(BF16), vc_ref[h].astype(BF16), NT)
           + _dg(p_n[h].astype(BF16), qkv_ref[:, col(2, h)].astype(BF16), NN) for h in heads]
    for h in heads:
        den = p_c[h].sum(-1, keepdims=True) + p_n[h].sum(-1, keepdims=True)
        o_ref[:, col(0, h)] = (acc[h] / den).astype(o_ref.dtype)


def _attention_sample(qkv, cache_k, cache_v, bias, o):
    Bsz, T, _ = qkv.shape
    R = cache_k.shape[4]
    cache = pl.BlockSpec((None, None, C_HEADS, C_HEAD_DIM, R), lambda b: (o, b, 0, 0, 0))
    return pl.pallas_call(
        _att_sample_kernel,
        grid=(Bsz,),
        in_specs=[pl.BlockSpec((None, T, 3 * D_MODEL), lambda b: (b, 0, 0)), cache, cache,
                  pl.BlockSpec((C_HEADS, T, R + T), lambda b: (0, 0, 0))],
        out_specs=pl.BlockSpec((None, T, D_MODEL), lambda b: (b, 0, 0)),
        out_shape=jax.ShapeDtypeStruct((Bsz, T, D_MODEL), BF16),
        compiler_params=_params("parallel"),
    )(qkv, cache_k, cache_v, bias)


def _rel_bias_tile(table, n_q, n_k, q_offset):
    period = n_q + n_k - 1
    m = np.arange(period)
    d = np.where(m < n_k, m, m - period)
    idx = np.clip(q_offset - d, -C_REL_CLIP, C_REL_CLIP) + C_REL_CLIP
    diag = table[:, idx]
    flat = jnp.tile(diag, (1, n_q))[:, :n_q * (period - 1)]
    return flat.reshape(table.shape[0], n_q, period - 1)[:, :, :n_k]


TM = 512
TM_PROJ = 1024


def _trunk(x, wkv0, shift0, gla0, cache_k, cache_v, P):
    Bsz, T, _ = x.shape
    M = Bsz * T
    xf = x.reshape(M, D_MODEL)
    wkv_o, shift_o, gla_o, k_o, v_o = [], [], [], [], []
    for layer in range(DEPTH):
        if layer % 2 == 0:
            e = layer // 2
            pa, pb = _in_proj(xf, P['w_in_a'], P['w_in_b'], e, TM_PROJ)
            pa = pa.reshape(Bsz, T, A_PROJ)
            pb = pb.reshape(Bsz, T, B_PROJ_PAD)
            ya, wkv = _rwkv_mix(pa, shift0[e], wkv0[e], {k: v[e] for k, v in P['a'].items()})
            yb, gs = _gla_mix(pb, gla0[e], {k: v[e] for k, v in P['b'].items()})
            wkv_o.append(wkv)
            shift_o.append(pa[:, -1])
            gla_o.append(gs)
            pairs = [(ya.reshape(M, A_WIDTH), P['w_out_a'], e), (yb.reshape(M, B_VWIDTH), P['w_out_b'], e)]
        else:
            o = layer // 2
            if cache_k is None:
                x3 = xf.reshape(Bsz, T, D_MODEL)
                att = _attention_prompt(_qkv_padded(x3, P['c_w_qkv_scaled'], o), P['c_bias_prompt'][o])
                keep = min(C_PAST_ROWS, T)
                rows_t = _kv_tail(x3, P['c_w_kv_t'], o, keep).reshape(Bsz, 2, C_HEADS, C_HEAD_DIM, keep)
                k_o.append(jnp.swapaxes(rows_t[:, 0], -1, -2))
                v_o.append(jnp.swapaxes(rows_t[:, 1], -1, -2))
            else:
                qkv = _matmul(xf, P['c_w_qkv'][o], TM, D_MODEL).reshape(Bsz, T, 3 * D_MODEL)
                att = _attention_sample(qkv, jnp.swapaxes(cache_k, -1, -2), jnp.swapaxes(cache_v, -1, -2),
                                        P['c_bias_sample'][o], o)
                rows = qkv.reshape(Bsz, T, 3, C_HEADS, C_HEAD_DIM)
                k_o.append(rows[:, :, 1].transpose(0, 2, 1, 3))
                v_o.append(rows[:, :, 2].transpose(0, 2, 1, 3))
            pairs = [(att.reshape(M, D_MODEL), P['c_w_o'], o)]
        xf = _mix_ffn(pairs, xf, P['ln1_w'], P['ln1_b'], P['ffn_w_in'], P['ffn_w_out'], P['ln2_w'], P['ln2_b'],
                      layer, TM)
    return (xf.reshape(Bsz, T, D_MODEL), jnp.stack(wkv_o), jnp.stack(shift_o), jnp.stack(gla_o),
            jnp.stack(k_o), jnp.stack(v_o))


def _prepare(w_in_mix, a_mu, a_w0, a_w2, a_a0, a_a2, a_g2, a_k_k, a_k_a, a_r_k, a_ln_w, a_ln_b,
             b_alpha_up, b_alpha_bias, b_norm_w, w_out_mix, c_w_qkv, c_rel_bias, c_w_o,
             ln1_w, ln1_b, ln2_w, ln2_b, ffn_w_in, ffn_w_out, sample_len, cache_rows):
    kw, vw = B_KWIDTH, B_VWIDTH
    wb = w_in_mix[:, :, A_PROJ:]
    main = jnp.concatenate([wb[:, :, :2 * kw + vw], wb[:, :, 2 * kw + vw + B_GATE_LORA:]], axis=-1)
    lora = jnp.pad(wb[:, :, 2 * kw + vw:2 * kw + vw + B_GATE_LORA], ((0, 0), (0, 0), (0, B_LORA_PAD - B_GATE_LORA)))
    row3 = lambda t: t.reshape(t.shape[0], 1, -1)
    bias_prompt = jnp.stack([_rel_bias_tile(t, ATT_QB, ATT_WIN, C_PAST_ROWS) for t in c_rel_bias])
    rel = np.arange(ATT_WIN)[None, :] // CHUNK - np.arange(ATT_QB)[:, None] // CHUNK
    band = (rel >= 0) & (rel <= C_PAST_CHUNKS)
    bias_prompt = jnp.where(band, bias_prompt * LOG2E, NEG_INF)
    q_scale = jnp.concatenate([jnp.full((D_MODEL,), ATT_SCALE * LOG2E, F32), jnp.ones((2 * D_MODEL,), F32)])
    P = {
        'w_in_a': w_in_mix[:, :, :A_PROJ].astype(BF16),
        'w_in_b': jnp.concatenate([main, lora], axis=-1).astype(BF16),
        'a': dict(mu=row3(a_mu), w0=row3(a_w0), w2=a_w2, a0=row3(a_a0), a2=a_a2, g2=a_g2, k_k=row3(a_k_k),
                  k_a=row3(a_k_a), r_k=row3(a_r_k), ln_w=row3(a_ln_w), ln_b=row3(a_ln_b)),
        'b': dict(up=jnp.pad(b_alpha_up, ((0, 0), (0, B_LORA_PAD - B_GATE_LORA), (0, 0))),
                  bias=row3(b_alpha_bias), nw=row3(b_norm_w)),
        'w_out_a': w_out_mix[:, :A_WIDTH].astype(BF16),
        'w_out_b': w_out_mix[:, A_WIDTH:].astype(BF16),
        'c_w_qkv': c_w_qkv.astype(BF16),
        'c_w_qkv_scaled': (c_w_qkv * q_scale).astype(BF16),
        'c_w_kv_t': jnp.swapaxes(c_w_qkv[:, :, D_MODEL:], 1, 2).astype(BF16),
        'c_w_o': c_w_o.astype(BF16),
        'c_bias_prompt': bias_prompt.reshape(bias_prompt.shape[0], C_HEADS // 2, 2 * ATT_QB, ATT_WIN),
        'c_bias_sample': jnp.stack([_rel_bias_tile(t, sample_len, cache_rows + sample_len, cache_rows)
                                    for t in c_rel_bias]),
        'ln1_w': row3(ln1_w), 'ln1_b': row3(ln1_b), 'ln2_w': row3(ln2_w), 'ln2_b': row3(ln2_b),
        'ffn_w_in': ffn_w_in.astype(BF16),
        'ffn_w_out': ffn_w_out.astype(BF16),
    }
    return P


def kernel(x_prompt, x_sample, state_a_wkv, state_a_shift, state_b_gla, cache_c_k, cache_c_v, w_in_mix, a_mu, a_w0, a_w2, a_a0, a_a2, a_g2, a_k_k, a_k_a, a_r_k, a_ln_w, a_ln_b, b_alpha_up, b_alpha_bias, b_norm_w, w_out_mix, c_w_qkv, c_rel_bias, c_w_o, ln1_w, ln1_b, ln2_w, ln2_b, ffn_w_in, ffn_w_out):
    P = _prepare(w_in_mix, a_mu, a_w0, a_w2, a_a0, a_a2, a_g2, a_k_k, a_k_a, a_r_k, a_ln_w, a_ln_b,
                 b_alpha_up, b_alpha_bias, b_norm_w, w_out_mix, c_w_qkv, c_rel_bias, c_w_o,
                 ln1_w, ln1_b, ln2_w, ln2_b, ffn_w_in, ffn_w_out, x_sample.shape[1], cache_c_k.shape[3])
    bp = x_prompt.shape[0]
    dt = x_prompt.dtype
    n_even = state_a_wkv.shape[0]
    wkv_zero = jnp.zeros((n_even, bp, A_HEADS, A_HEAD_DIM, A_HEAD_DIM), dt)
    shift_zero = jnp.zeros((n_even, bp, A_PROJ), dt)
    gla_zero = jnp.zeros((n_even, bp, B_HEADS, B_KEY_DIM, B_VAL_DIM), dt)
    y_prompt, p_wkv, p_shift, p_gla, p_k, p_v = _trunk(x_prompt, wkv_zero, shift_zero, gla_zero, None, None, P)
    y_sample, s_wkv, s_shift, s_gla, s_k, s_v = _trunk(
        x_sample, state_a_wkv, state_a_shift, state_b_gla, cache_c_k, cache_c_v, P)
    return (y_prompt, y_sample, p_wkv, p_shift, p_gla, p_k, p_v, s_wkv, s_shift, s_gla, s_k, s_v)
```

```python
import functools

import jax
import jax.numpy as jnp
import numpy as np
from jax import lax
from jax.experimental import pallas as pl
from jax.experimental.pallas import tpu as pltpu

F32 = jnp.float32
BF16 = jnp.bfloat16

D_MODEL = 1024
DEPTH = 4
CHUNK = 64
A_WIDTH = 512
A_HEAD_DIM = 64
A_HEADS = 8
A_DECAY_LORA = 64
A_ICL_LORA = 64
A_GATE_LORA = 128
A_PROJ = 3 * A_WIDTH + A_DECAY_LORA + A_ICL_LORA + A_GATE_LORA
A_NORM_EPS = 64e-5
B_VWIDTH = 512
B_KWIDTH = 256
B_HEADS = 4
B_KEY_DIM = 64
B_VAL_DIM = 128
B_GATE_LORA = 16
B_GATE_NORM = 16.0
B_LORA_PAD = 128
B_PROJ_PAD = 2 * B_KWIDTH + 2 * B_VWIDTH + B_LORA_PAD
C_HEAD_DIM = 64
C_HEADS = 16
C_PAST_CHUNKS = 8
C_PAST_ROWS = C_PAST_CHUNKS * CHUNK
C_REL_CLIP = 128
FFN_HIDDEN = 2816
LN_EPS = 1e-5
DEEPNORM_ALPHA = (2.0 * DEPTH) ** 0.25
NEG_INF = -1e30

LANE = 128
SUBLANE = 8
BF16_ROWS = 2 * SUBLANE
PAIR = LANE
HALF = PAIR // 2
A_PAIRS = A_HEADS // 2
B_PAIRS = B_HEADS // 2
VMEM_LIMIT = 56 * 1024 * 1024

NN = (((1,), (0,)), ((), ()))
NT = (((1,), (1,)), ((), ()))
TN = (((0,), (0,)), ((), ()))


def _params(*sem):
    return pltpu.CompilerParams(dimension_semantics=sem, vmem_limit_bytes=VMEM_LIMIT)


def _dg(a, b, dn):
    return lax.dot_general(a, b, dn, preferred_element_type=F32)


def _dot1(a, b, dn=NN):
    return _dg(a.astype(BF16), b.astype(BF16), dn)


def _split(a):
    hi = a.astype(BF16)
    lo = (a - hi.astype(F32)).astype(BF16)
    return hi, lo


def _dot3s(a, b, dn=NN):
    return _dg(a[0], b[0], dn) + _dg(a[0], b[1], dn) + _dg(a[1], b[0], dn)


def _dot3(a, b, dn=NN):
    return _dot3s(_split(a), _split(b), dn)


def _dot_xl(a_exact, b, dn=NN):
    bh, bl = _split(b)
    return _dg(a_exact, bh, dn) + _dg(a_exact, bl, dn)


def _dot_xr(a, b_exact, dn=NN):
    ah, al = _split(a)
    return _dg(ah, b_exact, dn) + _dg(al, b_exact, dn)


def _ln(z, w, b):
    mu = jnp.mean(z, -1, keepdims=True)
    d = z - mu
    var = jnp.mean(d * d, -1, keepdims=True)
    return d * lax.rsqrt(var + LN_EPS) * w + b


def _sigmoid(x):
    return 1.0 / (1.0 + jnp.exp(-x))


def _softplus(x):
    return jnp.maximum(x, 0.0) + jnp.log(1.0 + jnp.exp(-jnp.abs(x)))


def _stack_masked(x, m0):
    return jnp.concatenate([jnp.where(m0, x, 0.0), jnp.where(m0, 0.0, x)], axis=0)


def _fold(x):
    n = x.shape[0] // 2
    return x[:n] + x[n:]


def _mm_kernel(x_ref, w_ref, o_ref):
    o_ref[...] = jnp.dot(x_ref[...].astype(BF16), w_ref[...], preferred_element_type=F32)


def _matmul(x, w, tm, tn):
    M, K = x.shape
    N = w.shape[1]
    tm = next(t for t in (tm, 256, 128, 64) if M % t == 0)
    return pl.pallas_call(
        _mm_kernel,
        grid=(M // tm, N // tn),
        in_specs=[pl.BlockSpec((tm, K), lambda i, j: (i, 0)),
                  pl.BlockSpec((K, tn), lambda i, j: (0, j))],
        out_specs=pl.BlockSpec((tm, tn), lambda i, j: (i, j)),
        out_shape=jax.ShapeDtypeStruct((M, N), F32),
        compiler_params=_params("parallel", "arbitrary"),
    )(x, w)


def _in_proj_kernel(x_ref, wa_ref, wb_ref, oa_ref, ob_ref):
    xb = x_ref[...].astype(BF16)
    oa_ref[...] = jnp.dot(xb, wa_ref[...], preferred_element_type=F32)
    ob_ref[...] = jnp.dot(xb, wb_ref[...], preferred_element_type=F32)


def _in_proj(x, wa, wb, e, tm):
    M, K = x.shape
    tm = next(t for t in (tm, 512, 256, 128, 64) if M % t == 0)
    na, nb = wa.shape[2], wb.shape[2]
    resident = lambda n: pl.BlockSpec((None, K, n), lambda i: (e, 0, 0), pipeline_mode=pl.Buffered(1))
    return pl.pallas_call(
        _in_proj_kernel,
        grid=(M // tm,),
        in_specs=[pl.BlockSpec((tm, K), lambda i: (i, 0)), resident(na), resident(nb)],
        out_specs=[pl.BlockSpec((tm, na), lambda i: (i, 0)), pl.BlockSpec((tm, nb), lambda i: (i, 0))],
        out_shape=[jax.ShapeDtypeStruct((M, na), F32), jax.ShapeDtypeStruct((M, nb), F32)],
        compiler_params=_params("parallel"),
    )(x, wa, wb)


def _mix_ffn_kernel(n_pairs, *refs):
    res_ref, lw1_ref, lb1_ref, wg_ref, wu_ref, wo_ref, lw2_ref, lb2_ref, o_ref = refs[2 * n_pairs:]
    tm = o_ref.shape[0]
    n_half = 2 if tm % (2 * BF16_ROWS) == 0 else 1
    halves = [slice(i * (tm // n_half), (i + 1) * (tm // n_half)) for i in range(n_half)]
    mm = lambda a, w: jnp.dot(a, w, preferred_element_type=F32)
    acc = [mm(refs[0][h, :].astype(BF16), refs[1][...]) for h in halves]
    for p in range(1, n_pairs):
        acc = [acc[i] + mm(refs[2 * p][h, :].astype(BF16), refs[2 * p + 1][...]) for i, h in enumerate(halves)]
    x = [_ln(DEEPNORM_ALPHA * res_ref[h, :] + acc[i], lw1_ref[...], lb1_ref[...]) for i, h in enumerate(halves)]
    xb = [v.astype(BF16) for v in x]
    g = [mm(v, wg_ref[...]) for v in xb]
    u = [mm(v, wu_ref[...]) for v in xb]
    act = [(g[i] * _sigmoid(g[i]) * u[i]).astype(BF16) for i in range(n_half)]
    y = [mm(v, wo_ref[...]) for v in act]
    for i, h in enumerate(halves):
        o_ref[h, :] = _ln(DEEPNORM_ALPHA * x[i] + y[i], lw2_ref[...], lb2_ref[...])


def _mix_ffn(pairs, res, lw1, lb1, w_in, w_out, lw2, lb2, layer, tm):
    M = res.shape[0]
    tm = next(t for t in (tm, 256, 128, 64) if M % t == 0)
    resident = lambda shape, idx: pl.BlockSpec((None,) + shape, idx, pipeline_mode=pl.Buffered(1))
    vec = pl.BlockSpec((None, 1, D_MODEL), lambda i: (layer, 0, 0))
    in_specs, args = [], []
    for a, w, n in pairs:
        K = a.shape[1]
        in_specs += [pl.BlockSpec((tm, K), lambda i: (i, 0)), resident((K, D_MODEL), lambda i, n=n: (n, 0, 0))]
        args += [a, w]
    in_specs += [pl.BlockSpec((tm, D_MODEL), lambda i: (i, 0)), vec, vec,
                 resident((D_MODEL, FFN_HIDDEN), lambda i: (layer, 0, 0)),
                 resident((D_MODEL, FFN_HIDDEN), lambda i: (layer, 0, 1)),
                 resident((FFN_HIDDEN, D_MODEL), lambda i: (layer, 0, 0)), vec, vec]
    return pl.pallas_call(
        functools.partial(_mix_ffn_kernel, len(pairs)),
        grid=(M // tm,),
        in_specs=in_specs,
        out_specs=pl.BlockSpec((tm, D_MODEL), lambda i: (i, 0)),
        out_shape=jax.ShapeDtypeStruct((M, D_MODEL), F32),
        compiler_params=_params("parallel"),
    )(*args, res, lw1, lb1, w_in, w_in, w_out, lw2, lb2)


def _inv_unit_lower(ns, eye, nilpotency):
    size = eye.shape[0]
    xs = [eye - n for n in ns]
    ps = [_dot1(n, n) for n in ns]
    k = 2
    while k < nilpotency:
        k *= 2
        if k < nilpotency:
            rs = [_dot1(p, jnp.concatenate([x, p], axis=1)) for x, p in zip(xs, ps)]
            xs = [x + r[:, :size] for x, r in zip(xs, rs)]
            ps = [r[:, size:] for r in rs]
        else:
            xs = [x + _dot1(p, x) for x, p in zip(xs, ps)]
    return xs


def _segsum(x, seg, split=False):
    n = x.shape[1] // PAIR
    rows = x.shape[0]
    xs = jnp.concatenate([x[:, j * PAIR:(j + 1) * PAIR] for j in range(n)], axis=0)
    s = _dot_xr(xs, seg) if split else _dot1(xs, seg)
    return jnp.concatenate([s[j * rows:(j + 1) * rows] for j in range(n)], axis=1)


def _head_mask():
    hr = lax.broadcasted_iota(jnp.int32, (PAIR, PAIR), 0) // A_HEAD_DIM
    hc = lax.broadcasted_iota(jnp.int32, (PAIR, PAIR), 1) // A_HEAD_DIM
    return (hr == hc).astype(BF16)


def _rwkv_prep_kernel(n_chunks, pa_ref, prev_ref, shift_ref, mu_ref, w0_ref, w2_ref, a0_ref, a2_ref, g2_ref,
                      kk_ref, ka_ref, rk_ref,
                      mh_ref, ch_ref, qc_ref, oc_ref, g_ref, bonus_ref):
    i = pl.program_id(1)
    L = CHUNK
    pa = pa_ref[...]
    first = jnp.where(i == 0, shift_ref[...], prev_ref[SUBLANE - 1:SUBLANE, :])
    rows = lax.broadcasted_iota(jnp.int32, pa.shape, 0)
    prev = jnp.where(rows == 0, first, pltpu.roll(pa, shift=1, axis=0))
    xs = pa + (prev - pa) * mu_ref[...]
    W = A_WIDTH
    r = xs[:, 0:W]
    k0 = xs[:, W:2 * W]
    v = xs[:, 2 * W:3 * W]
    xw = xs[:, 3 * W:3 * W + A_DECAY_LORA]
    xa = xs[:, 3 * W + A_DECAY_LORA:3 * W + A_DECAY_LORA + A_ICL_LORA]
    xg = xs[:, 3 * W + A_DECAY_LORA + A_ICL_LORA:]
    w = -_softplus(-(w0_ref[...] + _dot3(jnp.tanh(xw), w2_ref[...]))) - 0.5
    logw = -jnp.exp(w)
    a = _sigmoid(a0_ref[...] + _dot1(xa, a2_ref[...]))
    g_ref[...] = _dot1(_sigmoid(xg), g2_ref[...])
    seg = _head_mask()
    kk = k0 * kk_ref[...]
    kk = kk * lax.rsqrt(_segsum(kk * kk, seg) + 1e-12)
    k = k0 * (1.0 + (a - 1.0) * ka_ref[...])
    bonus_ref[...] = _segsum(r * k * rk_ref[...], seg) * v
    b = kk * a

    tri = (lax.broadcasted_iota(jnp.int32, (L, L), 0) >= lax.broadcasted_iota(jnp.int32, (L, L), 1)).astype(BF16)
    row = lax.broadcasted_iota(jnp.int32, (PAIR, PAIR), 0)
    col = lax.broadcasted_iota(jnp.int32, (PAIR, PAIR), 1)
    eye = (row == col).astype(F32)
    strict = row > col
    incl2 = (lax.broadcasted_iota(jnp.int32, (PAIR, 2 * PAIR), 0)
             >= lax.broadcasted_iota(jnp.int32, (PAIR, 2 * PAIR), 1) % PAIR)
    m0 = lax.broadcasted_iota(jnp.int32, (L, PAIR), 1) < HALF
    pairs = range(A_PAIRS)
    lanes = [slice(j * PAIR, (j + 1) * PAIR) for j in pairs]
    chains = [(c, j) for c in range(n_chunks) for j in pairs]
    every = range(len(chains))
    rows_of = [slice(c * L, (c + 1) * L) for c in range(n_chunks)]
    kkg, rg, kinv, binv, kd, bd, e_last = [], [], [], [], [], [], []
    for sl in rows_of:
        lw = logw[sl]
        cum = _dot_xl(tri, lw)
        c_last = cum[L - 1:L, :]
        e_neg = jnp.exp(-cum)
        e_rem = jnp.exp(c_last - cum)
        kkg.append(kk[sl] * jnp.exp(cum - lw))
        rg.append(r[sl] * jnp.exp(cum))
        kinv.append(k[sl] * e_neg)
        binv.append(b[sl] * e_neg)
        kd.append(k[sl] * e_rem)
        bd.append(b[sl] * e_rem)
        e_last.append(jnp.exp(c_last))
    sm = lambda xs_, c, j: _stack_masked(xs_[c][:, lanes[j]].astype(BF16), m0)
    kkg_b = [sm(kkg, c, j) for c, j in chains]
    v_b = [_stack_masked(v[rows_of[c], lanes[j]].astype(BF16), m0) for c, j in chains]
    bd_b = [sm(bd, c, j) for c, j in chains]
    kd_b = [sm(kd, c, j) for c, j in chains]
    rg_b = [sm(rg, c, j) for c, j in chains]
    kinv_b = [sm(kinv, c, j) for c, j in chains]
    binv_b = [sm(binv, c, j) for c, j in chains]
    aa = [_dg(jnp.concatenate([kkg_b[n], rg_b[n]], axis=0), jnp.concatenate([binv_b[n], kinv_b[n]], axis=0), NT)
          for n in every]
    a_kb = [jnp.where(strict, x[:PAIR, :PAIR], 0.0) for x in aa]
    a_kk = [jnp.where(strict, x[:PAIR, PAIR:], 0.0).astype(BF16) for x in aa]
    a_rbk = [jnp.where(incl2, x[PAIR:, :], 0.0).astype(BF16) for x in aa]
    t = _inv_unit_lower(a_kb, eye, L)
    av = [_dg(a_kk[n], v_b[n], NN).astype(BF16) for n in every]
    wu = [_dg(t[n].astype(BF16), jnp.concatenate([kkg_b[n], av[n]], axis=1), NN) for n in every]
    wu_b = [x.astype(BF16) for x in wu]
    mc = [_dg(bd_b[n], wu_b[n], TN) for n in every]
    kv = [_dg(kd_b[n], v_b[n], TN) for n in every]
    rhs = [jnp.concatenate([jnp.concatenate([wu_b[n][:, :PAIR], (-wu[n][:, PAIR:]).astype(BF16)], axis=1),
                            jnp.concatenate([jnp.zeros_like(v_b[n]), v_b[n]], axis=1)], axis=0) for n in every]
    qo = [_dg(a_rbk[n], rhs[n], NN) for n in every]
    for n, (c, j) in enumerate(chains):
        mh_ref[c, j] = (eye * e_last[c][:, lanes[j]] - mc[n][:, :PAIR]).astype(BF16)
        ch_ref[c, j] = kv[n] - mc[n][:, PAIR:]
        qc_ref[rows_of[c], lanes[j]] = (rg[c][:, lanes[j]] - _fold(qo[n][:, :PAIR])).astype(BF16)
        oc_ref[rows_of[c], lanes[j]] = _fold(qo[n][:, PAIR:])


def _rwkv_scan_kernel(n_seq, n_chunks, mh_ref, ch_ref, qc_ref, oc_ref, g_ref, bonus_ref, h0_ref, lnw_ref, lnb_ref,
                      y_ref, hout_ref, h_ref):
    i = pl.program_id(1)
    L = CHUNK

    @pl.when(i == 0)
    def _():
        h_ref[...] = h0_ref[...]

    seg = _head_mask()
    inv_n = 1.0 / A_HEAD_DIM
    lanes = [slice(j * PAIR, (j + 1) * PAIR) for j in range(A_PAIRS)]
    chains = [(s, j) for s in range(n_seq) for j in range(A_PAIRS)]
    hb = [h_ref[s, j] for s, j in chains]
    for c in range(n_chunks):
        sl = slice(c * L, (c + 1) * L)
        hb_b = [x.astype(BF16) for x in hb]
        o = [_dg(qc_ref[s, sl, lanes[j]], hb_b[n], NN) + oc_ref[s, sl, lanes[j]] for n, (s, j) in enumerate(chains)]
        hb = [_dg(mh_ref[s, c, j], hb_b[n], NN) + ch_ref[s, c, j] for n, (s, j) in enumerate(chains)]
        for s in range(n_seq):
            os_ = jnp.concatenate(o[s * A_PAIRS:(s + 1) * A_PAIRS], axis=1)
            mean = _segsum(os_, seg, split=True) * inv_n
            d = os_ - mean
            var = _segsum(d * d, seg, split=True) * inv_n
            on = d * lax.rsqrt(var + A_NORM_EPS) * lnw_ref[...] + lnb_ref[...]
            y_ref[s, sl, :] = ((on + bonus_ref[s, sl, :]) * g_ref[s, sl, :]).astype(y_ref.dtype)
    for n, (s, j) in enumerate(chains):
        h_ref[s, j] = hb[n]
    hout_ref[...] = h_ref[...]


def _rwkv_mix(pa, shift0, wkv0, prm):
    Bsz, T, _ = pa.shape
    n1 = next(n for n in (8, 4, 2, 1) if (T // CHUNK) % n == 0)
    R1 = n1 * CHUNK
    W = A_WIDTH
    nblk = T // R1
    row = lambda n: pl.BlockSpec((1, n), lambda b, i: (0, 0))
    full = lambda s: pl.BlockSpec(s, lambda b, i: (0,) * len(s))
    act = lambda r, n: pl.BlockSpec((None, r, n), lambda b, i: (b, i, 0))
    mat = lambda n: pl.BlockSpec((None, n, A_PAIRS, PAIR, PAIR), lambda b, i: (b, i, 0, 0, 0))
    sub = R1 // SUBLANE
    mats = lambda dt: jax.ShapeDtypeStruct((Bsz, T // CHUNK, A_PAIRS, PAIR, PAIR), dt)
    acts = lambda dt: jax.ShapeDtypeStruct((Bsz, T, W), dt)
    mh, ch, qc, oc, g, bonus = pl.pallas_call(
        functools.partial(_rwkv_prep_kernel, n1),
        grid=(Bsz, nblk),
        in_specs=[act(R1, A_PROJ),
                  pl.BlockSpec((None, SUBLANE, A_PROJ), lambda b, i: (b, jnp.maximum(i * sub - 1, 0), 0)),
                  pl.BlockSpec((None, 1, A_PROJ), lambda b, i: (b, 0, 0)),
                  row(A_PROJ), row(W), full((A_DECAY_LORA, W)), row(W), full((A_ICL_LORA, W)),
                  full((A_GATE_LORA, W)), row(W), row(W), row(W)],
        out_specs=[mat(n1), mat(n1), act(R1, W), act(R1, W), act(R1, W), act(R1, W)],
        out_shape=[mats(BF16), mats(F32), acts(BF16), acts(F32), acts(F32), acts(F32)],
        compiler_params=_params("parallel", "parallel"),
    )(pa, pa, shift0[:, None, :], prm['mu'], prm['w0'], prm['w2'], prm['a0'], prm['a2'], prm['g2'],
      prm['k_k'], prm['k_a'], prm['r_k'])

    n2 = min(4, T // CHUNK)
    R2 = n2 * CHUNK
    hh = wkv0.transpose(0, 1, 3, 2).reshape(Bsz, A_PAIRS, 2, A_HEAD_DIM, A_HEAD_DIM)
    zero = jnp.zeros_like(hh[:, :, 0])
    h0 = jnp.concatenate([jnp.concatenate([hh[:, :, 0], zero], axis=-1),
                          jnp.concatenate([zero, hh[:, :, 1]], axis=-1)], axis=-2)
    ns = next(n for n in (4, 2, 1) if Bsz % n == 0)
    state = pl.BlockSpec((ns, A_PAIRS, PAIR, PAIR), lambda b, i: (b, 0, 0, 0))
    act2 = lambda r, n: pl.BlockSpec((ns, r, n), lambda b, i: (b, i, 0))
    mat2 = lambda n: pl.BlockSpec((ns, n, A_PAIRS, PAIR, PAIR), lambda b, i: (b, i, 0, 0, 0))
    y, hout = pl.pallas_call(
        functools.partial(_rwkv_scan_kernel, ns, n2),
        grid=(Bsz // ns, T // R2),
        in_specs=[mat2(n2), mat2(n2), act2(R2, W), act2(R2, W), act2(R2, W), act2(R2, W), state,
                  row(W), row(W)],
        out_specs=[act2(R2, W), state],
        out_shape=[acts(BF16), jax.ShapeDtypeStruct((Bsz, A_PAIRS, PAIR, PAIR), F32)],
        scratch_shapes=[pltpu.VMEM((ns, A_PAIRS, PAIR, PAIR), F32)],
        compiler_params=_params("parallel", "arbitrary"),
    )(mh, ch, qc, oc, g, bonus, h0, prm['ln_w'], prm['ln_b'])
    hd = jnp.stack([hout[:, :, :HALF, :HALF], hout[:, :, HALF:, HALF:]], axis=2)
    wkv = hd.reshape(Bsz, A_HEADS, A_HEAD_DIM, A_HEAD_DIM).transpose(0, 1, 3, 2)
    return y, wkv


def _gla_kernel(n_seq, n_chunks, pb_ref, s0_ref, up_ref, bias_ref, nw_ref, y_ref, sout_ref, s_ref):
    i = pl.program_id(1)
    L = CHUNK

    @pl.when(i == 0)
    def _():
        s_ref[...] = s0_ref[...]

    kw, vw = B_KWIDTH, B_VWIDTH
    incl = lax.broadcasted_iota(jnp.int32, (L, L), 0) >= lax.broadcasted_iota(jnp.int32, (L, L), 1)
    tri = incl.astype(BF16)
    m0 = lax.broadcasted_iota(jnp.int32, (L, PAIR), 1) < HALF
    top = lax.broadcasted_iota(jnp.int32, (PAIR, B_VAL_DIM), 0) < HALF
    scale = B_KEY_DIM ** -0.5
    pairs = range(B_PAIRS)
    heads = range(B_HEADS)
    pl_ = lambda h: slice((h // 2) * PAIR, (h // 2 + 1) * PAIR)
    units = [(s, c) for c in range(n_chunks) for s in range(n_seq)]
    rows_of = lambda c: slice(c * L, (c + 1) * L)
    vslice = lambda h: slice(h * B_VAL_DIM, (h + 1) * B_VAL_DIM)
    z = [_dot3(pb_ref[s, rows_of(c), 2 * kw + 2 * vw:], up_ref[...]) + bias_ref[...] for s, c in units]
    bc = [_dot_xl(tri, -_softplus(-x) * (1.0 / B_GATE_NORM)) for x in z]
    b_last = [x[L - 1:L, :] for x in bc]
    q_dec, k_inv, k_state, e_last_t = [], [], [], []
    for n, (s, c) in enumerate(units):
        k = pb_ref[s, rows_of(c), kw:2 * kw]
        q_dec.append(pb_ref[s, rows_of(c), 0:kw] * scale * jnp.exp(bc[n]))
        k_inv.append((k * jnp.exp(-bc[n])).astype(BF16))
        k_state.append((k * jnp.exp(b_last[n] - bc[n])).astype(BF16))
        e_last_t.append(jnp.broadcast_to(jnp.exp(b_last[n]), (LANE, kw)).T)
    uh = [(n, h) for n in range(len(units)) for h in heads]
    vb = [pb_ref[units[n][0], rows_of(units[n][1]), 2 * kw + h * B_VAL_DIM:2 * kw + (h + 1) * B_VAL_DIM].astype(BF16)
          for n, h in uh]
    qm = [jnp.where(m0 if h % 2 == 0 else ~m0, q_dec[n][:, pl_(h)], 0.0).astype(BF16) for n, h in uh]
    scores = [jnp.where(incl, _dg(qm[i_], k_inv[n][:, pl_(h)], NT), 0.0).astype(BF16) for i_, (n, h) in enumerate(uh)]
    upd = [_dg(k_state[n][:, pl_(h)], vb[i_], TN) for i_, (n, h) in enumerate(uh)]
    intra = [_dg(scores[i_], vb[i_], NN) for i_ in range(len(uh))]
    states = [[s_ref[s, j] for j in pairs] for s in range(n_seq)]
    before = []
    for n, (s, c) in enumerate(units):
        before.append([x.astype(BF16) for x in states[s]])
        states[s] = [states[s][j] * e_last_t[n][j * PAIR:(j + 1) * PAIR, :]
                     + jnp.where(top, upd[n * B_HEADS + 2 * j], upd[n * B_HEADS + 2 * j + 1]) for j in pairs]
    o = [_dg(qm[i_], before[n][h // 2], NN) + intra[i_] for i_, (n, h) in enumerate(uh)]
    for i_, (n, h) in enumerate(uh):
        s, c = units[n]
        oh = o[i_] * lax.rsqrt(jnp.mean(o[i_] * o[i_], -1, keepdims=True) + LN_EPS) * nw_ref[...]
        rg_h = pb_ref[s, rows_of(c), 2 * kw + vw + h * B_VAL_DIM:2 * kw + vw + (h + 1) * B_VAL_DIM]
        y_ref[s, rows_of(c), vslice(h)] = (oh * (rg_h * _sigmoid(rg_h))).astype(y_ref.dtype)
    for s in range(n_seq):
        for j in pairs:
            s_ref[s, j] = states[s][j]
    sout_ref[...] = s_ref[...]


def _gla_mix(pb, s0, prm):
    Bsz, T, _ = pb.shape
    n = min(4, T // CHUNK)
    Rr = n * CHUNK
    ns = next(n for n in (4, 2, 1) if Bsz % n == 0)
    st = pl.BlockSpec((ns, B_PAIRS, PAIR, B_VAL_DIM), lambda b, i: (b, 0, 0, 0))
    y, s = pl.pallas_call(
        functools.partial(_gla_kernel, ns, n),
        grid=(Bsz // ns, T // Rr),
        in_specs=[pl.BlockSpec((ns, Rr, B_PROJ_PAD), lambda b, i: (b, i, 0)), st,
                  pl.BlockSpec((B_LORA_PAD, B_KWIDTH), lambda b, i: (0, 0)),
                  pl.BlockSpec((1, B_KWIDTH), lambda b, i: (0, 0)),
                  pl.BlockSpec((1, B_VAL_DIM), lambda b, i: (0, 0))],
        out_specs=[pl.BlockSpec((ns, Rr, B_VWIDTH), lambda b, i: (b, i, 0)), st],
        out_shape=[jax.ShapeDtypeStruct((Bsz, T, B_VWIDTH), BF16),
                   jax.ShapeDtypeStruct((Bsz, B_PAIRS, PAIR, B_VAL_DIM), F32)],
        scratch_shapes=[pltpu.VMEM((ns, B_PAIRS, PAIR, B_VAL_DIM), F32)],
        compiler_params=_params("parallel", "arbitrary"),
    )(pb, s0.reshape(Bsz, B_PAIRS, PAIR, B_VAL_DIM), prm['up'], prm['bias'], prm['nw'])
    return y, s.reshape(Bsz, B_HEADS, B_KEY_DIM, B_VAL_DIM)


ATT_QB = 2 * CHUNK
ATT_SCALE = C_HEAD_DIM ** -0.5
LOG2E = 1.4426950408889634


ATT_WIN = C_PAST_ROWS + ATT_QB
ATT_MAX_QB = 16
ATT_PAD = ATT_MAX_QB * ATT_QB


def _qkv_pad_kernel(n_pad, x_ref, w_ref, o_ref):
    r = pl.program_id(1)

    @pl.when(r < n_pad)
    def _():
        o_ref[...] = jnp.zeros_like(o_ref)

    @pl.when(r >= n_pad)
    def _():
        o_ref[...] = jnp.dot(x_ref[...].astype(BF16), w_ref[...], preferred_element_type=F32).astype(BF16)


def _qkv_padded(x, w, o):
    Bsz, T, D = x.shape
    N = w.shape[2]
    tm = next(t for t in (TM_PROJ, 512, 256, 128) if T % t == 0)
    n_pad = ATT_PAD // tm
    return pl.pallas_call(
        functools.partial(_qkv_pad_kernel, n_pad),
        grid=(Bsz, T // tm + n_pad),
        in_specs=[pl.BlockSpec((None, tm, D), lambda b, r: (b, jnp.maximum(r - n_pad, 0), 0)),
                  pl.BlockSpec((None, D, N), lambda b, r: (o, 0, 0), pipeline_mode=pl.Buffered(1))],
        out_specs=pl.BlockSpec((None, tm, N), lambda b, r: (b, r, 0)),
        out_shape=jax.ShapeDtypeStruct((Bsz, ATT_PAD + T, N), BF16),
        compiler_params=_params("parallel", "arbitrary"),
    )(x, w)


def _kv_tail_kernel(x_ref, wt_ref, o_ref):
    o_ref[...] = _dg(wt_ref[...], x_ref[...].astype(BF16), NT)


def _kv_tail(x, w_t, o, keep):
    Bsz, T, D = x.shape
    tm = next(t for t in (512, 256, 128) if keep % t == 0 and T % t == 0)
    first = (T - keep) // tm
    return pl.pallas_call(
        _kv_tail_kernel,
        grid=(Bsz, keep // tm, 2),
        in_specs=[pl.BlockSpec((None, tm, D), lambda b, r, j: (b, first + r, 0)),
                  pl.BlockSpec((None, D, D), lambda b, r, j: (o, j, 0))],
        out_specs=pl.BlockSpec((None, None, D, tm), lambda b, r, j: (b, j, 0, r)),
        out_shape=jax.ShapeDtypeStruct((Bsz, 2, D, keep), F32),
        compiler_params=_params("parallel", "parallel", "arbitrary"),
    )(x, w_t)


def _att_prompt_kernel(n_qb, q_ref, k_ref, v_ref, bias_ref, o_ref):
    g = pl.program_id(2)
    m0 = lax.broadcasted_iota(jnp.int32, (ATT_QB, PAIR), 1) < HALF
    col = lax.broadcasted_iota(jnp.int32, (2 * ATT_QB, ATT_WIN), 1)

    def run(window_has_padding):
        blocks = range(n_qb)
        rows = [slice(t * ATT_QB, (t + 1) * ATT_QB) for t in blocks]
        wins = [pl.ds(pl.multiple_of(ATT_PAD - C_PAST_ROWS + (g * n_qb + t) * ATT_QB, ATT_QB), ATT_WIN)
                for t in blocks]
        q = [_stack_masked(q_ref[rows[t], :], m0) for t in blocks]
        s = [_dg(q[t], k_ref[wins[t], :], NT) + bias_ref[...] for t in blocks]
        if window_has_padding:
            s = [jnp.where(col >= C_PAST_ROWS - (g * n_qb + t) * ATT_QB, s[t], NEG_INF) for t in blocks]
        p = [jnp.exp2(x - x.max(-1, keepdims=True)) for x in s]
        ones = jnp.ones((ATT_WIN, PAIR), BF16)
        pv = [_dg(p[t].astype(BF16), jnp.concatenate([v_ref[wins[t], :], ones], axis=1), NN) for t in blocks]
        o = [x[:, :PAIR] / x[:, PAIR:] for x in pv]
        for t in blocks:
            o_ref[rows[t], :] = jnp.where(m0, o[t][:ATT_QB], o[t][ATT_QB:]).astype(o_ref.dtype)

    first = g * (n_qb * ATT_QB) < C_PAST_ROWS
    pl.when(first)(lambda: run(True))
    pl.when(jnp.logical_not(first))(lambda: run(False))


def _attention_prompt(qkv, bias):
    Bsz, Tp, _ = qkv.shape
    T = Tp - ATT_PAD
    nhp = C_HEADS // 2
    nblk = T // ATT_QB
    n_qb = next(n for n in (ATT_MAX_QB, 8, 4, 2, 1) if nblk % n == 0)
    rows = n_qb * ATT_QB
    skip = ATT_PAD // rows
    return pl.pallas_call(
        functools.partial(_att_prompt_kernel, n_qb),
        grid=(Bsz, nhp, T // rows),
        in_specs=[pl.BlockSpec((None, rows, LANE), lambda b, hp, g: (b, g + skip, hp)),
                  pl.BlockSpec((None, Tp, LANE), lambda b, hp, g: (b, 0, nhp + hp)),
                  pl.BlockSpec((None, Tp, LANE), lambda b, hp, g: (b, 0, 2 * nhp + hp)),
                  pl.BlockSpec((None, 2 * ATT_QB, ATT_WIN), lambda b, hp, g: (hp, 0, 0))],
        out_specs=pl.BlockSpec((None, rows, LANE), lambda b, hp, g: (b, g, hp)),
        out_shape=jax.ShapeDtypeStruct((Bsz, T, D_MODEL), BF16),
        compiler_params=_params("parallel", "parallel", "arbitrary"),
    )(qkv, qkv, qkv, bias)


def _att_sample_kernel(qkv_ref, kc_ref, vc_ref, bias_ref, o_ref):
    R = kc_ref.shape[2]
    heads = range(C_HEADS)
    col = lambda part, h: slice(part * D_MODEL + h * C_HEAD_DIM, part * D_MODEL + (h + 1) * C_HEAD_DIM)
    q = [(qkv_ref[:, col(0, h)] * ATT_SCALE).astype(BF16) for h in heads]
    s_c = [_dg(q[h], kc_ref[h].astype(BF16), NN) + bias_ref[h, :, 0:R] for h in heads]
    s_n = [_dg(q[h], qkv_ref[:, col(1, h)].astype(BF16), NT) + bias_ref[h, :, R:] for h in heads]
    m = [jnp.maximum(s_c[h].max(-1, keepdims=True), s_n[h].max(-1, keepdims=True)) for h in heads]
    p_c = [jnp.exp(s_c[h] - m[h]) for h in heads]
    p_n = [jnp.exp(s_n[h] - m[h]) for h in heads]
    acc = [_dg(p_c[h].astype(BF16), vc_ref[h].astype(BF16), NT)
           + _dg(p_n[h].astype(BF16), qkv_ref[:, col(2, h)].astype(BF16), NN) for h in heads]
    for h in heads:
        den = p_c[h].sum(-1, keepdims=True) + p_n[h].sum(-1, keepdims=True)
        o_ref[:, col(0, h)] = (acc[h] / den).astype(o_ref.dtype)


def _attention_sample(qkv, cache_k, cache_v, bias, o):
    Bsz, T, _ = qkv.shape
    R = cache_k.shape[4]
    cache = pl.BlockSpec((None, None, C_HEADS, C_HEAD_DIM, R), lambda b: (o, b, 0, 0, 0))
    return pl.pallas_call(
        _att_sample_kernel,
        grid=(Bsz,),
        in_specs=[pl.BlockSpec((None, T, 3 * D_MODEL), lambda b: (b, 0, 0)), cache, cache,
                  pl.BlockSpec((C_HEADS, T, R + T), lambda b: (0, 0, 0))],
        out_specs=pl.BlockSpec((None, T, D_MODEL), lambda b: (b, 0, 0)),
        out_shape=jax.ShapeDtypeStruct((Bsz, T, D_MODEL), BF16),
        compiler_params=_params("parallel"),
    )(qkv, cache_k, cache_v, bias)


def _rel_bias_tile(table, n_q, n_k, q_offset):
    period = n_q + n_k - 1
    m = np.arange(period)
    d = np.where(m < n_k, m, m - period)
    idx = np.clip(q_offset - d, -C_REL_CLIP, C_REL_CLIP) + C_REL_CLIP
    diag = table[:, idx]
    flat = jnp.tile(diag, (1, n_q))[:, :n_q * (period - 1)]
    return flat.reshape(table.shape[0], n_q, period - 1)[:, :, :n_k]


TM = 512
TM_PROJ = 1024


def _trunk(x, wkv0, shift0, gla0, cache_k, cache_v, P):
    Bsz, T, _ = x.shape
    M = Bsz * T
    xf = x.reshape(M, D_MODEL)
    wkv_o, shift_o, gla_o, k_o, v_o = [], [], [], [], []
    for layer in range(DEPTH):
        if layer % 2 == 0:
            e = layer // 2
            pa, pb = _in_proj(xf, P['w_in_a'], P['w_in_b'], e, TM_PROJ)
            pa = pa.reshape(Bsz, T, A_PROJ)
            pb = pb.reshape(Bsz, T, B_PROJ_PAD)
            ya, wkv = _rwkv_mix(pa, shift0[e], wkv0[e], {k: v[e] for k, v in P['a'].items()})
            yb, gs = _gla_mix(pb, gla0[e], {k: v[e] for k, v in P['b'].items()})
            wkv_o.append(wkv)
            shift_o.append(pa[:, -1])
            gla_o.append(gs)
            pairs = [(ya.reshape(M, A_WIDTH), P['w_out_a'], e), (yb.reshape(M, B_VWIDTH), P['w_out_b'], e)]
        else:
            o = layer // 2
            if cache_k is None:
                x3 = xf.reshape(Bsz, T, D_MODEL)
                att = _attention_prompt(_qkv_padded(x3, P['c_w_qkv_scaled'], o), P['c_bias_prompt'][o])
                keep = min(C_PAST_ROWS, T)
                rows_t = _kv_tail(x3, P['c_w_kv_t'], o, keep).reshape(Bsz, 2, C_HEADS, C_HEAD_DIM, keep)
                k_o.append(jnp.swapaxes(rows_t[:, 0], -1, -2))
                v_o.append(jnp.swapaxes(rows_t[:, 1], -1, -2))
            else:
                qkv = _matmul(xf, P['c_w_qkv'][o], TM, D_MODEL).reshape(Bsz, T, 3 * D_MODEL)
                att = _attention_sample(qkv, jnp.swapaxes(cache_k, -1, -2), jnp.swapaxes(cache_v, -1, -2),
                                        P['c_bias_sample'][o], o)
                rows = qkv.reshape(Bsz, T, 3, C_HEADS, C_HEAD_DIM)
                k_o.append(rows[:, :, 1].transpose(0, 2, 1, 3))
                v_o.append(rows[:, :, 2].transpose(0, 2, 1, 3))
            pairs = [(att.reshape(M, D_MODEL), P['c_w_o'], o)]
        xf = _mix_ffn(pairs, xf, P['ln1_w'], P['ln1_b'], P['ffn_w_in'], P['ffn_w_out'], P['ln2_w'], P['ln2_b'],
                      layer, TM)
    return (xf.reshape(Bsz, T, D_MODEL), jnp.stack(wkv_o), jnp.stack(shift_o), jnp.stack(gla_o),
            jnp.stack(k_o), jnp.stack(v_o))


def _prepare(w_in_mix, a_mu, a_w0, a_w2, a_a0, a_a2, a_g2, a_k_k, a_k_a, a_r_k, a_ln_w, a_ln_b,
             b_alpha_up, b_alpha_bias, b_norm_w, w_out_mix, c_w_qkv, c_rel_bias, c_w_o,
             ln1_w, ln1_b, ln2_w, ln2_b, ffn_w_in, ffn_w_out, sample_len, cache_rows):
    kw, vw = B_KWIDTH, B_VWIDTH
    wb = w_in_mix[:, :, A_PROJ:]
    main = jnp.concatenate([wb[:, :, :2 * kw + vw], wb[:, :, 2 * kw + vw + B_GATE_LORA:]], axis=-1)
    lora = jnp.pad(wb[:, :, 2 * kw + vw:2 * kw + vw + B_GATE_LORA], ((0, 0), (0, 0), (0, B_LORA_PAD - B_GATE_LORA)))
    row3 = lambda t: t.reshape(t.shape[0], 1, -1)
    bias_prompt = jnp.stack([_rel_bias_tile(t, ATT_QB, ATT_WIN, C_PAST_ROWS) for t in c_rel_bias])
    rel = np.arange(ATT_WIN)[None, :] // CHUNK - np.arange(ATT_QB)[:, None] // CHUNK
    band = (rel >= 0) & (rel <= C_PAST_CHUNKS)
    bias_prompt = jnp.where(band, bias_prompt * LOG2E, NEG_INF)
    q_scale = jnp.concatenate([jnp.full((D_MODEL,), ATT_SCALE * LOG2E, F32), jnp.ones((2 * D_MODEL,), F32)])
    P = {
        'w_in_a': w_in_mix[:, :, :A_PROJ].astype(BF16),
        'w_in_b': jnp.concatenate([main, lora], axis=-1).astype(BF16),
        'a': dict(mu=row3(a_mu), w0=row3(a_w0), w2=a_w2, a0=row3(a_a0), a2=a_a2, g2=a_g2, k_k=row3(a_k_k),
                  k_a=row3(a_k_a), r_k=row3(a_r_k), ln_w=row3(a_ln_w), ln_b=row3(a_ln_b)),
        'b': dict(up=jnp.pad(b_alpha_up, ((0, 0), (0, B_LORA_PAD - B_GATE_LORA), (0, 0))),
                  bias=row3(b_alpha_bias), nw=row3(b_norm_w)),
        'w_out_a': w_out_mix[:, :A_WIDTH].astype(BF16),
        'w_out_b': w_out_mix[:, A_WIDTH:].astype(BF16),
        'c_w_qkv': c_w_qkv.astype(BF16),
        'c_w_qkv_scaled': (c_w_qkv * q_scale).astype(BF16),
        'c_w_kv_t': jnp.swapaxes(c_w_qkv[:, :, D_MODEL:], 1, 2).astype(BF16),
        'c_w_o': c_w_o.astype(BF16),
        'c_bias_prompt': bias_prompt.reshape(bias_prompt.shape[0], C_HEADS // 2, 2 * ATT_QB, ATT_WIN),
        'c_bias_sample': jnp.stack([_rel_bias_tile(t, sample_len, cache_rows + sample_len, cache_rows)
                                    for t in c_rel_bias]),
        'ln1_w': row3(ln1_w), 'ln1_b': row3(ln1_b), 'ln2_w': row3(ln2_w), 'ln2_b': row3(ln2_b),
        'ffn_w_in': ffn_w_in.astype(BF16),
        'ffn_w_out': ffn_w_out.astype(BF16),
    }
    return P


def kernel(x_prompt, x_sample, state_a_wkv, state_a_shift, state_b_gla, cache_c_k, cache_c_v, w_in_mix, a_mu, a_w0, a_w2, a_a0, a_a2, a_g2, a_k_k, a_k_a, a_r_k, a_ln_w, a_ln_b, b_alpha_up, b_alpha_bias, b_norm_w, w_out_mix, c_w_qkv, c_rel_bias, c_w_o, ln1_w, ln1_b, ln2_w, ln2_b, ffn_w_in, ffn_w_out):
    P = _prepare(w_in_mix, a_mu, a_w0, a_w2, a_a0, a_a2, a_g2, a_k_k, a_k_a, a_r_k, a_ln_w, a_ln_b,
                 b_alpha_up, b_alpha_bias, b_norm_w, w_out_mix, c_w_qkv, c_rel_bias, c_w_o,
                 ln1_w, ln1_b, ln2_w, ln2_b, ffn_w_in, ffn_w_out, x_sample.shape[1], cache_c_k.shape[3])
    bp = x_prompt.shape[0]
    dt = x_prompt.dtype
    n_even = state_a_wkv.shape[0]
    wkv_zero = jnp.zeros((n_even, bp, A_HEADS, A_HEAD_DIM, A_HEAD_DIM), dt)
    shift_zero = jnp.zeros((n_even, bp, A_PROJ), dt)
    gla_zero = jnp.zeros((n_even, bp, B_HEADS, B_KEY_DIM, B_VAL_DIM), dt)
    y_prompt, p_wkv, p_shift, p_gla, p_k, p_v = _trunk(x_prompt, wkv_zero, shift_zero, gla_zero, None, None, P)
    y_sample, s_wkv, s_shift, s_gla, s_k, s_v = _trunk(
        x_sample, state_a_wkv, state_a_shift, state_b_gla, cache_c_k, cache_c_v, P)
    return (y_prompt, y_sample, p_wkv, p_shift, p_gla, p_k, p_v, s_wkv, s_shift, s_gla, s_k, s_v)
```

```python
import functools

import jax
import jax.numpy as jnp
import numpy as np
from jax import lax
from jax.experimental import pallas as pl
from jax.experimental.pallas import tpu as pltpu

F32 = jnp.float32
BF16 = jnp.bfloat16

D_MODEL = 1024
DEPTH = 4
CHUNK = 64
A_WIDTH = 512
A_HEAD_DIM = 64
A_HEADS = 8
A_DECAY_LORA = 64
A_ICL_LORA = 64
A_GATE_LORA = 128
A_PROJ = 3 * A_WIDTH + A_DECAY_LORA + A_ICL_LORA + A_GATE_LORA
A_NORM_EPS = 64e-5
B_VWIDTH = 512
B_KWIDTH = 256
B_HEADS = 4
B_KEY_DIM = 64
B_VAL_DIM = 128
B_GATE_LORA = 16
B_GATE_NORM = 16.0
B_LORA_PAD = 128
B_PROJ_PAD = 2 * B_KWIDTH + 2 * B_VWIDTH + B_LORA_PAD
C_HEAD_DIM = 64
C_HEADS = 16
C_PAST_CHUNKS = 8
C_PAST_ROWS = C_PAST_CHUNKS * CHUNK
C_REL_CLIP = 128
FFN_HIDDEN = 2816
LN_EPS = 1e-5
DEEPNORM_ALPHA = (2.0 * DEPTH) ** 0.25
NEG_INF = -1e30

LANE = 128
SUBLANE = 8
BF16_ROWS = 2 * SUBLANE
PAIR = LANE
HALF = PAIR // 2
A_PAIRS = A_HEADS // 2
B_PAIRS = B_HEADS // 2
VMEM_LIMIT = 56 * 1024 * 1024

NN = (((1,), (0,)), ((), ()))
NT = (((1,), (1,)), ((), ()))
TN = (((0,), (0,)), ((), ()))


def _params(*sem):
    return pltpu.CompilerParams(dimension_semantics=sem, vmem_limit_bytes=VMEM_LIMIT)


def _dg(a, b, dn):
    return lax.dot_general(a, b, dn, preferred_element_type=F32)


def _dot1(a, b, dn=NN):
    return _dg(a.astype(BF16), b.astype(BF16), dn)


def _split(a):
    hi = a.astype(BF16)
    lo = (a - hi.astype(F32)).astype(BF16)
    return hi, lo


def _dot3s(a, b, dn=NN):
    return _dg(a[0], b[0], dn) + _dg(a[0], b[1], dn) + _dg(a[1], b[0], dn)


def _dot3(a, b, dn=NN):
    return _dot3s(_split(a), _split(b), dn)


def _dot_xl(a_exact, b, dn=NN):
    bh, bl = _split(b)
    return _dg(a_exact, bh, dn) + _dg(a_exact, bl, dn)


def _dot_xr(a, b_exact, dn=NN):
    ah, al = _split(a)
    return _dg(ah, b_exact, dn) + _dg(al, b_exact, dn)


def _ln(z, w, b):
    mu = jnp.mean(z, -1, keepdims=True)
    d = z - mu
    var = jnp.mean(d * d, -1, keepdims=True)
    return d * lax.rsqrt(var + LN_EPS) * w + b


def _sigmoid(x):
    return 1.0 / (1.0 + jnp.exp(-x))


def _softplus(x):
    return jnp.maximum(x, 0.0) + jnp.log(1.0 + jnp.exp(-jnp.abs(x)))


def _stack_masked(x, m0):
    return jnp.concatenate([jnp.where(m0, x, 0.0), jnp.where(m0, 0.0, x)], axis=0)


def _fold(x):
    n = x.shape[0] // 2
    return x[:n] + x[n:]


def _mm_kernel(x_ref, w_ref, o_ref):
    o_ref[...] = jnp.dot(x_ref[...].astype(BF16), w_ref[...], preferred_element_type=F32)


def _matmul(x, w, tm, tn):
    M, K = x.shape
    N = w.shape[1]
    tm = next(t for t in (tm, 256, 128, 64) if M % t == 0)
    return pl.pallas_call(
        _mm_kernel,
        grid=(M // tm, N // tn),
        in_specs=[pl.BlockSpec((tm, K), lambda i, j: (i, 0)),
                  pl.BlockSpec((K, tn), lambda i, j: (0, j))],
        out_specs=pl.BlockSpec((tm, tn), lambda i, j: (i, j)),
        out_shape=jax.ShapeDtypeStruct((M, N), F32),
        compiler_params=_params("parallel", "arbitrary"),
    )(x, w)


def _in_proj_kernel(x_ref, wa_ref, wb_ref, oa_ref, ob_ref):
    xb = x_ref[...].astype(BF16)
    oa_ref[...] = jnp.dot(xb, wa_ref[...], preferred_element_type=F32)
    ob_ref[...] = jnp.dot(xb, wb_ref[...], preferred_element_type=F32)


def _in_proj(x, wa, wb, e, tm):
    M, K = x.shape
    tm = next(t for t in (tm, 512, 256, 128, 64) if M % t == 0)
    na, nb = wa.shape[2], wb.shape[2]
    resident = lambda n: pl.BlockSpec((None, K, n), lambda i: (e, 0, 0), pipeline_mode=pl.Buffered(1))
    return pl.pallas_call(
        _in_proj_kernel,
        grid=(M // tm,),
        in_specs=[pl.BlockSpec((tm, K), lambda i: (i, 0)), resident(na), resident(nb)],
        out_specs=[pl.BlockSpec((tm, na), lambda i: (i, 0)), pl.BlockSpec((tm, nb), lambda i: (i, 0))],
        out_shape=[jax.ShapeDtypeStruct((M, na), F32), jax.ShapeDtypeStruct((M, nb), F32)],
        compiler_params=_params("parallel"),
    )(x, wa, wb)


def _mix_ffn_kernel(n_pairs, *refs):
    res_ref, lw1_ref, lb1_ref, wg_ref, wu_ref, wo_ref, lw2_ref, lb2_ref, o_ref = refs[2 * n_pairs:]
    tm = o_ref.shape[0]
    n_half = 2 if tm % (2 * BF16_ROWS) == 0 else 1
    halves = [slice(i * (tm // n_half), (i + 1) * (tm // n_half)) for i in range(n_half)]
    mm = lambda a, w: jnp.dot(a, w, preferred_element_type=F32)
    acc = [mm(refs[0][h, :].astype(BF16), refs[1][...]) for h in halves]
    for p in range(1, n_pairs):
        acc = [acc[i] + mm(refs[2 * p][h, :].astype(BF16), refs[2 * p + 1][...]) for i, h in enumerate(halves)]
    x = [_ln(DEEPNORM_ALPHA * res_ref[h, :] + acc[i], lw1_ref[...], lb1_ref[...]) for i, h in enumerate(halves)]
    xb = [v.astype(BF16) for v in x]
    g = [mm(v, wg_ref[...]) for v in xb]
    u = [mm(v, wu_ref[...]) for v in xb]
    act = [(g[i] * _sigmoid(g[i]) * u[i]).astype(BF16) for i in range(n_half)]
    y = [mm(v, wo_ref[...]) for v in act]
    for i, h in enumerate(halves):
        o_ref[h, :] = _ln(DEEPNORM_ALPHA * x[i] + y[i], lw2_ref[...], lb2_ref[...])


def _mix_ffn(pairs, res, lw1, lb1, w_in, w_out, lw2, lb2, layer, tm):
    M = res.shape[0]
    tm = next(t for t in (tm, 256, 128, 64) if M % t == 0)
    resident = lambda shape, idx: pl.BlockSpec((None,) + shape, idx, pipeline_mode=pl.Buffered(1))
    vec = pl.BlockSpec((None, 1, D_MODEL), lambda i: (layer, 0, 0))
    in_specs, args = [], []
    for a, w, n in pairs:
        K = a.shape[1]
        in_specs += [pl.BlockSpec((tm, K), lambda i: (i, 0)), resident((K, D_MODEL), lambda i, n=n: (n, 0, 0))]
        args += [a, w]
    in_specs += [pl.BlockSpec((tm, D_MODEL), lambda i: (i, 0)), vec, vec,
                 resident((D_MODEL, FFN_HIDDEN), lambda i: (layer, 0, 0)),
                 resident((D_MODEL, FFN_HIDDEN), lambda i: (layer, 0, 1)),
                 resident((FFN_HIDDEN, D_MODEL), lambda i: (layer, 0, 0)), vec, vec]
    return pl.pallas_call(
        functools.partial(_mix_ffn_kernel, len(pairs)),
        grid=(M // tm,),
        in_specs=in_specs,
        out_specs=pl.BlockSpec((tm, D_MODEL), lambda i: (i, 0)),
        out_shape=jax.ShapeDtypeStruct((M, D_MODEL), F32),
        compiler_params=_params("parallel"),
    )(*args, res, lw1, lb1, w_in, w_in, w_out, lw2, lb2)


def _inv_unit_lower(ns, eye, nilpotency):
    size = eye.shape[0]
    xs = [eye - n for n in ns]
    ps = [_dot1(n, n) for n in ns]
    k = 2
    while k < nilpotency:
        k *= 2
        if k < nilpotency:
            rs = [_dot1(p, jnp.concatenate([x, p], axis=1)) for x, p in zip(xs, ps)]
            xs = [x + r[:, :size] for x, r in zip(xs, rs)]
            ps = [r[:, size:] for r in rs]
        else:
            xs = [x + _dot1(p, x) for x, p in zip(xs, ps)]
    return xs


def _segsum(x, seg, split=False):
    n = x.shape[1] // PAIR
    rows = x.shape[0]
    xs = jnp.concatenate([x[:, j * PAIR:(j + 1) * PAIR] for j in range(n)], axis=0)
    s = _dot_xr(xs, seg) if split else _dot1(xs, seg)
    return jnp.concatenate([s[j * rows:(j + 1) * rows] for j in range(n)], axis=1)


def _head_mask():
    hr = lax.broadcasted_iota(jnp.int32, (PAIR, PAIR), 0) // A_HEAD_DIM
    hc = lax.broadcasted_iota(jnp.int32, (PAIR, PAIR), 1) // A_HEAD_DIM
    return (hr == hc).astype(BF16)


def _rwkv_prep_kernel(n_chunks, pa_ref, prev_ref, shift_ref, mu_ref, w0_ref, w2_ref, a0_ref, a2_ref, g2_ref,
                      kk_ref, ka_ref, rk_ref,
                      mh_ref, ch_ref, qc_ref, oc_ref, g_ref, bonus_ref):
    i = pl.program_id(1)
    L = CHUNK
    pa = pa_ref[...]
    first = jnp.where(i == 0, shift_ref[...], prev_ref[SUBLANE - 1:SUBLANE, :])
    rows = lax.broadcasted_iota(jnp.int32, pa.shape, 0)
    prev = jnp.where(rows == 0, first, pltpu.roll(pa, shift=1, axis=0))
    xs = pa + (prev - pa) * mu_ref[...]
    W = A_WIDTH
    r = xs[:, 0:W]
    k0 = xs[:, W:2 * W]
    v = xs[:, 2 * W:3 * W]
    xw = xs[:, 3 * W:3 * W + A_DECAY_LORA]
    xa = xs[:, 3 * W + A_DECAY_LORA:3 * W + A_DECAY_LORA + A_ICL_LORA]
    xg = xs[:, 3 * W + A_DECAY_LORA + A_ICL_LORA:]
    w = -_softplus(-(w0_ref[...] + _dot3(jnp.tanh(xw), w2_ref[...]))) - 0.5
    logw = -jnp.exp(w)
    a = _sigmoid(a0_ref[...] + _dot1(xa, a2_ref[...]))
    g_ref[...] = _dot1(_sigmoid(xg), g2_ref[...])
    seg = _head_mask()
    kk = k0 * kk_ref[...]
    kk = kk * lax.rsqrt(_segsum(kk * kk, seg) + 1e-12)
    k = k0 * (1.0 + (a - 1.0) * ka_ref[...])
    bonus_ref[...] = _segsum(r * k * rk_ref[...], seg) * v
    b = kk * a

    tri = (lax.broadcasted_iota(jnp.int32, (L, L), 0) >= lax.broadcasted_iota(jnp.int32, (L, L), 1)).astype(BF16)
    row = lax.broadcasted_iota(jnp.int32, (PAIR, PAIR), 0)
    col = lax.broadcasted_iota(jnp.int32, (PAIR, PAIR), 1)
    eye = (row == col).astype(F32)
    strict = row > col
    incl2 = (lax.broadcasted_iota(jnp.int32, (PAIR, 2 * PAIR), 0)
             >= lax.broadcasted_iota(jnp.int32, (PAIR, 2 * PAIR), 1) % PAIR)
    m0 = lax.broadcasted_iota(jnp.int32, (L, PAIR), 1) < HALF
    pairs = range(A_PAIRS)
    lanes = [slice(j * PAIR, (j + 1) * PAIR) for j in pairs]
    chains = [(c, j) for c in range(n_chunks) for j in pairs]
    every = range(len(chains))
    rows_of = [slice(c * L, (c + 1) * L) for c in range(n_chunks)]
    kkg, rg, kinv, binv, kd, bd, e_last = [], [], [], [], [], [], []
    for sl in rows_of:
        lw = logw[sl]
        cum = _dot_xl(tri, lw)
        c_last = cum[L - 1:L, :]
        e_neg = jnp.exp(-cum)
        e_rem = jnp.exp(c_last - cum)
        kkg.append(kk[sl] * jnp.exp(cum - lw))
        rg.append(r[sl] * jnp.exp(cum))
        kinv.append(k[sl] * e_neg)
        binv.append(b[sl] * e_neg)
        kd.append(k[sl] * e_rem)
        bd.append(b[sl] * e_rem)
        e_last.append(jnp.exp(c_last))
    sm = lambda xs_, c, j: _stack_masked(xs_[c][:, lanes[j]].astype(BF16), m0)
    kkg_b = [sm(kkg, c, j) for c, j in chains]
    v_b = [_stack_masked(v[rows_of[c], lanes[j]].astype(BF16), m0) for c, j in chains]
    bd_b = [sm(bd, c, j) for c, j in chains]
    kd_b = [sm(kd, c, j) for c, j in chains]
    rg_b = [sm(rg, c, j) for c, j in chains]
    kinv_b = [sm(kinv, c, j) for c, j in chains]
    binv_b = [sm(binv, c, j) for c, j in chains]
    aa = [_dg(jnp.concatenate([kkg_b[n], rg_b[n]], axis=0), jnp.concatenate([binv_b[n], kinv_b[n]], axis=0), NT)
          for n in every]
    a_kb = [jnp.where(strict, x[:PAIR, :PAIR], 0.0) for x in aa]
    a_kk = [jnp.where(strict, x[:PAIR, PAIR:], 0.0).astype(BF16) for x in aa]
    a_rbk = [jnp.where(incl2, x[PAIR:, :], 0.0).astype(BF16) for x in aa]
    t = _inv_unit_lower(a_kb, eye, L)
    av = [_dg(a_kk[n], v_b[n], NN).astype(BF16) for n in every]
    wu = [_dg(t[n].astype(BF16), jnp.concatenate([kkg_b[n], av[n]], axis=1), NN) for n in every]
    wu_b = [x.astype(BF16) for x in wu]
    mc = [_dg(bd_b[n], wu_b[n], TN) for n in every]
    kv = [_dg(kd_b[n], v_b[n], TN) for n in every]
    rhs = [jnp.concatenate([jnp.concatenate([wu_b[n][:, :PAIR], (-wu[n][:, PAIR:]).astype(BF16)], axis=1),
                            jnp.concatenate([jnp.zeros_like(v_b[n]), v_b[n]], axis=1)], axis=0) for n in every]
    qo = [_dg(a_rbk[n], rhs[n], NN) for n in every]
    for n, (c, j) in enumerate(chains):
        mh_ref[c, j] = (eye * e_last[c][:, lanes[j]] - mc[n][:, :PAIR]).astype(BF16)
        ch_ref[c, j] = kv[n] - mc[n][:, PAIR:]
        qc_ref[rows_of[c], lanes[j]] = (rg[c][:, lanes[j]] - _fold(qo[n][:, :PAIR])).astype(BF16)
        oc_ref[rows_of[c], lanes[j]] = _fold(qo[n][:, PAIR:])


def _rwkv_scan_kernel(n_seq, n_chunks, mh_ref, ch_ref, qc_ref, oc_ref, g_ref, bonus_ref, h0_ref, lnw_ref, lnb_ref,
                      y_ref, hout_ref, h_ref):
    i = pl.program_id(1)
    L = CHUNK

    @pl.when(i == 0)
    def _():
        h_ref[...] = h0_ref[...]

    seg = _head_mask()
    inv_n = 1.0 / A_HEAD_DIM
    lanes = [slice(j * PAIR, (j + 1) * PAIR) for j in range(A_PAIRS)]
    chains = [(s, j) for s in range(n_seq) for j in range(A_PAIRS)]
    hb = [h_ref[s, j] for s, j in chains]
    for c in range(n_chunks):
        sl = slice(c * L, (c + 1) * L)
        hb_b = [x.astype(BF16) for x in hb]
        o = [_dg(qc_ref[s, sl, lanes[j]], hb_b[n], NN) + oc_ref[s, sl, lanes[j]] for n, (s, j) in enumerate(chains)]
        hb = [_dg(mh_ref[s, c, j], hb_b[n], NN) + ch_ref[s, c, j] for n, (s, j) in enumerate(chains)]
        for s in range(n_seq):
            os_ = jnp.concatenate(o[s * A_PAIRS:(s + 1) * A_PAIRS], axis=1)
            mean = _segsum(os_, seg, split=True) * inv_n
            d = os_ - mean
            var = _segsum(d * d, seg, split=True) * inv_n
            on = d * lax.rsqrt(var + A_NORM_EPS) * lnw_ref[...] + lnb_ref[...]
            y_ref[s, sl, :] = ((on + bonus_ref[s, sl, :]) * g_ref[s, sl, :]).astype(y_ref.dtype)
    for n, (s, j) in enumerate(chains):
        h_ref[s, j] = hb[n]
    hout_ref[...] = h_ref[...]


def _rwkv_mix(pa, shift0, wkv0, prm):
    Bsz, T, _ = pa.shape
    n1 = next(n for n in (8, 4, 2, 1) if (T // CHUNK) % n == 0)
    R1 = n1 * CHUNK
    W = A_WIDTH
    nblk = T // R1
    row = lambda n: pl.BlockSpec((1, n), lambda b, i: (0, 0))
    full = lambda s: pl.BlockSpec(s, lambda b, i: (0,) * len(s))
    act = lambda r, n: pl.BlockSpec((None, r, n), lambda b, i: (b, i, 0))
    mat = lambda n: pl.BlockSpec((None, n, A_PAIRS, PAIR, PAIR), lambda b, i: (b, i, 0, 0, 0))
    sub = R1 // SUBLANE
    mats = lambda dt: jax.ShapeDtypeStruct((Bsz, T // CHUNK, A_PAIRS, PAIR, PAIR), dt)
    acts = lambda dt: jax.ShapeDtypeStruct((Bsz, T, W), dt)
    mh, ch, qc, oc, g, bonus = pl.pallas_call(
        functools.partial(_rwkv_prep_kernel, n1),
        grid=(Bsz, nblk),
        in_specs=[act(R1, A_PROJ),
                  pl.BlockSpec((None, SUBLANE, A_PROJ), lambda b, i: (b, jnp.maximum(i * sub - 1, 0), 0)),
                  pl.BlockSpec((None, 1, A_PROJ), lambda b, i: (b, 0, 0)),
                  row(A_PROJ), row(W), full((A_DECAY_LORA, W)), row(W), full((A_ICL_LORA, W)),
                  full((A_GATE_LORA, W)), row(W), row(W), row(W)],
        out_specs=[mat(n1), mat(n1), act(R1, W), act(R1, W), act(R1, W), act(R1, W)],
        out_shape=[mats(BF16), mats(F32), acts(BF16), acts(F32), acts(F32), acts(F32)],
        compiler_params=_params("parallel", "parallel"),
    )(pa, pa, shift0[:, None, :], prm['mu'], prm['w0'], prm['w2'], prm['a0'], prm['a2'], prm['g2'],
      prm['k_k'], prm['k_a'], prm['r_k'])

    n2 = min(4, T // CHUNK)
    R2 = n2 * CHUNK
    hh = wkv0.transpose(0, 1, 3, 2).reshape(Bsz, A_PAIRS, 2, A_HEAD_DIM, A_HEAD_DIM)
    zero = jnp.zeros_like(hh[:, :, 0])
    h0 = jnp.concatenate([jnp.concatenate([hh[:, :, 0], zero], axis=-1),
                          jnp.concatenate([zero, hh[:, :, 1]], axis=-1)], axis=-2)
    ns = next(n for n in (4, 2, 1) if Bsz % n == 0)
    state = pl.BlockSpec((ns, A_PAIRS, PAIR, PAIR), lambda b, i: (b, 0, 0, 0))
    act2 = lambda r, n: pl.BlockSpec((ns, r, n), lambda b, i: (b, i, 0))
    mat2 = lambda n: pl.BlockSpec((ns, n, A_PAIRS, PAIR, PAIR), lambda b, i: (b, i, 0, 0, 0))
    y, hout = pl.pallas_call(
        functools.partial(_rwkv_scan_kernel, ns, n2),
        grid=(Bsz // ns, T // R2),
        in_specs=[mat2(n2), mat2(n2), act2(R2, W), act2(R2, W), act2(R2, W), act2(R2, W), state,
                  row(W), row(W)],
        out_specs=[act2(R2, W), state],
        out_shape=[acts(BF16), jax.ShapeDtypeStruct((Bsz, A_PAIRS, PAIR, PAIR), F32)],
        scratch_shapes=[pltpu.VMEM((ns, A_PAIRS, PAIR, PAIR), F32)],
        compiler_params=_params("parallel", "arbitrary"),
    )(mh, ch, qc, oc, g, bonus, h0, prm['ln_w'], prm['ln_b'])
    hd = jnp.stack([hout[:, :, :HALF, :HALF], hout[:, :, HALF:, HALF:]], axis=2)
    wkv = hd.reshape(Bsz, A_HEADS, A_HEAD_DIM, A_HEAD_DIM).transpose(0, 1, 3, 2)
    return y, wkv


def _gla_kernel(n_seq, n_chunks, pb_ref, s0_ref, up_ref, bias_ref, nw_ref, y_ref, sout_ref, s_ref):
    i = pl.program_id(1)
    L = CHUNK

    @pl.when(i == 0)
    def _():
        s_ref[...] = s0_ref[...]

    kw, vw = B_KWIDTH, B_VWIDTH
    incl = lax.broadcasted_iota(jnp.int32, (L, L), 0) >= lax.broadcasted_iota(jnp.int32, (L, L), 1)
    tri = incl.astype(BF16)
    m0 = lax.broadcasted_iota(jnp.int32, (L, PAIR), 1) < HALF
    top = lax.broadcasted_iota(jnp.int32, (PAIR, B_VAL_DIM), 0) < HALF
    scale = B_KEY_DIM ** -0.5
    pairs = range(B_PAIRS)
    heads = range(B_HEADS)
    pl_ = lambda h: slice((h // 2) * PAIR, (h // 2 + 1) * PAIR)
    units = [(s, c) for c in range(n_chunks) for s in range(n_seq)]
    rows_of = lambda c: slice(c * L, (c + 1) * L)
    vslice = lambda h: slice(h * B_VAL_DIM, (h + 1) * B_VAL_DIM)
    z = [_dot3(pb_ref[s, rows_of(c), 2 * kw + 2 * vw:], up_ref[...]) + bias_ref[...] for s, c in units]
    bc = [_dot_xl(tri, -_softplus(-x) * (1.0 / B_GATE_NORM)) for x in z]
    b_last = [x[L - 1:L, :] for x in bc]
    q_dec, k_inv, k_state, e_last_t = [], [], [], []
    for n, (s, c) in enumerate(units):
        k = pb_ref[s, rows_of(c), kw:2 * kw]
        q_dec.append(pb_ref[s, rows_of(c), 0:kw] * scale * jnp.exp(bc[n]))
        k_inv.append((k * jnp.exp(-bc[n])).astype(BF16))
        k_state.append((k * jnp.exp(b_last[n] - bc[n])).astype(BF16))
        e_last_t.append(jnp.broadcast_to(jnp.exp(b_last[n]), (LANE, kw)).T)
    uh = [(n, h) for n in range(len(units)) for h in heads]
    vb = [pb_ref[units[n][0], rows_of(units[n][1]), 2 * kw + h * B_VAL_DIM:2 * kw + (h + 1) * B_VAL_DIM].astype(BF16)
          for n, h in uh]
    qm = [jnp.where(m0 if h % 2 == 0 else ~m0, q_dec[n][:, pl_(h)], 0.0).astype(BF16) for n, h in uh]
    scores = [jnp.where(incl, _dg(qm[i_], k_inv[n][:, pl_(h)], NT), 0.0).astype(BF16) for i_, (n, h) in enumerate(uh)]
    upd = [_dg(k_state[n][:, pl_(h)], vb[i_], TN) for i_, (n, h) in enumerate(uh)]
    intra = [_dg(scores[i_], vb[i_], NN) for i_ in range(len(uh))]
    states = [[s_ref[s, j] for j in pairs] for s in range(n_seq)]
    before = []
    for n, (s, c) in enumerate(units):
        before.append([x.astype(BF16) for x in states[s]])
        states[s] = [states[s][j] * e_last_t[n][j * PAIR:(j + 1) * PAIR, :]
                     + jnp.where(top, upd[n * B_HEADS + 2 * j], upd[n * B_HEADS + 2 * j + 1]) for j in pairs]
    o = [_dg(qm[i_], before[n][h // 2], NN) + intra[i_] for i_, (n, h) in enumerate(uh)]
    for i_, (n, h) in enumerate(uh):
        s, c = units[n]
        oh = o[i_] * lax.rsqrt(jnp.mean(o[i_] * o[i_], -1, keepdims=True) + LN_EPS) * nw_ref[...]
        rg_h = pb_ref[s, rows_of(c), 2 * kw + vw + h * B_VAL_DIM:2 * kw + vw + (h + 1) * B_VAL_DIM]
        y_ref[s, rows_of(c), vslice(h)] = (oh * (rg_h * _sigmoid(rg_h))).astype(y_ref.dtype)
    for s in range(n_seq):
        for j in pairs:
            s_ref[s, j] = states[s][j]
    sout_ref[...] = s_ref[...]


def _gla_mix(pb, s0, prm):
    Bsz, T, _ = pb.shape
    n = min(4, T // CHUNK)
    Rr = n * CHUNK
    ns = next(n for n in (4, 2, 1) if Bsz % n == 0)
    st = pl.BlockSpec((ns, B_PAIRS, PAIR, B_VAL_DIM), lambda b, i: (b, 0, 0, 0))
    y, s = pl.pallas_call(
        functools.partial(_gla_kernel, ns, n),
        grid=(Bsz // ns, T // Rr),
        in_specs=[pl.BlockSpec((ns, Rr, B_PROJ_PAD), lambda b, i: (b, i, 0)), st,
                  pl.BlockSpec((B_LORA_PAD, B_KWIDTH), lambda b, i: (0, 0)),
                  pl.BlockSpec((1, B_KWIDTH), lambda b, i: (0, 0)),
                  pl.BlockSpec((1, B_VAL_DIM), lambda b, i: (0, 0))],
        out_specs=[pl.BlockSpec((ns, Rr, B_VWIDTH), lambda b, i: (b, i, 0)), st],
        out_shape=[jax.ShapeDtypeStruct((Bsz, T, B_VWIDTH), BF16),
                   jax.ShapeDtypeStruct((Bsz, B_PAIRS, PAIR, B_VAL_DIM), F32)],
        scratch_shapes=[pltpu.VMEM((ns, B_PAIRS, PAIR, B_VAL_DIM), F32)],
        compiler_params=_params("parallel", "arbitrary"),
    )(pb, s0.reshape(Bsz, B_PAIRS, PAIR, B_VAL_DIM), prm['up'], prm['bias'], prm['nw'])
    return y, s.reshape(Bsz, B_HEADS, B_KEY_DIM, B_VAL_DIM)


ATT_QB = 2 * CHUNK
ATT_SCALE = C_HEAD_DIM ** -0.5
LOG2E = 1.4426950408889634


ATT_WIN = C_PAST_ROWS + ATT_QB
ATT_MAX_QB = 16
ATT_PAD = ATT_MAX_QB * ATT_QB


def _qkv_pad_kernel(n_pad, x_ref, w_ref, o_ref):
    r = pl.program_id(1)

    @pl.when(r < n_pad)
    def _():
        o_ref[...] = jnp.zeros_like(o_ref)

    @pl.when(r >= n_pad)
    def _():
        o_ref[...] = jnp.dot(x_ref[...].astype(BF16), w_ref[...], preferred_element_type=F32).astype(BF16)


def _qkv_padded(x, w, o):
    Bsz, T, D = x.shape
    N = w.shape[2]
    tm = next(t for t in (TM_PROJ, 512, 256, 128) if T % t == 0)
    n_pad = ATT_PAD // tm
    return pl.pallas_call(
        functools.partial(_qkv_pad_kernel, n_pad),
        grid=(Bsz, T // tm + n_pad),
        in_specs=[pl.BlockSpec((None, tm, D), lambda b, r: (b, jnp.maximum(r - n_pad, 0), 0)),
                  pl.BlockSpec((None, D, N), lambda b, r: (o, 0, 0), pipeline_mode=pl.Buffered(1))],
        out_specs=pl.BlockSpec((None, tm, N), lambda b, r: (b, r, 0)),
        out_shape=jax.ShapeDtypeStruct((Bsz, ATT_PAD + T, N), BF16),
        compiler_params=_params("parallel", "arbitrary"),
    )(x, w)


def _kv_tail_kernel(x_ref, wt_ref, o_ref):
    o_ref[...] = _dg(wt_ref[...], x_ref[...].astype(BF16), NT)


def _kv_tail(x, w_t, o, keep):
    Bsz, T, D = x.shape
    tm = next(t for t in (512, 256, 128) if keep % t == 0 and T % t == 0)
    first = (T - keep) // tm
    return pl.pallas_call(
        _kv_tail_kernel,
        grid=(Bsz, keep // tm, 2),
        in_specs=[pl.BlockSpec((None, tm, D), lambda b, r, j: (b, first + r, 0)),
                  pl.BlockSpec((None, D, D), lambda b, r, j: (o, j, 0))],
        out_specs=pl.BlockSpec((None, None, D, tm), lambda b, r, j: (b, j, 0, r)),
        out_shape=jax.ShapeDtypeStruct((Bsz, 2, D, keep), F32),
        compiler_params=_params("parallel", "parallel", "arbitrary"),
    )(x, w_t)


def _att_prompt_kernel(n_qb, q_ref, k_ref, v_ref, bias_ref, o_ref):
    g = pl.program_id(2)
    m0 = lax.broadcasted_iota(jnp.int32, (ATT_QB, PAIR), 1) < HALF
    col = lax.broadcasted_iota(jnp.int32, (2 * ATT_QB, ATT_WIN), 1)

    def run(window_has_padding):
        rows = [slice(t * ATT_QB, (t + 1) * ATT_QB) for t in range(n_qb)]
        wins = [pl.ds(pl.multiple_of(ATT_PAD - C_PAST_ROWS + (g * n_qb + t) * ATT_QB, ATT_QB), ATT_WIN)
                for t in range(n_qb)]
        ones = jnp.ones((ATT_WIN, PAIR), BF16)
        scores = lambda t: _dg(_stack_masked(q_ref[rows[t], :], m0), k_ref[wins[t], :], NT) + bias_ref[...]
        s_next = scores(0)
        for t in range(n_qb):
            s = s_next
            if t + 1 < n_qb:
                s_next = scores(t + 1)
            if window_has_padding:
                s = jnp.where(col >= C_PAST_ROWS - (g * n_qb + t) * ATT_QB, s, NEG_INF)
            p = jnp.exp2(s - s.max(-1, keepdims=True))
            pv = _dg(p.astype(BF16), jnp.concatenate([v_ref[wins[t], :], ones], axis=1), NN)
            o = pv[:, :PAIR] / pv[:, PAIR:]
            o_ref[rows[t], :] = jnp.where(m0, o[:ATT_QB], o[ATT_QB:]).astype(o_ref.dtype)

    first = g * (n_qb * ATT_QB) < C_PAST_ROWS
    pl.when(first)(lambda: run(True))
    pl.when(jnp.logical_not(first))(lambda: run(False))


def _attention_prompt(qkv, bias):
    Bsz, Tp, _ = qkv.shape
    T = Tp - ATT_PAD
    nhp = C_HEADS // 2
    nblk = T // ATT_QB
    n_qb = next(n for n in (ATT_MAX_QB, 8, 4, 2, 1) if nblk % n == 0)
    rows = n_qb * ATT_QB
    skip = ATT_PAD // rows
    return pl.pallas_call(
        functools.partial(_att_prompt_kernel, n_qb),
        grid=(Bsz, nhp, T // rows),
        in_specs=[pl.BlockSpec((None, rows, LANE), lambda b, hp, g: (b, g + skip, hp)),
                  pl.BlockSpec((None, Tp, LANE), lambda b, hp, g: (b, 0, nhp + hp)),
                  pl.BlockSpec((None, Tp, LANE), lambda b, hp, g: (b, 0, 2 * nhp + hp)),
                  pl.BlockSpec((None, 2 * ATT_QB, ATT_WIN), lambda b, hp, g: (hp, 0, 0))],
        out_specs=pl.BlockSpec((None, rows, LANE), lambda b, hp, g: (b, g, hp)),
        out_shape=jax.ShapeDtypeStruct((Bsz, T, D_MODEL), BF16),
        compiler_params=_params("parallel", "parallel", "arbitrary"),
    )(qkv, qkv, qkv, bias)


def _att_sample_kernel(qkv_ref, kc_ref, vc_ref, bias_ref, o_ref):
    R = kc_ref.shape[2]
    heads = range(C_HEADS)
    col = lambda part, h: slice(part * D_MODEL + h * C_HEAD_DIM, part * D_MODEL + (h + 1) * C_HEAD_DIM)
    q = [(qkv_ref[:, col(0, h)] * ATT_SCALE).astype(BF16) for h in heads]
    s_c = [_dg(q[h], kc_ref[h].astype(BF16), NN) + bias_ref[h, :, 0:R] for h in heads]
    s_n = [_dg(q[h], qkv_ref[:, col(1, h)].astype(BF16), NT) + bias_ref[h, :, R:] for h in heads]
    m = [jnp.maximum(s_c[h].max(-1, keepdims=True), s_n[h].max(-1, keepdims=True)) for h in heads]
    p_c = [jnp.exp(s_c[h] - m[h]) for h in heads]
    p_n = [jnp.exp(s_n[h] - m[h]) for h in heads]
    acc = [_dg(p_c[h].astype(BF16), vc_ref[h].astype(BF16), NT)
           + _dg(p_n[h].astype(BF16), qkv_ref[:, col(2, h)].astype(BF16), NN) for h in heads]
    for h in heads:
        den = p_c[h].sum(-1, keepdims=True) + p_n[h].sum(-1, keepdims=True)
        o_ref[:, col(0, h)] = (acc[h] / den).astype(o_ref.dtype)


def _attention_sample(qkv, cache_k, cache_v, bias, o):
    Bsz, T, _ = qkv.shape
    R = cache_k.shape[4]
    cache = pl.BlockSpec((None, None, C_HEADS, C_HEAD_DIM, R), lambda b: (o, b, 0, 0, 0))
    return pl.pallas_call(
        _att_sample_kernel,
        grid=(Bsz,),
        in_specs=[pl.BlockSpec((None, T, 3 * D_MODEL), lambda b: (b, 0, 0)), cache, cache,
                  pl.BlockSpec((C_HEADS, T, R + T), lambda b: (0, 0, 0))],
        out_specs=pl.BlockSpec((None, T, D_MODEL), lambda b: (b, 0, 0)),
        out_shape=jax.ShapeDtypeStruct((Bsz, T, D_MODEL), BF16),
        compiler_params=_params("parallel"),
    )(qkv, cache_k, cache_v, bias)


def _rel_bias_tile(table, n_q, n_k, q_offset):
    period = n_q + n_k - 1
    m = np.arange(period)
    d = np.where(m < n_k, m, m - period)
    idx = np.clip(q_offset - d, -C_REL_CLIP, C_REL_CLIP) + C_REL_CLIP
    diag = table[:, idx]
    flat = jnp.tile(diag, (1, n_q))[:, :n_q * (period - 1)]
    return flat.reshape(table.shape[0], n_q, period - 1)[:, :, :n_k]


TM = 512
TM_PROJ = 1024


def _trunk(x, wkv0, shift0, gla0, cache_k, cache_v, P):
    Bsz, T, _ = x.shape
    M = Bsz * T
    xf = x.reshape(M, D_MODEL)
    wkv_o, shift_o, gla_o, k_o, v_o = [], [], [], [], []
    for layer in range(DEPTH):
        if layer % 2 == 0:
            e = layer // 2
            pa, pb = _in_proj(xf, P['w_in_a'], P['w_in_b'], e, TM_PROJ)
            pa = pa.reshape(Bsz, T, A_PROJ)
            pb = pb.reshape(Bsz, T, B_PROJ_PAD)
            ya, wkv = _rwkv_mix(pa, shift0[e], wkv0[e], {k: v[e] for k, v in P['a'].items()})
            yb, gs = _gla_mix(pb, gla0[e], {k: v[e] for k, v in P['b'].items()})
            wkv_o.append(wkv)
            shift_o.append(pa[:, -1])
            gla_o.append(gs)
            pairs = [(ya.reshape(M, A_WIDTH), P['w_out_a'], e), (yb.reshape(M, B_VWIDTH), P['w_out_b'], e)]
        else:
            o = layer // 2
            if cache_k is None:
                x3 = xf.reshape(Bsz, T, D_MODEL)
                att = _attention_prompt(_qkv_padded(x3, P['c_w_qkv_scaled'], o), P['c_bias_prompt'][o])
                keep = min(C_PAST_ROWS, T)
                rows_t = _kv_tail(x3, P['c_w_kv_t'], o, keep).reshape(Bsz, 2, C_HEADS, C_HEAD_DIM, keep)
                k_o.append(jnp.swapaxes(rows_t[:, 0], -1, -2))
                v_o.append(jnp.swapaxes(rows_t[:, 1], -1, -2))
            else:
                qkv = _matmul(xf, P['c_w_qkv'][o], TM, D_MODEL).reshape(Bsz, T, 3 * D_MODEL)
                att = _attention_sample(qkv, jnp.swapaxes(cache_k, -1, -2), jnp.swapaxes(cache_v, -1, -2),
                                        P['c_bias_sample'][o], o)
                rows = qkv.reshape(Bsz, T, 3, C_HEADS, C_HEAD_DIM)
                k_o.append(rows[:, :, 1].transpose(0, 2, 1, 3))
                v_o.append(rows[:, :, 2].transpose(0, 2, 1, 3))
            pairs = [(att.reshape(M, D_MODEL), P['c_w_o'], o)]
        xf = _mix_ffn(pairs, xf, P['ln1_w'], P['ln1_b'], P['ffn_w_in'], P['ffn_w_out'], P['ln2_w'], P['ln2_b'],
                      layer, TM)
    return (xf.reshape(Bsz, T, D_MODEL), jnp.stack(wkv_o), jnp.stack(shift_o), jnp.stack(gla_o),
            jnp.stack(k_o), jnp.stack(v_o))


def _prepare(w_in_mix, a_mu, a_w0, a_w2, a_a0, a_a2, a_g2, a_k_k, a_k_a, a_r_k, a_ln_w, a_ln_b,
             b_alpha_up, b_alpha_bias, b_norm_w, w_out_mix, c_w_qkv, c_rel_bias, c_w_o,
             ln1_w, ln1_b, ln2_w, ln2_b, ffn_w_in, ffn_w_out, sample_len, cache_rows):
    kw, vw = B_KWIDTH, B_VWIDTH
    wb = w_in_mix[:, :, A_PROJ:]
    main = jnp.concatenate([wb[:, :, :2 * kw + vw], wb[:, :, 2 * kw + vw + B_GATE_LORA:]], axis=-1)
    lora = jnp.pad(wb[:, :, 2 * kw + vw:2 * kw + vw + B_GATE_LORA], ((0, 0), (0, 0), (0, B_LORA_PAD - B_GATE_LORA)))
    row3 = lambda t: t.reshape(t.shape[0], 1, -1)
    bias_prompt = jnp.stack([_rel_bias_tile(t, ATT_QB, ATT_WIN, C_PAST_ROWS) for t in c_rel_bias])
    rel = np.arange(ATT_WIN)[None, :] // CHUNK - np.arange(ATT_QB)[:, None] // CHUNK
    band = (rel >= 0) & (rel <= C_PAST_CHUNKS)
    bias_prompt = jnp.where(band, bias_prompt * LOG2E, NEG_INF)
    q_scale = jnp.concatenate([jnp.full((D_MODEL,), ATT_SCALE * LOG2E, F32), jnp.ones((2 * D_MODEL,), F32)])
    P = {
        'w_in_a': w_in_mix[:, :, :A_PROJ].astype(BF16),
        'w_in_b': jnp.concatenate([main, lora], axis=-1).astype(BF16),
        'a': dict(mu=row3(a_mu), w0=row3(a_w0), w2=a_w2, a0=row3(a_a0), a2=a_a2, g2=a_g2, k_k=row3(a_k_k),
                  k_a=row3(a_k_a), r_k=row3(a_r_k), ln_w=row3(a_ln_w), ln_b=row3(a_ln_b)),
        'b': dict(up=jnp.pad(b_alpha_up, ((0, 0), (0, B_LORA_PAD - B_GATE_LORA), (0, 0))),
                  bias=row3(b_alpha_bias), nw=row3(b_norm_w)),
        'w_out_a': w_out_mix[:, :A_WIDTH].astype(BF16),
        'w_out_b': w_out_mix[:, A_WIDTH:].astype(BF16),
        'c_w_qkv': c_w_qkv.astype(BF16),
        'c_w_qkv_scaled': (c_w_qkv * q_scale).astype(BF16),
        'c_w_kv_t': jnp.swapaxes(c_w_qkv[:, :, D_MODEL:], 1, 2).astype(BF16),
        'c_w_o': c_w_o.astype(BF16),
        'c_bias_prompt': bias_prompt.reshape(bias_prompt.shape[0], C_HEADS // 2, 2 * ATT_QB, ATT_WIN),
        'c_bias_sample': jnp.stack([_rel_bias_tile(t, sample_len, cache_rows + sample_len, cache_rows)
                                    for t in c_rel_bias]),
        'ln1_w': row3(ln1_w), 'ln1_b': row3(ln1_b), 'ln2_w': row3(ln2_w), 'ln2_b': row3(ln2_b),
        'ffn_w_in': ffn_w_in.astype(BF16),
        'ffn_w_out': ffn_w_out.astype(BF16),
    }
    return P


def kernel(x_prompt, x_sample, state_a_wkv, state_a_shift, state_b_gla, cache_c_k, cache_c_v, w_in_mix, a_mu, a_w0, a_w2, a_a0, a_a2, a_g2, a_k_k, a_k_a, a_r_k, a_ln_w, a_ln_b, b_alpha_up, b_alpha_bias, b_norm_w, w_out_mix, c_w_qkv, c_rel_bias, c_w_o, ln1_w, ln1_b, ln2_w, ln2_b, ffn_w_in, ffn_w_out):
    P = _prepare(w_in_mix, a_mu, a_w0, a_w2, a_a0, a_a2, a_g2, a_k_k, a_k_a, a_r_k, a_ln_w, a_ln_b,
                 b_alpha_up, b_alpha_bias, b_norm_w, w_out_mix, c_w_qkv, c_rel_bias, c_w_o,
                 ln1_w, ln1_b, ln2_w, ln2_b, ffn_w_in, ffn_w_out, x_sample.shape[1], cache_c_k.shape[3])
    bp = x_prompt.shape[0]
    dt = x_prompt.dtype
    n_even = state_a_wkv.shape[0]
    wkv_zero = jnp.zeros((n_even, bp, A_HEADS, A_HEAD_DIM, A_HEAD_DIM), dt)
    shift_zero = jnp.zeros((n_even, bp, A_PROJ), dt)
    gla_zero = jnp.zeros((n_even, bp, B_HEADS, B_KEY_DIM, B_VAL_DIM), dt)
    y_prompt, p_wkv, p_shift, p_gla, p_k, p_v = _trunk(x_prompt, wkv_zero, shift_zero, gla_zero, None, None, P)
    y_sample, s_wkv, s_shift, s_gla, s_k, s_v = _trunk(
        x_sample, state_a_wkv, state_a_shift, state_b_gla, cache_c_k, cache_c_v, P)
    return (y_prompt, y_sample, p_wkv, p_shift, p_gla, p_k, p_v, s_wkv, s_shift, s_gla, s_k, s_v)
```
